```python
import math
import jax
import jax.numpy as jnp
from jax import lax
import numpy as np

D_MODEL = 1024
BATCH = 32
SEQ = 256
DEPTH = 2
DEC_BATCH = 8
DEC_SEQ = 1024
PAST_LEN = 256

GRID_W = 64
N_EVEN = (DEPTH + 1) // 2
N_ODD = DEPTH // 2
N_MOD = 6
EPS = 1e-6
GN_EPS = 1e-5

RET_DK = 128
RET_DV = 128
RET_HEADS = (D_MODEL // 2) // RET_DV
RET_W = RET_HEADS * RET_DV
RET_QK = RET_HEADS * RET_DK
RET_CHUNK = 64
SC_W = D_MODEL // 2
EVEN_IN = 2 * RET_QK + 2 * RET_W + 3 * SC_W
EVEN_OUT = RET_W + SC_W

HY_W = D_MODEL // 2
HY_ORDER = 2
HY_EMB = 33
HY_FFN = 64
HY_TARGET = 1e-2
HY_FAST_PCT = 0.3
HY_SLOW_PCT = 1.5
DA_DH = 64
DA_HEADS = (D_MODEL // 2) // (2 * DA_DH)
DA_W = DA_HEADS * 2 * DA_DH
ODD_IN = 3 * HY_W + 3 * DA_W
ODD_OUT = HY_W + DA_W
ROPE_THETA = 10000.0
Q_BLOCK = 128

N_GROUPS = 4
EXP_PER_GROUP = 8
N_EXPERTS = N_GROUPS * EXP_PER_GROUP
TOP_K = 2
D_EXPERT = 512
MOE_BLOCK = 128

kernel_name = 'hybrid_diffusion_prefix_step'


def split_cols(x, sizes):
    out, start = [], 0
    for s in sizes:
        out.append(x[..., start:start + s])
        start += s
    return out


def rmsnorm(x, g):
    xf = x.astype(jnp.float32)
    y = xf * lax.rsqrt(jnp.mean(xf * xf, axis=-1, keepdims=True) + EPS)
    return (y * g.astype(jnp.float32)).astype(x.dtype)


def conv3(z, w):
    zp = jnp.pad(z, ((0, 0), (1, 1), (0, 0)))
    return zp[:, :-2] * w[0] + zp[:, 1:-1] * w[1] + zp[:, 2:] * w[2]


def retention_dir(q, k, v, log_g, s0):
    b, length, h, _ = q.shape
    dv = v.shape[-1]
    n = length // RET_CHUNK

    def chunks(a):
        return a.reshape(b, n, RET_CHUNK, h, a.shape[-1]).transpose(1, 0, 3, 2, 4)

    idx = jnp.arange(RET_CHUNK, dtype=jnp.float32)
    diff = idx[:, None] - idx[None, :]
    dmat = jnp.where(diff >= 0, jnp.exp(jnp.maximum(diff, 0.0)[None] * log_g[:, None, None]), 0.0)
    xi = jnp.exp((idx + 1.0)[None, :] * log_g[:, None])
    zeta = jnp.exp((RET_CHUNK - 1.0 - idx)[None, :] * log_g[:, None])
    g_chunk = jnp.exp(RET_CHUNK * log_g)

    def step(r, qkv):
        qi, ki, vi = qkv
        inner = jnp.einsum('bhij,bhje->bhie', jnp.einsum('bhid,bhjd->bhij', qi, ki) * dmat, vi)
        cross = jnp.einsum('bhid,bhde->bhie', qi, r) * xi[None, :, :, None]
        r = g_chunk[None, :, None, None] * r + jnp.einsum('bhjd,bhje->bhde', ki * zeta[None, :, :, None], vi)
        return r, inner + cross

    r, o = lax.scan(step, s0, (chunks(q), chunks(k), chunks(v)))
    return o.transpose(1, 0, 3, 2, 4).reshape(b, length, h, dv), r


def even_mixer(u, w_in, w_out, dec_fwd, dec_bwd, sc_w, s0):
    f32 = jnp.float32
    b, length, _ = u.shape
    q, k, v, g, gate_b, gate_c, hx = split_cols(u @ w_in, [RET_QK, RET_QK, RET_W, RET_W, SC_W, SC_W, SC_W])
    q = q.reshape(b, length, RET_HEADS, RET_DK).astype(f32)
    k = k.reshape(b, length, RET_HEADS, RET_DK).astype(f32) * (RET_DK ** -0.5)
    v = v.reshape(b, length, RET_HEADS, RET_DV).astype(f32)
    lg_f = jnp.log1p(-jnp.exp(dec_fwd.astype(f32)))
    lg_b = jnp.log1p(-jnp.exp(dec_bwd.astype(f32)))
    s0 = s0.astype(f32)
    o_f, s_f = retention_dir(q, k, v, lg_f, s0[:, 0])
    o_b, s_b = retention_dir(q[:, ::-1], k[:, ::-1], v[:, ::-1], lg_b, s0[:, 1])
    o = o_f + o_b[:, ::-1]
    mu = jnp.mean(o, axis=-1, keepdims=True)
    var = jnp.mean(jnp.square(o - mu), axis=-1, keepdims=True)
    o = ((o - mu) * lax.rsqrt(var + GN_EPS)).reshape(b, length, RET_W).astype(u.dtype)
    ret = jax.nn.silu(g) * o
    sc = gate_b * conv3(gate_c * hx, sc_w)
    y = jnp.concatenate([ret, sc], axis=-1) @ w_out
    return y, jnp.stack([s_f, s_b], axis=1)


def hyena_filters(length, w1, b1, fr1, w2, b2, fr2, w3):
    f32 = jnp.float32
    t = jnp.linspace(0.0, 1.0, length, dtype=f32)[:, None]
    bands = (HY_EMB - 1) // 2
    w = 2.0 * math.pi * jnp.arange(length, dtype=f32)[:, None] / length
    f = jnp.linspace(1e-4, bands - 1, bands, dtype=f32)[None, :]
    z = jnp.concatenate([t, jnp.cos(f * w), -jnp.sin(f * w)], axis=-1)
    h = jnp.sin(fr1.astype(f32) * (z @ w1.astype(f32) + b1.astype(f32)))
    h = jnp.sin(fr2.astype(f32) * (h @ w2.astype(f32) + b2.astype(f32)))
    h = (h @ w3.astype(f32)).reshape(length, HY_ORDER, 2, HY_W)
    deltas = jnp.abs(jnp.linspace(math.log(HY_TARGET) / HY_SLOW_PCT, math.log(HY_TARGET) / HY_FAST_PCT, HY_W, dtype=f32))
    h = h * jnp.exp(-t * deltas[None, :])[:, None, None, :]
    h = h.transpose(1, 2, 0, 3)
    filt = jnp.concatenate([h[:, 0], jnp.zeros((HY_ORDER, 1, HY_W), f32), h[:, 1, :0:-1]], axis=1)
    filt = filt / (jnp.sum(jnp.abs(filt), axis=1, keepdims=True) + EPS)
    return jnp.fft.rfft(filt, axis=1)


def hyena_mixer(hy_in, conv_w, w1, b1, fr1, w2, b2, fr2, w3, bias):
    length = hy_in.shape[1]
    x1, x2, v = split_cols(conv3(hy_in, conv_w), [HY_W, HY_W, HY_W])
    kf = hyena_filters(length, w1, b1, fr1, w2, b2, fr2, w3)
    bias = bias.astype(jnp.float32)

    def long_conv(s, o):
        sf = s.astype(jnp.float32)
        y = jnp.fft.irfft(jnp.fft.rfft(sf, n=2 * length, axis=1) * kf[o][None], n=2 * length, axis=1)[:, :length]
        return (y + sf * bias[o]).astype(s.dtype)

    z = x1 * long_conv(v, 0)
    return x2 * long_conv(z, 1)


def axial_rope_angles(length):
    rows = length // GRID_W
    row = jnp.repeat(jnp.arange(rows, dtype=jnp.float32), GRID_W)
    col = jnp.tile(jnp.arange(GRID_W, dtype=jnp.float32), rows)
    half = DA_DH // 2
    freqs = ROPE_THETA ** (-jnp.arange(0, half, 2, dtype=jnp.float32) / half)
    return row[:, None] * freqs[None, :], col[:, None] * freqs[None, :]


def rotate(x, ang):
    m = ang.shape[-1]
    cos, sin = jnp.cos(ang).astype(x.dtype), jnp.sin(ang).astype(x.dtype)
    x1, x2 = x[..., :m], x[..., m:]
    return jnp.concatenate([x1 * cos - x2 * sin, x1 * sin + x2 * cos], axis=-1)


def rope2d(x, ang_row, ang_col):
    half = DA_DH // 2
    return jnp.concatenate([rotate(x[..., :half], ang_row), rotate(x[..., half:], ang_col)], axis=-1)


def diff_attention(q, k, v, lam):
    b, _, h, lq, dh = q.shape
    nb = lq // Q_BLOCK
    qb = jnp.moveaxis(q.reshape(b, 2, h, nb, Q_BLOCK, dh), 3, 0)
    scale = dh ** -0.5

    def block(qi):
        s = jnp.einsum('bmhqd,bmhkd->bmhqk', qi, k).astype(jnp.float32) * scale
        a = jax.nn.softmax(s, axis=-1)
        w = a[:, 0] - lam * a[:, 1]
        return jnp.einsum('bhqk,bhke->bhqe', w.astype(v.dtype), v)

    o = lax.map(block, qb)
    return o.transpose(1, 0, 3, 2, 4).reshape(b, lq, h, v.shape[-1])


def odd_mixer(u, w_in, w_out, hy_conv_w, hy_w1, hy_b1, hy_fr1, hy_w2, hy_b2, hy_fr2, hy_w3, hy_bias,
              lam_vec, subln, lambda_init, rope_ang, ctx_k, ctx_v):
    b, length, _ = u.shape
    hy_in, q, k, v = split_cols(u @ w_in, [3 * HY_W, DA_W, DA_W, DA_W])
    hy = hyena_mixer(hy_in, hy_conv_w, hy_w1, hy_b1, hy_fr1, hy_w2, hy_b2, hy_fr2, hy_w3, hy_bias)
    q = q.reshape(b, length, DA_HEADS, 2, DA_DH).transpose(0, 3, 2, 1, 4)
    k = k.reshape(b, length, DA_HEADS, 2, DA_DH).transpose(0, 3, 2, 1, 4)
    v = v.reshape(b, length, DA_HEADS, 2 * DA_DH).transpose(0, 2, 1, 3)
    if rope_ang is not None:
        q = rope2d(q, rope_ang[0], rope_ang[1])
        k = rope2d(k, rope_ang[0], rope_ang[1])
    if ctx_k is not None:
        k_all = jnp.concatenate([k, ctx_k.astype(k.dtype)], axis=3)
        v_all = jnp.concatenate([v, ctx_v.astype(v.dtype)], axis=2)
    else:
        k_all, v_all = k, v
    lv = lam_vec.astype(jnp.float32)
    lam = jnp.exp(jnp.sum(lv[0] * lv[1])) - jnp.exp(jnp.sum(lv[2] * lv[3])) + lambda_init
    o = diff_attention(q, k_all, v_all, lam)
    o = rmsnorm(o, subln) * (1.0 - lambda_init)
    y = jnp.concatenate([hy, o.reshape(b, length, DA_W)], axis=-1) @ w_out
    return y, k, v


def hier_moe(x, w_grp, b_grp, w_exp, b_exp, w1, w3, w2):
    n, d = x.shape
    f32 = jnp.float32
    xf = x.astype(f32)
    g_logits = xf @ w_grp.astype(f32) + b_grp.astype(f32)
    g_idx = jnp.argmax(g_logits, axis=-1)
    g_w = jnp.take_along_axis(jax.nn.softmax(g_logits, axis=-1), g_idx[:, None], axis=1)[:, 0]
    e_logits = (xf @ w_exp.astype(f32) + b_exp.astype(f32)).reshape(n, N_GROUPS, EXP_PER_GROUP)
    e_sel = jnp.take_along_axis(e_logits, g_idx[:, None, None], axis=1)[:, 0]
    top_v, top_i = lax.top_k(e_sel, TOP_K)
    gate = g_w[:, None] * jax.nn.softmax(top_v, axis=-1)
    expert = g_idx[:, None] * EXP_PER_GROUP + top_i
    a = n * TOP_K
    eid = expert.reshape(a).astype(jnp.int32)
    tok = jnp.repeat(jnp.arange(n, dtype=jnp.int32), TOP_K)
    gw = gate.reshape(a)
    order = jnp.argsort(eid)
    eid_s, tok_s, gw_s = eid[order], tok[order], gw[order]
    counts = jax.ops.segment_sum(jnp.ones((a,), jnp.int32), eid, num_segments=N_EXPERTS)
    padded = (counts + MOE_BLOCK - 1) // MOE_BLOCK * MOE_BLOCK
    pad_end = jnp.cumsum(padded)
    pad_start = pad_end - padded
    start = jnp.cumsum(counts) - counts
    pos = pad_start[eid_s] + jnp.arange(a, dtype=jnp.int32) - start[eid_s]
    p = (a + N_EXPERTS * (MOE_BLOCK - 1) + MOE_BLOCK - 1) // MOE_BLOCK * MOE_BLOCK
    nb = p // MOE_BLOCK
    buf_tok = jnp.full((p,), n, jnp.int32).at[pos].set(tok_s)
    buf_w = jnp.zeros((p,), f32).at[pos].set(gw_s)
    blk_start = jnp.arange(nb, dtype=jnp.int32) * MOE_BLOCK
    blk_exp = jnp.minimum(jnp.sum(pad_end[None, :] <= blk_start[:, None], axis=1), N_EXPERTS - 1)
    x_pad = jnp.concatenate([x, jnp.zeros((1, d), x.dtype)], axis=0)
    xb = x_pad[buf_tok].reshape(nb, MOE_BLOCK, d)

    def expert_block(args):
        xi, e = args
        hdn = jax.nn.silu(xi @ w1[e]) * (xi @ w3[e])
        return hdn @ w2[e]

    yb = lax.map(expert_block, (xb, blk_exp)).reshape(p, d)
    y = jax.ops.segment_sum(yb * buf_w[:, None].astype(yb.dtype), buf_tok, num_segments=n + 1)
    return y[:n]


def setup_inputs(seed: int = 0) -> dict:
    key = jax.random.key(seed)
    keys = jax.random.split(key, 40)
    f32 = jnp.float32

    def nrm(i, shape, scale):
        return jax.random.normal(keys[i], shape, f32) * scale

    base_decay = -(5.0 + jnp.arange(RET_HEADS, dtype=f32)) * math.log(2.0)
    return {
        'x_prompt': nrm(0, (BATCH, SEQ, D_MODEL), 1.0),
        'x_sample': nrm(1, (DEC_BATCH, DEC_SEQ, D_MODEL), 1.0),
        'c': nrm(2, (DEC_BATCH, D_MODEL), 1.0),
        'c_ctx': nrm(3, (D_MODEL,), 1.0),
        'state_ret': nrm(4, (DEC_BATCH, N_EVEN, 2, RET_HEADS, RET_DK, RET_DV), 0.5),
        'cache_k': nrm(5, (DEC_BATCH, N_ODD, 2, DA_HEADS, PAST_LEN, DA_DH), 1.0),
        'cache_v': nrm(6, (DEC_BATCH, N_ODD, DA_HEADS, PAST_LEN, 2 * DA_DH), 1.0),
        'ada_w': nrm(7, (DEPTH, D_MODEL, N_MOD * D_MODEL), 0.5 * D_MODEL ** -0.5),
        'ada_b': nrm(8, (DEPTH, N_MOD * D_MODEL), 0.01),
        'norm1': 1.0 + nrm(9, (DEPTH, D_MODEL), 0.01),
        'norm2': 1.0 + nrm(10, (DEPTH, D_MODEL), 0.01),
        'final_norm': 1.0 + nrm(11, (D_MODEL,), 0.01),
        'ev_w_in': nrm(12, (N_EVEN, D_MODEL, EVEN_IN), D_MODEL ** -0.5),
        'ev_w_out': nrm(13, (N_EVEN, EVEN_OUT, D_MODEL), EVEN_OUT ** -0.5),
        'ret_decay_fwd': base_decay[None, :] + nrm(14, (N_EVEN, RET_HEADS), 0.05),
        'ret_decay_bwd': base_decay[None, :] + nrm(15, (N_EVEN, RET_HEADS), 0.05),
        'sc_conv_w': nrm(16, (N_EVEN, 3, SC_W), 3.0 ** -0.5),
        'od_w_in': nrm(17, (N_ODD, D_MODEL, ODD_IN), D_MODEL ** -0.5),
        'od_w_out': nrm(18, (N_ODD, ODD_OUT, D_MODEL), ODD_OUT ** -0.5),
        'hy_conv_w': nrm(19, (N_ODD, 3, 3 * HY_W), 3.0 ** -0.5),
        'hy_w1': nrm(20, (N_ODD, HY_EMB, HY_FFN), HY_EMB ** -0.5),
        'hy_b1': nrm(21, (N_ODD, HY_FFN), 0.1),
        'hy_fr1': 1.0 + nrm(22, (N_ODD, HY_FFN), 0.05),
        'hy_w2': nrm(23, (N_ODD, HY_FFN, HY_FFN), HY_FFN ** -0.5),
        'hy_b2': nrm(24, (N_ODD, HY_FFN), 0.1),
        'hy_fr2': 1.0 + nrm(25, (N_ODD, HY_FFN), 0.05),
        'hy_w3': nrm(26, (N_ODD, HY_FFN, HY_ORDER * 2 * HY_W), HY_FFN ** -0.5),
        'hy_bias': nrm(27, (N_ODD, HY_ORDER, HY_W), 1.0),
        'da_lambda': nrm(28, (N_ODD, 4, DA_DH), 0.1),
        'da_subln': 1.0 + nrm(29, (N_ODD, 2 * DA_DH), 0.01),
        'moe_w_grp': nrm(30, (DEPTH, D_MODEL, N_GROUPS), D_MODEL ** -0.5),
        'moe_b_grp': nrm(31, (DEPTH, N_GROUPS), 0.01),
        'moe_w_exp': nrm(32, (DEPTH, D_MODEL, N_EXPERTS), D_MODEL ** -0.5),
        'moe_b_exp': nrm(33, (DEPTH, N_EXPERTS), 0.01),
        'moe_w1': nrm(34, (DEPTH, N_EXPERTS, D_MODEL, D_EXPERT), D_MODEL ** -0.5),
        'moe_w3': nrm(35, (DEPTH, N_EXPERTS, D_MODEL, D_EXPERT), D_MODEL ** -0.5),
        'moe_w2': nrm(36, (DEPTH, N_EXPERTS, D_EXPERT, D_MODEL), D_EXPERT ** -0.5),
    }


def reference(x_prompt, x_sample, c, c_ctx, state_ret, cache_k, cache_v,
              ada_w, ada_b, norm1, norm2, final_norm,
              ev_w_in, ev_w_out, ret_decay_fwd, ret_decay_bwd, sc_conv_w,
              od_w_in, od_w_out, hy_conv_w, hy_w1, hy_b1, hy_fr1, hy_w2, hy_b2, hy_fr2, hy_w3, hy_bias,
              da_lambda, da_subln,
              moe_w_grp, moe_b_grp, moe_w_exp, moe_b_exp, moe_w1, moe_w3, moe_w2):

    def stream(x, cvec, s_in, k_in, v_in, rope_ang):
        ret_states, keys, vals = [], [], []
        for l in range(DEPTH):
            mods = jnp.split(jax.nn.silu(cvec) @ ada_w[l] + ada_b[l], N_MOD, axis=-1)
            sh1, sc1, g1, sh2, sc2, g2 = [m[..., None, :] for m in mods]
            u = rmsnorm(x, norm1[l]) * (1.0 + sc1) + sh1
            i = l // 2
            if l % 2 == 0:
                if s_in is None:
                    s0 = jnp.zeros((x.shape[0], 2, RET_HEADS, RET_DK, RET_DV), jnp.float32)
                else:
                    s0 = s_in[:, i]
                mix, s_out = even_mixer(u, ev_w_in[i], ev_w_out[i], ret_decay_fwd[i], ret_decay_bwd[i],
                                        sc_conv_w[i], s0)
                ret_states.append(s_out)
            else:
                lambda_init = 0.8 - 0.6 * math.exp(-0.3 * l)
                mix, k_c, v_c = odd_mixer(u, od_w_in[i], od_w_out[i], hy_conv_w[i], hy_w1[i], hy_b1[i], hy_fr1[i],
                                          hy_w2[i], hy_b2[i], hy_fr2[i], hy_w3[i], hy_bias[i],
                                          da_lambda[i], da_subln[i], lambda_init, rope_ang,
                                          None if k_in is None else k_in[:, i],
                                          None if v_in is None else v_in[:, i])
                keys.append(k_c)
                vals.append(v_c)
            x = x + g1 * mix
            u = rmsnorm(x, norm2[l]) * (1.0 + sc2) + sh2
            ff = hier_moe(u.reshape(-1, D_MODEL), moe_w_grp[l], moe_b_grp[l], moe_w_exp[l], moe_b_exp[l],
                          moe_w1[l], moe_w3[l], moe_w2[l])
            x = x + g2 * ff.reshape(x.shape)
        return rmsnorm(x, final_norm), ret_states, keys, vals

    y_prompt, ret_list, k_list, v_list = stream(x_prompt, c_ctx, None, None, None, None)
    y_sample, _, _, _ = stream(x_sample, c, state_ret, cache_k, cache_v, axial_rope_angles(x_sample.shape[1]))
    new_state_ret = jnp.stack(ret_list, axis=1)
    new_cache_k = jnp.stack(k_list, axis=1)
    new_cache_v = jnp.stack(v_list, axis=1)
    return (y_prompt, y_sample, new_state_ret, new_cache_k, new_cache_v)
```

```python
import functools
import math

import numpy as np
import jax
import jax.numpy as jnp
from jax import lax
from jax.experimental import pallas as pl
from jax.experimental.pallas import tpu as pltpu

F32 = jnp.float32
BF16 = jnp.bfloat16
I32 = jnp.int32
HIGHEST = lax.Precision.HIGHEST

D_MODEL = 1024
BATCH = 32
SEQ = 256
DEPTH = 2
DEC_BATCH = 8
DEC_SEQ = 1024
PAST_LEN = 256
GRID_W = 64
N_MOD = 6
EPS = 1e-6
GN_EPS = 1e-5

RET_DK = 128
RET_DV = 128
RET_HEADS = 4
RET_W = 512
RET_QK = 512
SC_W = 512
EVEN_IN = 2 * RET_QK + 2 * RET_W + 3 * SC_W

HY_W = 512
HY_ORDER = 2
HY_EMB = 33
HY_FFN = 64
HY_TARGET = 1e-2
HY_FAST_PCT = 0.3
HY_SLOW_PCT = 1.5
DA_DH = 64
DA_HEADS = 4
DA_W = 512
ODD_IN = 3 * HY_W + 3 * DA_W
ROPE_THETA = 10000.0

N_GROUPS = 4
EXP_PER_GROUP = 8
N_EXPERTS = 32
TOP_K = 2
D_EXPERT = 512

N_SAMPLE = DEC_BATCH * DEC_SEQ
N_PROMPT = BATCH * SEQ
N_TOK = N_SAMPLE + N_PROMPT
SEG = 1024
N_SEG = N_TOK // SEG
MOD_ROWS = 8

LANES = 128
SUBLANES = 8
VMEM_LIMIT = 56 * 1024 * 1024

TM = 256
MOE_BLK = 256
MOE_ROWS = (N_TOK * TOP_K + N_EXPERTS * (MOE_BLK - 1) + MOE_BLK - 1) // MOE_BLK * MOE_BLK
MOE_NB = MOE_ROWS // MOE_BLK
HY_SLAB = 256
ATT_TQ = 256


def _cparams(sem, vmem=VMEM_LIMIT):
    return pltpu.CompilerParams(dimension_semantics=sem, vmem_limit_bytes=vmem)


def _silu(x):
    return x * (1.0 / (1.0 + jnp.exp(-x)))


MODS_TN = 1536


def _mods_kernel(c_ref, w_ref, b_ref, o_ref):
    s = _silu(c_ref[...])
    o_ref[0] = jnp.dot(s, w_ref[0], preferred_element_type=F32, precision=HIGHEST) + b_ref[0]


def _mods(cvecs, ada_w, ada_b):
    n = N_MOD * D_MODEL
    return pl.pallas_call(
        _mods_kernel,
        grid=(DEPTH, n // MODS_TN),
        in_specs=[
            pl.BlockSpec((16, D_MODEL), lambda l, j: (0, 0)),
            pl.BlockSpec((1, D_MODEL, MODS_TN), lambda l, j: (l, 0, j)),
            pl.BlockSpec((1, 1, MODS_TN), lambda l, j: (l, 0, j)),
        ],
        out_specs=pl.BlockSpec((1, 16, MODS_TN), lambda l, j: (l, 0, j)),
        out_shape=jax.ShapeDtypeStruct((DEPTH, 16, n), F32),
        compiler_params=_cparams(("parallel", "parallel")),
        name="adaln_mods",
    )(cvecs, ada_w, ada_b.reshape(DEPTH, 1, n))


def _norm_mod(x, g, mod, shift_idx, scale_idx):
    ms = jnp.mean(x * x, axis=-1, keepdims=True)
    y = x * lax.rsqrt(ms + EPS) * g
    return y * (1.0 + mod[scale_idx:scale_idx + 1]) + mod[shift_idx:shift_idx + 1]


def _nmm_kernel(x_ref, g_ref, mod_ref, w_ref, o_ref, *, shift_idx, scale_idx):
    u = _norm_mod(x_ref[...], g_ref[...], mod_ref[0], shift_idx, scale_idx)
    o_ref[...] = jnp.dot(u.astype(BF16), w_ref[...], preferred_element_type=F32)


def _norm_mod_matmul(x, g, mods, w_bf16, shift_idx, scale_idx):
    n = w_bf16.shape[1]
    return pl.pallas_call(
        functools.partial(_nmm_kernel, shift_idx=shift_idx, scale_idx=scale_idx),
        grid=(N_TOK // TM,),
        in_specs=[
            pl.BlockSpec((TM, D_MODEL), lambda i: (i, 0)),
            pl.BlockSpec((1, D_MODEL), lambda i: (0, 0)),
            pl.BlockSpec((1, MOD_ROWS, D_MODEL), lambda i: (i * TM // SEG, 0, 0)),
            pl.BlockSpec((D_MODEL, n), lambda i: (0, 0)),
        ],
        out_specs=pl.BlockSpec((TM, n), lambda i: (i, 0)),
        out_shape=jax.ShapeDtypeStruct((N_TOK, n), F32),
        compiler_params=_cparams(("parallel",)),
        name="norm_mod_inproj",
    )(x, g.reshape(1, D_MODEL), mods, w_bf16)


def _proj_res_kernel(x_ref, al_ref, ac_ref, bl_ref, bc_ref, wa_ref, wb_ref, mod_ref, o_ref, *, gate_idx):
    latent = pl.program_id(0) < N_SAMPLE // TM
    a = jnp.where(latent, al_ref[...], ac_ref[...])
    b = jnp.where(latent, bl_ref[...], bc_ref[...])
    y = jnp.dot(a.astype(BF16), wa_ref[...], preferred_element_type=F32)
    y = y + jnp.dot(b.astype(BF16), wb_ref[...], preferred_element_type=F32)
    o_ref[...] = x_ref[...] + mod_ref[0][gate_idx:gate_idx + 1] * y


def _proj_residual(x, a, b, w_bf16, mods, gate_idx):
    half = a[0].shape[1]
    n_lat = N_SAMPLE // TM
    lat = pl.BlockSpec((TM, half), lambda i: (jnp.minimum(i, n_lat - 1), 0))
    ctx = pl.BlockSpec((TM, half), lambda i: (jnp.maximum(i - n_lat, 0), 0))
    return pl.pallas_call(
        functools.partial(_proj_res_kernel, gate_idx=gate_idx),
        grid=(N_TOK // TM,),
        in_specs=[
            pl.BlockSpec((TM, D_MODEL), lambda i: (i, 0)),
            lat, ctx, lat, ctx,
            pl.BlockSpec((half, D_MODEL), lambda i: (0, 0)),
            pl.BlockSpec((half, D_MODEL), lambda i: (1, 0)),
            pl.BlockSpec((1, MOD_ROWS, D_MODEL), lambda i: (i * TM // SEG, 0, 0)),
        ],
        out_specs=pl.BlockSpec((TM, D_MODEL), lambda i: (i, 0)),
        out_shape=jax.ShapeDtypeStruct((N_TOK, D_MODEL), F32),
        compiler_params=_cparams(("parallel",)),
        name="outproj_residual",
    )(x, a[0], a[1], b[0], b[1], w_bf16, w_bf16, mods)


def _conv3_rows(z, w):
    n = z.shape[0]
    row = lax.broadcasted_iota(I32, z.shape, 0)
    zm = jnp.where(row == 0, 0.0, pltpu.roll(z, 1, 0))
    zp = jnp.where(row == n - 1, 0.0, pltpu.roll(z, n - 1, 0))
    return zm * w[0:1] + z * w[1:2] + zp * w[2:3]


def _log_gamma(dec):
    return jnp.log1p(-jnp.exp(dec))


def _even_kernel(*refs, seq, latent, tq):
    (q_ref, k_ref, v_ref, g_ref, gb_ref, gc_ref, hx_ref, scw_ref, df_ref, db_ref) = refs[:10]
    if latent:
        s0_ref, ret_ref, sc_ref, mask_ref = refs[10:]
    else:
        ret_ref, sc_ref, st_ref, mask_ref = refs[10:]
    lgf = _log_gamma(df_ref[0])
    lgb = _log_gamma(db_ref[0])

    @pl.when(pl.program_id(1) == 0)
    def _():
        for r in range(seq // tq):
            t = lax.broadcasted_iota(I32, (tq, seq), 0) + r * tq
            s = lax.broadcasted_iota(I32, (tq, seq), 1)
            d = (t - s).astype(F32)
            mf = jnp.where(d >= 0, jnp.exp(jnp.maximum(d, 0.0) * lgf), 0.0)
            mb = jnp.where(d <= 0, jnp.exp(jnp.maximum(-d, 0.0) * lgb), 0.0)
            mask_ref[r * tq:(r + 1) * tq, :] = mf + mb

    k = k_ref[...] * (RET_DK ** -0.5)
    kb = k.astype(BF16)
    vb = v_ref[...].astype(BF16)
    if latent:
        s0f = s0_ref[0, 0, 0, 0].astype(BF16)
        s0b = s0_ref[0, 0, 1, 0].astype(BF16)
    for r in range(seq // tq):
        rows = slice(r * tq, (r + 1) * tq)
        qb = q_ref[rows, :].astype(BF16)
        s = lax.dot_general(qb, kb, (((1,), (1,)), ((), ())), preferred_element_type=F32)
        p = (s * mask_ref[rows, :]).astype(BF16)
        o = jnp.dot(p, vb, preferred_element_type=F32)
        if latent:
            tpos = (lax.broadcasted_iota(I32, (tq, 1), 0) + r * tq).astype(F32)
            o = o + jnp.dot(qb, s0f, preferred_element_type=F32) * jnp.exp((tpos + 1.0) * lgf)
            o = o + jnp.dot(qb, s0b, preferred_element_type=F32) * jnp.exp((seq - tpos) * lgb)
        mu = jnp.mean(o, axis=-1, keepdims=True)
        var = jnp.mean(jnp.square(o - mu), axis=-1, keepdims=True)
        on = (o - mu) * lax.rsqrt(var + GN_EPS)
        ret_ref[rows, :] = _silu(g_ref[rows, :]) * on

    if not latent:
        spos = lax.broadcasted_iota(I32, (seq, 1), 0).astype(F32)
        kf = (k * jnp.exp((seq - 1.0 - spos) * lgf)).T.astype(BF16)
        kr = (k * jnp.exp(spos * lgb)).T.astype(BF16)
        st_ref[0, 0, 0, 0] = jnp.dot(kf, vb, preferred_element_type=F32)
        st_ref[0, 0, 1, 0] = jnp.dot(kr, vb, preferred_element_type=F32)

    sc_ref[...] = gb_ref[...] * _conv3_rows(gc_ref[...] * hx_ref[...], scw_ref[...])


def _even_mixer(proj, sc_w_pad, dec_f, dec_b, state=None):
    latent = state is not None
    seq = DEC_SEQ if latent else SEQ
    nb = DEC_BATCH if latent else BATCH
    row0 = 0 if latent else N_SAMPLE // seq
    tq = min(seq, 256)
    hw = RET_DK

    def col(c0):
        return pl.BlockSpec((seq, hw), lambda h, b: (row0 + b, c0 + h))

    in_specs = [col(0), col(4), col(8), col(12), col(16), col(20), col(24),
                pl.BlockSpec((SUBLANES, hw), lambda h, b: (0, h)),
                pl.BlockSpec((1, 1, 1), lambda h, b: (h, 0, 0)),
                pl.BlockSpec((1, 1, 1), lambda h, b: (h, 0, 0))]
    args = [proj] * 7 + [sc_w_pad, dec_f.reshape(RET_HEADS, 1, 1), dec_b.reshape(RET_HEADS, 1, 1)]
    out_spec = pl.BlockSpec((seq, hw), lambda h, b: (b, h))
    out_shape = [jax.ShapeDtypeStruct((nb * seq, RET_W), F32), jax.ShapeDtypeStruct((nb * seq, SC_W), F32)]
    out_specs = [out_spec, out_spec]
    st_spec = pl.BlockSpec((1, 1, 2, 1, RET_DK, RET_DV), lambda h, b: (b, 0, 0, h, 0, 0))
    if latent:
        in_specs.append(st_spec)
        args.append(state)
    else:
        out_shape.append(jax.ShapeDtypeStruct((BATCH, 1, 2, RET_HEADS, RET_DK, RET_DV), F32))
        out_specs.append(st_spec)
    return pl.pallas_call(
        functools.partial(_even_kernel, seq=seq, latent=latent, tq=tq),
        grid=(RET_HEADS, nb),
        in_specs=in_specs,
        out_specs=out_specs,
        out_shape=out_shape,
        scratch_shapes=[pltpu.VMEM((seq, seq), F32)],
        compiler_params=_cparams(("arbitrary", "arbitrary")),
        name="even_mixer_latent" if latent else "even_mixer_context",
    )(*args)


def _dft_tables(length):
    f = np.arange(length, dtype=np.float64)[:, None]
    t = np.arange(length, dtype=np.float64)[None, :]
    ang = np.pi * ((f * t) % (2 * length)) / length
    cos, sin = np.cos(ang), np.sin(ang)
    sin[0, :] = (-1.0) ** np.arange(length)
    fwd = np.concatenate([cos, sin], axis=0)
    wgt = np.full((2 * length,), 2.0)
    wgt[0] = 1.0
    wgt[length] = 1.0
    inv = (fwd * wgt[:, None] / (2.0 * length)).T
    return fwd.astype(np.float32), np.ascontiguousarray(inv).astype(np.float32)


def _hyena_consts(length):
    t = np.linspace(0.0, 1.0, length)[:, None]
    bands = (HY_EMB - 1) // 2
    w = 2.0 * np.pi * np.arange(length)[:, None] / length
    f = np.linspace(1e-4, bands - 1, bands)[None, :]
    z = np.concatenate([t, np.cos(f * w), -np.sin(f * w)], axis=-1)
    zpad = np.zeros((length, LANES))
    zpad[:, :HY_EMB] = z
    deltas = np.abs(np.linspace(math.log(HY_TARGET) / HY_SLOW_PCT, math.log(HY_TARGET) / HY_FAST_PCT, HY_W))
    decay = np.exp(-t * deltas[None, :])
    return zpad.astype(np.float32), decay.astype(np.float32)


def _cast_kernel(x_ref, o_ref):
    o_ref[...] = x_ref[...].astype(o_ref.dtype)


def _cast_bf16(x, name):
    rows = min(x.shape[0], 512)
    return pl.pallas_call(
        _cast_kernel,
        grid=(x.shape[0] // rows,),
        in_specs=[pl.BlockSpec((rows, x.shape[1]), lambda i: (i, 0))],
        out_specs=pl.BlockSpec((rows, x.shape[1]), lambda i: (i, 0)),
        out_shape=jax.ShapeDtypeStruct(x.shape, BF16),
        compiler_params=_cparams(("parallel",)),
        name=name,
    )(x)


def _hy_filter_kernel(z_ref, dec_ref, w1_ref, b1_ref, f1_ref, w2_ref, b2_ref, f2_ref, w3f_ref, w3b_ref,
                      fwd_ref, ka_ref, ka2_ref, kb_ref, *, seq):
    h = jnp.sin(f1_ref[...] * (jnp.dot(z_ref[...], w1_ref[...], preferred_element_type=F32,
                                       precision=HIGHEST) + b1_ref[...]))
    h = jnp.sin(f2_ref[...] * (jnp.dot(h, w2_ref[...], preferred_element_type=F32,
                                       precision=HIGHEST) + b2_ref[...]))
    dec = dec_ref[...]
    hf = jnp.dot(h, w3f_ref[...], preferred_element_type=F32, precision=HIGHEST) * dec
    hb = jnp.dot(h, w3b_ref[...], preferred_element_type=F32, precision=HIGHEST) * dec
    row = lax.broadcasted_iota(I32, hb.shape, 0)
    hb = jnp.where(row == 0, 0.0, hb)
    inv_norm = 1.0 / (jnp.sum(jnp.abs(hf), axis=0, keepdims=True)
                      + jnp.sum(jnp.abs(hb), axis=0, keepdims=True) + EPS)
    p = (hf + hb) * inv_norm
    q = (hf - hb) * inv_norm
    fp = jnp.dot(fwd_ref[...], p, preferred_element_type=F32, precision=HIGHEST)
    fq = jnp.dot(fwd_ref[seq:, :], q, preferred_element_type=F32, precision=HIGHEST)
    kr = fp[:seq]
    row = lax.broadcasted_iota(I32, kr.shape, 0)
    ka_ref[0] = kr
    ka2_ref[0] = jnp.where(row == 0, fp[seq:seq + 1], kr)
    kb_ref[0] = jnp.where(row == 0, 0.0, -fq)


def _hy_filters(seq, zfeat, decay, fwd, w1, b1, f1, w2, b2, f2, w3):
    ns = HY_W // HY_SLAB
    per_o = 2 * ns
    small = lambda shape: pl.BlockSpec(shape, lambda o, s: (0, 0))
    out_spec = pl.BlockSpec((1, seq, HY_SLAB), lambda o, s: (o, 0, s))
    out = jax.ShapeDtypeStruct((HY_ORDER, seq, HY_W), F32)
    return pl.pallas_call(
        functools.partial(_hy_filter_kernel, seq=seq),
        grid=(HY_ORDER, ns),
        in_specs=[
            small((seq, LANES)),
            pl.BlockSpec((seq, HY_SLAB), lambda o, s: (0, s)),
            small((LANES, LANES)), small((1, LANES)), small((1, LANES)),
            small((LANES, LANES)), small((1, LANES)), small((1, LANES)),
            pl.BlockSpec((LANES, HY_SLAB), lambda o, s: (0, o * per_o + s)),
            pl.BlockSpec((LANES, HY_SLAB), lambda o, s: (0, o * per_o + ns + s)),
            small((2 * seq, seq)),
        ],
        out_specs=[out_spec, out_spec, out_spec],
        out_shape=[out, out, out],
        compiler_params=_cparams(("parallel", "parallel")),
        name=f"hyena_filters_{seq}",
    )(zfeat, decay, w1, b1, f1, w2, b2, f2, w3, w3, fwd)


def _hy_apply_kernel(x1_ref, x2_ref, v_ref, c1_ref, c2_ref, cv_ref, fwd_ref, inv_ref,
                     ka_ref, ka2_ref, kb_ref, bias_ref, o_ref, *, seq):
    x1 = _conv3_rows(x1_ref[...], c1_ref[...])
    x2 = _conv3_rows(x2_ref[...], c2_ref[...])
    v = _conv3_rows(v_ref[...], cv_ref[...])

    def long_conv(sig, o):
        spec = jnp.dot(fwd_ref[...], sig.astype(BF16), preferred_element_type=F32)
        xr, xs = spec[:seq], spec[seq:]
        kb = kb_ref[o]
        yr = (xr * ka_ref[o] + xs * kb).astype(BF16)
        ys = (xs * ka2_ref[o] - xr * kb).astype(BF16)
        y = jnp.dot(inv_ref[:, :seq], yr, preferred_element_type=F32)
        y = y + jnp.dot(inv_ref[:, seq:], ys, preferred_element_type=F32)
        return y + sig * bias_ref[o:o + 1]

    z = x1 * long_conv(v, 0)
    o_ref[...] = x2 * long_conv(z, 1)


def _hy_apply(proj, conv_w_pad, fwd_bf, inv_bf, filters, bias_pad, latent):
    seq = DEC_SEQ if latent else SEQ
    nb = DEC_BATCH if latent else BATCH
    row0 = 0 if latent else N_SAMPLE // seq
    ns = HY_W // HY_SLAB
    const = lambda shape: pl.BlockSpec(shape, lambda s, b: (0,) * len(shape))
    col = lambda c0: pl.BlockSpec((seq, HY_SLAB), lambda s, b: (row0 + b, c0 + s))
    cw = lambda c0: pl.BlockSpec((SUBLANES, HY_SLAB), lambda s, b: (0, c0 + s))
    filt = pl.BlockSpec((HY_ORDER, seq, HY_SLAB), lambda s, b: (0, 0, s))
    in_specs = [col(0), col(ns), col(2 * ns), cw(0), cw(ns), cw(2 * ns),
                const((2 * seq, seq)), const((seq, 2 * seq)), filt, filt, filt,
                pl.BlockSpec((SUBLANES, HY_SLAB), lambda s, b: (0, s))]
    args = [proj] * 3 + [conv_w_pad] * 3 + [fwd_bf, inv_bf, *filters, bias_pad]
    return pl.pallas_call(
        functools.partial(_hy_apply_kernel, seq=seq),
        grid=(ns, nb),
        in_specs=in_specs,
        out_specs=pl.BlockSpec((seq, HY_SLAB), lambda s, b: (b, s)),
        out_shape=jax.ShapeDtypeStruct((nb * seq, HY_W), F32),
        compiler_params=_cparams(("arbitrary", "arbitrary")),
        name="hyena_latent" if latent else "hyena_context",
    )(*args)


def _rope_tables(length):
    rows = length // GRID_W
    row = np.repeat(np.arange(rows, dtype=np.float64), GRID_W)
    colp = np.tile(np.arange(GRID_W, dtype=np.float64), rows)
    half = DA_DH // 2
    freqs = ROPE_THETA ** (-np.arange(0, half, 2, dtype=np.float64) / half)
    ang = np.concatenate([row[:, None] * freqs[None, :]] * 2 + [colp[:, None] * freqs[None, :]] * 2, axis=1)
    ang = np.concatenate([ang, ang], axis=1)
    lane = np.arange(LANES)
    sign = np.where((lane % half) < half // 2, -1.0, 1.0)[None, :]
    return np.cos(ang).astype(np.float32), (np.sin(ang) * sign).astype(np.float32)


def _rope(x, cos, sin_signed):
    q = DA_DH // 4
    lane = lax.broadcasted_iota(I32, x.shape, 1)
    partner = jnp.where((lane % (2 * q)) < q, pltpu.roll(x, LANES - q, 1), pltpu.roll(x, q, 1))
    return x * cos + partner * sin_signed


def _attn_kernel(*refs, seq, latent, lambda_init):
    q_ref, k_ref, v_ref, lam_ref, sub_ref = refs[:5]
    if latent:
        cos_ref, sin_ref, ck_ref, cv_ref, o_ref = refs[5:]
    else:
        o_ref, nk_ref, nv_ref = refs[5:]
    k = k_ref[...]
    v = v_ref[...]
    if latent:
        k = _rope(k, cos_ref[...], sin_ref[...])
        k_all = jnp.concatenate([k, jnp.concatenate([ck_ref[0, 0, 0, 0], ck_ref[0, 0, 1, 0]], axis=1)], axis=0)
        v_all = jnp.concatenate([v, cv_ref[0, 0, 0]], axis=0)
    else:
        k_all, v_all = k, v
        nk_ref[0, 0, 0, 0] = k[:, :DA_DH]
        nk_ref[0, 0, 1, 0] = k[:, DA_DH:]
        nv_ref[0, 0, 0] = v
    kb = k_all.astype(BF16)
    vb = v_all.astype(BF16)
    lv = lam_ref[...]
    lam = (jnp.exp(jnp.sum(lv[0:1] * lv[1:2], axis=1, keepdims=True))
           - jnp.exp(jnp.sum(lv[2:3] * lv[3:4], axis=1, keepdims=True)) + lambda_init)
    scale = DA_DH ** -0.5
    dn = (((1,), (1,)), ((), ()))
    tq = min(seq, ATT_TQ)
    for r in range(seq // tq):
        rows = slice(r * tq, (r + 1) * tq)
        q = q_ref[rows, :]
        if latent:
            q = _rope(q, cos_ref[rows, :], sin_ref[rows, :])
        first = lax.broadcasted_iota(I32, q.shape, 1) < DA_DH
        q1 = jnp.where(first, q, 0.0).astype(BF16)
        q2 = jnp.where(first, 0.0, q).astype(BF16)
        s1 = lax.dot_general(q1, kb, dn, preferred_element_type=F32) * scale
        s2 = lax.dot_general(q2, kb, dn, preferred_element_type=F32) * scale
        e1 = jnp.exp(s1 - jnp.max(s1, axis=-1, keepdims=True))
        e2 = jnp.exp(s2 - jnp.max(s2, axis=-1, keepdims=True))
        w = (e1 * (1.0 / jnp.sum(e1, axis=-1, keepdims=True))
             - e2 * (lam / jnp.sum(e2, axis=-1, keepdims=True)))
        o = jnp.dot(w.astype(BF16), vb, preferred_element_type=F32)
        o = o * lax.rsqrt(jnp.mean(o * o, axis=-1, keepdims=True) + EPS) * sub_ref[...]
        o_ref[rows, :] = o * (1.0 - lambda_init)


def _attention(proj, lam_vec, subln, lambda_init, latent, rope=None, ctx_k=None, ctx_v=None):
    seq = DEC_SEQ if latent else SEQ
    nb = DEC_BATCH if latent else BATCH
    row0 = 0 if latent else N_SAMPLE // seq
    hw = 2 * DA_DH
    c0 = 3 * HY_W // hw
    col = lambda c: pl.BlockSpec((seq, hw), lambda b, h: (row0 + b, c + h))
    in_specs = [col(c0), col(c0 + DA_HEADS), col(c0 + 2 * DA_HEADS),
                pl.BlockSpec((4, DA_DH), lambda b, h: (0, 0)),
                pl.BlockSpec((1, hw), lambda b, h: (0, 0))]
    args = [proj] * 3 + [lam_vec, subln.reshape(1, hw)]
    out_spec = pl.BlockSpec((seq, hw), lambda b, h: (b, h))
    out_shape = [jax.ShapeDtypeStruct((nb * seq, DA_W), F32)]
    out_specs = [out_spec]
    k_spec = pl.BlockSpec((1, 1, 2, 1, SEQ, DA_DH), lambda b, h: (b, 0, 0, h, 0, 0))
    v_spec = pl.BlockSpec((1, 1, 1, SEQ, hw), lambda b, h: (b, 0, h, 0, 0))
    if latent:
        in_specs += [pl.BlockSpec((seq, hw), lambda b, h: (0, 0))] * 2 + [k_spec, v_spec]
        args += [rope[0], rope[1], ctx_k, ctx_v]
    else:
        out_shape += [jax.ShapeDtypeStruct((BATCH, 1, 2, DA_HEADS, SEQ, DA_DH), F32),
                      jax.ShapeDtypeStruct((BATCH, 1, DA_HEADS, SEQ, hw), F32)]
        out_specs += [k_spec, v_spec]
    return pl.pallas_call(
        functools.partial(_attn_kernel, seq=seq, latent=latent, lambda_init=lambda_init),
        grid=(nb, DA_HEADS),
        in_specs=in_specs,
        out_specs=out_specs,
        out_shape=out_shape,
        compiler_params=_cparams(("arbitrary", "arbitrary")),
        name="diff_attn_latent" if latent else "diff_attn_context",
    )(*args)


def _router_kernel(x_ref, g_ref, mod_ref, wr_ref, br_ref, tril_ref,
                   u_ref, idx_ref, gate_ref, cnt_ref, carry_ref):
    @pl.when(pl.program_id(0) == 0)
    def _():
        carry_ref[...] = jnp.zeros_like(carry_ref)

    u = _norm_mod(x_ref[...], g_ref[...], mod_ref[0], 3, 4)
    u_ref[...] = u
    logits = jnp.dot(u, wr_ref[...], preferred_element_type=F32, precision=HIGHEST) + br_ref[...]
    lane = lax.broadcasted_iota(I32, logits.shape, 1)
    lane_f = lane.astype(F32)
    neg = -jnp.inf
    big = float(LANES)

    def first_max(vals):
        m = jnp.max(vals, axis=-1, keepdims=True)
        return m, jnp.min(jnp.where(vals == m, lane_f, big), axis=-1, keepdims=True)

    is_grp = lane < N_GROUPS
    gmax, gidx = first_max(jnp.where(is_grp, logits, neg))
    gsum = jnp.sum(jnp.where(is_grp, jnp.exp(logits - gmax), 0.0), axis=-1, keepdims=True)
    g_w = 1.0 / gsum
    lo = N_GROUPS + EXP_PER_GROUP * gidx
    in_grp = jnp.logical_and(lane_f >= lo, lane_f < lo + EXP_PER_GROUP)
    el = jnp.where(in_grp, logits, neg)
    v1, i1 = first_max(el)
    v2, i2 = first_max(jnp.where(lane_f == i1, neg, el))
    e = jnp.exp(v2 - v1)
    p1 = 1.0 / (1.0 + e)
    e1 = i1 - N_GROUPS
    e2 = i2 - N_GROUPS
    hit1 = lane_f == e1
    hit2 = lane_f == e2
    onehot = jnp.where(jnp.logical_or(hit1, hit2), 1.0, 0.0)
    cum = jnp.dot(tril_ref[...], onehot.astype(BF16), preferred_element_type=F32)
    before = cum - onehot + carry_ref[...]
    r1 = jnp.sum(jnp.where(hit1, before, 0.0), axis=-1, keepdims=True)
    r2 = jnp.sum(jnp.where(hit2, before, 0.0), axis=-1, keepdims=True)
    carry_ref[...] = carry_ref[...] + cum[TM - 1:TM, :]
    cnt_ref[...] = jnp.broadcast_to(carry_ref[...], cnt_ref.shape)
    info = jnp.where(lane == 0, e1, jnp.where(lane == 1, e2, jnp.where(lane == 2, r1, r2)))
    idx_ref[...] = info.astype(I32)
    gate_ref[...] = jnp.where(lane == 0, g_w * p1, g_w * (e * p1))


def _router(x, g, mods, w_route, b_route, tril):
    return pl.pallas_call(
        _router_kernel,
        grid=(N_TOK // TM,),
        in_specs=[
            pl.BlockSpec((TM, D_MODEL), lambda i: (i, 0)),
            pl.BlockSpec((1, D_MODEL), lambda i: (0, 0)),
            pl.BlockSpec((1, MOD_ROWS, D_MODEL), lambda i: (i * TM // SEG, 0, 0)),
            pl.BlockSpec((D_MODEL, LANES), lambda i: (0, 0)),
            pl.BlockSpec((1, LANES), lambda i: (0, 0)),
            pl.BlockSpec((TM, TM), lambda i: (0, 0)),
        ],
        out_specs=[
            pl.BlockSpec((TM, D_MODEL), lambda i: (i, 0)),
            pl.BlockSpec((TM, LANES), lambda i: (i, 0)),
            pl.BlockSpec((TM, LANES), lambda i: (i, 0)),
            pl.BlockSpec((SUBLANES, LANES), lambda i: (0, 0)),
        ],
        out_shape=[
            jax.ShapeDtypeStruct((N_TOK, D_MODEL), F32),
            jax.ShapeDtypeStruct((N_TOK, LANES), I32),
            jax.ShapeDtypeStruct((N_TOK, LANES), F32),
            jax.ShapeDtypeStruct((SUBLANES, LANES), F32),
        ],
        scratch_shapes=[pltpu.VMEM((1, LANES), F32)],
        compiler_params=_cparams(("arbitrary",)),
        name="moe_router",
    )(x, g.reshape(1, D_MODEL), mods, w_route, b_route, tril)


def _row_copy(src, src_row, dst, dst_row, sem):
    return pltpu.make_async_copy(src.at[pl.ds(src_row, 1)], dst.at[pl.ds(dst_row, 1)], sem)


def _dispatch_kernel(pos_ref, u_ref, xb_in, xb_ref, sem):
    del xb_in

    def copies(r):
        return (_row_copy(u_ref, r, xb_ref, pos_ref[0, 0, r], sem.at[0]),
                _row_copy(u_ref, r, xb_ref, pos_ref[0, 1, r], sem.at[1]))

    def start(r, c):
        for cp in copies(r):
            cp.start()
        return c

    def wait(r, c):
        for cp in copies(r):
            cp.wait()
        return c

    lax.fori_loop(0, TM, start, 0)
    lax.fori_loop(0, TM, wait, 0)


def _dispatch(u, pos, xb_zero):
    return pl.pallas_call(
        _dispatch_kernel,
        grid=(N_TOK // TM,),
        in_specs=[
            pl.BlockSpec((1, TOP_K, TM), lambda i: (i, 0, 0), memory_space=pltpu.SMEM),
            pl.BlockSpec((TM, D_MODEL), lambda i: (i, 0)),
            pl.BlockSpec(memory_space=pl.ANY),
        ],
        out_specs=pl.BlockSpec(memory_space=pl.ANY),
        out_shape=jax.ShapeDtypeStruct((MOE_ROWS, D_MODEL), F32),
        scratch_shapes=[pltpu.SemaphoreType.DMA((2,))],
        input_output_aliases={2: 0},
        compiler_params=_cparams(("arbitrary",)),
        name="moe_dispatch",
    )(pos, u, xb_zero)


def _expert_kernel(meta_ref, xb_ref, w1_ref, w3_ref, w2_ref, yb_ref, w1b, w3b, w2b):
    j = pl.program_id(0)
    flag = meta_ref[1, j]

    @pl.when(flag == 2)
    def _():
        w1b[...] = w1_ref[0, 0].astype(BF16)
        w3b[...] = w3_ref[0, 0].astype(BF16)
        w2b[...] = w2_ref[0, 0].astype(BF16)

    @pl.when(flag > 0)
    def _():
        x = xb_ref[...].astype(BF16)
        h1 = jnp.dot(x, w1b[...], preferred_element_type=F32)
        h3 = jnp.dot(x, w3b[...], preferred_element_type=F32)
        hdn = (_silu(h1) * h3).astype(BF16)
        yb_ref[...] = jnp.dot(hdn, w2b[...], preferred_element_type=F32)

    @pl.when(flag == 0)
    def _():
        yb_ref[...] = jnp.zeros_like(yb_ref)


def _experts(meta, xb, w1, w3, w2, layer):
    wspec_in = pl.BlockSpec((1, 1, D_MODEL, D_EXPERT), lambda j, m: (layer, m[0, j], 0, 0))
    wspec_out = pl.BlockSpec((1, 1, D_EXPERT, D_MODEL), lambda j, m: (layer, m[0, j], 0, 0))
    rows = pl.BlockSpec((MOE_BLK, D_MODEL), lambda j, m: (m[2, j], 0))
    return pl.pallas_call(
        _expert_kernel,
        grid_spec=pltpu.PrefetchScalarGridSpec(
            num_scalar_prefetch=1,
            grid=(MOE_NB,),
            in_specs=[rows, wspec_in, wspec_in, wspec_out],
            out_specs=pl.BlockSpec((MOE_BLK, D_MODEL), lambda j, m: (j, 0)),
            scratch_shapes=[pltpu.VMEM((D_MODEL, D_EXPERT), BF16), pltpu.VMEM((D_MODEL, D_EXPERT), BF16),
                            pltpu.VMEM((D_EXPERT, D_MODEL), BF16)],
        ),
        out_shape=jax.ShapeDtypeStruct((MOE_ROWS, D_MODEL), F32),
        compiler_params=_cparams(("arbitrary",)),
        name="moe_experts",
    )(meta, xb, w1, w3, w2)


def _combine_kernel(pos_ref, x_ref, gate_ref, mod_ref, yb_ref, o_ref, buf, sem):
    def copies(r):
        return (_row_copy(yb_ref, pos_ref[0, 0, r], buf.at[0], r, sem.at[0]),
                _row_copy(yb_ref, pos_ref[0, 1, r], buf.at[1], r, sem.at[1]))

    def start(r, c):
        for cp in copies(r):
            cp.start()
        return c

    def wait(r, c):
        for cp in copies(r):
            cp.wait()
        return c

    lax.fori_loop(0, TM, start, 0)
    lax.fori_loop(0, TM, wait, 0)
    gate = gate_ref[...]
    y = gate[:, 0:1] * buf[0] + gate[:, 1:2] * buf[1]
    o_ref[...] = x_ref[...] + mod_ref[0][5:6] * y


def _combine(x, pos, gates, mods, yb):
    return pl.pallas_call(
        _combine_kernel,
        grid=(N_TOK // TM,),
        in_specs=[
            pl.BlockSpec((1, TOP_K, TM), lambda i: (i, 0, 0), memory_space=pltpu.SMEM),
            pl.BlockSpec((TM, D_MODEL), lambda i: (i, 0)),
            pl.BlockSpec((TM, LANES), lambda i: (i, 0)),
            pl.BlockSpec((1, MOD_ROWS, D_MODEL), lambda i: (i * TM // SEG, 0, 0)),
            pl.BlockSpec(memory_space=pl.ANY),
        ],
        out_specs=pl.BlockSpec((TM, D_MODEL), lambda i: (i, 0)),
        out_shape=jax.ShapeDtypeStruct((N_TOK, D_MODEL), F32),
        scratch_shapes=[pltpu.VMEM((TOP_K, TM, D_MODEL), F32), pltpu.SemaphoreType.DMA((2,))],
        compiler_params=_cparams(("arbitrary",)),
        name="moe_combine",
    )(pos, x, gates, mods, yb)


def _moe_layer(x, layer, norm2, mods, w_grp, b_grp, w_exp, b_exp, w1, w3, w2, tril):
    w_route = jnp.zeros((D_MODEL, LANES), F32)
    w_route = w_route.at[:, :N_GROUPS].set(w_grp).at[:, N_GROUPS:N_GROUPS + N_EXPERTS].set(w_exp)
    b_route = jnp.zeros((1, LANES), F32)
    b_route = b_route.at[0, :N_GROUPS].set(b_grp).at[0, N_GROUPS:N_GROUPS + N_EXPERTS].set(b_exp)
    u, idx, gates, cnt = _router(x, norm2, mods, w_route, b_route, tril)

    counts = cnt[0, :N_EXPERTS].astype(I32)
    padded = (counts + MOE_BLK - 1) // MOE_BLK * MOE_BLK
    pad_end = jnp.cumsum(padded)
    pad_start = pad_end - padded
    pos = jnp.stack([pad_start[idx[:, 0]] + idx[:, 2], pad_start[idx[:, 1]] + idx[:, 3]], axis=0)
    pos = pos.reshape(TOP_K, N_TOK // TM, TM).transpose(1, 0, 2)
    blk_start = jnp.arange(MOE_NB, dtype=I32) * MOE_BLK
    blk_exp = jnp.minimum(jnp.sum(pad_end[None, :] <= blk_start[:, None], axis=1), N_EXPERTS - 1).astype(I32)
    used = pad_end[-1] // MOE_BLK
    active = jnp.arange(MOE_NB, dtype=I32) < used
    first = blk_start == pad_start[blk_exp]
    flag = jnp.where(active, jnp.where(first, 2, 1), 0).astype(I32)
    last_exp = blk_exp[jnp.maximum(used - 1, 0)]
    meta = jnp.stack([jnp.where(active, blk_exp, last_exp), flag,
                      jnp.minimum(jnp.arange(MOE_NB, dtype=I32), used - 1)], axis=0).astype(I32)

    xb = _dispatch(u, pos, jnp.zeros((MOE_ROWS, D_MODEL), F32))
    yb = _experts(meta, xb, w1, w3, w2, layer)
    return _combine(x, pos, gates, mods, yb)


def _final_kernel(x_ref, g_ref, o_ref):
    x = x_ref[...]
    o_ref[...] = x * lax.rsqrt(jnp.mean(x * x, axis=-1, keepdims=True) + EPS) * g_ref[...]


def _final_norm(x, g, row0, rows):
    return pl.pallas_call(
        _final_kernel,
        grid=(rows // TM,),
        in_specs=[pl.BlockSpec((TM, D_MODEL), lambda i: (row0 // TM + i, 0)),
                  pl.BlockSpec((1, D_MODEL), lambda i: (0, 0))],
        out_specs=pl.BlockSpec((TM, D_MODEL), lambda i: (i, 0)),
        out_shape=jax.ShapeDtypeStruct((rows, D_MODEL), F32),
        compiler_params=_cparams(("parallel",)),
        name="final_norm",
    )(x, g.reshape(1, D_MODEL))


def _pad_rows(w, rows=SUBLANES):
    return jnp.zeros((rows, w.shape[1]), w.dtype).at[:w.shape[0]].set(w)


def _pad2(w, rows, cols):
    return jnp.zeros((rows, cols), w.dtype).at[:w.shape[0], :w.shape[1]].set(w)


def kernel(x_prompt, x_sample, c, c_ctx, state_ret, cache_k, cache_v, ada_w, ada_b, norm1, norm2, final_norm, ev_w_in, ev_w_out, ret_decay_fwd, ret_decay_bwd, sc_conv_w, od_w_in, od_w_out, hy_conv_w, hy_w1, hy_b1, hy_fr1, hy_w2, hy_b2, hy_fr2, hy_w3, hy_bias, da_lambda, da_subln, moe_w_grp, moe_b_grp, moe_w_exp, moe_b_exp, moe_w1, moe_w3, moe_w2):
    x = jnp.concatenate([x_sample.reshape(N_SAMPLE, D_MODEL), x_prompt.reshape(N_PROMPT, D_MODEL)], axis=0)

    cvecs = jnp.zeros((16, D_MODEL), F32).at[:DEC_BATCH].set(c).at[DEC_BATCH].set(c_ctx)
    m = _mods(cvecs, ada_w, ada_b).reshape(DEPTH, 16, N_MOD, D_MODEL)
    seg_mods = jnp.concatenate(
        [m[:, :DEC_BATCH], jnp.broadcast_to(m[:, DEC_BATCH:DEC_BATCH + 1], (DEPTH, N_SEG - DEC_BATCH, N_MOD, D_MODEL))],
        axis=1)
    seg_mods = jnp.pad(seg_mods, ((0, 0), (0, 0), (0, MOD_ROWS - N_MOD), (0, 0)))

    tril = jnp.asarray(np.tril(np.ones((TM, TM), np.float32)), BF16)

    for l in range(DEPTH):
        mods = seg_mods[l]
        i = l // 2
        if l % 2 == 0:
            proj = _norm_mod_matmul(x, norm1[l], mods, ev_w_in[i].astype(BF16), 0, 1)
            scw = _pad_rows(sc_conv_w[i])
            ret_l, sc_l = _even_mixer(proj, scw, ret_decay_fwd[i], ret_decay_bwd[i], state=state_ret[:, i:i + 1])
            ret_c, sc_c, new_state = _even_mixer(proj, scw, ret_decay_fwd[i], ret_decay_bwd[i])
            x = _proj_residual(x, (ret_l, ret_c), (sc_l, sc_c), ev_w_out[i].astype(BF16), mods, 2)
        else:
            lambda_init = 0.8 - 0.6 * math.exp(-0.3 * l)
            proj = _norm_mod_matmul(x, norm1[l], mods, od_w_in[i].astype(BF16), 0, 1)
            conv_w = _pad_rows(hy_conv_w[i])
            bias = _pad_rows(hy_bias[i])
            w1 = _pad2(hy_w1[i], LANES, LANES)
            b1 = _pad2(hy_b1[i][None], 1, LANES)
            f1 = _pad2(hy_fr1[i][None], 1, LANES)
            w2 = _pad2(hy_w2[i], LANES, LANES)
            b2 = _pad2(hy_b2[i][None], 1, LANES)
            f2 = _pad2(hy_fr2[i][None], 1, LANES)
            w3 = _pad2(hy_w3[i], LANES, hy_w3.shape[2])
            hy = []
            for latent in (True, False):
                seq = DEC_SEQ if latent else SEQ
                fwd_np, inv_np = _dft_tables(seq)
                zfeat, decay = _hyena_consts(seq)
                fwd = jnp.asarray(fwd_np)
                filters = _hy_filters(seq, jnp.asarray(zfeat), jnp.asarray(decay), fwd, w1, b1, f1, w2, b2, f2, w3)
                fwd_bf = _cast_bf16(fwd, f"dft_fwd_bf16_{seq}")
                inv_bf = _cast_bf16(jnp.asarray(inv_np), f"dft_inv_bf16_{seq}")
                hy.append(_hy_apply(proj, conv_w, fwd_bf, inv_bf, filters, bias, latent))
            cos, sin = _rope_tables(DEC_SEQ)
            att_l = _attention(proj, da_lambda[i], da_subln[i], lambda_init, True,
                               rope=(jnp.asarray(cos), jnp.asarray(sin)),
                               ctx_k=cache_k[:, i:i + 1], ctx_v=cache_v[:, i:i + 1])[0]
            att_c, new_k, new_v = _attention(proj, da_lambda[i], da_subln[i], lambda_init, False)
            x = _proj_residual(x, hy, (att_l, att_c), od_w_out[i].astype(BF16), mods, 2)
        x = _moe_layer(x, l, norm2[l], mods, moe_w_grp[l], moe_b_grp[l], moe_w_exp[l], moe_b_exp[l],
                       moe_w1, moe_w3, moe_w2, tril)

    y_sample = _final_norm(x, final_norm, 0, N_SAMPLE).reshape(DEC_BATCH, DEC_SEQ, D_MODEL)
    y_prompt = _final_norm(x, final_norm, N_SAMPLE, N_PROMPT).reshape(BATCH, SEQ, D_MODEL)
    return (y_prompt, y_sample, new_state, new_k, new_v)
```

```python
import functools
import math

import numpy as np
import jax
import jax.numpy as jnp
from jax import lax
from jax.experimental import pallas as pl
from jax.experimental.pallas import tpu as pltpu

F32 = jnp.float32
BF16 = jnp.bfloat16
I32 = jnp.int32
HIGHEST = lax.Precision.HIGHEST

D_MODEL = 1024
BATCH = 32
SEQ = 256
DEPTH = 2
DEC_BATCH = 8
DEC_SEQ = 1024
PAST_LEN = 256
GRID_W = 64
N_MOD = 6
EPS = 1e-6
GN_EPS = 1e-5

RET_DK = 128
RET_DV = 128
RET_HEADS = 4
RET_W = 512
RET_QK = 512
SC_W = 512
EVEN_IN = 2 * RET_QK + 2 * RET_W + 3 * SC_W

HY_W = 512
HY_ORDER = 2
HY_EMB = 33
HY_FFN = 64
HY_TARGET = 1e-2
HY_FAST_PCT = 0.3
HY_SLOW_PCT = 1.5
DA_DH = 64
DA_HEADS = 4
DA_W = 512
ODD_IN = 3 * HY_W + 3 * DA_W
ROPE_THETA = 10000.0

N_GROUPS = 4
EXP_PER_GROUP = 8
N_EXPERTS = 32
TOP_K = 2
D_EXPERT = 512

N_SAMPLE = DEC_BATCH * DEC_SEQ
N_PROMPT = BATCH * SEQ
N_TOK = N_SAMPLE + N_PROMPT
SEG = 1024
N_SEG = N_TOK // SEG
MOD_ROWS = 8

LANES = 128
SUBLANES = 8
VMEM_LIMIT = 56 * 1024 * 1024

TM = 256
MOE_BLK = 256
MOE_ROWS = (N_TOK * TOP_K + N_EXPERTS * (MOE_BLK - 1) + MOE_BLK - 1) // MOE_BLK * MOE_BLK
MOE_NB = MOE_ROWS // MOE_BLK
HY_SLAB = 256
ATT_TQ = 256


def _cparams(sem, vmem=VMEM_LIMIT):
    return pltpu.CompilerParams(dimension_semantics=sem, vmem_limit_bytes=vmem)


def _silu(x):
    return x * (1.0 / (1.0 + jnp.exp(-x)))


MODS_TN = 1536


def _mods_kernel(c_ref, w_ref, b_ref, o_ref):
    s = _silu(c_ref[...])
    o_ref[0] = jnp.dot(s, w_ref[0], preferred_element_type=F32, precision=HIGHEST) + b_ref[0]


def _mods(cvecs, ada_w, ada_b):
    n = N_MOD * D_MODEL
    return pl.pallas_call(
        _mods_kernel,
        grid=(DEPTH, n // MODS_TN),
        in_specs=[
            pl.BlockSpec((16, D_MODEL), lambda l, j: (0, 0)),
            pl.BlockSpec((1, D_MODEL, MODS_TN), lambda l, j: (l, 0, j)),
            pl.BlockSpec((1, 1, MODS_TN), lambda l, j: (l, 0, j)),
        ],
        out_specs=pl.BlockSpec((1, 16, MODS_TN), lambda l, j: (l, 0, j)),
        out_shape=jax.ShapeDtypeStruct((DEPTH, 16, n), F32),
        compiler_params=_cparams(("parallel", "parallel")),
        name="adaln_mods",
    )(cvecs, ada_w, ada_b.reshape(DEPTH, 1, n))


def _norm_mod(x, g, mod, shift_idx, scale_idx):
    ms = jnp.mean(x * x, axis=-1, keepdims=True)
    y = x * lax.rsqrt(ms + EPS) * g
    return y * (1.0 + mod[scale_idx:scale_idx + 1]) + mod[shift_idx:shift_idx + 1]


N_LAT_TILES = N_SAMPLE // TM


def _group_specs(width):
    lat = pl.BlockSpec((TM, width), lambda i: (jnp.minimum(i, N_LAT_TILES - 1), 0))
    ctx = pl.BlockSpec((TM, width), lambda i: (jnp.maximum(i - N_LAT_TILES, 0), 0))
    return [lat, ctx]


def _group_pick(lat_ref, ctx_ref):
    return jnp.where(pl.program_id(0) < N_LAT_TILES, lat_ref[...], ctx_ref[...])


def _rows_specs(x):
    if isinstance(x, tuple):
        return _group_specs(D_MODEL), list(x)
    return [pl.BlockSpec((TM, D_MODEL), lambda i: (i, 0))], [x]


def _rows_value(refs):
    return _group_pick(*refs) if len(refs) == 2 else refs[0][...]


def _nmm_kernel(*refs, n_x, shift_idx, scale_idx):
    g_ref, mod_ref, w_ref, o_ref = refs[n_x:]
    u = _norm_mod(_rows_value(refs[:n_x]), g_ref[...], mod_ref[0], shift_idx, scale_idx)
    o_ref[...] = jnp.dot(u.astype(BF16), w_ref[...], preferred_element_type=F32)


def _norm_mod_matmul(x, g, mods, w_bf16, shift_idx, scale_idx):
    n = w_bf16.shape[1]
    x_specs, x_args = _rows_specs(x)
    return pl.pallas_call(
        functools.partial(_nmm_kernel, n_x=len(x_args), shift_idx=shift_idx, scale_idx=scale_idx),
        grid=(N_TOK // TM,),
        in_specs=x_specs + [
            pl.BlockSpec((1, D_MODEL), lambda i: (0, 0)),
            pl.BlockSpec((1, MOD_ROWS, D_MODEL), lambda i: (i * TM // SEG, 0, 0)),
            pl.BlockSpec((D_MODEL, n), lambda i: (0, 0)),
        ],
        out_specs=pl.BlockSpec((TM, n), lambda i: (i, 0)),
        out_shape=jax.ShapeDtypeStruct((N_TOK, n), F32),
        compiler_params=_cparams(("parallel",)),
        name="norm_mod_inproj",
    )(*x_args, g.reshape(1, D_MODEL), mods, w_bf16)


def _proj_res_kernel(*refs, n_x, gate_idx):
    al_ref, ac_ref, bl_ref, bc_ref, wa_ref, wb_ref, mod_ref, o_ref = refs[n_x:]
    a = _group_pick(al_ref, ac_ref)
    b = _group_pick(bl_ref, bc_ref)
    y = jnp.dot(a.astype(BF16), wa_ref[...], preferred_element_type=F32)
    y = y + jnp.dot(b.astype(BF16), wb_ref[...], preferred_element_type=F32)
    o_ref[...] = _rows_value(refs[:n_x]) + mod_ref[0][gate_idx:gate_idx + 1] * y


def _proj_residual(x, a, b, w_bf16, mods, gate_idx):
    half = a[0].shape[1]
    x_specs, x_args = _rows_specs(x)
    return pl.pallas_call(
        functools.partial(_proj_res_kernel, n_x=len(x_args), gate_idx=gate_idx),
        grid=(N_TOK // TM,),
        in_specs=x_specs + _group_specs(half) + _group_specs(half) + [
            pl.BlockSpec((half, D_MODEL), lambda i: (0, 0)),
            pl.BlockSpec((half, D_MODEL), lambda i: (1, 0)),
            pl.BlockSpec((1, MOD_ROWS, D_MODEL), lambda i: (i * TM // SEG, 0, 0)),
        ],
        out_specs=pl.BlockSpec((TM, D_MODEL), lambda i: (i, 0)),
        out_shape=jax.ShapeDtypeStruct((N_TOK, D_MODEL), F32),
        compiler_params=_cparams(("parallel",)),
        name="outproj_residual",
    )(*x_args, a[0], a[1], b[0], b[1], w_bf16, w_bf16, mods)


def _conv3_rows(z, w):
    n = z.shape[0]
    row = lax.broadcasted_iota(I32, z.shape, 0)
    zm = jnp.where(row == 0, 0.0, pltpu.roll(z, 1, 0))
    zp = jnp.where(row == n - 1, 0.0, pltpu.roll(z, n - 1, 0))
    return zm * w[0:1] + z * w[1:2] + zp * w[2:3]


def _log_gamma(dec):
    return jnp.log1p(-jnp.exp(dec))


def _even_kernel(*refs, seq, latent, tq):
    (q_ref, k_ref, v_ref, g_ref, gb_ref, gc_ref, hx_ref, scw_ref, df_ref, db_ref) = refs[:10]
    if latent:
        s0_ref, ret_ref, sc_ref, mask_ref = refs[10:]
    else:
        ret_ref, sc_ref, st_ref, mask_ref = refs[10:]
    lgf = _log_gamma(df_ref[0])
    lgb = _log_gamma(db_ref[0])

    @pl.when(pl.program_id(1) == 0)
    def _():
        for r in range(seq // tq):
            t = lax.broadcasted_iota(I32, (tq, seq), 0) + r * tq
            s = lax.broadcasted_iota(I32, (tq, seq), 1)
            d = (t - s).astype(F32)
            mf = jnp.where(d >= 0, jnp.exp(jnp.maximum(d, 0.0) * lgf), 0.0)
            mb = jnp.where(d <= 0, jnp.exp(jnp.maximum(-d, 0.0) * lgb), 0.0)
            mask_ref[r * tq:(r + 1) * tq, :] = mf + mb

    k = k_ref[...] * (RET_DK ** -0.5)
    kb = k.astype(BF16)
    vb = v_ref[...].astype(BF16)
    if latent:
        s0f = s0_ref[0, 0, 0, 0].astype(BF16)
        s0b = s0_ref[0, 0, 1, 0].astype(BF16)
    for r in range(seq // tq):
        rows = slice(r * tq, (r + 1) * tq)
        qb = q_ref[rows, :].astype(BF16)
        s = lax.dot_general(qb, kb, (((1,), (1,)), ((), ())), preferred_element_type=F32)
        p = (s * mask_ref[rows, :]).astype(BF16)
        o = jnp.dot(p, vb, preferred_element_type=F32)
        if latent:
            tpos = (lax.broadcasted_iota(I32, (tq, 1), 0) + r * tq).astype(F32)
            o = o + jnp.dot(qb, s0f, preferred_element_type=F32) * jnp.exp((tpos + 1.0) * lgf)
            o = o + jnp.dot(qb, s0b, preferred_element_type=F32) * jnp.exp((seq - tpos) * lgb)
        mu = jnp.mean(o, axis=-1, keepdims=True)
        var = jnp.mean(jnp.square(o - mu), axis=-1, keepdims=True)
        on = (o - mu) * lax.rsqrt(var + GN_EPS)
        ret_ref[rows, :] = _silu(g_ref[rows, :]) * on

    if not latent:
        spos = lax.broadcasted_iota(I32, (seq, 1), 0).astype(F32)
        kf = (k * jnp.exp((seq - 1.0 - spos) * lgf)).T.astype(BF16)
        kr = (k * jnp.exp(spos * lgb)).T.astype(BF16)
        st_ref[0, 0, 0, 0] = jnp.dot(kf, vb, preferred_element_type=F32)
        st_ref[0, 0, 1, 0] = jnp.dot(kr, vb, preferred_element_type=F32)

    sc_ref[...] = gb_ref[...] * _conv3_rows(gc_ref[...] * hx_ref[...], scw_ref[...])


def _even_mixer(proj, sc_w_pad, dec_f, dec_b, state=None):
    latent = state is not None
    seq = DEC_SEQ if latent else SEQ
    nb = DEC_BATCH if latent else BATCH
    row0 = 0 if latent else N_SAMPLE // seq
    tq = min(seq, 256)
    hw = RET_DK

    def col(c0):
        return pl.BlockSpec((seq, hw), lambda h, b: (row0 + b, c0 + h))

    in_specs = [col(0), col(4), col(8), col(12), col(16), col(20), col(24),
                pl.BlockSpec((SUBLANES, hw), lambda h, b: (0, h)),
                pl.BlockSpec((1, 1, 1), lambda h, b: (h, 0, 0)),
                pl.BlockSpec((1, 1, 1), lambda h, b: (h, 0, 0))]
    args = [proj] * 7 + [sc_w_pad, dec_f.reshape(RET_HEADS, 1, 1), dec_b.reshape(RET_HEADS, 1, 1)]
    out_spec = pl.BlockSpec((seq, hw), lambda h, b: (b, h))
    out_shape = [jax.ShapeDtypeStruct((nb * seq, RET_W), F32), jax.ShapeDtypeStruct((nb * seq, SC_W), F32)]
    out_specs = [out_spec, out_spec]
    st_spec = pl.BlockSpec((1, 1, 2, 1, RET_DK, RET_DV), lambda h, b: (b, 0, 0, h, 0, 0))
    if latent:
        in_specs.append(st_spec)
        args.append(state)
    else:
        out_shape.append(jax.ShapeDtypeStruct((BATCH, 1, 2, RET_HEADS, RET_DK, RET_DV), F32))
        out_specs.append(st_spec)
    return pl.pallas_call(
        functools.partial(_even_kernel, seq=seq, latent=latent, tq=tq),
        grid=(RET_HEADS, nb),
        in_specs=in_specs,
        out_specs=out_specs,
        out_shape=out_shape,
        scratch_shapes=[pltpu.VMEM((seq, seq), F32)],
        compiler_params=_cparams(("arbitrary", "arbitrary")),
        name="even_mixer_latent" if latent else "even_mixer_context",
    )(*args)


def _dft_tables(length):
    f = np.arange(length, dtype=np.float64)[:, None]
    t = np.arange(length, dtype=np.float64)[None, :]
    ang = np.pi * ((f * t) % (2 * length)) / length
    cos, sin = np.cos(ang), np.sin(ang)
    sin[0, :] = (-1.0) ** np.arange(length)
    fwd = np.concatenate([cos, sin], axis=0)
    wgt = np.full((2 * length,), 2.0)
    wgt[0] = 1.0
    wgt[length] = 1.0
    inv = (fwd * wgt[:, None] / (2.0 * length)).T
    return fwd.astype(np.float32), np.ascontiguousarray(inv).astype(np.float32)


def _hyena_consts(length):
    t = np.linspace(0.0, 1.0, length)[:, None]
    bands = (HY_EMB - 1) // 2
    w = 2.0 * np.pi * np.arange(length)[:, None] / length
    f = np.linspace(1e-4, bands - 1, bands)[None, :]
    z = np.concatenate([t, np.cos(f * w), -np.sin(f * w)], axis=-1)
    zpad = np.zeros((length, LANES))
    zpad[:, :HY_EMB] = z
    deltas = np.abs(np.linspace(math.log(HY_TARGET) / HY_SLOW_PCT, math.log(HY_TARGET) / HY_FAST_PCT, HY_W))
    decay = np.exp(-t * deltas[None, :])
    return zpad.astype(np.float32), decay.astype(np.float32)


def _cast_kernel(x_ref, o_ref):
    o_ref[...] = x_ref[...].astype(o_ref.dtype)


def _cast_bf16(x, name):
    rows = min(x.shape[0], 512)
    return pl.pallas_call(
        _cast_kernel,
        grid=(x.shape[0] // rows,),
        in_specs=[pl.BlockSpec((rows, x.shape[1]), lambda i: (i, 0))],
        out_specs=pl.BlockSpec((rows, x.shape[1]), lambda i: (i, 0)),
        out_shape=jax.ShapeDtypeStruct(x.shape, BF16),
        compiler_params=_cparams(("parallel",)),
        name=name,
    )(x)


def _hy_filter_kernel(z_ref, dec_ref, w1_ref, b1_ref, f1_ref, w2_ref, b2_ref, f2_ref, w3f_ref, w3b_ref,
                      fwd_ref, ka_ref, ka2_ref, kb_ref, *, seq):
    h = jnp.sin(f1_ref[...] * (jnp.dot(z_ref[...], w1_ref[...], preferred_element_type=F32,
                                       precision=HIGHEST) + b1_ref[...]))
    h = jnp.sin(f2_ref[...] * (jnp.dot(h, w2_ref[...], preferred_element_type=F32,
                                       precision=HIGHEST) + b2_ref[...]))
    dec = dec_ref[...]
    hf = jnp.dot(h, w3f_ref[...], preferred_element_type=F32, precision=HIGHEST) * dec
    hb = jnp.dot(h, w3b_ref[...], preferred_element_type=F32, precision=HIGHEST) * dec
    row = lax.broadcasted_iota(I32, hb.shape, 0)
    hb = jnp.where(row == 0, 0.0, hb)
    inv_norm = 1.0 / (jnp.sum(jnp.abs(hf), axis=0, keepdims=True)
                      + jnp.sum(jnp.abs(hb), axis=0, keepdims=True) + EPS)
    p = (hf + hb) * inv_norm
    q = (hf - hb) * inv_norm
    fp = jnp.dot(fwd_ref[...], p, preferred_element_type=F32, precision=HIGHEST)
    fq = jnp.dot(fwd_ref[seq:, :], q, preferred_element_type=F32, precision=HIGHEST)
    kr = fp[:seq]
    row = lax.broadcasted_iota(I32, kr.shape, 0)
    ka_ref[0] = kr
    ka2_ref[0] = jnp.where(row == 0, fp[seq:seq + 1], kr)
    kb_ref[0] = jnp.where(row == 0, 0.0, -fq)


def _hy_filters(seq, zfeat, decay, fwd, w1, b1, f1, w2, b2, f2, w3):
    ns = HY_W // HY_SLAB
    per_o = 2 * ns
    small = lambda shape: pl.BlockSpec(shape, lambda o, s: (0, 0))
    out_spec = pl.BlockSpec((1, seq, HY_SLAB), lambda o, s: (o, 0, s))
    out = jax.ShapeDtypeStruct((HY_ORDER, seq, HY_W), F32)
    return pl.pallas_call(
        functools.partial(_hy_filter_kernel, seq=seq),
        grid=(HY_ORDER, ns),
        in_specs=[
            small((seq, LANES)),
            pl.BlockSpec((seq, HY_SLAB), lambda o, s: (0, s)),
            small((LANES, LANES)), small((1, LANES)), small((1, LANES)),
            small((LANES, LANES)), small((1, LANES)), small((1, LANES)),
            pl.BlockSpec((LANES, HY_SLAB), lambda o, s: (0, o * per_o + s)),
            pl.BlockSpec((LANES, HY_SLAB), lambda o, s: (0, o * per_o + ns + s)),
            small((2 * seq, seq)),
        ],
        out_specs=[out_spec, out_spec, out_spec],
        out_shape=[out, out, out],
        compiler_params=_cparams(("parallel", "parallel")),
        name=f"hyena_filters_{seq}",
    )(zfeat, decay, w1, b1, f1, w2, b2, f2, w3, w3, fwd)


def _hy_apply_kernel(x1_ref, x2_ref, v_ref, c1_ref, c2_ref, cv_ref, fwd_ref, inv_ref,
                     ka_ref, ka2_ref, kb_ref, bias_ref, o_ref, *, seq):
    x1 = _conv3_rows(x1_ref[...], c1_ref[...])
    x2 = _conv3_rows(x2_ref[...], c2_ref[...])
    v = _conv3_rows(v_ref[...], cv_ref[...])

    def long_conv(sig, o):
        spec = jnp.dot(fwd_ref[...], sig.astype(BF16), preferred_element_type=F32)
        xr, xs = spec[:seq], spec[seq:]
        kb = kb_ref[o]
        yr = (xr * ka_ref[o] + xs * kb).astype(BF16)
        ys = (xs * ka2_ref[o] - xr * kb).astype(BF16)
        y = jnp.dot(inv_ref[:, :seq], yr, preferred_element_type=F32)
        y = y + jnp.dot(inv_ref[:, seq:], ys, preferred_element_type=F32)
        return y + sig * bias_ref[o:o + 1]

    z = x1 * long_conv(v, 0)
    o_ref[...] = x2 * long_conv(z, 1)


def _hy_apply(proj, conv_w_pad, fwd_bf, inv_bf, filters, bias_pad, latent):
    seq = DEC_SEQ if latent else SEQ
    nb = DEC_BATCH if latent else BATCH
    row0 = 0 if latent else N_SAMPLE // seq
    ns = HY_W // HY_SLAB
    const = lambda shape: pl.BlockSpec(shape, lambda s, b: (0,) * len(shape))
    col = lambda c0: pl.BlockSpec((seq, HY_SLAB), lambda s, b: (row0 + b, c0 + s))
    cw = lambda c0: pl.BlockSpec((SUBLANES, HY_SLAB), lambda s, b: (0, c0 + s))
    filt = pl.BlockSpec((HY_ORDER, seq, HY_SLAB), lambda s, b: (0, 0, s))
    in_specs = [col(0), col(ns), col(2 * ns), cw(0), cw(ns), cw(2 * ns),
                const((2 * seq, seq)), const((seq, 2 * seq)), filt, filt, filt,
                pl.BlockSpec((SUBLANES, HY_SLAB), lambda s, b: (0, s))]
    args = [proj] * 3 + [conv_w_pad] * 3 + [fwd_bf, inv_bf, *filters, bias_pad]
    return pl.pallas_call(
        functools.partial(_hy_apply_kernel, seq=seq),
        grid=(ns, nb),
        in_specs=in_specs,
        out_specs=pl.BlockSpec((seq, HY_SLAB), lambda s, b: (b, s)),
        out_shape=jax.ShapeDtypeStruct((nb * seq, HY_W), F32),
        compiler_params=_cparams(("arbitrary", "arbitrary")),
        name="hyena_latent" if latent else "hyena_context",
    )(*args)


def _rope_tables(length):
    rows = length // GRID_W
    row = np.repeat(np.arange(rows, dtype=np.float64), GRID_W)
    colp = np.tile(np.arange(GRID_W, dtype=np.float64), rows)
    half = DA_DH // 2
    freqs = ROPE_THETA ** (-np.arange(0, half, 2, dtype=np.float64) / half)
    ang = np.concatenate([row[:, None] * freqs[None, :]] * 2 + [colp[:, None] * freqs[None, :]] * 2, axis=1)
    ang = np.concatenate([ang, ang], axis=1)
    lane = np.arange(LANES)
    sign = np.where((lane % half) < half // 2, -1.0, 1.0)[None, :]
    return np.cos(ang).astype(np.float32), (np.sin(ang) * sign).astype(np.float32)


def _rope(x, cos, sin_signed):
    q = DA_DH // 4
    lane = lax.broadcasted_iota(I32, x.shape, 1)
    partner = jnp.where((lane % (2 * q)) < q, pltpu.roll(x, LANES - q, 1), pltpu.roll(x, q, 1))
    return x * cos + partner * sin_signed


def _attn_kernel(*refs, seq, latent, lambda_init):
    q_ref, k_ref, v_ref, lam_ref, sub_ref = refs[:5]
    if latent:
        cos_ref, sin_ref, ck_ref, cv_ref, o_ref = refs[5:]
    else:
        o_ref, nk_ref, nv_ref = refs[5:]
    k = k_ref[...]
    v = v_ref[...]
    if latent:
        k = _rope(k, cos_ref[...], sin_ref[...])
        k_all = jnp.concatenate([k, jnp.concatenate([ck_ref[0, 0, 0, 0], ck_ref[0, 0, 1, 0]], axis=1)], axis=0)
        v_all = jnp.concatenate([v, cv_ref[0, 0, 0]], axis=0)
    else:
        k_all, v_all = k, v
        nk_ref[0, 0, 0, 0] = k[:, :DA_DH]
        nk_ref[0, 0, 1, 0] = k[:, DA_DH:]
        nv_ref[0, 0, 0] = v
    kb = k_all.astype(BF16)
    vb = v_all.astype(BF16)
    lv = lam_ref[...]
    lam = (jnp.exp(jnp.sum(lv[0:1] * lv[1:2], axis=1, keepdims=True))
           - jnp.exp(jnp.sum(lv[2:3] * lv[3:4], axis=1, keepdims=True)) + lambda_init)
    scale = DA_DH ** -0.5
    dn = (((1,), (1,)), ((), ()))
    tq = min(seq, ATT_TQ)
    for r in range(seq // tq):
        rows = slice(r * tq, (r + 1) * tq)
        q = q_ref[rows, :]
        if latent:
            q = _rope(q, cos_ref[rows, :], sin_ref[rows, :])
        first = lax.broadcasted_iota(I32, q.shape, 1) < DA_DH
        q1 = jnp.where(first, q, 0.0).astype(BF16)
        q2 = jnp.where(first, 0.0, q).astype(BF16)
        s1 = lax.dot_general(q1, kb, dn, preferred_element_type=F32) * scale
        s2 = lax.dot_general(q2, kb, dn, preferred_element_type=F32) * scale
        e1 = jnp.exp(s1 - jnp.max(s1, axis=-1, keepdims=True))
        e2 = jnp.exp(s2 - jnp.max(s2, axis=-1, keepdims=True))
        w = (e1 * (1.0 / jnp.sum(e1, axis=-1, keepdims=True))
             - e2 * (lam / jnp.sum(e2, axis=-1, keepdims=True)))
        o = jnp.dot(w.astype(BF16), vb, preferred_element_type=F32)
        o = o * lax.rsqrt(jnp.mean(o * o, axis=-1, keepdims=True) + EPS) * sub_ref[...]
        o_ref[rows, :] = o * (1.0 - lambda_init)


def _attention(proj, lam_vec, subln, lambda_init, latent, rope=None, ctx_k=None, ctx_v=None):
    seq = DEC_SEQ if latent else SEQ
    nb = DEC_BATCH if latent else BATCH
    row0 = 0 if latent else N_SAMPLE // seq
    hw = 2 * DA_DH
    c0 = 3 * HY_W // hw
    col = lambda c: pl.BlockSpec((seq, hw), lambda b, h: (row0 + b, c + h))
    in_specs = [col(c0), col(c0 + DA_HEADS), col(c0 + 2 * DA_HEADS),
                pl.BlockSpec((4, DA_DH), lambda b, h: (0, 0)),
                pl.BlockSpec((1, hw), lambda b, h: (0, 0))]
    args = [proj] * 3 + [lam_vec, subln.reshape(1, hw)]
    out_spec = pl.BlockSpec((seq, hw), lambda b, h: (b, h))
    out_shape = [jax.ShapeDtypeStruct((nb * seq, DA_W), F32)]
    out_specs = [out_spec]
    k_spec = pl.BlockSpec((1, 1, 2, 1, SEQ, DA_DH), lambda b, h: (b, 0, 0, h, 0, 0))
    v_spec = pl.BlockSpec((1, 1, 1, SEQ, hw), lambda b, h: (b, 0, h, 0, 0))
    if latent:
        in_specs += [pl.BlockSpec((seq, hw), lambda b, h: (0, 0))] * 2 + [k_spec, v_spec]
        args += [rope[0], rope[1], ctx_k, ctx_v]
    else:
        out_shape += [jax.ShapeDtypeStruct((BATCH, 1, 2, DA_HEADS, SEQ, DA_DH), F32),
                      jax.ShapeDtypeStruct((BATCH, 1, DA_HEADS, SEQ, hw), F32)]
        out_specs += [k_spec, v_spec]
    return pl.pallas_call(
        functools.partial(_attn_kernel, seq=seq, latent=latent, lambda_init=lambda_init),
        grid=(nb, DA_HEADS),
        in_specs=in_specs,
        out_specs=out_specs,
        out_shape=out_shape,
        compiler_params=_cparams(("arbitrary", "arbitrary")),
        name="diff_attn_latent" if latent else "diff_attn_context",
    )(*args)


HALF_D = D_MODEL // 2
HI16 = -65536


def _pack_bf16_pair(lo, hi):
    lo_bits = lax.bitcast_convert_type(lo.astype(BF16).astype(F32), I32)
    hi_bits = lax.bitcast_convert_type(hi.astype(BF16).astype(F32), I32)
    return (hi_bits & HI16) | lax.shift_right_logical(lo_bits, 16)


def _unpack_bf16_pair(p):
    lo = lax.bitcast_convert_type(lax.shift_left(p, 16), F32).astype(BF16)
    hi = lax.bitcast_convert_type(p & HI16, F32).astype(BF16)
    return lo, hi


def _router_kernel(x_ref, g_ref, mod_ref, wr_ref, br_ref, tril_ref,
                   u_ref, idx_ref, gate_ref, cnt_ref, carry_ref):
    @pl.when(pl.program_id(0) == 0)
    def _():
        carry_ref[...] = jnp.zeros_like(carry_ref)

    u = _norm_mod(x_ref[...], g_ref[...], mod_ref[0], 3, 4)
    u_ref[...] = _pack_bf16_pair(u[:, :HALF_D], u[:, HALF_D:])
    logits = jnp.dot(u, wr_ref[...], preferred_element_type=F32, precision=HIGHEST) + br_ref[...]
    lane = lax.broadcasted_iota(I32, logits.shape, 1)
    lane_f = lane.astype(F32)
    neg = -jnp.inf
    big = float(LANES)

    def first_max(vals):
        m = jnp.max(vals, axis=-1, keepdims=True)
        return m, jnp.min(jnp.where(vals == m, lane_f, big), axis=-1, keepdims=True)

    is_grp = lane < N_GROUPS
    gmax, gidx = first_max(jnp.where(is_grp, logits, neg))
    gsum = jnp.sum(jnp.where(is_grp, jnp.exp(logits - gmax), 0.0), axis=-1, keepdims=True)
    g_w = 1.0 / gsum
    lo = N_GROUPS + EXP_PER_GROUP * gidx
    in_grp = jnp.logical_and(lane_f >= lo, lane_f < lo + EXP_PER_GROUP)
    el = jnp.where(in_grp, logits, neg)
    v1, i1 = first_max(el)
    v2, i2 = first_max(jnp.where(lane_f == i1, neg, el))
    e = jnp.exp(v2 - v1)
    p1 = 1.0 / (1.0 + e)
    e1 = i1 - N_GROUPS
    e2 = i2 - N_GROUPS
    hit1 = lane_f == e1
    hit2 = lane_f == e2
    onehot = jnp.where(jnp.logical_or(hit1, hit2), 1.0, 0.0)
    cum = jnp.dot(tril_ref[...], onehot.astype(BF16), preferred_element_type=F32)
    before = cum - onehot + carry_ref[...]
    r1 = jnp.sum(jnp.where(hit1, before, 0.0), axis=-1, keepdims=True)
    r2 = jnp.sum(jnp.where(hit2, before, 0.0), axis=-1, keepdims=True)
    carry_ref[...] = carry_ref[...] + cum[TM - 1:TM, :]
    cnt_ref[...] = jnp.broadcast_to(carry_ref[...], cnt_ref.shape)
    info = jnp.where(lane == 0, e1, jnp.where(lane == 1, e2, jnp.where(lane == 2, r1, r2)))
    idx_ref[0] = info.T[:SUBLANES].astype(I32)
    gate_ref[...] = jnp.where(lane == 0, g_w * p1, g_w * (e * p1))


def _router(x, g, mods, w_route, b_route, tril):
    return pl.pallas_call(
        _router_kernel,
        grid=(N_TOK // TM,),
        in_specs=[
            pl.BlockSpec((TM, D_MODEL), lambda i: (i, 0)),
            pl.BlockSpec((1, D_MODEL), lambda i: (0, 0)),
            pl.BlockSpec((1, MOD_ROWS, D_MODEL), lambda i: (i * TM // SEG, 0, 0)),
            pl.BlockSpec((D_MODEL, LANES), lambda i: (0, 0)),
            pl.BlockSpec((1, LANES), lambda i: (0, 0)),
            pl.BlockSpec((TM, TM), lambda i: (0, 0)),
        ],
        out_specs=[
            pl.BlockSpec((TM, HALF_D), lambda i: (i, 0)),
            pl.BlockSpec((1, SUBLANES, TM), lambda i: (i, 0, 0)),
            pl.BlockSpec((TM, LANES), lambda i: (i, 0)),
            pl.BlockSpec((SUBLANES, LANES), lambda i: (0, 0)),
        ],
        out_shape=[
            jax.ShapeDtypeStruct((N_TOK, HALF_D), I32),
            jax.ShapeDtypeStruct((N_TOK // TM, SUBLANES, TM), I32),
            jax.ShapeDtypeStruct((N_TOK, LANES), F32),
            jax.ShapeDtypeStruct((SUBLANES, LANES), F32),
        ],
        scratch_shapes=[pltpu.VMEM((1, LANES), F32)],
        compiler_params=_cparams(("arbitrary",)),
        name="moe_router",
    )(x, g.reshape(1, D_MODEL), mods, w_route, b_route, tril)


def _row_copy(src, src_row, dst, dst_row, sem):
    return pltpu.make_async_copy(src.at[pl.ds(src_row, 1)], dst.at[pl.ds(dst_row, 1)], sem)


ISSUE_UNROLL = 8


def _slot(idx_ref, start_ref, k, r):
    return start_ref[0, idx_ref[0, k, r]] + idx_ref[0, TOP_K + k, r]


def _dispatch_kernel(idx_ref, start_ref, u_ref, xb_in, xb_ref, sem):
    del xb_in

    def start(r, c):
        for k in range(TOP_K):
            _row_copy(u_ref, r, xb_ref, _slot(idx_ref, start_ref, k, r), sem.at[k]).start()
        return c

    lax.fori_loop(0, TM, start, 0, unroll=ISSUE_UNROLL)
    for k in range(TOP_K):
        for _ in range(TM):
            _row_copy(u_ref, 0, xb_ref, 0, sem.at[k]).wait()


def _dispatch(u, idx, seg_start, xb_zero):
    return pl.pallas_call(
        _dispatch_kernel,
        grid=(N_TOK // TM,),
        in_specs=[
            pl.BlockSpec((1, SUBLANES, TM), lambda i: (i, 0, 0), memory_space=pltpu.SMEM),
            pl.BlockSpec(memory_space=pltpu.SMEM),
            pl.BlockSpec((TM, HALF_D), lambda i: (i, 0)),
            pl.BlockSpec(memory_space=pl.ANY),
        ],
        out_specs=pl.BlockSpec(memory_space=pl.ANY),
        out_shape=jax.ShapeDtypeStruct((MOE_ROWS, HALF_D), I32),
        scratch_shapes=[pltpu.SemaphoreType.DMA((TOP_K,))],
        input_output_aliases={3: 0},
        compiler_params=_cparams(("arbitrary",)),
        name="moe_dispatch",
    )(idx, seg_start, u, xb_zero)


def _expert_kernel(meta_ref, xb_ref, w1_ref, w3_ref, w2_ref, yb_ref, w1b, w3b, w2b):
    j = pl.program_id(0)
    flag = meta_ref[1, j]

    @pl.when(flag == 2)
    def _():
        w1b[...] = w1_ref[0, 0].astype(BF16)
        w3b[...] = w3_ref[0, 0].astype(BF16)
        w2b[...] = w2_ref[0, 0].astype(BF16)

    @pl.when(flag > 0)
    def _():
        lo, hi = _unpack_bf16_pair(xb_ref[...])

        def up(wb):
            return (jnp.dot(lo, wb[:HALF_D, :], preferred_element_type=F32)
                    + jnp.dot(hi, wb[HALF_D:, :], preferred_element_type=F32))

        h1 = up(w1b)
        h3 = up(w3b)
        hdn = (_silu(h1) * h3).astype(BF16)
        yb_ref[...] = jnp.dot(hdn, w2b[...], preferred_element_type=F32)

    @pl.when(flag == 0)
    def _():
        yb_ref[...] = jnp.zeros_like(yb_ref)


def _experts(meta, xb, w1, w3, w2, layer):
    wspec_in = pl.BlockSpec((1, 1, D_MODEL, D_EXPERT), lambda j, m: (layer, m[0, j], 0, 0))
    wspec_out = pl.BlockSpec((1, 1, D_EXPERT, D_MODEL), lambda j, m: (layer, m[0, j], 0, 0))
    rows = pl.BlockSpec((MOE_BLK, HALF_D), lambda j, m: (m[2, j], 0))
    return pl.pallas_call(
        _expert_kernel,
        grid_spec=pltpu.PrefetchScalarGridSpec(
            num_scalar_prefetch=1,
            grid=(MOE_NB,),
            in_specs=[rows, wspec_in, wspec_in, wspec_out],
            out_specs=pl.BlockSpec((MOE_BLK, D_MODEL), lambda j, m: (j, 0)),
            scratch_shapes=[pltpu.VMEM((D_MODEL, D_EXPERT), BF16), pltpu.VMEM((D_MODEL, D_EXPERT), BF16),
                            pltpu.VMEM((D_EXPERT, D_MODEL), BF16)],
        ),
        out_shape=jax.ShapeDtypeStruct((MOE_ROWS, D_MODEL), F32),
        compiler_params=_cparams(("arbitrary",)),
        name="moe_experts",
    )(meta, xb, w1, w3, w2)


def _combine_kernel(idx_ref, start_ref, x_ref, gate_ref, mod_ref, yb_ref, o_ref, buf, sem):
    def start(r, c):
        for k in range(TOP_K):
            _row_copy(yb_ref, _slot(idx_ref, start_ref, k, r), buf.at[k], r, sem.at[k]).start()
        return c

    lax.fori_loop(0, TM, start, 0, unroll=ISSUE_UNROLL)
    for k in range(TOP_K):
        for _ in range(TM):
            _row_copy(yb_ref, 0, buf.at[k], 0, sem.at[k]).wait()
    gate = gate_ref[...]
    y = gate[:, 0:1] * buf[0] + gate[:, 1:2] * buf[1]
    o_ref[...] = x_ref[...] + mod_ref[0][5:6] * y


def _combine(x, idx, seg_start, gates, mods, yb):
    return pl.pallas_call(
        _combine_kernel,
        grid=(N_TOK // TM,),
        in_specs=[
            pl.BlockSpec((1, SUBLANES, TM), lambda i: (i, 0, 0), memory_space=pltpu.SMEM),
            pl.BlockSpec(memory_space=pltpu.SMEM),
            pl.BlockSpec((TM, D_MODEL), lambda i: (i, 0)),
            pl.BlockSpec((TM, LANES), lambda i: (i, 0)),
            pl.BlockSpec((1, MOD_ROWS, D_MODEL), lambda i: (i * TM // SEG, 0, 0)),
            pl.BlockSpec(memory_space=pl.ANY),
        ],
        out_specs=pl.BlockSpec((TM, D_MODEL), lambda i: (i, 0)),
        out_shape=jax.ShapeDtypeStruct((N_TOK, D_MODEL), F32),
        scratch_shapes=[pltpu.VMEM((TOP_K, TM, D_MODEL), F32), pltpu.SemaphoreType.DMA((2,))],
        compiler_params=_cparams(("arbitrary",)),
        name="moe_combine",
    )(idx, seg_start, x, gates, mods, yb)


def _moe_layer(x, layer, norm2, mods, w_grp, b_grp, w_exp, b_exp, w1, w3, w2, tril):
    w_route = jnp.zeros((D_MODEL, LANES), F32)
    w_route = w_route.at[:, :N_GROUPS].set(w_grp).at[:, N_GROUPS:N_GROUPS + N_EXPERTS].set(w_exp)
    b_route = jnp.zeros((1, LANES), F32)
    b_route = b_route.at[0, :N_GROUPS].set(b_grp).at[0, N_GROUPS:N_GROUPS + N_EXPERTS].set(b_exp)
    u, idx, gates, cnt = _router(x, norm2, mods, w_route, b_route, tril)

    counts = cnt[0, :N_EXPERTS].astype(I32)
    padded = (counts + MOE_BLK - 1) // MOE_BLK * MOE_BLK
    pad_end = jnp.cumsum(padded)
    pad_start = pad_end - padded
    seg_start = jnp.zeros((1, LANES), I32).at[0, :N_EXPERTS].set(pad_start)
    blk_start = jnp.arange(MOE_NB, dtype=I32) * MOE_BLK
    blk_exp = jnp.minimum(jnp.sum(pad_end[None, :] <= blk_start[:, None], axis=1), N_EXPERTS - 1).astype(I32)
    used = pad_end[-1] // MOE_BLK
    active = jnp.arange(MOE_NB, dtype=I32) < used
    first = blk_start == pad_start[blk_exp]
    flag = jnp.where(active, jnp.where(first, 2, 1), 0).astype(I32)
    last_exp = blk_exp[jnp.maximum(used - 1, 0)]
    meta = jnp.stack([jnp.where(active, blk_exp, last_exp), flag,
                      jnp.minimum(jnp.arange(MOE_NB, dtype=I32), used - 1)], axis=0).astype(I32)

    xb = _dispatch(u, idx, seg_start, jnp.zeros((MOE_ROWS, HALF_D), I32))
    yb = _experts(meta, xb, w1, w3, w2, layer)
    return _combine(x, idx, seg_start, gates, mods, yb)


def _final_kernel(x_ref, g_ref, o_ref):
    x = x_ref[...]
    o_ref[...] = x * lax.rsqrt(jnp.mean(x * x, axis=-1, keepdims=True) + EPS) * g_ref[...]


def _final_norm(x, g, row0, rows):
    return pl.pallas_call(
        _final_kernel,
        grid=(rows // TM,),
        in_specs=[pl.BlockSpec((TM, D_MODEL), lambda i: (row0 // TM + i, 0)),
                  pl.BlockSpec((1, D_MODEL), lambda i: (0, 0))],
        out_specs=pl.BlockSpec((TM, D_MODEL), lambda i: (i, 0)),
        out_shape=jax.ShapeDtypeStruct((rows, D_MODEL), F32),
        compiler_params=_cparams(("parallel",)),
        name="final_norm",
    )(x, g.reshape(1, D_MODEL))


def _pad_rows(w, rows=SUBLANES):
    return jnp.zeros((rows, w.shape[1]), w.dtype).at[:w.shape[0]].set(w)


def _pad2(w, rows, cols):
    return jnp.zeros((rows, cols), w.dtype).at[:w.shape[0], :w.shape[1]].set(w)


def kernel(x_prompt, x_sample, c, c_ctx, state_ret, cache_k, cache_v, ada_w, ada_b, norm1, norm2, final_norm, ev_w_in, ev_w_out, ret_decay_fwd, ret_decay_bwd, sc_conv_w, od_w_in, od_w_out, hy_conv_w, hy_w1, hy_b1, hy_fr1, hy_w2, hy_b2, hy_fr2, hy_w3, hy_bias, da_lambda, da_subln, moe_w_grp, moe_b_grp, moe_w_exp, moe_b_exp, moe_w1, moe_w3, moe_w2):
    x = (x_sample.reshape(N_SAMPLE, D_MODEL), x_prompt.reshape(N_PROMPT, D_MODEL))

    cvecs = jnp.zeros((16, D_MODEL), F32).at[:DEC_BATCH].set(c).at[DEC_BATCH].set(c_ctx)
    m = _mods(cvecs, ada_w, ada_b).reshape(DEPTH, 16, N_MOD, D_MODEL)
    seg_mods = jnp.concatenate(
        [m[:, :DEC_BATCH], jnp.broadcast_to(m[:, DEC_BATCH:DEC_BATCH + 1], (DEPTH, N_SEG - DEC_BATCH, N_MOD, D_MODEL))],
        axis=1)
    seg_mods = jnp.pad(seg_mods, ((0, 0), (0, 0), (0, MOD_ROWS - N_MOD), (0, 0)))

    tril = jnp.asarray(np.tril(np.ones((TM, TM), np.float32)), BF16)

    for l in range(DEPTH):
        mods = seg_mods[l]
        i = l // 2
        if l % 2 == 0:
            proj = _norm_mod_matmul(x, norm1[l], mods, ev_w_in[i].astype(BF16), 0, 1)
            scw = _pad_rows(sc_conv_w[i])
            ret_l, sc_l = _even_mixer(proj, scw, ret_decay_fwd[i], ret_decay_bwd[i], state=state_ret[:, i:i + 1])
            ret_c, sc_c, new_state = _even_mixer(proj, scw, ret_decay_fwd[i], ret_decay_bwd[i])
            x = _proj_residual(x, (ret_l, ret_c), (sc_l, sc_c), ev_w_out[i].astype(BF16), mods, 2)
        else:
            lambda_init = 0.8 - 0.6 * math.exp(-0.3 * l)
            proj = _norm_mod_matmul(x, norm1[l], mods, od_w_in[i].astype(BF16), 0, 1)
            conv_w = _pad_rows(hy_conv_w[i])
            bias = _pad_rows(hy_bias[i])
            w1 = _pad2(hy_w1[i], LANES, LANES)
            b1 = _pad2(hy_b1[i][None], 1, LANES)
            f1 = _pad2(hy_fr1[i][None], 1, LANES)
            w2 = _pad2(hy_w2[i], LANES, LANES)
            b2 = _pad2(hy_b2[i][None], 1, LANES)
            f2 = _pad2(hy_fr2[i][None], 1, LANES)
            w3 = _pad2(hy_w3[i], LANES, hy_w3.shape[2])
            hy = []
            for latent in (True, False):
                seq = DEC_SEQ if latent else SEQ
                fwd_np, inv_np = _dft_tables(seq)
                zfeat, decay = _hyena_consts(seq)
                fwd = jnp.asarray(fwd_np)
                filters = _hy_filters(seq, jnp.asarray(zfeat), jnp.asarray(decay), fwd, w1, b1, f1, w2, b2, f2, w3)
                fwd_bf = _cast_bf16(fwd, f"dft_fwd_bf16_{seq}")
                inv_bf = _cast_bf16(jnp.asarray(inv_np), f"dft_inv_bf16_{seq}")
                hy.append(_hy_apply(proj, conv_w, fwd_bf, inv_bf, filters, bias, latent))
            cos, sin = _rope_tables(DEC_SEQ)
            att_l = _attention(proj, da_lambda[i], da_subln[i], lambda_init, True,
                               rope=(jnp.asarray(cos), jnp.asarray(sin)),
                               ctx_k=cache_k[:, i:i + 1], ctx_v=cache_v[:, i:i + 1])[0]
            att_c, new_k, new_v = _attention(proj, da_lambda[i], da_subln[i], lambda_init, False)
            x = _proj_residual(x, hy, (att_l, att_c), od_w_out[i].astype(BF16), mods, 2)
        x = _moe_layer(x, l, norm2[l], mods, moe_w_grp[l], moe_b_grp[l], moe_w_exp[l], moe_b_exp[l],
                       moe_w1, moe_w3, moe_w2, tril)

    y_sample = _final_norm(x, final_norm, 0, N_SAMPLE).reshape(DEC_BATCH, DEC_SEQ, D_MODEL)
    y_prompt = _final_norm(x, final_norm, N_SAMPLE, N_PROMPT).reshape(BATCH, SEQ, D_MODEL)
    return (y_prompt, y_sample, new_state, new_k, new_v)
```

```python
import functools
import math

import numpy as np
import jax
import jax.numpy as jnp
from jax import lax
from jax.experimental import pallas as pl
from jax.experimental.pallas import tpu as pltpu

F32 = jnp.float32
BF16 = jnp.bfloat16
I32 = jnp.int32
HIGHEST = lax.Precision.HIGHEST

D_MODEL = 1024
BATCH = 32
SEQ = 256
DEPTH = 2
DEC_BATCH = 8
DEC_SEQ = 1024
PAST_LEN = 256
GRID_W = 64
N_MOD = 6
EPS = 1e-6
GN_EPS = 1e-5

RET_DK = 128
RET_DV = 128
RET_HEADS = 4
RET_W = 512
RET_QK = 512
SC_W = 512
EVEN_IN = 2 * RET_QK + 2 * RET_W + 3 * SC_W

HY_W = 512
HY_ORDER = 2
HY_EMB = 33
HY_FFN = 64
HY_TARGET = 1e-2
HY_FAST_PCT = 0.3
HY_SLOW_PCT = 1.5
DA_DH = 64
DA_HEADS = 4
DA_W = 512
ODD_IN = 3 * HY_W + 3 * DA_W
ROPE_THETA = 10000.0

N_GROUPS = 4
EXP_PER_GROUP = 8
N_EXPERTS = 32
TOP_K = 2
D_EXPERT = 512

N_SAMPLE = DEC_BATCH * DEC_SEQ
N_PROMPT = BATCH * SEQ
N_TOK = N_SAMPLE + N_PROMPT
SEG = 1024
N_SEG = N_TOK // SEG
MOD_ROWS = 8

LANES = 128
SUBLANES = 8
VMEM_LIMIT = 56 * 1024 * 1024

TM = 256
MOE_BLK = 256
ROW_ALIGN = SUBLANES
R_LOC = (TOP_K * TM + N_EXPERTS * (ROW_ALIGN - 1) + LANES - 1) // LANES * LANES
CHUNKS = (256, 128, 64, 32, 16, 8)
MOE_ROWS = ((N_TOK * TOP_K + N_EXPERTS * (N_TOK // TM) * (ROW_ALIGN - 1) + N_EXPERTS * (MOE_BLK - 1))
            // MOE_BLK + 1) * MOE_BLK
MOE_NB = MOE_ROWS // MOE_BLK
HY_SLAB = 256
ATT_TQ = 256


def _cparams(sem, vmem=VMEM_LIMIT):
    return pltpu.CompilerParams(dimension_semantics=sem, vmem_limit_bytes=vmem)


def _silu(x):
    return x * (1.0 / (1.0 + jnp.exp(-x)))


MODS_TN = 1536


def _mods_kernel(c_ref, w_ref, b_ref, o_ref):
    s = _silu(c_ref[...])
    o_ref[0] = jnp.dot(s, w_ref[0], preferred_element_type=F32, precision=HIGHEST) + b_ref[0]


def _mods(cvecs, ada_w, ada_b):
    n = N_MOD * D_MODEL
    return pl.pallas_call(
        _mods_kernel,
        grid=(DEPTH, n // MODS_TN),
        in_specs=[
            pl.BlockSpec((16, D_MODEL), lambda l, j: (0, 0)),
            pl.BlockSpec((1, D_MODEL, MODS_TN), lambda l, j: (l, 0, j)),
            pl.BlockSpec((1, 1, MODS_TN), lambda l, j: (l, 0, j)),
        ],
        out_specs=pl.BlockSpec((1, 16, MODS_TN), lambda l, j: (l, 0, j)),
        out_shape=jax.ShapeDtypeStruct((DEPTH, 16, n), F32),
        compiler_params=_cparams(("parallel", "parallel")),
        name="adaln_mods",
    )(cvecs, ada_w, ada_b.reshape(DEPTH, 1, n))


def _norm_mod(x, g, mod, shift_idx, scale_idx):
    ms = jnp.mean(x * x, axis=-1, keepdims=True)
    y = x * lax.rsqrt(ms + EPS) * g
    return y * (1.0 + mod[scale_idx:scale_idx + 1]) + mod[shift_idx:shift_idx + 1]


N_LAT_TILES = N_SAMPLE // TM


def _group_specs(width):
    lat = pl.BlockSpec((TM, width), lambda i: (jnp.minimum(i, N_LAT_TILES - 1), 0))
    ctx = pl.BlockSpec((TM, width), lambda i: (jnp.maximum(i - N_LAT_TILES, 0), 0))
    return [lat, ctx]


def _group_pick(lat_ref, ctx_ref):
    return jnp.where(pl.program_id(0) < N_LAT_TILES, lat_ref[...], ctx_ref[...])


def _rows_specs(x):
    if isinstance(x, tuple):
        return _group_specs(D_MODEL), list(x)
    return [pl.BlockSpec((TM, D_MODEL), lambda i: (i, 0))], [x]


def _rows_value(refs):
    return _group_pick(*refs) if len(refs) == 2 else refs[0][...]


def _nmm_kernel(*refs, n_x, shift_idx, scale_idx):
    g_ref, mod_ref, w_ref, o_ref = refs[n_x:]
    u = _norm_mod(_rows_value(refs[:n_x]), g_ref[...], mod_ref[0], shift_idx, scale_idx)
    o_ref[...] = jnp.dot(u.astype(BF16), w_ref[...], preferred_element_type=F32)


def _norm_mod_matmul(x, g, mods, w_bf16, shift_idx, scale_idx):
    n = w_bf16.shape[1]
    x_specs, x_args = _rows_specs(x)
    return pl.pallas_call(
        functools.partial(_nmm_kernel, n_x=len(x_args), shift_idx=shift_idx, scale_idx=scale_idx),
        grid=(N_TOK // TM,),
        in_specs=x_specs + [
            pl.BlockSpec((1, D_MODEL), lambda i: (0, 0)),
            pl.BlockSpec((1, MOD_ROWS, D_MODEL), lambda i: (i * TM // SEG, 0, 0)),
            pl.BlockSpec((D_MODEL, n), lambda i: (0, 0)),
        ],
        out_specs=pl.BlockSpec((TM, n), lambda i: (i, 0)),
        out_shape=jax.ShapeDtypeStruct((N_TOK, n), F32),
        compiler_params=_cparams(("parallel",)),
        name="norm_mod_inproj",
    )(*x_args, g.reshape(1, D_MODEL), mods, w_bf16)


def _proj_res_kernel(*refs, n_x, gate_idx):
    al_ref, ac_ref, bl_ref, bc_ref, wa_ref, wb_ref, mod_ref, o_ref = refs[n_x:]
    a = _group_pick(al_ref, ac_ref)
    b = _group_pick(bl_ref, bc_ref)
    y = jnp.dot(a.astype(BF16), wa_ref[...], preferred_element_type=F32)
    y = y + jnp.dot(b.astype(BF16), wb_ref[...], preferred_element_type=F32)
    o_ref[...] = _rows_value(refs[:n_x]) + mod_ref[0][gate_idx:gate_idx + 1] * y


def _proj_residual(x, a, b, w_bf16, mods, gate_idx):
    half = a[0].shape[1]
    x_specs, x_args = _rows_specs(x)
    return pl.pallas_call(
        functools.partial(_proj_res_kernel, n_x=len(x_args), gate_idx=gate_idx),
        grid=(N_TOK // TM,),
        in_specs=x_specs + _group_specs(half) + _group_specs(half) + [
            pl.BlockSpec((half, D_MODEL), lambda i: (0, 0)),
            pl.BlockSpec((half, D_MODEL), lambda i: (1, 0)),
            pl.BlockSpec((1, MOD_ROWS, D_MODEL), lambda i: (i * TM // SEG, 0, 0)),
        ],
        out_specs=pl.BlockSpec((TM, D_MODEL), lambda i: (i, 0)),
        out_shape=jax.ShapeDtypeStruct((N_TOK, D_MODEL), F32),
        compiler_params=_cparams(("parallel",)),
        name="outproj_residual",
    )(*x_args, a[0], a[1], b[0], b[1], w_bf16, w_bf16, mods)


def _conv3_rows(z, w):
    n = z.shape[0]
    row = lax.broadcasted_iota(I32, z.shape, 0)
    zm = jnp.where(row == 0, 0.0, pltpu.roll(z, 1, 0))
    zp = jnp.where(row == n - 1, 0.0, pltpu.roll(z, n - 1, 0))
    return zm * w[0:1] + z * w[1:2] + zp * w[2:3]


def _log_gamma(dec):
    return jnp.log1p(-jnp.exp(dec))


def _even_kernel(*refs, seq, latent, tq):
    (q_ref, k_ref, v_ref, g_ref, gb_ref, gc_ref, hx_ref, scw_ref, df_ref, db_ref) = refs[:10]
    if latent:
        s0_ref, ret_ref, sc_ref, mask_ref = refs[10:]
    else:
        ret_ref, sc_ref, st_ref, mask_ref = refs[10:]
    lgf = _log_gamma(df_ref[0])
    lgb = _log_gamma(db_ref[0])

    @pl.when(pl.program_id(1) == 0)
    def _():
        for r in range(seq // tq):
            t = lax.broadcasted_iota(I32, (tq, seq), 0) + r * tq
            s = lax.broadcasted_iota(I32, (tq, seq), 1)
            d = (t - s).astype(F32)
            mf = jnp.where(d >= 0, jnp.exp(jnp.maximum(d, 0.0) * lgf), 0.0)
            mb = jnp.where(d <= 0, jnp.exp(jnp.maximum(-d, 0.0) * lgb), 0.0)
            mask_ref[r * tq:(r + 1) * tq, :] = mf + mb

    k = k_ref[...] * (RET_DK ** -0.5)
    kb = k.astype(BF16)
    vb = v_ref[...].astype(BF16)
    if latent:
        s0f = s0_ref[0, 0, 0, 0].astype(BF16)
        s0b = s0_ref[0, 0, 1, 0].astype(BF16)
    for r in range(seq // tq):
        rows = slice(r * tq, (r + 1) * tq)
        qb = q_ref[rows, :].astype(BF16)
        s = lax.dot_general(qb, kb, (((1,), (1,)), ((), ())), preferred_element_type=F32)
        p = (s * mask_ref[rows, :]).astype(BF16)
        o = jnp.dot(p, vb, preferred_element_type=F32)
        if latent:
            tpos = (lax.broadcasted_iota(I32, (tq, 1), 0) + r * tq).astype(F32)
            o = o + jnp.dot(qb, s0f, preferred_element_type=F32) * jnp.exp((tpos + 1.0) * lgf)
            o = o + jnp.dot(qb, s0b, preferred_element_type=F32) * jnp.exp((seq - tpos) * lgb)
        mu = jnp.mean(o, axis=-1, keepdims=True)
        var = jnp.mean(jnp.square(o - mu), axis=-1, keepdims=True)
        on = (o - mu) * lax.rsqrt(var + GN_EPS)
        ret_ref[rows, :] = _silu(g_ref[rows, :]) * on

    if not latent:
        spos = lax.broadcasted_iota(I32, (seq, 1), 0).astype(F32)
        kf = (k * jnp.exp((seq - 1.0 - spos) * lgf)).T.astype(BF16)
        kr = (k * jnp.exp(spos * lgb)).T.astype(BF16)
        st_ref[0, 0, 0, 0] = jnp.dot(kf, vb, preferred_element_type=F32)
        st_ref[0, 0, 1, 0] = jnp.dot(kr, vb, preferred_element_type=F32)

    sc_ref[...] = gb_ref[...] * _conv3_rows(gc_ref[...] * hx_ref[...], scw_ref[...])


def _even_mixer(proj, sc_w_pad, dec_f, dec_b, state=None):
    latent = state is not None
    seq = DEC_SEQ if latent else SEQ
    nb = DEC_BATCH if latent else BATCH
    row0 = 0 if latent else N_SAMPLE // seq
    tq = min(seq, 256)
    hw = RET_DK

    def col(c0):
        return pl.BlockSpec((seq, hw), lambda h, b: (row0 + b, c0 + h))

    in_specs = [col(0), col(4), col(8), col(12), col(16), col(20), col(24),
                pl.BlockSpec((SUBLANES, hw), lambda h, b: (0, h)),
                pl.BlockSpec((1, 1, 1), lambda h, b: (h, 0, 0)),
                pl.BlockSpec((1, 1, 1), lambda h, b: (h, 0, 0))]
    args = [proj] * 7 + [sc_w_pad, dec_f.reshape(RET_HEADS, 1, 1), dec_b.reshape(RET_HEADS, 1, 1)]
    out_spec = pl.BlockSpec((seq, hw), lambda h, b: (b, h))
    out_shape = [jax.ShapeDtypeStruct((nb * seq, RET_W), F32), jax.ShapeDtypeStruct((nb * seq, SC_W), F32)]
    out_specs = [out_spec, out_spec]
    st_spec = pl.BlockSpec((1, 1, 2, 1, RET_DK, RET_DV), lambda h, b: (b, 0, 0, h, 0, 0))
    if latent:
        in_specs.append(st_spec)
        args.append(state)
    else:
        out_shape.append(jax.ShapeDtypeStruct((BATCH, 1, 2, RET_HEADS, RET_DK, RET_DV), F32))
        out_specs.append(st_spec)
    return pl.pallas_call(
        functools.partial(_even_kernel, seq=seq, latent=latent, tq=tq),
        grid=(RET_HEADS, nb),
        in_specs=in_specs,
        out_specs=out_specs,
        out_shape=out_shape,
        scratch_shapes=[pltpu.VMEM((seq, seq), F32)],
        compiler_params=_cparams(("arbitrary", "arbitrary")),
        name="even_mixer_latent" if latent else "even_mixer_context",
    )(*args)


def _dft_tables(length):
    f = np.arange(length, dtype=np.float64)[:, None]
    t = np.arange(length, dtype=np.float64)[None, :]
    ang = np.pi * ((f * t) % (2 * length)) / length
    cos, sin = np.cos(ang), np.sin(ang)
    sin[0, :] = (-1.0) ** np.arange(length)
    fwd = np.concatenate([cos, sin], axis=0)
    wgt = np.full((2 * length,), 2.0)
    wgt[0] = 1.0
    wgt[length] = 1.0
    inv = (fwd * wgt[:, None] / (2.0 * length)).T
    return fwd.astype(np.float32), np.ascontiguousarray(inv).astype(np.float32)


def _hyena_consts(length):
    t = np.linspace(0.0, 1.0, length)[:, None]
    bands = (HY_EMB - 1) // 2
    w = 2.0 * np.pi * np.arange(length)[:, None] / length
    f = np.linspace(1e-4, bands - 1, bands)[None, :]
    z = np.concatenate([t, np.cos(f * w), -np.sin(f * w)], axis=-1)
    zpad = np.zeros((length, LANES))
    zpad[:, :HY_EMB] = z
    deltas = np.abs(np.linspace(math.log(HY_TARGET) / HY_SLOW_PCT, math.log(HY_TARGET) / HY_FAST_PCT, HY_W))
    decay = np.exp(-t * deltas[None, :])
    return zpad.astype(np.float32), decay.astype(np.float32)


def _cast_kernel(x_ref, o_ref):
    o_ref[...] = x_ref[...].astype(o_ref.dtype)


def _cast_bf16(x, name):
    rows = min(x.shape[0], 512)
    return pl.pallas_call(
        _cast_kernel,
        grid=(x.shape[0] // rows,),
        in_specs=[pl.BlockSpec((rows, x.shape[1]), lambda i: (i, 0))],
        out_specs=pl.BlockSpec((rows, x.shape[1]), lambda i: (i, 0)),
        out_shape=jax.ShapeDtypeStruct(x.shape, BF16),
        compiler_params=_cparams(("parallel",)),
        name=name,
    )(x)


def _hy_filter_kernel(z_ref, dec_ref, w1_ref, b1_ref, f1_ref, w2_ref, b2_ref, f2_ref, w3f_ref, w3b_ref,
                      fwd_ref, ka_ref, ka2_ref, kb_ref, *, seq):
    h = jnp.sin(f1_ref[...] * (jnp.dot(z_ref[...], w1_ref[...], preferred_element_type=F32,
                                       precision=HIGHEST) + b1_ref[...]))
    h = jnp.sin(f2_ref[...] * (jnp.dot(h, w2_ref[...], preferred_element_type=F32,
                                       precision=HIGHEST) + b2_ref[...]))
    dec = dec_ref[...]
    hf = jnp.dot(h, w3f_ref[...], preferred_element_type=F32, precision=HIGHEST) * dec
    hb = jnp.dot(h, w3b_ref[...], preferred_element_type=F32, precision=HIGHEST) * dec
    row = lax.broadcasted_iota(I32, hb.shape, 0)
    hb = jnp.where(row == 0, 0.0, hb)
    inv_norm = 1.0 / (jnp.sum(jnp.abs(hf), axis=0, keepdims=True)
                      + jnp.sum(jnp.abs(hb), axis=0, keepdims=True) + EPS)
    p = (hf + hb) * inv_norm
    q = (hf - hb) * inv_norm
    fp = jnp.dot(fwd_ref[...], p, preferred_element_type=F32, precision=HIGHEST)
    fq = jnp.dot(fwd_ref[seq:, :], q, preferred_element_type=F32, precision=HIGHEST)
    kr = fp[:seq]
    row = lax.broadcasted_iota(I32, kr.shape, 0)
    ka_ref[0] = kr
    ka2_ref[0] = jnp.where(row == 0, fp[seq:seq + 1], kr)
    kb_ref[0] = jnp.where(row == 0, 0.0, -fq)


def _hy_filters(seq, zfeat, decay, fwd, w1, b1, f1, w2, b2, f2, w3):
    ns = HY_W // HY_SLAB
    per_o = 2 * ns
    small = lambda shape: pl.BlockSpec(shape, lambda o, s: (0, 0))
    out_spec = pl.BlockSpec((1, seq, HY_SLAB), lambda o, s: (o, 0, s))
    out = jax.ShapeDtypeStruct((HY_ORDER, seq, HY_W), F32)
    return pl.pallas_call(
        functools.partial(_hy_filter_kernel, seq=seq),
        grid=(HY_ORDER, ns),
        in_specs=[
            small((seq, LANES)),
            pl.BlockSpec((seq, HY_SLAB), lambda o, s: (0, s)),
            small((LANES, LANES)), small((1, LANES)), small((1, LANES)),
            small((LANES, LANES)), small((1, LANES)), small((1, LANES)),
            pl.BlockSpec((LANES, HY_SLAB), lambda o, s: (0, o * per_o + s)),
            pl.BlockSpec((LANES, HY_SLAB), lambda o, s: (0, o * per_o + ns + s)),
            small((2 * seq, seq)),
        ],
        out_specs=[out_spec, out_spec, out_spec],
        out_shape=[out, out, out],
        compiler_params=_cparams(("parallel", "parallel")),
        name=f"hyena_filters_{seq}",
    )(zfeat, decay, w1, b1, f1, w2, b2, f2, w3, w3, fwd)


def _hy_apply_kernel(x1_ref, x2_ref, v_ref, c1_ref, c2_ref, cv_ref, fwd_ref, inv_ref,
                     ka_ref, ka2_ref, kb_ref, bias_ref, o_ref, *, seq):
    x1 = _conv3_rows(x1_ref[...], c1_ref[...])
    x2 = _conv3_rows(x2_ref[...], c2_ref[...])
    v = _conv3_rows(v_ref[...], cv_ref[...])

    def long_conv(sig, o):
        spec = jnp.dot(fwd_ref[...], sig.astype(BF16), preferred_element_type=F32)
        xr, xs = spec[:seq], spec[seq:]
        kb = kb_ref[o]
        yr = (xr * ka_ref[o] + xs * kb).astype(BF16)
        ys = (xs * ka2_ref[o] - xr * kb).astype(BF16)
        y = jnp.dot(inv_ref[:, :seq], yr, preferred_element_type=F32)
        y = y + jnp.dot(inv_ref[:, seq:], ys, preferred_element_type=F32)
        return y + sig * bias_ref[o:o + 1]

    z = x1 * long_conv(v, 0)
    o_ref[...] = x2 * long_conv(z, 1)


def _hy_apply(proj, conv_w_pad, fwd_bf, inv_bf, filters, bias_pad, latent):
    seq = DEC_SEQ if latent else SEQ
    nb = DEC_BATCH if latent else BATCH
    row0 = 0 if latent else N_SAMPLE // seq
    ns = HY_W // HY_SLAB
    const = lambda shape: pl.BlockSpec(shape, lambda s, b: (0,) * len(shape))
    col = lambda c0: pl.BlockSpec((seq, HY_SLAB), lambda s, b: (row0 + b, c0 + s))
    cw = lambda c0: pl.BlockSpec((SUBLANES, HY_SLAB), lambda s, b: (0, c0 + s))
    filt = pl.BlockSpec((HY_ORDER, seq, HY_SLAB), lambda s, b: (0, 0, s))
    in_specs = [col(0), col(ns), col(2 * ns), cw(0), cw(ns), cw(2 * ns),
                const((2 * seq, seq)), const((seq, 2 * seq)), filt, filt, filt,
                pl.BlockSpec((SUBLANES, HY_SLAB), lambda s, b: (0, s))]
    args = [proj] * 3 + [conv_w_pad] * 3 + [fwd_bf, inv_bf, *filters, bias_pad]
    return pl.pallas_call(
        functools.partial(_hy_apply_kernel, seq=seq),
        grid=(ns, nb),
        in_specs=in_specs,
        out_specs=pl.BlockSpec((seq, HY_SLAB), lambda s, b: (b, s)),
        out_shape=jax.ShapeDtypeStruct((nb * seq, HY_W), F32),
        compiler_params=_cparams(("arbitrary", "arbitrary")),
        name="hyena_latent" if latent else "hyena_context",
    )(*args)


def _rope_tables(length):
    rows = length // GRID_W
    row = np.repeat(np.arange(rows, dtype=np.float64), GRID_W)
    colp = np.tile(np.arange(GRID_W, dtype=np.float64), rows)
    half = DA_DH // 2
    freqs = ROPE_THETA ** (-np.arange(0, half, 2, dtype=np.float64) / half)
    ang = np.concatenate([row[:, None] * freqs[None, :]] * 2 + [colp[:, None] * freqs[None, :]] * 2, axis=1)
    ang = np.concatenate([ang, ang], axis=1)
    lane = np.arange(LANES)
    sign = np.where((lane % half) < half // 2, -1.0, 1.0)[None, :]
    return np.cos(ang).astype(np.float32), (np.sin(ang) * sign).astype(np.float32)


def _rope(x, cos, sin_signed):
    q = DA_DH // 4
    lane = lax.broadcasted_iota(I32, x.shape, 1)
    partner = jnp.where((lane % (2 * q)) < q, pltpu.roll(x, LANES - q, 1), pltpu.roll(x, q, 1))
    return x * cos + partner * sin_signed


def _attn_kernel(*refs, seq, latent, lambda_init):
    q_ref, k_ref, v_ref, lam_ref, sub_ref = refs[:5]
    if latent:
        cos_ref, sin_ref, ck_ref, cv_ref, o_ref = refs[5:]
    else:
        o_ref, nk_ref, nv_ref = refs[5:]
    k = k_ref[...]
    v = v_ref[...]
    if latent:
        k = _rope(k, cos_ref[...], sin_ref[...])
        k_all = jnp.concatenate([k, jnp.concatenate([ck_ref[0, 0, 0, 0], ck_ref[0, 0, 1, 0]], axis=1)], axis=0)
        v_all = jnp.concatenate([v, cv_ref[0, 0, 0]], axis=0)
    else:
        k_all, v_all = k, v
        nk_ref[0, 0, 0, 0] = k[:, :DA_DH]
        nk_ref[0, 0, 1, 0] = k[:, DA_DH:]
        nv_ref[0, 0, 0] = v
    kb = k_all.astype(BF16)
    vb = v_all.astype(BF16)
    lv = lam_ref[...]
    lam = (jnp.exp(jnp.sum(lv[0:1] * lv[1:2], axis=1, keepdims=True))
           - jnp.exp(jnp.sum(lv[2:3] * lv[3:4], axis=1, keepdims=True)) + lambda_init)
    scale = DA_DH ** -0.5
    dn = (((1,), (1,)), ((), ()))
    tq = min(seq, ATT_TQ)
    for r in range(seq // tq):
        rows = slice(r * tq, (r + 1) * tq)
        q = q_ref[rows, :]
        if latent:
            q = _rope(q, cos_ref[rows, :], sin_ref[rows, :])
        first = lax.broadcasted_iota(I32, q.shape, 1) < DA_DH
        q1 = jnp.where(first, q, 0.0).astype(BF16)
        q2 = jnp.where(first, 0.0, q).astype(BF16)
        s1 = lax.dot_general(q1, kb, dn, preferred_element_type=F32) * scale
        s2 = lax.dot_general(q2, kb, dn, preferred_element_type=F32) * scale
        e1 = jnp.exp(s1 - jnp.max(s1, axis=-1, keepdims=True))
        e2 = jnp.exp(s2 - jnp.max(s2, axis=-1, keepdims=True))
        w = (e1 * (1.0 / jnp.sum(e1, axis=-1, keepdims=True))
             - e2 * (lam / jnp.sum(e2, axis=-1, keepdims=True)))
        o = jnp.dot(w.astype(BF16), vb, preferred_element_type=F32)
        o = o * lax.rsqrt(jnp.mean(o * o, axis=-1, keepdims=True) + EPS) * sub_ref[...]
        o_ref[rows, :] = o * (1.0 - lambda_init)


def _attention(proj, lam_vec, subln, lambda_init, latent, rope=None, ctx_k=None, ctx_v=None):
    seq = DEC_SEQ if latent else SEQ
    nb = DEC_BATCH if latent else BATCH
    row0 = 0 if latent else N_SAMPLE // seq
    hw = 2 * DA_DH
    c0 = 3 * HY_W // hw
    col = lambda c: pl.BlockSpec((seq, hw), lambda b, h: (row0 + b, c + h))
    in_specs = [col(c0), col(c0 + DA_HEADS), col(c0 + 2 * DA_HEADS),
                pl.BlockSpec((4, DA_DH), lambda b, h: (0, 0)),
                pl.BlockSpec((1, hw), lambda b, h: (0, 0))]
    args = [proj] * 3 + [lam_vec, subln.reshape(1, hw)]
    out_spec = pl.BlockSpec((seq, hw), lambda b, h: (b, h))
    out_shape = [jax.ShapeDtypeStruct((nb * seq, DA_W), F32)]
    out_specs = [out_spec]
    k_spec = pl.BlockSpec((1, 1, 2, 1, SEQ, DA_DH), lambda b, h: (b, 0, 0, h, 0, 0))
    v_spec = pl.BlockSpec((1, 1, 1, SEQ, hw), lambda b, h: (b, 0, h, 0, 0))
    if latent:
        in_specs += [pl.BlockSpec((seq, hw), lambda b, h: (0, 0))] * 2 + [k_spec, v_spec]
        args += [rope[0], rope[1], ctx_k, ctx_v]
    else:
        out_shape += [jax.ShapeDtypeStruct((BATCH, 1, 2, DA_HEADS, SEQ, DA_DH), F32),
                      jax.ShapeDtypeStruct((BATCH, 1, DA_HEADS, SEQ, hw), F32)]
        out_specs += [k_spec, v_spec]
    return pl.pallas_call(
        functools.partial(_attn_kernel, seq=seq, latent=latent, lambda_init=lambda_init),
        grid=(nb, DA_HEADS),
        in_specs=in_specs,
        out_specs=out_specs,
        out_shape=out_shape,
        compiler_params=_cparams(("arbitrary", "arbitrary")),
        name="diff_attn_latent" if latent else "diff_attn_context",
    )(*args)


HALF_D = D_MODEL // 2
HI16 = -65536


def _pack_bf16_pair(lo, hi):
    lo_bits = lax.bitcast_convert_type(lo.astype(BF16).astype(F32), I32)
    hi_bits = lax.bitcast_convert_type(hi.astype(BF16).astype(F32), I32)
    return (hi_bits & HI16) | lax.shift_right_logical(lo_bits, 16)


def _unpack_bf16_pair(p):
    lo = lax.bitcast_convert_type(lax.shift_left(p, 16), F32).astype(BF16)
    hi = lax.bitcast_convert_type(p & HI16, F32).astype(BF16)
    return lo, hi


def _router_kernel(x_ref, g_ref, mod_ref, wr_ref, br_ref, tril_ref, triu_ref,
                   u_ref, loct_ref, info_ref, grp_ref, tot_ref, carry_ref):
    @pl.when(pl.program_id(0) == 0)
    def _():
        carry_ref[...] = jnp.zeros_like(carry_ref)

    u = _norm_mod(x_ref[...], g_ref[...], mod_ref[0], 3, 4)
    u_ref[...] = _pack_bf16_pair(u[:, :HALF_D], u[:, HALF_D:])
    logits = jnp.dot(u, wr_ref[...], preferred_element_type=F32, precision=HIGHEST) + br_ref[...]
    lane = lax.broadcasted_iota(I32, logits.shape, 1)
    lane_f = lane.astype(F32)
    neg = -jnp.inf
    big = float(LANES)

    def first_max(vals):
        m = jnp.max(vals, axis=-1, keepdims=True)
        return m, jnp.min(jnp.where(vals == m, lane_f, big), axis=-1, keepdims=True)

    is_grp = lane < N_GROUPS
    gmax, gidx = first_max(jnp.where(is_grp, logits, neg))
    gsum = jnp.sum(jnp.where(is_grp, jnp.exp(logits - gmax), 0.0), axis=-1, keepdims=True)
    g_w = 1.0 / gsum
    lo = N_GROUPS + EXP_PER_GROUP * gidx
    in_grp = jnp.logical_and(lane_f >= lo, lane_f < lo + EXP_PER_GROUP)
    el = jnp.where(in_grp, logits, neg)
    v1, i1 = first_max(el)
    v2, i2 = first_max(jnp.where(lane_f == i1, neg, el))
    e = jnp.exp(v2 - v1)
    p1 = 1.0 / (1.0 + e)
    hit1 = lane_f == i1 - N_GROUPS
    hit2 = lane_f == i2 - N_GROUPS
    onehot = jnp.where(jnp.logical_or(hit1, hit2), 1.0, 0.0)
    cum = jnp.dot(tril_ref[...], onehot.astype(BF16), preferred_element_type=F32)
    n = cum[TM - 1:TM, :]
    n_pad = jnp.floor((n + (ROW_ALIGN - 1.0)) * (1.0 / ROW_ALIGN)) * ROW_ALIGN
    loc_off = jnp.dot(jnp.broadcast_to(n_pad, (SUBLANES, LANES)).astype(BF16), triu_ref[...],
                      preferred_element_type=F32)[0:1]
    base = cum - onehot + loc_off
    loc1 = jnp.sum(jnp.where(hit1, base, 0.0), axis=-1, keepdims=True)
    loc2 = jnp.sum(jnp.where(hit2, base, 0.0), axis=-1, keepdims=True)
    info = jnp.where(lane == 0, g_w * p1, jnp.where(lane == 1, g_w * (e * p1), jnp.where(lane == 2, loc1, loc2)))
    info_ref[...] = info
    loct_ref[0] = info.T[:SUBLANES]
    row = lax.broadcasted_iota(I32, (SUBLANES, LANES), 0)
    grp = jnp.where(row == 0, carry_ref[...], jnp.where(row == 1, n_pad, jnp.where(row == 2, loc_off, 0.0)))
    grp_ref[0] = grp.astype(I32)
    carry_ref[...] = carry_ref[...] + n_pad
    tot_ref[...] = jnp.broadcast_to(carry_ref[...], tot_ref.shape).astype(I32)


def _router(x, g, mods, w_route, b_route, tril, triu):
    nt = N_TOK // TM
    return pl.pallas_call(
        _router_kernel,
        grid=(nt,),
        in_specs=[
            pl.BlockSpec((TM, D_MODEL), lambda i: (i, 0)),
            pl.BlockSpec((1, D_MODEL), lambda i: (0, 0)),
            pl.BlockSpec((1, MOD_ROWS, D_MODEL), lambda i: (i * TM // SEG, 0, 0)),
            pl.BlockSpec((D_MODEL, LANES), lambda i: (0, 0)),
            pl.BlockSpec((1, LANES), lambda i: (0, 0)),
            pl.BlockSpec((TM, TM), lambda i: (0, 0)),
            pl.BlockSpec((LANES, LANES), lambda i: (0, 0)),
        ],
        out_specs=[
            pl.BlockSpec((TM, HALF_D), lambda i: (i, 0)),
            pl.BlockSpec((1, SUBLANES, TM), lambda i: (i, 0, 0)),
            pl.BlockSpec((TM, LANES), lambda i: (i, 0)),
            pl.BlockSpec((1, SUBLANES, LANES), lambda i: (i, 0, 0)),
            pl.BlockSpec((SUBLANES, LANES), lambda i: (0, 0)),
        ],
        out_shape=[
            jax.ShapeDtypeStruct((N_TOK, HALF_D), I32),
            jax.ShapeDtypeStruct((nt, SUBLANES, TM), F32),
            jax.ShapeDtypeStruct((N_TOK, LANES), F32),
            jax.ShapeDtypeStruct((nt, SUBLANES, LANES), I32),
            jax.ShapeDtypeStruct((SUBLANES, LANES), I32),
        ],
        scratch_shapes=[pltpu.VMEM((1, LANES), F32)],
        compiler_params=_cparams(("arbitrary",)),
        name="moe_router",
    )(x, g.reshape(1, D_MODEL), mods, w_route, b_route, tril, triu)


def _for_chunks(n, fn):
    for size in CHUNKS:
        @pl.when((n & size) != 0)
        def _():
            fn(n & ~(2 * size - 1), size)


def _group_copies(grp_ref, seg_ref, e, make, act):
    n = grp_ref[0, 1, e]
    src = grp_ref[0, 2, e]
    dst = seg_ref[0, e] + grp_ref[0, 0, e]

    def one(off, size):
        act(make(pl.ds(pl.multiple_of(src + off, ROW_ALIGN), size),
                 pl.ds(pl.multiple_of(dst + off, ROW_ALIGN), size)))

    _for_chunks(n, one)


def _all_experts(body):
    lax.fori_loop(0, N_EXPERTS, lambda e, c: (body(e), c)[1], 0)


def _dispatch_kernel(grp_ref, seg_ref, loct_ref, u_ref, xb_ref, buf, zbuf, sem):
    lo, hi = _unpack_bf16_pair(u_ref[...])
    row = lax.broadcasted_iota(I32, (R_LOC, TM), 0).astype(F32)
    sel = jnp.logical_or(row == loct_ref[0, 2:3, :], row == loct_ref[0, 3:4, :])
    pt = jnp.where(sel, 1.0, 0.0).astype(BF16)
    buf[...] = _pack_bf16_pair(jnp.dot(pt, lo, preferred_element_type=F32),
                               jnp.dot(pt, hi, preferred_element_type=F32))

    def make(loc, slot):
        return pltpu.make_async_copy(buf.at[loc], xb_ref.at[slot], sem.at[0])

    _all_experts(lambda e: _group_copies(grp_ref, seg_ref, e, make, lambda cp: cp.start()))
    _all_experts(lambda e: _group_copies(grp_ref, seg_ref, e, make, lambda cp: cp.wait()))

    @pl.when(pl.program_id(0) == pl.num_programs(0) - 1)
    def _():
        zbuf[...] = jnp.zeros_like(zbuf)

        def tail(e, act):
            total = seg_ref[1, e]
            dst = seg_ref[0, e] + total

            def one(off, size):
                act(pltpu.make_async_copy(zbuf.at[pl.ds(0, size)],
                                          xb_ref.at[pl.ds(pl.multiple_of(dst + off, ROW_ALIGN), size)], sem.at[1]))

            _for_chunks((-total) & (MOE_BLK - 1), one)

        _all_experts(lambda e: tail(e, lambda cp: cp.start()))
        _all_experts(lambda e: tail(e, lambda cp: cp.wait()))

        last = N_EXPERTS - 1
        used = (seg_ref[0, last] + seg_ref[1, last] + MOE_BLK - 1) // MOE_BLK

        def spare(j):
            return pltpu.make_async_copy(zbuf, xb_ref.at[pl.ds(pl.multiple_of(j * MOE_BLK, MOE_BLK), MOE_BLK)],
                                         sem.at[1])

        lax.fori_loop(used, MOE_NB, lambda j, c: (spare(j).start(), c)[1], 0)
        lax.fori_loop(used, MOE_NB, lambda j, c: (spare(j).wait(), c)[1], 0)


def _dispatch(grp, seg, loct, u):
    return pl.pallas_call(
        _dispatch_kernel,
        grid=(N_TOK // TM,),
        in_specs=[
            pl.BlockSpec((1, SUBLANES, LANES), lambda i: (i, 0, 0), memory_space=pltpu.SMEM),
            pl.BlockSpec(memory_space=pltpu.SMEM),
            pl.BlockSpec((1, SUBLANES, TM), lambda i: (i, 0, 0)),
            pl.BlockSpec((TM, HALF_D), lambda i: (i, 0)),
        ],
        out_specs=pl.BlockSpec(memory_space=pl.ANY),
        out_shape=jax.ShapeDtypeStruct((MOE_ROWS, HALF_D), I32),
        scratch_shapes=[pltpu.VMEM((R_LOC, HALF_D), I32), pltpu.VMEM((MOE_BLK, HALF_D), I32),
                        pltpu.SemaphoreType.DMA((2,))],
        compiler_params=_cparams(("arbitrary",)),
        name="moe_dispatch",
    )(grp, seg, loct, u)


def _expert_kernel(meta_ref, xb_ref, w1_ref, w3_ref, w2_ref, yb_ref, w1b, w3b, w2b):
    j = pl.program_id(0)
    flag = meta_ref[1, j]

    @pl.when(flag == 2)
    def _():
        w1b[...] = w1_ref[0, 0].astype(BF16)
        w3b[...] = w3_ref[0, 0].astype(BF16)
        w2b[...] = w2_ref[0, 0].astype(BF16)

    @pl.when(flag > 0)
    def _():
        lo, hi = _unpack_bf16_pair(xb_ref[...])

        def up(wb):
            return (jnp.dot(lo, wb[:HALF_D, :], preferred_element_type=F32)
                    + jnp.dot(hi, wb[HALF_D:, :], preferred_element_type=F32))

        h1 = up(w1b)
        h3 = up(w3b)
        hdn = (_silu(h1) * h3).astype(BF16)
        yb_ref[...] = jnp.dot(hdn, w2b[...], preferred_element_type=F32)

    @pl.when(flag == 0)
    def _():
        yb_ref[...] = jnp.zeros_like(yb_ref)


def _experts(meta, xb, w1, w3, w2, layer):
    wspec_in = pl.BlockSpec((1, 1, D_MODEL, D_EXPERT), lambda j, m: (layer, m[0, j], 0, 0))
    wspec_out = pl.BlockSpec((1, 1, D_EXPERT, D_MODEL), lambda j, m: (layer, m[0, j], 0, 0))
    rows = pl.BlockSpec((MOE_BLK, HALF_D), lambda j, m: (m[2, j], 0))
    return pl.pallas_call(
        _expert_kernel,
        grid_spec=pltpu.PrefetchScalarGridSpec(
            num_scalar_prefetch=1,
            grid=(MOE_NB,),
            in_specs=[rows, wspec_in, wspec_in, wspec_out],
            out_specs=pl.BlockSpec((MOE_BLK, D_MODEL), lambda j, m: (j, 0)),
            scratch_shapes=[pltpu.VMEM((D_MODEL, D_EXPERT), BF16), pltpu.VMEM((D_MODEL, D_EXPERT), BF16),
                            pltpu.VMEM((D_EXPERT, D_MODEL), BF16)],
        ),
        out_shape=jax.ShapeDtypeStruct((MOE_ROWS, D_MODEL), F32),
        compiler_params=_cparams(("arbitrary",)),
        name="moe_experts",
    )(meta, xb, w1, w3, w2)


def _combine_kernel(grp_ref, seg_ref, x_ref, info_ref, mod_ref, yb_ref, o_ref, ybuf, sem):
    @pl.when(pl.program_id(0) == 0)
    def _():
        ybuf[...] = jnp.zeros_like(ybuf)

    def make(loc, slot):
        return pltpu.make_async_copy(yb_ref.at[slot], ybuf.at[loc], sem.at[0])

    _all_experts(lambda e: _group_copies(grp_ref, seg_ref, e, make, lambda cp: cp.start()))
    _all_experts(lambda e: _group_copies(grp_ref, seg_ref, e, make, lambda cp: cp.wait()))
    info = info_ref[...]
    col = lax.broadcasted_iota(I32, (TM, R_LOC), 1).astype(F32)
    p = jnp.where(col == info[:, 2:3], info[:, 0:1], jnp.where(col == info[:, 3:4], info[:, 1:2], 0.0))
    y = jnp.dot(p.astype(BF16), ybuf[...].astype(BF16), preferred_element_type=F32)
    o_ref[...] = x_ref[...] + mod_ref[0][5:6] * y


def _combine(x, grp, seg, info, mods, yb):
    return pl.pallas_call(
        _combine_kernel,
        grid=(N_TOK // TM,),
        in_specs=[
            pl.BlockSpec((1, SUBLANES, LANES), lambda i: (i, 0, 0), memory_space=pltpu.SMEM),
            pl.BlockSpec(memory_space=pltpu.SMEM),
            pl.BlockSpec((TM, D_MODEL), lambda i: (i, 0)),
            pl.BlockSpec((TM, LANES), lambda i: (i, 0)),
            pl.BlockSpec((1, MOD_ROWS, D_MODEL), lambda i: (i * TM // SEG, 0, 0)),
            pl.BlockSpec(memory_space=pl.ANY),
        ],
        out_specs=pl.BlockSpec((TM, D_MODEL), lambda i: (i, 0)),
        out_shape=jax.ShapeDtypeStruct((N_TOK, D_MODEL), F32),
        scratch_shapes=[pltpu.VMEM((R_LOC, D_MODEL), F32), pltpu.SemaphoreType.DMA((1,))],
        compiler_params=_cparams(("arbitrary",)),
        name="moe_combine",
    )(grp, seg, x, info, mods, yb)


def _moe_layer(x, layer, norm2, mods, w_grp, b_grp, w_exp, b_exp, w1, w3, w2, tril, triu):
    w_route = jnp.zeros((D_MODEL, LANES), F32)
    w_route = w_route.at[:, :N_GROUPS].set(w_grp).at[:, N_GROUPS:N_GROUPS + N_EXPERTS].set(w_exp)
    b_route = jnp.zeros((1, LANES), F32)
    b_route = b_route.at[0, :N_GROUPS].set(b_grp).at[0, N_GROUPS:N_GROUPS + N_EXPERTS].set(b_exp)
    u, loct, info, grp, tot = _router(x, norm2, mods, w_route, b_route, tril, triu)

    total = tot[0, :N_EXPERTS]
    padded = (total + MOE_BLK - 1) // MOE_BLK * MOE_BLK
    pad_end = jnp.cumsum(padded)
    pad_start = pad_end - padded
    seg = jnp.zeros((2, LANES), I32).at[0, :N_EXPERTS].set(pad_start).at[1, :N_EXPERTS].set(total)
    blk_start = jnp.arange(MOE_NB, dtype=I32) * MOE_BLK
    blk_exp = jnp.minimum(jnp.sum(pad_end[None, :] <= blk_start[:, None], axis=1), N_EXPERTS - 1).astype(I32)
    used = pad_end[-1] // MOE_BLK
    active = jnp.arange(MOE_NB, dtype=I32) < used
    first = blk_start == pad_start[blk_exp]
    flag = jnp.where(active, jnp.where(first, 2, 1), 0).astype(I32)
    last_exp = blk_exp[jnp.maximum(used - 1, 0)]
    meta = jnp.stack([jnp.where(active, blk_exp, last_exp), flag,
                      jnp.minimum(jnp.arange(MOE_NB, dtype=I32), used - 1)], axis=0).astype(I32)

    xb = _dispatch(grp, seg, loct, u)
    yb = _experts(meta, xb, w1, w3, w2, layer)
    return _combine(x, grp, seg, info, mods, yb)


def _final_kernel(x_ref, g_ref, o_ref):
    x = x_ref[...]
    o_ref[...] = x * lax.rsqrt(jnp.mean(x * x, axis=-1, keepdims=True) + EPS) * g_ref[...]


def _final_norm(x, g, row0, rows):
    return pl.pallas_call(
        _final_kernel,
        grid=(rows // TM,),
        in_specs=[pl.BlockSpec((TM, D_MODEL), lambda i: (row0 // TM + i, 0)),
                  pl.BlockSpec((1, D_MODEL), lambda i: (0, 0))],
        out_specs=pl.BlockSpec((TM, D_MODEL), lambda i: (i, 0)),
        out_shape=jax.ShapeDtypeStruct((rows, D_MODEL), F32),
        compiler_params=_cparams(("parallel",)),
        name="final_norm",
    )(x, g.reshape(1, D_MODEL))


def _pad_rows(w, rows=SUBLANES):
    return jnp.zeros((rows, w.shape[1]), w.dtype).at[:w.shape[0]].set(w)


def _pad2(w, rows, cols):
    return jnp.zeros((rows, cols), w.dtype).at[:w.shape[0], :w.shape[1]].set(w)


def kernel(x_prompt, x_sample, c, c_ctx, state_ret, cache_k, cache_v, ada_w, ada_b, norm1, norm2, final_norm, ev_w_in, ev_w_out, ret_decay_fwd, ret_decay_bwd, sc_conv_w, od_w_in, od_w_out, hy_conv_w, hy_w1, hy_b1, hy_fr1, hy_w2, hy_b2, hy_fr2, hy_w3, hy_bias, da_lambda, da_subln, moe_w_grp, moe_b_grp, moe_w_exp, moe_b_exp, moe_w1, moe_w3, moe_w2):
    x = (x_sample.reshape(N_SAMPLE, D_MODEL), x_prompt.reshape(N_PROMPT, D_MODEL))

    cvecs = jnp.zeros((16, D_MODEL), F32).at[:DEC_BATCH].set(c).at[DEC_BATCH].set(c_ctx)
    m = _mods(cvecs, ada_w, ada_b).reshape(DEPTH, 16, N_MOD, D_MODEL)
    seg_mods = jnp.concatenate(
        [m[:, :DEC_BATCH], jnp.broadcast_to(m[:, DEC_BATCH:DEC_BATCH + 1], (DEPTH, N_SEG - DEC_BATCH, N_MOD, D_MODEL))],
        axis=1)
    seg_mods = jnp.pad(seg_mods, ((0, 0), (0, 0), (0, MOD_ROWS - N_MOD), (0, 0)))

    tril = jnp.asarray(np.tril(np.ones((TM, TM), np.float32)), BF16)
    triu = jnp.asarray(np.triu(np.ones((LANES, LANES), np.float32), 1), BF16)

    for l in range(DEPTH):
        mods = seg_mods[l]
        i = l // 2
        if l % 2 == 0:
            proj = _norm_mod_matmul(x, norm1[l], mods, ev_w_in[i].astype(BF16), 0, 1)
            scw = _pad_rows(sc_conv_w[i])
            ret_l, sc_l = _even_mixer(proj, scw, ret_decay_fwd[i], ret_decay_bwd[i], state=state_ret[:, i:i + 1])
            ret_c, sc_c, new_state = _even_mixer(proj, scw, ret_decay_fwd[i], ret_decay_bwd[i])
            x = _proj_residual(x, (ret_l, ret_c), (sc_l, sc_c), ev_w_out[i].astype(BF16), mods, 2)
        else:
            lambda_init = 0.8 - 0.6 * math.exp(-0.3 * l)
            proj = _norm_mod_matmul(x, norm1[l], mods, od_w_in[i].astype(BF16), 0, 1)
            conv_w = _pad_rows(hy_conv_w[i])
            bias = _pad_rows(hy_bias[i])
            w1 = _pad2(hy_w1[i], LANES, LANES)
            b1 = _pad2(hy_b1[i][None], 1, LANES)
            f1 = _pad2(hy_fr1[i][None], 1, LANES)
            w2 = _pad2(hy_w2[i], LANES, LANES)
            b2 = _pad2(hy_b2[i][None], 1, LANES)
            f2 = _pad2(hy_fr2[i][None], 1, LANES)
            w3 = _pad2(hy_w3[i], LANES, hy_w3.shape[2])
            hy = []
            for latent in (True, False):
                seq = DEC_SEQ if latent else SEQ
                fwd_np, inv_np = _dft_tables(seq)
                zfeat, decay = _hyena_consts(seq)
                fwd = jnp.asarray(fwd_np)
                filters = _hy_filters(seq, jnp.asarray(zfeat), jnp.asarray(decay), fwd, w1, b1, f1, w2, b2, f2, w3)
                fwd_bf = _cast_bf16(fwd, f"dft_fwd_bf16_{seq}")
                inv_bf = _cast_bf16(jnp.asarray(inv_np), f"dft_inv_bf16_{seq}")
                hy.append(_hy_apply(proj, conv_w, fwd_bf, inv_bf, filters, bias, latent))
            cos, sin = _rope_tables(DEC_SEQ)
            att_l = _attention(proj, da_lambda[i], da_subln[i], lambda_init, True,
                               rope=(jnp.asarray(cos), jnp.asarray(sin)),
                               ctx_k=cache_k[:, i:i + 1], ctx_v=cache_v[:, i:i + 1])[0]
            att_c, new_k, new_v = _attention(proj, da_lambda[i], da_subln[i], lambda_init, False)
            x = _proj_residual(x, hy, (att_l, att_c), od_w_out[i].astype(BF16), mods, 2)
        x = _moe_layer(x, l, norm2[l], mods, moe_w_grp[l], moe_b_grp[l], moe_w_exp[l], moe_b_exp[l],
                       moe_w1, moe_w3, moe_w2, tril, triu)

    y_sample = _final_norm(x, final_norm, 0, N_SAMPLE).reshape(DEC_BATCH, DEC_SEQ, D_MODEL)
    y_prompt = _final_norm(x, final_norm, N_SAMPLE, N_PROMPT).reshape(BATCH, SEQ, D_MODEL)
    return (y_prompt, y_sample, new_state, new_k, new_v)
```

```python
import functools
import math

import numpy as np
import jax
import jax.numpy as jnp
from jax import lax
from jax.experimental import pallas as pl
from jax.experimental.pallas import tpu as pltpu

F32 = jnp.float32
BF16 = jnp.bfloat16
I32 = jnp.int32
HIGHEST = lax.Precision.HIGHEST

D_MODEL = 1024
BATCH = 32
SEQ = 256
DEPTH = 2
DEC_BATCH = 8
DEC_SEQ = 1024
PAST_LEN = 256
GRID_W = 64
N_MOD = 6
EPS = 1e-6
GN_EPS = 1e-5

RET_DK = 128
RET_DV = 128
RET_HEADS = 4
RET_W = 512
RET_QK = 512
SC_W = 512
EVEN_IN = 2 * RET_QK + 2 * RET_W + 3 * SC_W

HY_W = 512
HY_ORDER = 2
HY_EMB = 33
HY_FFN = 64
HY_TARGET = 1e-2
HY_FAST_PCT = 0.3
HY_SLOW_PCT = 1.5
DA_DH = 64
DA_HEADS = 4
DA_W = 512
ODD_IN = 3 * HY_W + 3 * DA_W
ROPE_THETA = 10000.0

N_GROUPS = 4
EXP_PER_GROUP = 8
N_EXPERTS = 32
TOP_K = 2
D_EXPERT = 512

N_SAMPLE = DEC_BATCH * DEC_SEQ
N_PROMPT = BATCH * SEQ
N_TOK = N_SAMPLE + N_PROMPT
SEG = 1024
N_SEG = N_TOK // SEG
MOD_ROWS = 8

LANES = 128
SUBLANES = 8
VMEM_LIMIT = 56 * 1024 * 1024

TM = 256
MOE_BLK = 256
ROW_ALIGN = SUBLANES
R_LOC = (TOP_K * TM + N_EXPERTS * (ROW_ALIGN - 1) + LANES - 1) // LANES * LANES
CHUNKS = (256, 128, 64, 32, 16, 8)
MOE_ROWS = ((N_TOK * TOP_K + N_EXPERTS * (N_TOK // TM) * (ROW_ALIGN - 1) + N_EXPERTS * (MOE_BLK - 1))
            // MOE_BLK + 1) * MOE_BLK
MOE_NB = MOE_ROWS // MOE_BLK
HY_SLAB = 256
ATT_TQ = 256


def _cparams(sem, vmem=VMEM_LIMIT):
    return pltpu.CompilerParams(dimension_semantics=sem, vmem_limit_bytes=vmem)


def _silu(x):
    return x * (1.0 / (1.0 + jnp.exp(-x)))


def _split_bf16(a):
    hi = a.astype(BF16)
    return hi, (a - hi.astype(F32)).astype(BF16)


def _dot_3pass(a, b):
    ah, al = _split_bf16(a)
    bh, bl = _split_bf16(b)
    dot = functools.partial(jnp.dot, preferred_element_type=F32)
    return dot(ah, bh) + (dot(ah, bl) + dot(al, bh))


MODS_TN = 1536


def _mods_kernel(c_ref, w_ref, b_ref, o_ref):
    s = _silu(c_ref[...])
    o_ref[0] = jnp.dot(s, w_ref[0], preferred_element_type=F32, precision=HIGHEST) + b_ref[0]


def _mods(cvecs, ada_w, ada_b):
    n = N_MOD * D_MODEL
    return pl.pallas_call(
        _mods_kernel,
        grid=(DEPTH, n // MODS_TN),
        in_specs=[
            pl.BlockSpec((16, D_MODEL), lambda l, j: (0, 0)),
            pl.BlockSpec((1, D_MODEL, MODS_TN), lambda l, j: (l, 0, j)),
            pl.BlockSpec((1, 1, MODS_TN), lambda l, j: (l, 0, j)),
        ],
        out_specs=pl.BlockSpec((1, 16, MODS_TN), lambda l, j: (l, 0, j)),
        out_shape=jax.ShapeDtypeStruct((DEPTH, 16, n), F32),
        compiler_params=_cparams(("parallel", "parallel")),
        name="adaln_mods",
    )(cvecs, ada_w, ada_b.reshape(DEPTH, 1, n))


def _norm_mod(x, g, mod, shift_idx, scale_idx):
    ms = jnp.mean(x * x, axis=-1, keepdims=True)
    y = x * lax.rsqrt(ms + EPS) * g
    return y * (1.0 + mod[scale_idx:scale_idx + 1]) + mod[shift_idx:shift_idx + 1]


N_LAT_TILES = N_SAMPLE // TM


def _group_specs(width):
    lat = pl.BlockSpec((TM, width), lambda i: (jnp.minimum(i, N_LAT_TILES - 1), 0))
    ctx = pl.BlockSpec((TM, width), lambda i: (jnp.maximum(i - N_LAT_TILES, 0), 0))
    return [lat, ctx]


def _group_pick(lat_ref, ctx_ref):
    return jnp.where(pl.program_id(0) < N_LAT_TILES, lat_ref[...], ctx_ref[...])


def _rows_specs(x):
    if isinstance(x, tuple):
        return _group_specs(D_MODEL), list(x)
    return [pl.BlockSpec((TM, D_MODEL), lambda i: (i, 0))], [x]


def _rows_value(refs):
    return _group_pick(*refs) if len(refs) == 2 else refs[0][...]


def _nmm_kernel(*refs, n_x, shift_idx, scale_idx):
    g_ref, mod_ref, w_ref, o_ref = refs[n_x:]
    u = _norm_mod(_rows_value(refs[:n_x]), g_ref[...], mod_ref[0], shift_idx, scale_idx)
    o_ref[...] = jnp.dot(u.astype(BF16), w_ref[...], preferred_element_type=F32)


def _norm_mod_matmul(x, g, mods, w_bf16, shift_idx, scale_idx):
    n = w_bf16.shape[1]
    x_specs, x_args = _rows_specs(x)
    return pl.pallas_call(
        functools.partial(_nmm_kernel, n_x=len(x_args), shift_idx=shift_idx, scale_idx=scale_idx),
        grid=(N_TOK // TM,),
        in_specs=x_specs + [
            pl.BlockSpec((1, D_MODEL), lambda i: (0, 0)),
            pl.BlockSpec((1, MOD_ROWS, D_MODEL), lambda i: (i * TM // SEG, 0, 0)),
            pl.BlockSpec((D_MODEL, n), lambda i: (0, 0)),
        ],
        out_specs=pl.BlockSpec((TM, n), lambda i: (i, 0)),
        out_shape=jax.ShapeDtypeStruct((N_TOK, n), F32),
        compiler_params=_cparams(("parallel",)),
        name="norm_mod_inproj",
    )(*x_args, g.reshape(1, D_MODEL), mods, w_bf16)


def _proj_res_kernel(*refs, n_x, gate_idx):
    al_ref, ac_ref, bl_ref, bc_ref, wa_ref, wb_ref, mod_ref, o_ref = refs[n_x:]
    a = _group_pick(al_ref, ac_ref)
    b = _group_pick(bl_ref, bc_ref)
    y = jnp.dot(a.astype(BF16), wa_ref[...], preferred_element_type=F32)
    y = y + jnp.dot(b.astype(BF16), wb_ref[...], preferred_element_type=F32)
    o_ref[...] = _rows_value(refs[:n_x]) + mod_ref[0][gate_idx:gate_idx + 1] * y


def _proj_residual(x, a, b, w_bf16, mods, gate_idx):
    half = a[0].shape[1]
    x_specs, x_args = _rows_specs(x)
    return pl.pallas_call(
        functools.partial(_proj_res_kernel, n_x=len(x_args), gate_idx=gate_idx),
        grid=(N_TOK // TM,),
        in_specs=x_specs + _group_specs(half) + _group_specs(half) + [
            pl.BlockSpec((half, D_MODEL), lambda i: (0, 0)),
            pl.BlockSpec((half, D_MODEL), lambda i: (1, 0)),
            pl.BlockSpec((1, MOD_ROWS, D_MODEL), lambda i: (i * TM // SEG, 0, 0)),
        ],
        out_specs=pl.BlockSpec((TM, D_MODEL), lambda i: (i, 0)),
        out_shape=jax.ShapeDtypeStruct((N_TOK, D_MODEL), F32),
        compiler_params=_cparams(("parallel",)),
        name="outproj_residual",
    )(*x_args, a[0], a[1], b[0], b[1], w_bf16, w_bf16, mods)


def _conv3_rows(z, w):
    n = z.shape[0]
    row = lax.broadcasted_iota(I32, z.shape, 0)
    zm = jnp.where(row == 0, 0.0, pltpu.roll(z, 1, 0))
    zp = jnp.where(row == n - 1, 0.0, pltpu.roll(z, n - 1, 0))
    return zm * w[0:1] + z * w[1:2] + zp * w[2:3]


def _log_gamma(dec):
    return jnp.log1p(-jnp.exp(dec))


def _even_kernel(*refs, seq, latent, tq):
    (q_ref, k_ref, v_ref, g_ref, gb_ref, gc_ref, hx_ref, scw_ref, df_ref, db_ref) = refs[:10]
    if latent:
        s0_ref, ret_ref, sc_ref, mask_ref = refs[10:]
    else:
        ret_ref, sc_ref, st_ref, mask_ref = refs[10:]
    lgf = _log_gamma(df_ref[0])
    lgb = _log_gamma(db_ref[0])

    @pl.when(pl.program_id(1) == 0)
    def _():
        for r in range(seq // tq):
            t = lax.broadcasted_iota(I32, (tq, seq), 0) + r * tq
            s = lax.broadcasted_iota(I32, (tq, seq), 1)
            d = (t - s).astype(F32)
            mf = jnp.where(d >= 0, jnp.exp(jnp.maximum(d, 0.0) * lgf), 0.0)
            mb = jnp.where(d <= 0, jnp.exp(jnp.maximum(-d, 0.0) * lgb), 0.0)
            mask_ref[r * tq:(r + 1) * tq, :] = mf + mb

    k = k_ref[...] * (RET_DK ** -0.5)
    kb = k.astype(BF16)
    vb = v_ref[...].astype(BF16)
    if latent:
        s0f = s0_ref[0, 0, 0, 0].astype(BF16)
        s0b = s0_ref[0, 0, 1, 0].astype(BF16)
    for r in range(seq // tq):
        rows = slice(r * tq, (r + 1) * tq)
        qb = q_ref[rows, :].astype(BF16)
        s = lax.dot_general(qb, kb, (((1,), (1,)), ((), ())), preferred_element_type=F32)
        p = (s * mask_ref[rows, :]).astype(BF16)
        o = jnp.dot(p, vb, preferred_element_type=F32)
        if latent:
            tpos = (lax.broadcasted_iota(I32, (tq, 1), 0) + r * tq).astype(F32)
            o = o + jnp.dot(qb, s0f, preferred_element_type=F32) * jnp.exp((tpos + 1.0) * lgf)
            o = o + jnp.dot(qb, s0b, preferred_element_type=F32) * jnp.exp((seq - tpos) * lgb)
        mu = jnp.mean(o, axis=-1, keepdims=True)
        var = jnp.mean(jnp.square(o - mu), axis=-1, keepdims=True)
        on = (o - mu) * lax.rsqrt(var + GN_EPS)
        ret_ref[rows, :] = _silu(g_ref[rows, :]) * on

    if not latent:
        spos = lax.broadcasted_iota(I32, (seq, 1), 0).astype(F32)
        kf = (k * jnp.exp((seq - 1.0 - spos) * lgf)).T.astype(BF16)
        kr = (k * jnp.exp(spos * lgb)).T.astype(BF16)
        st_ref[0, 0, 0, 0] = jnp.dot(kf, vb, preferred_element_type=F32)
        st_ref[0, 0, 1, 0] = jnp.dot(kr, vb, preferred_element_type=F32)

    sc_ref[...] = gb_ref[...] * _conv3_rows(gc_ref[...] * hx_ref[...], scw_ref[...])


def _even_mixer(proj, sc_w_pad, dec_f, dec_b, state=None):
    latent = state is not None
    seq = DEC_SEQ if latent else SEQ
    nb = DEC_BATCH if latent else BATCH
    row0 = 0 if latent else N_SAMPLE // seq
    tq = min(seq, 256)
    hw = RET_DK

    def col(c0):
        return pl.BlockSpec((seq, hw), lambda h, b: (row0 + b, c0 + h))

    in_specs = [col(0), col(4), col(8), col(12), col(16), col(20), col(24),
                pl.BlockSpec((SUBLANES, hw), lambda h, b: (0, h)),
                pl.BlockSpec((1, 1, 1), lambda h, b: (h, 0, 0)),
                pl.BlockSpec((1, 1, 1), lambda h, b: (h, 0, 0))]
    args = [proj] * 7 + [sc_w_pad, dec_f.reshape(RET_HEADS, 1, 1), dec_b.reshape(RET_HEADS, 1, 1)]
    out_spec = pl.BlockSpec((seq, hw), lambda h, b: (b, h))
    out_shape = [jax.ShapeDtypeStruct((nb * seq, RET_W), F32), jax.ShapeDtypeStruct((nb * seq, SC_W), F32)]
    out_specs = [out_spec, out_spec]
    st_spec = pl.BlockSpec((1, 1, 2, 1, RET_DK, RET_DV), lambda h, b: (b, 0, 0, h, 0, 0))
    if latent:
        in_specs.append(st_spec)
        args.append(state)
    else:
        out_shape.append(jax.ShapeDtypeStruct((BATCH, 1, 2, RET_HEADS, RET_DK, RET_DV), F32))
        out_specs.append(st_spec)
    return pl.pallas_call(
        functools.partial(_even_kernel, seq=seq, latent=latent, tq=tq),
        grid=(RET_HEADS, nb),
        in_specs=in_specs,
        out_specs=out_specs,
        out_shape=out_shape,
        scratch_shapes=[pltpu.VMEM((seq, seq), F32)],
        compiler_params=_cparams(("arbitrary", "arbitrary")),
        name="even_mixer_latent" if latent else "even_mixer_context",
    )(*args)


def _dft_tables(length):
    f = np.arange(length, dtype=np.float64)[:, None]
    t = np.arange(length, dtype=np.float64)[None, :]
    ang = np.pi * ((f * t) % (2 * length)) / length
    cos, sin = np.cos(ang), np.sin(ang)
    sin[0, :] = (-1.0) ** np.arange(length)
    fwd = np.concatenate([cos, sin], axis=0)
    wgt = np.full((2 * length,), 2.0)
    wgt[0] = 1.0
    wgt[length] = 1.0
    inv = (fwd * wgt[:, None] / (2.0 * length)).T
    return fwd.astype(np.float32), np.ascontiguousarray(inv).astype(np.float32)


def _hyena_consts(length):
    t = np.linspace(0.0, 1.0, length)[:, None]
    bands = (HY_EMB - 1) // 2
    w = 2.0 * np.pi * np.arange(length)[:, None] / length
    f = np.linspace(1e-4, bands - 1, bands)[None, :]
    z = np.concatenate([t, np.cos(f * w), -np.sin(f * w)], axis=-1)
    zpad = np.zeros((length, LANES))
    zpad[:, :HY_EMB] = z
    deltas = np.abs(np.linspace(math.log(HY_TARGET) / HY_SLOW_PCT, math.log(HY_TARGET) / HY_FAST_PCT, HY_W))
    decay = np.exp(-t * deltas[None, :])
    return zpad.astype(np.float32), decay.astype(np.float32)


def _cast_kernel(x_ref, o_ref):
    o_ref[...] = x_ref[...].astype(o_ref.dtype)


def _cast_bf16(x, name):
    rows = min(x.shape[0], 512)
    return pl.pallas_call(
        _cast_kernel,
        grid=(x.shape[0] // rows,),
        in_specs=[pl.BlockSpec((rows, x.shape[1]), lambda i: (i, 0))],
        out_specs=pl.BlockSpec((rows, x.shape[1]), lambda i: (i, 0)),
        out_shape=jax.ShapeDtypeStruct(x.shape, BF16),
        compiler_params=_cparams(("parallel",)),
        name=name,
    )(x)


def _hy_filter_kernel(z_ref, dec_ref, w1_ref, b1_ref, f1_ref, w2_ref, b2_ref, f2_ref, w3f_ref, w3b_ref,
                      fwd_ref, ka_ref, ka2_ref, kb_ref, *, seq):
    h = jnp.sin(f1_ref[...] * (jnp.dot(z_ref[...], w1_ref[...], preferred_element_type=F32,
                                       precision=HIGHEST) + b1_ref[...]))
    h = jnp.sin(f2_ref[...] * (jnp.dot(h, w2_ref[...], preferred_element_type=F32,
                                       precision=HIGHEST) + b2_ref[...]))
    dec = dec_ref[...]
    hf = jnp.dot(h, w3f_ref[...], preferred_element_type=F32, precision=HIGHEST) * dec
    hb = jnp.dot(h, w3b_ref[...], preferred_element_type=F32, precision=HIGHEST) * dec
    row = lax.broadcasted_iota(I32, hb.shape, 0)
    hb = jnp.where(row == 0, 0.0, hb)
    inv_norm = 1.0 / (jnp.sum(jnp.abs(hf), axis=0, keepdims=True)
                      + jnp.sum(jnp.abs(hb), axis=0, keepdims=True) + EPS)
    p = (hf + hb) * inv_norm
    q = (hf - hb) * inv_norm
    fp = _dot_3pass(fwd_ref[...], p)
    fq = _dot_3pass(fwd_ref[seq:, :], q)
    kr = fp[:seq]
    row = lax.broadcasted_iota(I32, kr.shape, 0)
    ka_ref[0] = kr
    ka2_ref[0] = jnp.where(row == 0, fp[seq:seq + 1], kr)
    kb_ref[0] = jnp.where(row == 0, 0.0, -fq)


def _hy_filters(seq, zfeat, decay, fwd, w1, b1, f1, w2, b2, f2, w3):
    ns = HY_W // HY_SLAB
    per_o = 2 * ns
    small = lambda shape: pl.BlockSpec(shape, lambda o, s: (0, 0))
    out_spec = pl.BlockSpec((1, seq, HY_SLAB), lambda o, s: (o, 0, s))
    out = jax.ShapeDtypeStruct((HY_ORDER, seq, HY_W), F32)
    return pl.pallas_call(
        functools.partial(_hy_filter_kernel, seq=seq),
        grid=(HY_ORDER, ns),
        in_specs=[
            small((seq, LANES)),
            pl.BlockSpec((seq, HY_SLAB), lambda o, s: (0, s)),
            small((LANES, LANES)), small((1, LANES)), small((1, LANES)),
            small((LANES, LANES)), small((1, LANES)), small((1, LANES)),
            pl.BlockSpec((LANES, HY_SLAB), lambda o, s: (0, o * per_o + s)),
            pl.BlockSpec((LANES, HY_SLAB), lambda o, s: (0, o * per_o + ns + s)),
            small((2 * seq, seq)),
        ],
        out_specs=[out_spec, out_spec, out_spec],
        out_shape=[out, out, out],
        compiler_params=_cparams(("parallel", "parallel")),
        name=f"hyena_filters_{seq}",
    )(zfeat, decay, w1, b1, f1, w2, b2, f2, w3, w3, fwd)


def _hy_apply_kernel(x1_ref, x2_ref, v_ref, c1_ref, c2_ref, cv_ref, fwd_ref, inv_ref,
                     ka_ref, ka2_ref, kb_ref, bias_ref, o_ref, *, seq):
    x1 = _conv3_rows(x1_ref[...], c1_ref[...])
    x2 = _conv3_rows(x2_ref[...], c2_ref[...])
    v = _conv3_rows(v_ref[...], cv_ref[...])

    def long_conv(sig, o):
        spec = jnp.dot(fwd_ref[...], sig.astype(BF16), preferred_element_type=F32)
        xr, xs = spec[:seq], spec[seq:]
        kb = kb_ref[o]
        yr = (xr * ka_ref[o] + xs * kb).astype(BF16)
        ys = (xs * ka2_ref[o] - xr * kb).astype(BF16)
        y = jnp.dot(inv_ref[:, :seq], yr, preferred_element_type=F32)
        y = y + jnp.dot(inv_ref[:, seq:], ys, preferred_element_type=F32)
        return y + sig * bias_ref[o:o + 1]

    z = x1 * long_conv(v, 0)
    o_ref[...] = x2 * long_conv(z, 1)


def _hy_apply(proj, conv_w_pad, fwd_bf, inv_bf, filters, bias_pad, latent):
    seq = DEC_SEQ if latent else SEQ
    nb = DEC_BATCH if latent else BATCH
    row0 = 0 if latent else N_SAMPLE // seq
    ns = HY_W // HY_SLAB
    const = lambda shape: pl.BlockSpec(shape, lambda s, b: (0,) * len(shape))
    col = lambda c0: pl.BlockSpec((seq, HY_SLAB), lambda s, b: (row0 + b, c0 + s))
    cw = lambda c0: pl.BlockSpec((SUBLANES, HY_SLAB), lambda s, b: (0, c0 + s))
    filt = pl.BlockSpec((HY_ORDER, seq, HY_SLAB), lambda s, b: (0, 0, s))
    in_specs = [col(0), col(ns), col(2 * ns), cw(0), cw(ns), cw(2 * ns),
                const((2 * seq, seq)), const((seq, 2 * seq)), filt, filt, filt,
                pl.BlockSpec((SUBLANES, HY_SLAB), lambda s, b: (0, s))]
    args = [proj] * 3 + [conv_w_pad] * 3 + [fwd_bf, inv_bf, *filters, bias_pad]
    return pl.pallas_call(
        functools.partial(_hy_apply_kernel, seq=seq),
        grid=(ns, nb),
        in_specs=in_specs,
        out_specs=pl.BlockSpec((seq, HY_SLAB), lambda s, b: (b, s)),
        out_shape=jax.ShapeDtypeStruct((nb * seq, HY_W), F32),
        compiler_params=_cparams(("arbitrary", "arbitrary")),
        name="hyena_latent" if latent else "hyena_context",
    )(*args)


def _rope_tables(length):
    rows = length // GRID_W
    row = np.repeat(np.arange(rows, dtype=np.float64), GRID_W)
    colp = np.tile(np.arange(GRID_W, dtype=np.float64), rows)
    half = DA_DH // 2
    freqs = ROPE_THETA ** (-np.arange(0, half, 2, dtype=np.float64) / half)
    ang = np.concatenate([row[:, None] * freqs[None, :]] * 2 + [colp[:, None] * freqs[None, :]] * 2, axis=1)
    ang = np.concatenate([ang, ang], axis=1)
    lane = np.arange(LANES)
    sign = np.where((lane % half) < half // 2, -1.0, 1.0)[None, :]
    return np.cos(ang).astype(np.float32), (np.sin(ang) * sign).astype(np.float32)


def _rope(x, cos, sin_signed):
    q = DA_DH // 4
    lane = lax.broadcasted_iota(I32, x.shape, 1)
    partner = jnp.where((lane % (2 * q)) < q, pltpu.roll(x, LANES - q, 1), pltpu.roll(x, q, 1))
    return x * cos + partner * sin_signed


def _attn_kernel(*refs, seq, latent, lambda_init):
    q_ref, k_ref, v_ref, lam_ref, sub_ref = refs[:5]
    if latent:
        cos_ref, sin_ref, ck_ref, cv_ref, o_ref = refs[5:]
    else:
        o_ref, nk_ref, nv_ref = refs[5:]
    k = k_ref[...]
    v = v_ref[...]
    if latent:
        k = _rope(k, cos_ref[...], sin_ref[...])
        k_all = jnp.concatenate([k, jnp.concatenate([ck_ref[0, 0, 0, 0], ck_ref[0, 0, 1, 0]], axis=1)], axis=0)
        v_all = jnp.concatenate([v, cv_ref[0, 0, 0]], axis=0)
    else:
        k_all, v_all = k, v
        nk_ref[0, 0, 0, 0] = k[:, :DA_DH]
        nk_ref[0, 0, 1, 0] = k[:, DA_DH:]
        nv_ref[0, 0, 0] = v
    kb = k_all.astype(BF16)
    vb = v_all.astype(BF16)
    lv = lam_ref[...]
    lam = (jnp.exp(jnp.sum(lv[0:1] * lv[1:2], axis=1, keepdims=True))
           - jnp.exp(jnp.sum(lv[2:3] * lv[3:4], axis=1, keepdims=True)) + lambda_init)
    scale = DA_DH ** -0.5
    dn = (((1,), (1,)), ((), ()))
    tq = min(seq, ATT_TQ)
    for r in range(seq // tq):
        rows = slice(r * tq, (r + 1) * tq)
        q = q_ref[rows, :]
        if latent:
            q = _rope(q, cos_ref[rows, :], sin_ref[rows, :])
        first = lax.broadcasted_iota(I32, q.shape, 1) < DA_DH
        q1 = jnp.where(first, q, 0.0).astype(BF16)
        q2 = jnp.where(first, 0.0, q).astype(BF16)
        s1 = lax.dot_general(q1, kb, dn, preferred_element_type=F32) * scale
        s2 = lax.dot_general(q2, kb, dn, preferred_element_type=F32) * scale
        e1 = jnp.exp(s1 - jnp.max(s1, axis=-1, keepdims=True))
        e2 = jnp.exp(s2 - jnp.max(s2, axis=-1, keepdims=True))
        w = (e1 * (1.0 / jnp.sum(e1, axis=-1, keepdims=True))
             - e2 * (lam / jnp.sum(e2, axis=-1, keepdims=True)))
        o = jnp.dot(w.astype(BF16), vb, preferred_element_type=F32)
        o = o * lax.rsqrt(jnp.mean(o * o, axis=-1, keepdims=True) + EPS) * sub_ref[...]
        o_ref[rows, :] = o * (1.0 - lambda_init)


def _attention(proj, lam_vec, subln, lambda_init, latent, rope=None, ctx_k=None, ctx_v=None):
    seq = DEC_SEQ if latent else SEQ
    nb = DEC_BATCH if latent else BATCH
    row0 = 0 if latent else N_SAMPLE // seq
    hw = 2 * DA_DH
    c0 = 3 * HY_W // hw
    col = lambda c: pl.BlockSpec((seq, hw), lambda b, h: (row0 + b, c + h))
    in_specs = [col(c0), col(c0 + DA_HEADS), col(c0 + 2 * DA_HEADS),
                pl.BlockSpec((4, DA_DH), lambda b, h: (0, 0)),
                pl.BlockSpec((1, hw), lambda b, h: (0, 0))]
    args = [proj] * 3 + [lam_vec, subln.reshape(1, hw)]
    out_spec = pl.BlockSpec((seq, hw), lambda b, h: (b, h))
    out_shape = [jax.ShapeDtypeStruct((nb * seq, DA_W), F32)]
    out_specs = [out_spec]
    k_spec = pl.BlockSpec((1, 1, 2, 1, SEQ, DA_DH), lambda b, h: (b, 0, 0, h, 0, 0))
    v_spec = pl.BlockSpec((1, 1, 1, SEQ, hw), lambda b, h: (b, 0, h, 0, 0))
    if latent:
        in_specs += [pl.BlockSpec((seq, hw), lambda b, h: (0, 0))] * 2 + [k_spec, v_spec]
        args += [rope[0], rope[1], ctx_k, ctx_v]
    else:
        out_shape += [jax.ShapeDtypeStruct((BATCH, 1, 2, DA_HEADS, SEQ, DA_DH), F32),
                      jax.ShapeDtypeStruct((BATCH, 1, DA_HEADS, SEQ, hw), F32)]
        out_specs += [k_spec, v_spec]
    return pl.pallas_call(
        functools.partial(_attn_kernel, seq=seq, latent=latent, lambda_init=lambda_init),
        grid=(nb, DA_HEADS),
        in_specs=in_specs,
        out_specs=out_specs,
        out_shape=out_shape,
        compiler_params=_cparams(("arbitrary", "arbitrary")),
        name="diff_attn_latent" if latent else "diff_attn_context",
    )(*args)


HALF_D = D_MODEL // 2
HI16 = -65536


def _pack_bf16_pair(lo, hi):
    lo_bits = lax.bitcast_convert_type(lo.astype(BF16).astype(F32), I32)
    hi_bits = lax.bitcast_convert_type(hi.astype(BF16).astype(F32), I32)
    return (hi_bits & HI16) | lax.shift_right_logical(lo_bits, 16)


def _unpack_bf16_pair(p):
    lo = lax.bitcast_convert_type(lax.shift_left(p, 16), F32).astype(BF16)
    hi = lax.bitcast_convert_type(p & HI16, F32).astype(BF16)
    return lo, hi


def _router_kernel(x_ref, g_ref, mod_ref, wr_ref, br_ref, tril_ref, triu_ref,
                   u_ref, loct_ref, info_ref, grp_ref, tot_ref, carry_ref):
    @pl.when(pl.program_id(0) == 0)
    def _():
        carry_ref[...] = jnp.zeros_like(carry_ref)

    u = _norm_mod(x_ref[...], g_ref[...], mod_ref[0], 3, 4)
    u_ref[...] = _pack_bf16_pair(u[:, :HALF_D], u[:, HALF_D:])
    logits = _dot_3pass(u, wr_ref[...]) + br_ref[...]
    lane = lax.broadcasted_iota(I32, logits.shape, 1)
    lane_f = lane.astype(F32)
    neg = -jnp.inf
    big = float(LANES)

    def first_max(vals):
        m = jnp.max(vals, axis=-1, keepdims=True)
        return m, jnp.min(jnp.where(vals == m, lane_f, big), axis=-1, keepdims=True)

    is_grp = lane < N_GROUPS
    gmax, gidx = first_max(jnp.where(is_grp, logits, neg))
    gsum = jnp.sum(jnp.where(is_grp, jnp.exp(logits - gmax), 0.0), axis=-1, keepdims=True)
    g_w = 1.0 / gsum
    lo = N_GROUPS + EXP_PER_GROUP * gidx
    in_grp = jnp.logical_and(lane_f >= lo, lane_f < lo + EXP_PER_GROUP)
    el = jnp.where(in_grp, logits, neg)
    v1, i1 = first_max(el)
    v2, i2 = first_max(jnp.where(lane_f == i1, neg, el))
    e = jnp.exp(v2 - v1)
    p1 = 1.0 / (1.0 + e)
    hit1 = lane_f == i1 - N_GROUPS
    hit2 = lane_f == i2 - N_GROUPS
    onehot = jnp.where(jnp.logical_or(hit1, hit2), 1.0, 0.0)
    cum = jnp.dot(tril_ref[...], onehot.astype(BF16), preferred_element_type=F32)
    n = cum[TM - 1:TM, :]
    n_pad = jnp.floor((n + (ROW_ALIGN - 1.0)) * (1.0 / ROW_ALIGN)) * ROW_ALIGN
    loc_off = jnp.dot(jnp.broadcast_to(n_pad, (SUBLANES, LANES)).astype(BF16), triu_ref[...],
                      preferred_element_type=F32)[0:1]
    base = cum - onehot + loc_off
    loc1 = jnp.sum(jnp.where(hit1, base, 0.0), axis=-1, keepdims=True)
    loc2 = jnp.sum(jnp.where(hit2, base, 0.0), axis=-1, keepdims=True)
    info = jnp.where(lane == 0, g_w * p1, jnp.where(lane == 1, g_w * (e * p1), jnp.where(lane == 2, loc1, loc2)))
    info_ref[...] = info
    loct_ref[0] = info.T[:SUBLANES]
    row = lax.broadcasted_iota(I32, (SUBLANES, LANES), 0)
    grp = jnp.where(row == 0, carry_ref[...], jnp.where(row == 1, n_pad, jnp.where(row == 2, loc_off, 0.0)))
    grp_ref[0] = grp.astype(I32)
    carry_ref[...] = carry_ref[...] + n_pad
    tot_ref[...] = jnp.broadcast_to(carry_ref[...], tot_ref.shape).astype(I32)


def _router(x, g, mods, w_route, b_route, tril, triu):
    nt = N_TOK // TM
    return pl.pallas_call(
        _router_kernel,
        grid=(nt,),
        in_specs=[
            pl.BlockSpec((TM, D_MODEL), lambda i: (i, 0)),
            pl.BlockSpec((1, D_MODEL), lambda i: (0, 0)),
            pl.BlockSpec((1, MOD_ROWS, D_MODEL), lambda i: (i * TM // SEG, 0, 0)),
            pl.BlockSpec((D_MODEL, LANES), lambda i: (0, 0)),
            pl.BlockSpec((1, LANES), lambda i: (0, 0)),
            pl.BlockSpec((TM, TM), lambda i: (0, 0)),
            pl.BlockSpec((LANES, LANES), lambda i: (0, 0)),
        ],
        out_specs=[
            pl.BlockSpec((TM, HALF_D), lambda i: (i, 0)),
            pl.BlockSpec((1, SUBLANES, TM), lambda i: (i, 0, 0)),
            pl.BlockSpec((TM, LANES), lambda i: (i, 0)),
            pl.BlockSpec((1, SUBLANES, LANES), lambda i: (i, 0, 0)),
            pl.BlockSpec((SUBLANES, LANES), lambda i: (0, 0)),
        ],
        out_shape=[
            jax.ShapeDtypeStruct((N_TOK, HALF_D), I32),
            jax.ShapeDtypeStruct((nt, SUBLANES, TM), F32),
            jax.ShapeDtypeStruct((N_TOK, LANES), F32),
            jax.ShapeDtypeStruct((nt, SUBLANES, LANES), I32),
            jax.ShapeDtypeStruct((SUBLANES, LANES), I32),
        ],
        scratch_shapes=[pltpu.VMEM((1, LANES), F32)],
        compiler_params=_cparams(("arbitrary",)),
        name="moe_router",
    )(x, g.reshape(1, D_MODEL), mods, w_route, b_route, tril, triu)


def _for_chunks(n, fn):
    for size in CHUNKS:
        @pl.when((n & size) != 0)
        def _():
            fn(n & ~(2 * size - 1), size)


def _group_copies(grp_ref, seg_ref, e, make, act):
    n = grp_ref[0, 1, e]
    src = grp_ref[0, 2, e]
    dst = seg_ref[0, e] + grp_ref[0, 0, e]

    def one(off, size):
        act(make(pl.ds(pl.multiple_of(src + off, ROW_ALIGN), size),
                 pl.ds(pl.multiple_of(dst + off, ROW_ALIGN), size)))

    _for_chunks(n, one)


def _all_experts(body):
    lax.fori_loop(0, N_EXPERTS, lambda e, c: (body(e), c)[1], 0)


def _dispatch_kernel(grp_ref, grp1_ref, grp2_ref, seg_ref, loct_ref, u_ref, xb_ref, buf, zbuf, sem):
    i = pl.program_id(0)
    last = pl.num_programs(0) - 1
    cur = i % 2

    def copies(g_ref, b, act):
        def make(loc, slot):
            return pltpu.make_async_copy(buf.at[b, loc], xb_ref.at[slot], sem.at[b])
        _all_experts(lambda e: _group_copies(g_ref, seg_ref, e, make, act))

    @pl.when(i >= 2)
    def _():
        copies(grp2_ref, cur, lambda cp: cp.wait())

    lo, hi = _unpack_bf16_pair(u_ref[...])
    row = lax.broadcasted_iota(I32, (R_LOC, TM), 0).astype(F32)
    sel = jnp.logical_or(row == loct_ref[0, 2:3, :], row == loct_ref[0, 3:4, :])
    pt = jnp.where(sel, 1.0, 0.0).astype(BF16)
    buf[cur] = _pack_bf16_pair(jnp.dot(pt, lo, preferred_element_type=F32),
                               jnp.dot(pt, hi, preferred_element_type=F32))
    copies(grp_ref, cur, lambda cp: cp.start())

    @pl.when(i == last)
    def _():
        @pl.when(i >= 1)
        def _():
            copies(grp1_ref, 1 - cur, lambda cp: cp.wait())

        copies(grp_ref, cur, lambda cp: cp.wait())

        zbuf[...] = jnp.zeros_like(zbuf)

        def tail(e, act):
            total = seg_ref[1, e]
            dst = seg_ref[0, e] + total

            def one(off, size):
                act(pltpu.make_async_copy(zbuf.at[pl.ds(0, size)],
                                          xb_ref.at[pl.ds(pl.multiple_of(dst + off, ROW_ALIGN), size)], sem.at[2]))

            _for_chunks((-total) & (MOE_BLK - 1), one)

        _all_experts(lambda e: tail(e, lambda cp: cp.start()))
        _all_experts(lambda e: tail(e, lambda cp: cp.wait()))

        e_last = N_EXPERTS - 1
        used = (seg_ref[0, e_last] + seg_ref[1, e_last] + MOE_BLK - 1) // MOE_BLK

        def spare(j):
            return pltpu.make_async_copy(zbuf, xb_ref.at[pl.ds(pl.multiple_of(j * MOE_BLK, MOE_BLK), MOE_BLK)],
                                         sem.at[2])

        lax.fori_loop(used, MOE_NB, lambda j, c: (spare(j).start(), c)[1], 0)
        lax.fori_loop(used, MOE_NB, lambda j, c: (spare(j).wait(), c)[1], 0)


def _grp_spec(back):
    return pl.BlockSpec((1, SUBLANES, LANES), lambda i: (jnp.maximum(i - back, 0), 0, 0), memory_space=pltpu.SMEM)


def _dispatch(grp, seg, loct, u):
    return pl.pallas_call(
        _dispatch_kernel,
        grid=(N_TOK // TM,),
        in_specs=[
            _grp_spec(0), _grp_spec(1), _grp_spec(2),
            pl.BlockSpec(memory_space=pltpu.SMEM),
            pl.BlockSpec((1, SUBLANES, TM), lambda i: (i, 0, 0)),
            pl.BlockSpec((TM, HALF_D), lambda i: (i, 0)),
        ],
        out_specs=pl.BlockSpec(memory_space=pl.ANY),
        out_shape=jax.ShapeDtypeStruct((MOE_ROWS, HALF_D), I32),
        scratch_shapes=[pltpu.VMEM((2, R_LOC, HALF_D), I32), pltpu.VMEM((MOE_BLK, HALF_D), I32),
                        pltpu.SemaphoreType.DMA((3,))],
        compiler_params=_cparams(("arbitrary",)),
        name="moe_dispatch",
    )(grp, grp, grp, seg, loct, u)


META_EXPERT, META_FLAG, META_BLOCK, META_NEXT, META_SLOT = range(5)


def _expert_kernel(meta_ref, xb_ref, w1_hbm, w3_hbm, w2_hbm, yb_ref, w_in, w_out, w1b, w3b, w2b, sem, *, layer):
    j = pl.program_id(0)
    flag = meta_ref[META_FLAG, j]
    expert = meta_ref[META_EXPERT, j]
    slot = meta_ref[META_SLOT, j]

    def fetch(e, s):
        return (pltpu.make_async_copy(w1_hbm.at[layer, e], w_in.at[s, 0], sem.at[s, 0]),
                pltpu.make_async_copy(w3_hbm.at[layer, e], w_in.at[s, 1], sem.at[s, 1]),
                pltpu.make_async_copy(w2_hbm.at[layer, e], w_out.at[s], sem.at[s, 2]))

    @pl.when(j == 0)
    def _():
        for cp in fetch(expert, slot):
            cp.start()

    @pl.when(flag == 2)
    def _():
        for cp in fetch(expert, slot):
            cp.wait()
        nxt = meta_ref[META_NEXT, j]

        @pl.when(nxt >= 0)
        def _():
            for cp in fetch(nxt, 1 - slot):
                cp.start()

        w1b[...] = w_in[slot, 0].astype(BF16)
        w3b[...] = w_in[slot, 1].astype(BF16)
        w2b[...] = w_out[slot].astype(BF16)

    @pl.when(flag > 0)
    def _():
        lo, hi = _unpack_bf16_pair(xb_ref[...])

        def up(wb):
            return (jnp.dot(lo, wb[:HALF_D, :], preferred_element_type=F32)
                    + jnp.dot(hi, wb[HALF_D:, :], preferred_element_type=F32))

        h1 = up(w1b)
        h3 = up(w3b)
        hdn = (_silu(h1) * h3).astype(BF16)
        y = jnp.dot(hdn, w2b[...], preferred_element_type=F32)
        yb_ref[...] = _pack_bf16_pair(y[:, :HALF_D], y[:, HALF_D:])

    @pl.when(flag == 0)
    def _():
        yb_ref[...] = jnp.zeros_like(yb_ref)


def _experts(meta, xb, w1, w3, w2, layer):
    hbm = pl.BlockSpec(memory_space=pl.ANY)
    return pl.pallas_call(
        functools.partial(_expert_kernel, layer=layer),
        grid_spec=pltpu.PrefetchScalarGridSpec(
            num_scalar_prefetch=1,
            grid=(MOE_NB,),
            in_specs=[pl.BlockSpec((MOE_BLK, HALF_D), lambda j, m: (m[META_BLOCK, j], 0)), hbm, hbm, hbm],
            out_specs=pl.BlockSpec((MOE_BLK, HALF_D), lambda j, m: (j, 0)),
            scratch_shapes=[pltpu.VMEM((2, 2, D_MODEL, D_EXPERT), F32), pltpu.VMEM((2, D_EXPERT, D_MODEL), F32),
                            pltpu.VMEM((D_MODEL, D_EXPERT), BF16), pltpu.VMEM((D_MODEL, D_EXPERT), BF16),
                            pltpu.VMEM((D_EXPERT, D_MODEL), BF16), pltpu.SemaphoreType.DMA((2, 3))],
        ),
        out_shape=jax.ShapeDtypeStruct((MOE_ROWS, HALF_D), I32),
        compiler_params=_cparams(("arbitrary",)),
        name="moe_experts",
    )(meta, xb, w1, w3, w2)


def _combine_kernel(*refs, final):
    grp_ref, nxt_ref, seg_ref, x_ref, info_ref, mod_ref, yb_ref = refs[:7]
    if final:
        fin_ref, olat_ref, octx_ref, ybuf, sem = refs[7:]
    else:
        o_ref, ybuf, sem = refs[7:]
    i = pl.program_id(0)
    cur = i % 2

    def copies(g_ref, b, act):
        def make(loc, slot):
            return pltpu.make_async_copy(yb_ref.at[slot], ybuf.at[b, loc], sem.at[b])
        _all_experts(lambda e: _group_copies(g_ref, seg_ref, e, make, act))

    @pl.when(i == 0)
    def _():
        ybuf[...] = jnp.zeros_like(ybuf)
        copies(grp_ref, 0, lambda cp: cp.start())

    @pl.when(i + 1 < pl.num_programs(0))
    def _():
        copies(nxt_ref, 1 - cur, lambda cp: cp.start())

    copies(grp_ref, cur, lambda cp: cp.wait())
    info = info_ref[...]
    col = lax.broadcasted_iota(I32, (TM, R_LOC), 1).astype(F32)
    p = jnp.where(col == info[:, 2:3], info[:, 0:1], jnp.where(col == info[:, 3:4], info[:, 1:2], 0.0))
    p = p.astype(BF16)
    lo, hi = _unpack_bf16_pair(ybuf[cur])
    y = jnp.concatenate([jnp.dot(p, lo, preferred_element_type=F32),
                         jnp.dot(p, hi, preferred_element_type=F32)], axis=1)
    out = x_ref[...] + mod_ref[0][5:6] * y
    if not final:
        o_ref[...] = out
        return
    out = out * lax.rsqrt(jnp.mean(out * out, axis=-1, keepdims=True) + EPS) * fin_ref[...]

    @pl.when(i < N_LAT_TILES)
    def _():
        olat_ref[...] = out

    @pl.when(i >= N_LAT_TILES)
    def _():
        octx_ref[...] = out


def _combine(x, grp, seg, info, mods, yb, final_g=None):
    final = final_g is not None
    nt = N_TOK // TM
    in_specs = [
        _grp_spec(0),
        pl.BlockSpec((1, SUBLANES, LANES), lambda i: (jnp.minimum(i + 1, nt - 1), 0, 0), memory_space=pltpu.SMEM),
        pl.BlockSpec(memory_space=pltpu.SMEM),
        pl.BlockSpec((TM, D_MODEL), lambda i: (i, 0)),
        pl.BlockSpec((TM, LANES), lambda i: (i, 0)),
        pl.BlockSpec((1, MOD_ROWS, D_MODEL), lambda i: (i * TM // SEG, 0, 0)),
        pl.BlockSpec(memory_space=pl.ANY),
    ]
    args = [grp, grp, seg, x, info, mods, yb]
    if final:
        in_specs.append(pl.BlockSpec((1, D_MODEL), lambda i: (0, 0)))
        args.append(final_g.reshape(1, D_MODEL))
        out_specs = _group_specs(D_MODEL)
        out_shape = [jax.ShapeDtypeStruct((N_SAMPLE, D_MODEL), F32), jax.ShapeDtypeStruct((N_PROMPT, D_MODEL), F32)]
    else:
        out_specs = pl.BlockSpec((TM, D_MODEL), lambda i: (i, 0))
        out_shape = jax.ShapeDtypeStruct((N_TOK, D_MODEL), F32)
    return pl.pallas_call(
        functools.partial(_combine_kernel, final=final),
        grid=(nt,),
        in_specs=in_specs,
        out_specs=out_specs,
        out_shape=out_shape,
        scratch_shapes=[pltpu.VMEM((2, R_LOC, HALF_D), I32), pltpu.SemaphoreType.DMA((2,))],
        compiler_params=_cparams(("arbitrary",)),
        name="moe_combine_final" if final else "moe_combine",
    )(*args)


def _moe_layer(x, layer, norm2, mods, w_grp, b_grp, w_exp, b_exp, w1, w3, w2, tril, triu, final_g=None):
    w_route = jnp.zeros((D_MODEL, LANES), F32)
    w_route = w_route.at[:, :N_GROUPS].set(w_grp).at[:, N_GROUPS:N_GROUPS + N_EXPERTS].set(w_exp)
    b_route = jnp.zeros((1, LANES), F32)
    b_route = b_route.at[0, :N_GROUPS].set(b_grp).at[0, N_GROUPS:N_GROUPS + N_EXPERTS].set(b_exp)
    u, loct, info, grp, tot = _router(x, norm2, mods, w_route, b_route, tril, triu)

    total = tot[0, :N_EXPERTS]
    padded = (total + MOE_BLK - 1) // MOE_BLK * MOE_BLK
    pad_end = jnp.cumsum(padded)
    pad_start = pad_end - padded
    seg = jnp.zeros((2, LANES), I32).at[0, :N_EXPERTS].set(pad_start).at[1, :N_EXPERTS].set(total)
    blk_start = jnp.arange(MOE_NB, dtype=I32) * MOE_BLK
    blk_exp = jnp.minimum(jnp.sum(pad_end[None, :] <= blk_start[:, None], axis=1), N_EXPERTS - 1).astype(I32)
    used = pad_end[-1] // MOE_BLK
    blk = jnp.arange(MOE_NB, dtype=I32)
    active = blk < used
    first = jnp.logical_and(active, blk_start == pad_start[blk_exp])
    flag = jnp.where(active, jnp.where(first, 2, 1), 0)
    last_blk = jnp.maximum(used - 1, 0)
    nxt_blk = pad_end[blk_exp] // MOE_BLK
    nxt_exp = jnp.where(nxt_blk < used, blk_exp[jnp.minimum(nxt_blk, MOE_NB - 1)], -1)
    slot = (jnp.cumsum(first.astype(I32)) - 1) % 2
    meta = jnp.stack([jnp.where(active, blk_exp, blk_exp[last_blk]), flag, jnp.minimum(blk, last_blk),
                      nxt_exp, jnp.maximum(slot, 0)], axis=0).astype(I32)

    xb = _dispatch(grp, seg, loct, u)
    yb = _experts(meta, xb, w1, w3, w2, layer)
    return _combine(x, grp, seg, info, mods, yb, final_g)


def _pad_rows(w, rows=SUBLANES):
    return jnp.zeros((rows, w.shape[1]), w.dtype).at[:w.shape[0]].set(w)


def _pad2(w, rows, cols):
    return jnp.zeros((rows, cols), w.dtype).at[:w.shape[0], :w.shape[1]].set(w)


def kernel(x_prompt, x_sample, c, c_ctx, state_ret, cache_k, cache_v, ada_w, ada_b, norm1, norm2, final_norm, ev_w_in, ev_w_out, ret_decay_fwd, ret_decay_bwd, sc_conv_w, od_w_in, od_w_out, hy_conv_w, hy_w1, hy_b1, hy_fr1, hy_w2, hy_b2, hy_fr2, hy_w3, hy_bias, da_lambda, da_subln, moe_w_grp, moe_b_grp, moe_w_exp, moe_b_exp, moe_w1, moe_w3, moe_w2):
    x = (x_sample.reshape(N_SAMPLE, D_MODEL), x_prompt.reshape(N_PROMPT, D_MODEL))

    cvecs = jnp.zeros((16, D_MODEL), F32).at[:DEC_BATCH].set(c).at[DEC_BATCH].set(c_ctx)
    m = _mods(cvecs, ada_w, ada_b).reshape(DEPTH, 16, N_MOD, D_MODEL)
    seg_mods = jnp.concatenate(
        [m[:, :DEC_BATCH], jnp.broadcast_to(m[:, DEC_BATCH:DEC_BATCH + 1], (DEPTH, N_SEG - DEC_BATCH, N_MOD, D_MODEL))],
        axis=1)
    seg_mods = jnp.pad(seg_mods, ((0, 0), (0, 0), (0, MOD_ROWS - N_MOD), (0, 0)))

    tril = jnp.asarray(np.tril(np.ones((TM, TM), np.float32)), BF16)
    triu = jnp.asarray(np.triu(np.ones((LANES, LANES), np.float32), 1), BF16)

    for l in range(DEPTH):
        mods = seg_mods[l]
        i = l // 2
        if l % 2 == 0:
            proj = _norm_mod_matmul(x, norm1[l], mods, ev_w_in[i].astype(BF16), 0, 1)
            scw = _pad_rows(sc_conv_w[i])
            ret_l, sc_l = _even_mixer(proj, scw, ret_decay_fwd[i], ret_decay_bwd[i], state=state_ret[:, i:i + 1])
            ret_c, sc_c, new_state = _even_mixer(proj, scw, ret_decay_fwd[i], ret_decay_bwd[i])
            x = _proj_residual(x, (ret_l, ret_c), (sc_l, sc_c), ev_w_out[i].astype(BF16), mods, 2)
        else:
            lambda_init = 0.8 - 0.6 * math.exp(-0.3 * l)
            proj = _norm_mod_matmul(x, norm1[l], mods, od_w_in[i].astype(BF16), 0, 1)
            conv_w = _pad_rows(hy_conv_w[i])
            bias = _pad_rows(hy_bias[i])
            w1 = _pad2(hy_w1[i], LANES, LANES)
            b1 = _pad2(hy_b1[i][None], 1, LANES)
            f1 = _pad2(hy_fr1[i][None], 1, LANES)
            w2 = _pad2(hy_w2[i], LANES, LANES)
            b2 = _pad2(hy_b2[i][None], 1, LANES)
            f2 = _pad2(hy_fr2[i][None], 1, LANES)
            w3 = _pad2(hy_w3[i], LANES, hy_w3.shape[2])
            hy = []
            for latent in (True, False):
                seq = DEC_SEQ if latent else SEQ
                fwd_np, inv_np = _dft_tables(seq)
                zfeat, decay = _hyena_consts(seq)
                fwd = jnp.asarray(fwd_np)
                filters = _hy_filters(seq, jnp.asarray(zfeat), jnp.asarray(decay), fwd, w1, b1, f1, w2, b2, f2, w3)
                fwd_bf = _cast_bf16(fwd, f"dft_fwd_bf16_{seq}")
                inv_bf = _cast_bf16(jnp.asarray(inv_np), f"dft_inv_bf16_{seq}")
                hy.append(_hy_apply(proj, conv_w, fwd_bf, inv_bf, filters, bias, latent))
            cos, sin = _rope_tables(DEC_SEQ)
            att_l = _attention(proj, da_lambda[i], da_subln[i], lambda_init, True,
                               rope=(jnp.asarray(cos), jnp.asarray(sin)),
                               ctx_k=cache_k[:, i:i + 1], ctx_v=cache_v[:, i:i + 1])[0]
            att_c, new_k, new_v = _attention(proj, da_lambda[i], da_subln[i], lambda_init, False)
            x = _proj_residual(x, hy, (att_l, att_c), od_w_out[i].astype(BF16), mods, 2)
        x = _moe_layer(x, l, norm2[l], mods, moe_w_grp[l], moe_b_grp[l], moe_w_exp[l], moe_b_exp[l],
                       moe_w1, moe_w3, moe_w2, tril, triu, final_g=final_norm if l == DEPTH - 1 else None)

    y_sample, y_prompt = x
    return (y_prompt.reshape(BATCH, SEQ, D_MODEL), y_sample.reshape(DEC_BATCH, DEC_SEQ, D_MODEL),
            new_state, new_k, new_v)
```

```python
import functools
import math

import numpy as np
import jax
import jax.numpy as jnp
from jax import lax
from jax.experimental import pallas as pl
from jax.experimental.pallas import tpu as pltpu

F32 = jnp.float32
BF16 = jnp.bfloat16
I32 = jnp.int32
HIGHEST = lax.Precision.HIGHEST

D_MODEL = 1024
BATCH = 32
SEQ = 256
DEPTH = 2
DEC_BATCH = 8
DEC_SEQ = 1024
PAST_LEN = 256
GRID_W = 64
N_MOD = 6
EPS = 1e-6
GN_EPS = 1e-5

RET_DK = 128
RET_DV = 128
RET_HEADS = 4
RET_W = 512
RET_QK = 512
SC_W = 512
EVEN_IN = 2 * RET_QK + 2 * RET_W + 3 * SC_W

HY_W = 512
HY_ORDER = 2
HY_EMB = 33
HY_FFN = 64
HY_TARGET = 1e-2
HY_FAST_PCT = 0.3
HY_SLOW_PCT = 1.5
DA_DH = 64
DA_HEADS = 4
DA_W = 512
ODD_IN = 3 * HY_W + 3 * DA_W
ROPE_THETA = 10000.0

N_GROUPS = 4
EXP_PER_GROUP = 8
N_EXPERTS = 32
TOP_K = 2
D_EXPERT = 512

N_SAMPLE = DEC_BATCH * DEC_SEQ
N_PROMPT = BATCH * SEQ
N_TOK = N_SAMPLE + N_PROMPT
SEG = 1024
N_SEG = N_TOK // SEG
MOD_ROWS = 8

LANES = 128
SUBLANES = 8
VMEM_LIMIT = 56 * 1024 * 1024

TM = 256
MOE_BLK = 256
ROW_ALIGN = SUBLANES
R_LOC = (TOP_K * TM + N_EXPERTS * (ROW_ALIGN - 1) + LANES - 1) // LANES * LANES
CHUNKS = (256, 128, 64, 32, 16, 8)
MOE_ROWS = ((N_TOK * TOP_K + N_EXPERTS * (N_TOK // TM) * (ROW_ALIGN - 1) + N_EXPERTS * (MOE_BLK - 1))
            // MOE_BLK + 1) * MOE_BLK
MOE_NB = MOE_ROWS // MOE_BLK
HY_SLAB = 256
ATT_TQ = 256


def _cparams(sem, vmem=VMEM_LIMIT):
    return pltpu.CompilerParams(dimension_semantics=sem, vmem_limit_bytes=vmem)


def _silu(x):
    return x * (1.0 / (1.0 + jnp.exp(-x)))


def _split_bf16(a):
    hi = a.astype(BF16)
    return hi, (a - hi.astype(F32)).astype(BF16)


def _dot_3pass(a, b):
    ah, al = _split_bf16(a)
    bh, bl = _split_bf16(b)
    dot = functools.partial(jnp.dot, preferred_element_type=F32)
    return dot(ah, bh) + (dot(ah, bl) + dot(al, bh))


MODS_TN = 1536


def _mods_kernel(c_ref, w_ref, b_ref, o_ref):
    s = _silu(c_ref[...])
    o_ref[0] = jnp.dot(s, w_ref[0], preferred_element_type=F32, precision=HIGHEST) + b_ref[0]


def _mods(cvecs, ada_w, ada_b):
    n = N_MOD * D_MODEL
    return pl.pallas_call(
        _mods_kernel,
        grid=(DEPTH, n // MODS_TN),
        in_specs=[
            pl.BlockSpec((16, D_MODEL), lambda l, j: (0, 0)),
            pl.BlockSpec((1, D_MODEL, MODS_TN), lambda l, j: (l, 0, j)),
            pl.BlockSpec((1, 1, MODS_TN), lambda l, j: (l, 0, j)),
        ],
        out_specs=pl.BlockSpec((1, 16, MODS_TN), lambda l, j: (l, 0, j)),
        out_shape=jax.ShapeDtypeStruct((DEPTH, 16, n), F32),
        compiler_params=_cparams(("parallel", "parallel")),
        name="adaln_mods",
    )(cvecs, ada_w, ada_b.reshape(DEPTH, 1, n))


def _norm_mod(x, g, mod, shift_idx, scale_idx):
    ms = jnp.mean(x * x, axis=-1, keepdims=True)
    y = x * lax.rsqrt(ms + EPS) * g
    return y * (1.0 + mod[scale_idx:scale_idx + 1]) + mod[shift_idx:shift_idx + 1]


N_LAT_TILES = N_SAMPLE // TM
TM_WIDE = 512


def _group_specs(width, tm=TM):
    n_lat = N_SAMPLE // tm
    lat = pl.BlockSpec((tm, width), lambda i: (jnp.minimum(i, n_lat - 1), 0))
    ctx = pl.BlockSpec((tm, width), lambda i: (jnp.maximum(i - n_lat, 0), 0))
    return [lat, ctx]


def _group_pick(lat_ref, ctx_ref):
    n_lat = N_SAMPLE // lat_ref.shape[0]
    return jnp.where(pl.program_id(0) < n_lat, lat_ref[...], ctx_ref[...])


def _rows_specs(x, tm=TM):
    if isinstance(x, tuple):
        return _group_specs(D_MODEL, tm), list(x)
    return [pl.BlockSpec((tm, D_MODEL), lambda i: (i, 0))], [x]


def _rows_value(refs):
    return _group_pick(*refs) if len(refs) == 2 else refs[0][...]


def _nmm_kernel(*refs, n_x, shift_idx, scale_idx):
    g_ref, mod_ref, w_ref, o_ref = refs[n_x:]
    u = _norm_mod(_rows_value(refs[:n_x]), g_ref[...], mod_ref[0], shift_idx, scale_idx)
    o_ref[...] = jnp.dot(u.astype(BF16), w_ref[...], preferred_element_type=F32)


def _norm_mod_matmul(x, g, mods, w_bf16, shift_idx, scale_idx):
    n = w_bf16.shape[1]
    x_specs, x_args = _rows_specs(x)
    return pl.pallas_call(
        functools.partial(_nmm_kernel, n_x=len(x_args), shift_idx=shift_idx, scale_idx=scale_idx),
        grid=(N_TOK // TM,),
        in_specs=x_specs + [
            pl.BlockSpec((1, D_MODEL), lambda i: (0, 0)),
            pl.BlockSpec((1, MOD_ROWS, D_MODEL), lambda i: (i * TM // SEG, 0, 0)),
            pl.BlockSpec((D_MODEL, n), lambda i: (0, 0)),
        ],
        out_specs=pl.BlockSpec((TM, n), lambda i: (i, 0)),
        out_shape=jax.ShapeDtypeStruct((N_TOK, n), F32),
        compiler_params=_cparams(("parallel",)),
        name="norm_mod_inproj",
    )(*x_args, g.reshape(1, D_MODEL), mods, w_bf16)


def _proj_res_kernel(*refs, n_x, gate_idx):
    al_ref, ac_ref, bl_ref, bc_ref, wa_ref, wb_ref, mod_ref, o_ref = refs[n_x:]
    a = _group_pick(al_ref, ac_ref)
    b = _group_pick(bl_ref, bc_ref)
    y = jnp.dot(a.astype(BF16), wa_ref[...], preferred_element_type=F32)
    y = y + jnp.dot(b.astype(BF16), wb_ref[...], preferred_element_type=F32)
    o_ref[...] = _rows_value(refs[:n_x]) + mod_ref[0][gate_idx:gate_idx + 1] * y


def _proj_residual(x, a, b, w_bf16, mods, gate_idx):
    half = a[0].shape[1]
    tm = TM_WIDE
    x_specs, x_args = _rows_specs(x, tm)
    return pl.pallas_call(
        functools.partial(_proj_res_kernel, n_x=len(x_args), gate_idx=gate_idx),
        grid=(N_TOK // tm,),
        in_specs=x_specs + _group_specs(half, tm) + _group_specs(half, tm) + [
            pl.BlockSpec((half, D_MODEL), lambda i: (0, 0)),
            pl.BlockSpec((half, D_MODEL), lambda i: (1, 0)),
            pl.BlockSpec((1, MOD_ROWS, D_MODEL), lambda i: (i * tm // SEG, 0, 0)),
        ],
        out_specs=pl.BlockSpec((tm, D_MODEL), lambda i: (i, 0)),
        out_shape=jax.ShapeDtypeStruct((N_TOK, D_MODEL), F32),
        compiler_params=_cparams(("parallel",)),
        name="outproj_residual",
    )(*x_args, a[0], a[1], b[0], b[1], w_bf16, w_bf16, mods)


def _conv3_rows(z, w):
    n = z.shape[0]
    row = lax.broadcasted_iota(I32, z.shape, 0)
    zm = jnp.where(row == 0, 0.0, pltpu.roll(z, 1, 0))
    zp = jnp.where(row == n - 1, 0.0, pltpu.roll(z, n - 1, 0))
    return zm * w[0:1] + z * w[1:2] + zp * w[2:3]


def _log_gamma(dec):
    return jnp.log1p(-jnp.exp(dec))


def _even_kernel(*refs, seq, latent, tq, heads):
    (q_ref, k_ref, v_ref, g_ref, gb_ref, gc_ref, hx_ref, scw_ref, df_ref, db_ref) = refs[:10]
    if latent:
        s0_ref, ret_ref, sc_ref, mask_ref = refs[10:]
    else:
        ret_ref, sc_ref, st_ref, mask_ref = refs[10:]

    for hh in range(heads):
        lanes = slice(hh * RET_DK, (hh + 1) * RET_DK)
        lgf = _log_gamma(df_ref[hh])
        lgb = _log_gamma(db_ref[hh])

        @pl.when(pl.program_id(1) == 0)
        def _():
            for r in range(seq // tq):
                t = lax.broadcasted_iota(I32, (tq, seq), 0) + r * tq
                s = lax.broadcasted_iota(I32, (tq, seq), 1)
                d = (t - s).astype(F32)
                mf = jnp.where(d >= 0, jnp.exp(jnp.maximum(d, 0.0) * lgf), 0.0)
                mb = jnp.where(d <= 0, jnp.exp(jnp.maximum(-d, 0.0) * lgb), 0.0)
                mask_ref[hh, r * tq:(r + 1) * tq, :] = mf + mb

        k = k_ref[:, lanes] * (RET_DK ** -0.5)
        kb = k.astype(BF16)
        vb = v_ref[:, lanes].astype(BF16)
        if latent:
            s0f = s0_ref[0, 0, 0, hh].astype(BF16)
            s0b = s0_ref[0, 0, 1, hh].astype(BF16)
        for r in range(seq // tq):
            rows = slice(r * tq, (r + 1) * tq)
            qb = q_ref[rows, lanes].astype(BF16)
            s = lax.dot_general(qb, kb, (((1,), (1,)), ((), ())), preferred_element_type=F32)
            p = (s * mask_ref[hh, rows, :]).astype(BF16)
            o = jnp.dot(p, vb, preferred_element_type=F32)
            if latent:
                tpos = (lax.broadcasted_iota(I32, (tq, 1), 0) + r * tq).astype(F32)
                o = o + jnp.dot(qb, s0f, preferred_element_type=F32) * jnp.exp((tpos + 1.0) * lgf)
                o = o + jnp.dot(qb, s0b, preferred_element_type=F32) * jnp.exp((seq - tpos) * lgb)
            mu = jnp.mean(o, axis=-1, keepdims=True)
            var = jnp.mean(jnp.square(o - mu), axis=-1, keepdims=True)
            on = (o - mu) * lax.rsqrt(var + GN_EPS)
            ret_ref[rows, lanes] = _silu(g_ref[rows, lanes]) * on

        if not latent:
            spos = lax.broadcasted_iota(I32, (seq, 1), 0).astype(F32)
            kf = (k * jnp.exp((seq - 1.0 - spos) * lgf)).T.astype(BF16)
            kr = (k * jnp.exp(spos * lgb)).T.astype(BF16)
            st_ref[0, 0, 0, hh] = jnp.dot(kf, vb, preferred_element_type=F32)
            st_ref[0, 0, 1, hh] = jnp.dot(kr, vb, preferred_element_type=F32)

    sc_ref[...] = gb_ref[...] * _conv3_rows(gc_ref[...] * hx_ref[...], scw_ref[...])


def _even_mixer(proj, sc_w_pad, dec_f, dec_b, state=None):
    latent = state is not None
    seq = DEC_SEQ if latent else SEQ
    nb = DEC_BATCH if latent else BATCH
    row0 = 0 if latent else N_SAMPLE // seq
    tq = min(seq, 256)
    heads = 1 if latent else RET_HEADS
    steps = RET_HEADS // heads
    hw = heads * RET_DK

    def col(c0):
        return pl.BlockSpec((seq, hw), lambda h, b: (row0 + b, c0 * steps + h))

    in_specs = [col(0), col(1), col(2), col(3), col(4), col(5), col(6),
                pl.BlockSpec((SUBLANES, hw), lambda h, b: (0, h)),
                pl.BlockSpec((heads, 1, 1), lambda h, b: (h, 0, 0)),
                pl.BlockSpec((heads, 1, 1), lambda h, b: (h, 0, 0))]
    args = [proj] * 7 + [sc_w_pad, dec_f.reshape(RET_HEADS, 1, 1), dec_b.reshape(RET_HEADS, 1, 1)]
    out_spec = pl.BlockSpec((seq, hw), lambda h, b: (b, h))
    out_shape = [jax.ShapeDtypeStruct((nb * seq, RET_W), F32), jax.ShapeDtypeStruct((nb * seq, SC_W), F32)]
    out_specs = [out_spec, out_spec]
    st_spec = pl.BlockSpec((1, 1, 2, heads, RET_DK, RET_DV), lambda h, b: (b, 0, 0, h, 0, 0))
    if latent:
        in_specs.append(st_spec)
        args.append(state)
    else:
        out_shape.append(jax.ShapeDtypeStruct((BATCH, 1, 2, RET_HEADS, RET_DK, RET_DV), F32))
        out_specs.append(st_spec)
    return pl.pallas_call(
        functools.partial(_even_kernel, seq=seq, latent=latent, tq=tq, heads=heads),
        grid=(steps, nb),
        in_specs=in_specs,
        out_specs=out_specs,
        out_shape=out_shape,
        scratch_shapes=[pltpu.VMEM((heads, seq, seq), F32)],
        compiler_params=_cparams(("arbitrary", "arbitrary")),
        name="even_mixer_latent" if latent else "even_mixer_context",
    )(*args)


def _dft_tables(length):
    f = np.arange(length, dtype=np.float64)[:, None]
    t = np.arange(length, dtype=np.float64)[None, :]
    ang = np.pi * ((f * t) % (2 * length)) / length
    cos, sin = np.cos(ang), np.sin(ang)
    sin[0, :] = (-1.0) ** np.arange(length)
    fwd = np.concatenate([cos, sin], axis=0)
    wgt = np.full((2 * length,), 2.0)
    wgt[0] = 1.0
    wgt[length] = 1.0
    inv = (fwd * wgt[:, None] / (2.0 * length)).T
    return fwd.astype(np.float32), np.ascontiguousarray(inv).astype(np.float32)


def _hyena_consts(length):
    t = np.linspace(0.0, 1.0, length)[:, None]
    bands = (HY_EMB - 1) // 2
    w = 2.0 * np.pi * np.arange(length)[:, None] / length
    f = np.linspace(1e-4, bands - 1, bands)[None, :]
    z = np.concatenate([t, np.cos(f * w), -np.sin(f * w)], axis=-1)
    zpad = np.zeros((length, LANES))
    zpad[:, :HY_EMB] = z
    deltas = np.abs(np.linspace(math.log(HY_TARGET) / HY_SLOW_PCT, math.log(HY_TARGET) / HY_FAST_PCT, HY_W))
    decay = np.exp(-t * deltas[None, :])
    return zpad.astype(np.float32), decay.astype(np.float32)


def _cast_kernel(x_ref, o_ref):
    o_ref[...] = x_ref[...].astype(o_ref.dtype)


def _cast_bf16(x, name):
    rows = min(x.shape[0], 512)
    return pl.pallas_call(
        _cast_kernel,
        grid=(x.shape[0] // rows,),
        in_specs=[pl.BlockSpec((rows, x.shape[1]), lambda i: (i, 0))],
        out_specs=pl.BlockSpec((rows, x.shape[1]), lambda i: (i, 0)),
        out_shape=jax.ShapeDtypeStruct(x.shape, BF16),
        compiler_params=_cparams(("parallel",)),
        name=name,
    )(x)


def _hy_filter_kernel(z_ref, dec_ref, w1_ref, b1_ref, f1_ref, w2_ref, b2_ref, f2_ref, w3f_ref, w3b_ref,
                      fwd_ref, ka_ref, ka2_ref, kb_ref, *, seq):
    h = jnp.sin(f1_ref[...] * (jnp.dot(z_ref[...], w1_ref[...], preferred_element_type=F32,
                                       precision=HIGHEST) + b1_ref[...]))
    h = jnp.sin(f2_ref[...] * (jnp.dot(h, w2_ref[...], preferred_element_type=F32,
                                       precision=HIGHEST) + b2_ref[...]))
    dec = dec_ref[...]
    hf = jnp.dot(h, w3f_ref[...], preferred_element_type=F32, precision=HIGHEST) * dec
    hb = jnp.dot(h, w3b_ref[...], preferred_element_type=F32, precision=HIGHEST) * dec
    row = lax.broadcasted_iota(I32, hb.shape, 0)
    hb = jnp.where(row == 0, 0.0, hb)
    inv_norm = 1.0 / (jnp.sum(jnp.abs(hf), axis=0, keepdims=True)
                      + jnp.sum(jnp.abs(hb), axis=0, keepdims=True) + EPS)
    p = (hf + hb) * inv_norm
    q = (hf - hb) * inv_norm
    fp = _dot_3pass(fwd_ref[...], p)
    fq = _dot_3pass(fwd_ref[seq:, :], q)
    kr = fp[:seq]
    row = lax.broadcasted_iota(I32, kr.shape, 0)
    ka_ref[0] = kr
    ka2_ref[0] = jnp.where(row == 0, fp[seq:seq + 1], kr)
    kb_ref[0] = jnp.where(row == 0, 0.0, -fq)


def _hy_filters(seq, zfeat, decay, fwd, w1, b1, f1, w2, b2, f2, w3):
    ns = HY_W // HY_SLAB
    per_o = 2 * ns
    small = lambda shape: pl.BlockSpec(shape, lambda o, s: (0, 0))
    out_spec = pl.BlockSpec((1, seq, HY_SLAB), lambda o, s: (o, 0, s))
    out = jax.ShapeDtypeStruct((HY_ORDER, seq, HY_W), F32)
    return pl.pallas_call(
        functools.partial(_hy_filter_kernel, seq=seq),
        grid=(HY_ORDER, ns),
        in_specs=[
            small((seq, LANES)),
            pl.BlockSpec((seq, HY_SLAB), lambda o, s: (0, s)),
            small((LANES, LANES)), small((1, LANES)), small((1, LANES)),
            small((LANES, LANES)), small((1, LANES)), small((1, LANES)),
            pl.BlockSpec((LANES, HY_SLAB), lambda o, s: (0, o * per_o + s)),
            pl.BlockSpec((LANES, HY_SLAB), lambda o, s: (0, o * per_o + ns + s)),
            small((2 * seq, seq)),
        ],
        out_specs=[out_spec, out_spec, out_spec],
        out_shape=[out, out, out],
        compiler_params=_cparams(("parallel", "parallel")),
        name=f"hyena_filters_{seq}",
    )(zfeat, decay, w1, b1, f1, w2, b2, f2, w3, w3, fwd)


def _hy_apply_kernel(x1_ref, x2_ref, v_ref, c1_ref, c2_ref, cv_ref, fwd_ref, inv_ref,
                     ka_ref, ka2_ref, kb_ref, bias_ref, o_ref, *, seq):
    x1 = _conv3_rows(x1_ref[...], c1_ref[...])
    x2 = _conv3_rows(x2_ref[...], c2_ref[...])
    v = _conv3_rows(v_ref[...], cv_ref[...])

    def long_conv(sig, o):
        spec = jnp.dot(fwd_ref[...], sig.astype(BF16), preferred_element_type=F32)
        xr, xs = spec[:seq], spec[seq:]
        kb = kb_ref[o]
        yr = (xr * ka_ref[o] + xs * kb).astype(BF16)
        ys = (xs * ka2_ref[o] - xr * kb).astype(BF16)
        y = jnp.dot(inv_ref[:, :seq], yr, preferred_element_type=F32)
        y = y + jnp.dot(inv_ref[:, seq:], ys, preferred_element_type=F32)
        return y + sig * bias_ref[o:o + 1]

    z = x1 * long_conv(v, 0)
    o_ref[...] = x2 * long_conv(z, 1)


def _hy_apply(proj, conv_w_pad, fwd_bf, inv_bf, filters, bias_pad, latent):
    seq = DEC_SEQ if latent else SEQ
    nb = DEC_BATCH if latent else BATCH
    row0 = 0 if latent else N_SAMPLE // seq
    ns = HY_W // HY_SLAB
    const = lambda shape: pl.BlockSpec(shape, lambda s, b: (0,) * len(shape))
    col = lambda c0: pl.BlockSpec((seq, HY_SLAB), lambda s, b: (row0 + b, c0 + s))
    cw = lambda c0: pl.BlockSpec((SUBLANES, HY_SLAB), lambda s, b: (0, c0 + s))
    filt = pl.BlockSpec((HY_ORDER, seq, HY_SLAB), lambda s, b: (0, 0, s))
    in_specs = [col(0), col(ns), col(2 * ns), cw(0), cw(ns), cw(2 * ns),
                const((2 * seq, seq)), const((seq, 2 * seq)), filt, filt, filt,
                pl.BlockSpec((SUBLANES, HY_SLAB), lambda s, b: (0, s))]
    args = [proj] * 3 + [conv_w_pad] * 3 + [fwd_bf, inv_bf, *filters, bias_pad]
    return pl.pallas_call(
        functools.partial(_hy_apply_kernel, seq=seq),
        grid=(ns, nb),
        in_specs=in_specs,
        out_specs=pl.BlockSpec((seq, HY_SLAB), lambda s, b: (b, s)),
        out_shape=jax.ShapeDtypeStruct((nb * seq, HY_W), F32),
        compiler_params=_cparams(("arbitrary", "arbitrary")),
        name="hyena_latent" if latent else "hyena_context",
    )(*args)


def _rope_tables(length):
    rows = length // GRID_W
    row = np.repeat(np.arange(rows, dtype=np.float64), GRID_W)
    colp = np.tile(np.arange(GRID_W, dtype=np.float64), rows)
    half = DA_DH // 2
    freqs = ROPE_THETA ** (-np.arange(0, half, 2, dtype=np.float64) / half)
    ang = np.concatenate([row[:, None] * freqs[None, :]] * 2 + [colp[:, None] * freqs[None, :]] * 2, axis=1)
    ang = np.concatenate([ang, ang], axis=1)
    lane = np.arange(LANES)
    sign = np.where((lane % half) < half // 2, -1.0, 1.0)[None, :]
    return np.cos(ang).astype(np.float32), (np.sin(ang) * sign).astype(np.float32)


def _rope(x, cos, sin_signed):
    q = DA_DH // 4
    lane = lax.broadcasted_iota(I32, x.shape, 1)
    partner = jnp.where((lane % (2 * q)) < q, pltpu.roll(x, LANES - q, 1), pltpu.roll(x, q, 1))
    return x * cos + partner * sin_signed


def _attn_kernel(*refs, seq, latent, lambda_init, heads):
    q_ref, k_ref, v_ref, lam_ref, sub_ref = refs[:5]
    if latent:
        cos_ref, sin_ref, ck_ref, cv_ref, o_ref = refs[5:]
    else:
        o_ref, nk_ref, nv_ref = refs[5:]
    hw = 2 * DA_DH
    lv = lam_ref[...]
    lam = (jnp.exp(jnp.sum(lv[0:1] * lv[1:2], axis=1, keepdims=True))
           - jnp.exp(jnp.sum(lv[2:3] * lv[3:4], axis=1, keepdims=True)) + lambda_init)
    scale = DA_DH ** -0.5
    dn = (((1,), (1,)), ((), ()))
    tq = min(seq, ATT_TQ)
    for hh in range(heads):
        lanes = slice(hh * hw, (hh + 1) * hw)
        k = k_ref[:, lanes]
        v = v_ref[:, lanes]
        if latent:
            k = _rope(k, cos_ref[...], sin_ref[...])
            k_all = jnp.concatenate([k, jnp.concatenate([ck_ref[0, 0, 0, hh], ck_ref[0, 0, 1, hh]], axis=1)], axis=0)
            v_all = jnp.concatenate([v, cv_ref[0, 0, hh]], axis=0)
        else:
            k_all, v_all = k, v
            nk_ref[0, 0, 0, hh] = k[:, :DA_DH]
            nk_ref[0, 0, 1, hh] = k[:, DA_DH:]
            nv_ref[0, 0, hh] = v
        kb = k_all.astype(BF16)
        vb = v_all.astype(BF16)
        for r in range(seq // tq):
            rows = slice(r * tq, (r + 1) * tq)
            q = q_ref[rows, lanes]
            if latent:
                q = _rope(q, cos_ref[rows, :], sin_ref[rows, :])
            q = q * scale
            first = lax.broadcasted_iota(I32, q.shape, 1) < DA_DH
            q1 = jnp.where(first, q, 0.0).astype(BF16)
            q2 = jnp.where(first, 0.0, q).astype(BF16)
            s1 = lax.dot_general(q1, kb, dn, preferred_element_type=F32)
            s2 = lax.dot_general(q2, kb, dn, preferred_element_type=F32)
            e1 = jnp.exp(s1 - jnp.max(s1, axis=-1, keepdims=True))
            e2 = jnp.exp(s2 - jnp.max(s2, axis=-1, keepdims=True))
            o1 = jnp.dot(e1.astype(BF16), vb, preferred_element_type=F32)
            o2 = jnp.dot(e2.astype(BF16), vb, preferred_element_type=F32)
            o = (o1 * (1.0 / jnp.sum(e1, axis=-1, keepdims=True))
                 - o2 * (lam / jnp.sum(e2, axis=-1, keepdims=True)))
            o = o * lax.rsqrt(jnp.mean(o * o, axis=-1, keepdims=True) + EPS) * sub_ref[...]
            o_ref[rows, lanes] = o * (1.0 - lambda_init)


def _attention(proj, lam_vec, subln, lambda_init, latent, rope=None, ctx_k=None, ctx_v=None):
    seq = DEC_SEQ if latent else SEQ
    nb = DEC_BATCH if latent else BATCH
    row0 = 0 if latent else N_SAMPLE // seq
    hw = 2 * DA_DH
    heads = 1 if latent else DA_HEADS
    steps = DA_HEADS // heads
    col = lambda c: pl.BlockSpec((seq, heads * hw), lambda b, h: (row0 + b, c * steps + h))
    in_specs = [col(3), col(4), col(5),
                pl.BlockSpec((4, DA_DH), lambda b, h: (0, 0)),
                pl.BlockSpec((1, hw), lambda b, h: (0, 0))]
    args = [proj] * 3 + [lam_vec, subln.reshape(1, hw)]
    out_spec = pl.BlockSpec((seq, heads * hw), lambda b, h: (b, h))
    out_shape = [jax.ShapeDtypeStruct((nb * seq, DA_W), F32)]
    out_specs = [out_spec]
    k_spec = pl.BlockSpec((1, 1, 2, heads, SEQ, DA_DH), lambda b, h: (b, 0, 0, h, 0, 0))
    v_spec = pl.BlockSpec((1, 1, heads, SEQ, hw), lambda b, h: (b, 0, h, 0, 0))
    if latent:
        in_specs += [pl.BlockSpec((seq, hw), lambda b, h: (0, 0))] * 2 + [k_spec, v_spec]
        args += [rope[0], rope[1], ctx_k, ctx_v]
    else:
        out_shape += [jax.ShapeDtypeStruct((BATCH, 1, 2, DA_HEADS, SEQ, DA_DH), F32),
                      jax.ShapeDtypeStruct((BATCH, 1, DA_HEADS, SEQ, hw), F32)]
        out_specs += [k_spec, v_spec]
    return pl.pallas_call(
        functools.partial(_attn_kernel, seq=seq, latent=latent, lambda_init=lambda_init, heads=heads),
        grid=(nb, steps),
        in_specs=in_specs,
        out_specs=out_specs,
        out_shape=out_shape,
        compiler_params=_cparams(("arbitrary", "arbitrary")),
        name="diff_attn_latent" if latent else "diff_attn_context",
    )(*args)


HALF_D = D_MODEL // 2
HI16 = -65536


def _pack_bf16_pair(lo, hi):
    lo_bits = lax.bitcast_convert_type(lo.astype(BF16).astype(F32), I32)
    hi_bits = lax.bitcast_convert_type(hi.astype(BF16).astype(F32), I32)
    return (hi_bits & HI16) | lax.shift_right_logical(lo_bits, 16)


def _unpack_bf16_pair(p):
    lo = lax.bitcast_convert_type(lax.shift_left(p, 16), F32).astype(BF16)
    hi = lax.bitcast_convert_type(p & HI16, F32).astype(BF16)
    return lo, hi


GRP_EXPERT, GRP_REL, GRP_COUNT = range(3)


def _router_kernel(x_ref, g_ref, mod_ref, wr_ref, br_ref, tril_ref, triu_ref,
                   u_ref, loct_ref, info_ref, grp_ref, tot_ref, carry_ref):
    @pl.when(pl.program_id(0) == 0)
    def _():
        carry_ref[...] = jnp.zeros_like(carry_ref)

    u = _norm_mod(x_ref[...], g_ref[...], mod_ref[0], 3, 4)
    u_ref[...] = _pack_bf16_pair(u[:, :HALF_D], u[:, HALF_D:])
    logits = _dot_3pass(u, wr_ref[...]) + br_ref[...]
    lane = lax.broadcasted_iota(I32, logits.shape, 1)
    lane_f = lane.astype(F32)
    neg = -jnp.inf
    big = float(LANES)

    def first_max(vals):
        m = jnp.max(vals, axis=-1, keepdims=True)
        return m, jnp.min(jnp.where(vals == m, lane_f, big), axis=-1, keepdims=True)

    is_grp = lane < N_GROUPS
    gmax, gidx = first_max(jnp.where(is_grp, logits, neg))
    gsum = jnp.sum(jnp.where(is_grp, jnp.exp(logits - gmax), 0.0), axis=-1, keepdims=True)
    g_w = 1.0 / gsum
    lo = N_GROUPS + EXP_PER_GROUP * gidx
    in_grp = jnp.logical_and(lane_f >= lo, lane_f < lo + EXP_PER_GROUP)
    el = jnp.where(in_grp, logits, neg)
    v1, i1 = first_max(el)
    v2, i2 = first_max(jnp.where(lane_f == i1, neg, el))
    e = jnp.exp(v2 - v1)
    p1 = 1.0 / (1.0 + e)
    hit1 = lane_f == i1 - N_GROUPS
    hit2 = lane_f == i2 - N_GROUPS
    onehot = jnp.where(jnp.logical_or(hit1, hit2), 1.0, 0.0)
    cum = jnp.dot(tril_ref[...], onehot.astype(BF16), preferred_element_type=F32)
    n = cum[TM - 1:TM, :]
    n_pad = jnp.floor((n + (ROW_ALIGN - 1.0)) * (1.0 / ROW_ALIGN)) * ROW_ALIGN
    loc_off = jnp.dot(jnp.broadcast_to(n_pad, (SUBLANES, LANES)).astype(BF16), triu_ref[...],
                      preferred_element_type=F32)[0:1]
    base = cum - onehot + loc_off
    loc1 = jnp.sum(jnp.where(hit1, base, 0.0), axis=-1, keepdims=True)
    loc2 = jnp.sum(jnp.where(hit2, base, 0.0), axis=-1, keepdims=True)
    info = jnp.where(lane == 0, g_w * p1, jnp.where(lane == 1, g_w * (e * p1), jnp.where(lane == 2, loc1, loc2)))
    info_ref[...] = info
    loct_ref[0] = info.T[:SUBLANES]
    start8 = lax.broadcasted_iota(I32, (LANES, LANES), 0).astype(F32) * ROW_ALIGN
    owner = jnp.where(jnp.logical_and(start8 >= loc_off, start8 < loc_off + n_pad), 1.0, 0.0)
    lane128 = lax.broadcasted_iota(I32, (LANES, LANES), 1)
    c_exp = jnp.sum(owner * lane128.astype(F32), axis=-1, keepdims=True)
    c_rel = jnp.sum(owner * (carry_ref[...] + start8 - loc_off), axis=-1, keepdims=True)
    chunks = jnp.where(lane128 == 0, c_exp, jnp.where(lane128 == 1, c_rel, 0.0)).T
    n_chunks = jnp.sum(n_pad, axis=-1, keepdims=True) * (1.0 / ROW_ALIGN)
    row = lax.broadcasted_iota(I32, (SUBLANES, LANES), 0)
    grp = jnp.where(row == GRP_EXPERT, chunks[0:1], jnp.where(row == GRP_REL, chunks[1:2], n_chunks))
    grp_ref[0] = grp.astype(I32)
    carry_ref[...] = carry_ref[...] + n_pad
    tot_ref[...] = jnp.broadcast_to(carry_ref[...], tot_ref.shape).astype(I32)


def _router(x, g, mods, w_route, b_route, tril, triu):
    nt = N_TOK // TM
    return pl.pallas_call(
        _router_kernel,
        grid=(nt,),
        in_specs=[
            pl.BlockSpec((TM, D_MODEL), lambda i: (i, 0)),
            pl.BlockSpec((1, D_MODEL), lambda i: (0, 0)),
            pl.BlockSpec((1, MOD_ROWS, D_MODEL), lambda i: (i * TM // SEG, 0, 0)),
            pl.BlockSpec((D_MODEL, LANES), lambda i: (0, 0)),
            pl.BlockSpec((1, LANES), lambda i: (0, 0)),
            pl.BlockSpec((TM, TM), lambda i: (0, 0)),
            pl.BlockSpec((LANES, LANES), lambda i: (0, 0)),
        ],
        out_specs=[
            pl.BlockSpec((TM, HALF_D), lambda i: (i, 0)),
            pl.BlockSpec((1, SUBLANES, TM), lambda i: (i, 0, 0)),
            pl.BlockSpec((TM, LANES), lambda i: (i, 0)),
            pl.BlockSpec((1, SUBLANES, LANES), lambda i: (i, 0, 0)),
            pl.BlockSpec((SUBLANES, LANES), lambda i: (0, 0)),
        ],
        out_shape=[
            jax.ShapeDtypeStruct((N_TOK, HALF_D), I32),
            jax.ShapeDtypeStruct((nt, SUBLANES, TM), F32),
            jax.ShapeDtypeStruct((N_TOK, LANES), F32),
            jax.ShapeDtypeStruct((nt, SUBLANES, LANES), I32),
            jax.ShapeDtypeStruct((SUBLANES, LANES), I32),
        ],
        scratch_shapes=[pltpu.VMEM((1, LANES), F32)],
        compiler_params=_cparams(("arbitrary",)),
        name="moe_router",
    )(x, g.reshape(1, D_MODEL), mods, w_route, b_route, tril, triu)


def _for_chunks(n, fn):
    for size in CHUNKS:
        @pl.when((n & size) != 0)
        def _():
            fn(n & ~(2 * size - 1), size)


WAIT_SIZES = (512, 256, 128, 64, 32, 16, 8)


def _start_tile_copies(grp_ref, seg_ref, make):
    def body(c, carry):
        slot = seg_ref[0, grp_ref[0, GRP_EXPERT, c]] + grp_ref[0, GRP_REL, c]
        make(pl.ds(pl.multiple_of(c * ROW_ALIGN, ROW_ALIGN), ROW_ALIGN),
             pl.ds(pl.multiple_of(slot, ROW_ALIGN), ROW_ALIGN)).start()
        return carry

    lax.fori_loop(0, grp_ref[0, GRP_COUNT, 0], body, 0)


def _wait_tile_copies(grp_ref, make):
    rows = grp_ref[0, GRP_COUNT, 0] * ROW_ALIGN
    for size in WAIT_SIZES:
        @pl.when((rows & size) != 0)
        def _():
            make(pl.ds(0, size), pl.ds(0, size)).wait()


def _all_experts(body):
    lax.fori_loop(0, N_EXPERTS, lambda e, c: (body(e), c)[1], 0)


def _dispatch_kernel(grp_ref, grp1_ref, grp2_ref, seg_ref, loct_ref, u_ref, xb_ref, buf, zbuf, sem):
    i = pl.program_id(0)
    last = pl.num_programs(0) - 1
    cur = i % 2

    def maker(b):
        return lambda loc, slot: pltpu.make_async_copy(buf.at[b, loc], xb_ref.at[slot], sem.at[b])

    @pl.when(i >= 2)
    def _():
        _wait_tile_copies(grp2_ref, maker(cur))

    lo, hi = _unpack_bf16_pair(u_ref[...])
    row = lax.broadcasted_iota(I32, (R_LOC, TM), 0).astype(F32)
    sel = jnp.logical_or(row == loct_ref[0, 2:3, :], row == loct_ref[0, 3:4, :])
    pt = jnp.where(sel, 1.0, 0.0).astype(BF16)
    buf[cur] = _pack_bf16_pair(jnp.dot(pt, lo, preferred_element_type=F32),
                               jnp.dot(pt, hi, preferred_element_type=F32))
    _start_tile_copies(grp_ref, seg_ref, maker(cur))

    @pl.when(i == last)
    def _():
        @pl.when(i >= 1)
        def _():
            _wait_tile_copies(grp1_ref, maker(1 - cur))

        _wait_tile_copies(grp_ref, maker(cur))

        zbuf[...] = jnp.zeros_like(zbuf)

        def tail(e, act):
            total = seg_ref[1, e]
            dst = seg_ref[0, e] + total

            def one(off, size):
                act(pltpu.make_async_copy(zbuf.at[pl.ds(0, size)],
                                          xb_ref.at[pl.ds(pl.multiple_of(dst + off, ROW_ALIGN), size)], sem.at[2]))

            _for_chunks((-total) & (MOE_BLK - 1), one)

        _all_experts(lambda e: tail(e, lambda cp: cp.start()))
        _all_experts(lambda e: tail(e, lambda cp: cp.wait()))

        e_last = N_EXPERTS - 1
        used = (seg_ref[0, e_last] + seg_ref[1, e_last] + MOE_BLK - 1) // MOE_BLK

        def spare(j):
            return pltpu.make_async_copy(zbuf, xb_ref.at[pl.ds(pl.multiple_of(j * MOE_BLK, MOE_BLK), MOE_BLK)],
                                         sem.at[2])

        lax.fori_loop(used, MOE_NB, lambda j, c: (spare(j).start(), c)[1], 0)
        lax.fori_loop(used, MOE_NB, lambda j, c: (spare(j).wait(), c)[1], 0)


def _grp_spec(back):
    return pl.BlockSpec((1, SUBLANES, LANES), lambda i: (jnp.maximum(i - back, 0), 0, 0), memory_space=pltpu.SMEM)


def _dispatch(grp, seg, loct, u):
    return pl.pallas_call(
        _dispatch_kernel,
        grid=(N_TOK // TM,),
        in_specs=[
            _grp_spec(0), _grp_spec(1), _grp_spec(2),
            pl.BlockSpec(memory_space=pltpu.SMEM),
            pl.BlockSpec((1, SUBLANES, TM), lambda i: (i, 0, 0)),
            pl.BlockSpec((TM, HALF_D), lambda i: (i, 0)),
        ],
        out_specs=pl.BlockSpec(memory_space=pl.ANY),
        out_shape=jax.ShapeDtypeStruct((MOE_ROWS, HALF_D), I32),
        scratch_shapes=[pltpu.VMEM((2, R_LOC, HALF_D), I32), pltpu.VMEM((MOE_BLK, HALF_D), I32),
                        pltpu.SemaphoreType.DMA((3,))],
        compiler_params=_cparams(("arbitrary",)),
        name="moe_dispatch",
    )(grp, grp, grp, seg, loct, u)


META_EXPERT, META_FLAG, META_BLOCK, META_NEXT, META_SLOT = range(5)


def _expert_kernel(meta_ref, xb_ref, w1_hbm, w3_hbm, w2_hbm, yb_ref, w_in, w_out, w1b, w3b, w2b, sem, *, layer):
    j = pl.program_id(0)
    flag = meta_ref[META_FLAG, j]
    expert = meta_ref[META_EXPERT, j]
    slot = meta_ref[META_SLOT, j]

    def fetch(e, s):
        return (pltpu.make_async_copy(w1_hbm.at[layer, e], w_in.at[s, 0], sem.at[s, 0]),
                pltpu.make_async_copy(w3_hbm.at[layer, e], w_in.at[s, 1], sem.at[s, 1]),
                pltpu.make_async_copy(w2_hbm.at[layer, e], w_out.at[s], sem.at[s, 2]))

    @pl.when(j == 0)
    def _():
        for cp in fetch(expert, slot):
            cp.start()

    @pl.when(flag == 2)
    def _():
        for cp in fetch(expert, slot):
            cp.wait()
        nxt = meta_ref[META_NEXT, j]

        @pl.when(nxt >= 0)
        def _():
            for cp in fetch(nxt, 1 - slot):
                cp.start()

        w1b[...] = w_in[slot, 0].astype(BF16)
        w3b[...] = w_in[slot, 1].astype(BF16)
        w2b[...] = w_out[slot].astype(BF16)

    @pl.when(flag > 0)
    def _():
        lo, hi = _unpack_bf16_pair(xb_ref[...])

        def up(wb):
            return (jnp.dot(lo, wb[:HALF_D, :], preferred_element_type=F32)
                    + jnp.dot(hi, wb[HALF_D:, :], preferred_element_type=F32))

        h1 = up(w1b)
        h3 = up(w3b)
        hdn = (_silu(h1) * h3).astype(BF16)
        y = jnp.dot(hdn, w2b[...], preferred_element_type=F32)
        yb_ref[...] = _pack_bf16_pair(y[:, :HALF_D], y[:, HALF_D:])

    @pl.when(flag == 0)
    def _():
        yb_ref[...] = jnp.zeros_like(yb_ref)


def _experts(meta, xb, w1, w3, w2, layer):
    hbm = pl.BlockSpec(memory_space=pl.ANY)
    return pl.pallas_call(
        functools.partial(_expert_kernel, layer=layer),
        grid_spec=pltpu.PrefetchScalarGridSpec(
            num_scalar_prefetch=1,
            grid=(MOE_NB,),
            in_specs=[pl.BlockSpec((MOE_BLK, HALF_D), lambda j, m: (m[META_BLOCK, j], 0)), hbm, hbm, hbm],
            out_specs=pl.BlockSpec((MOE_BLK, HALF_D), lambda j, m: (j, 0)),
            scratch_shapes=[pltpu.VMEM((2, 2, D_MODEL, D_EXPERT), F32), pltpu.VMEM((2, D_EXPERT, D_MODEL), F32),
                            pltpu.VMEM((D_MODEL, D_EXPERT), BF16), pltpu.VMEM((D_MODEL, D_EXPERT), BF16),
                            pltpu.VMEM((D_EXPERT, D_MODEL), BF16), pltpu.SemaphoreType.DMA((2, 3))],
        ),
        out_shape=jax.ShapeDtypeStruct((MOE_ROWS, HALF_D), I32),
        compiler_params=_cparams(("arbitrary",)),
        name="moe_experts",
    )(meta, xb, w1, w3, w2)


def _combine_kernel(*refs, final):
    grp_ref, nxt_ref, seg_ref, x_ref, info_ref, mod_ref, yb_ref = refs[:7]
    if final:
        fin_ref, olat_ref, octx_ref, ybuf, sem = refs[7:]
    else:
        o_ref, ybuf, sem = refs[7:]
    i = pl.program_id(0)
    cur = i % 2

    def maker(b):
        return lambda loc, slot: pltpu.make_async_copy(yb_ref.at[slot], ybuf.at[b, loc], sem.at[b])

    @pl.when(i == 0)
    def _():
        ybuf[...] = jnp.zeros_like(ybuf)
        _start_tile_copies(grp_ref, seg_ref, maker(0))

    @pl.when(i + 1 < pl.num_programs(0))
    def _():
        _start_tile_copies(nxt_ref, seg_ref, maker(1 - cur))

    _wait_tile_copies(grp_ref, maker(cur))
    info = info_ref[...]
    col = lax.broadcasted_iota(I32, (TM, R_LOC), 1).astype(F32)
    p = jnp.where(col == info[:, 2:3], info[:, 0:1], jnp.where(col == info[:, 3:4], info[:, 1:2], 0.0))
    p = p.astype(BF16)
    lo, hi = _unpack_bf16_pair(ybuf[cur])
    y = jnp.concatenate([jnp.dot(p, lo, preferred_element_type=F32),
                         jnp.dot(p, hi, preferred_element_type=F32)], axis=1)
    out = x_ref[...] + mod_ref[0][5:6] * y
    if not final:
        o_ref[...] = out
        return
    out = out * lax.rsqrt(jnp.mean(out * out, axis=-1, keepdims=True) + EPS) * fin_ref[...]

    @pl.when(i < N_LAT_TILES)
    def _():
        olat_ref[...] = out

    @pl.when(i >= N_LAT_TILES)
    def _():
        octx_ref[...] = out


def _combine(x, grp, seg, info, mods, yb, final_g=None):
    final = final_g is not None
    nt = N_TOK // TM
    in_specs = [
        _grp_spec(0),
        pl.BlockSpec((1, SUBLANES, LANES), lambda i: (jnp.minimum(i + 1, nt - 1), 0, 0), memory_space=pltpu.SMEM),
        pl.BlockSpec(memory_space=pltpu.SMEM),
        pl.BlockSpec((TM, D_MODEL), lambda i: (i, 0)),
        pl.BlockSpec((TM, LANES), lambda i: (i, 0)),
        pl.BlockSpec((1, MOD_ROWS, D_MODEL), lambda i: (i * TM // SEG, 0, 0)),
        pl.BlockSpec(memory_space=pl.ANY),
    ]
    args = [grp, grp, seg, x, info, mods, yb]
    if final:
        in_specs.append(pl.BlockSpec((1, D_MODEL), lambda i: (0, 0)))
        args.append(final_g.reshape(1, D_MODEL))
        out_specs = _group_specs(D_MODEL)
        out_shape = [jax.ShapeDtypeStruct((N_SAMPLE, D_MODEL), F32), jax.ShapeDtypeStruct((N_PROMPT, D_MODEL), F32)]
    else:
        out_specs = pl.BlockSpec((TM, D_MODEL), lambda i: (i, 0))
        out_shape = jax.ShapeDtypeStruct((N_TOK, D_MODEL), F32)
    return pl.pallas_call(
        functools.partial(_combine_kernel, final=final),
        grid=(nt,),
        in_specs=in_specs,
        out_specs=out_specs,
        out_shape=out_shape,
        scratch_shapes=[pltpu.VMEM((2, R_LOC, HALF_D), I32), pltpu.SemaphoreType.DMA((2,))],
        compiler_params=_cparams(("arbitrary",)),
        name="moe_combine_final" if final else "moe_combine",
    )(*args)


def _moe_layer(x, layer, norm2, mods, w_grp, b_grp, w_exp, b_exp, w1, w3, w2, tril, triu, final_g=None):
    w_route = jnp.zeros((D_MODEL, LANES), F32)
    w_route = w_route.at[:, :N_GROUPS].set(w_grp).at[:, N_GROUPS:N_GROUPS + N_EXPERTS].set(w_exp)
    b_route = jnp.zeros((1, LANES), F32)
    b_route = b_route.at[0, :N_GROUPS].set(b_grp).at[0, N_GROUPS:N_GROUPS + N_EXPERTS].set(b_exp)
    u, loct, info, grp, tot = _router(x, norm2, mods, w_route, b_route, tril, triu)

    total = tot[0, :N_EXPERTS]
    padded = (total + MOE_BLK - 1) // MOE_BLK * MOE_BLK
    pad_end = jnp.cumsum(padded)
    pad_start = pad_end - padded
    seg = jnp.zeros((2, LANES), I32).at[0, :N_EXPERTS].set(pad_start).at[1, :N_EXPERTS].set(total)
    def owner(row):
        return jnp.minimum(jnp.sum((pad_end[None, :] <= row[:, None]).astype(I32), axis=1), N_EXPERTS - 1)

    def lookup(table, idx):
        return jnp.sum(jnp.where(idx[:, None] == jnp.arange(N_EXPERTS, dtype=I32)[None, :], table[None, :], 0), axis=1)

    blk = jnp.arange(MOE_NB, dtype=I32)
    rows_used = pad_end[-1]
    last_blk = jnp.maximum(rows_used // MOE_BLK - 1, 0)
    active = blk <= last_blk
    blk_exp = owner(jnp.minimum(blk, last_blk) * MOE_BLK)
    first = jnp.logical_and(active, blk * MOE_BLK == lookup(pad_start, blk_exp))
    flag = jnp.where(active, jnp.where(first, 2, 1), 0)
    seg_end = lookup(pad_end, blk_exp)
    nxt_exp = jnp.where(seg_end < rows_used, owner(seg_end), -1)
    slot = jnp.maximum(jnp.cumsum(first.astype(I32)) - 1, 0) % 2
    meta = jnp.stack([blk_exp, flag, jnp.minimum(blk, last_blk), nxt_exp, slot], axis=0).astype(I32)

    xb = _dispatch(grp, seg, loct, u)
    yb = _experts(meta, xb, w1, w3, w2, layer)
    return _combine(x, grp, seg, info, mods, yb, final_g)


def _pad_rows(w, rows=SUBLANES):
    return jnp.zeros((rows, w.shape[1]), w.dtype).at[:w.shape[0]].set(w)


def _pad2(w, rows, cols):
    return jnp.zeros((rows, cols), w.dtype).at[:w.shape[0], :w.shape[1]].set(w)


def kernel(x_prompt, x_sample, c, c_ctx, state_ret, cache_k, cache_v, ada_w, ada_b, norm1, norm2, final_norm, ev_w_in, ev_w_out, ret_decay_fwd, ret_decay_bwd, sc_conv_w, od_w_in, od_w_out, hy_conv_w, hy_w1, hy_b1, hy_fr1, hy_w2, hy_b2, hy_fr2, hy_w3, hy_bias, da_lambda, da_subln, moe_w_grp, moe_b_grp, moe_w_exp, moe_b_exp, moe_w1, moe_w3, moe_w2):
    x = (x_sample.reshape(N_SAMPLE, D_MODEL), x_prompt.reshape(N_PROMPT, D_MODEL))

    cvecs = jnp.zeros((16, D_MODEL), F32).at[:DEC_BATCH].set(c).at[DEC_BATCH].set(c_ctx)
    m = _mods(cvecs, ada_w, ada_b).reshape(DEPTH, 16, N_MOD, D_MODEL)
    seg_mods = jnp.concatenate(
        [m[:, :DEC_BATCH], jnp.broadcast_to(m[:, DEC_BATCH:DEC_BATCH + 1], (DEPTH, N_SEG - DEC_BATCH, N_MOD, D_MODEL))],
        axis=1)
    seg_mods = jnp.pad(seg_mods, ((0, 0), (0, 0), (0, MOD_ROWS - N_MOD), (0, 0)))

    tril = jnp.asarray(np.tril(np.ones((TM, TM), np.float32)), BF16)
    triu = jnp.asarray(np.triu(np.ones((LANES, LANES), np.float32), 1), BF16)

    for l in range(DEPTH):
        mods = seg_mods[l]
        i = l // 2
        if l % 2 == 0:
            proj = _norm_mod_matmul(x, norm1[l], mods, ev_w_in[i].astype(BF16), 0, 1)
            scw = _pad_rows(sc_conv_w[i])
            ret_l, sc_l = _even_mixer(proj, scw, ret_decay_fwd[i], ret_decay_bwd[i], state=state_ret[:, i:i + 1])
            ret_c, sc_c, new_state = _even_mixer(proj, scw, ret_decay_fwd[i], ret_decay_bwd[i])
            x = _proj_residual(x, (ret_l, ret_c), (sc_l, sc_c), ev_w_out[i].astype(BF16), mods, 2)
        else:
            lambda_init = 0.8 - 0.6 * math.exp(-0.3 * l)
            proj = _norm_mod_matmul(x, norm1[l], mods, od_w_in[i].astype(BF16), 0, 1)
            conv_w = _pad_rows(hy_conv_w[i])
            bias = _pad_rows(hy_bias[i])
            w1 = _pad2(hy_w1[i], LANES, LANES)
            b1 = _pad2(hy_b1[i][None], 1, LANES)
            f1 = _pad2(hy_fr1[i][None], 1, LANES)
            w2 = _pad2(hy_w2[i], LANES, LANES)
            b2 = _pad2(hy_b2[i][None], 1, LANES)
            f2 = _pad2(hy_fr2[i][None], 1, LANES)
            w3 = _pad2(hy_w3[i], LANES, hy_w3.shape[2])
            hy = []
            for latent in (True, False):
                seq = DEC_SEQ if latent else SEQ
                fwd_np, inv_np = _dft_tables(seq)
                zfeat, decay = _hyena_consts(seq)
                fwd = jnp.asarray(fwd_np)
                filters = _hy_filters(seq, jnp.asarray(zfeat), jnp.asarray(decay), fwd, w1, b1, f1, w2, b2, f2, w3)
                fwd_bf = _cast_bf16(fwd, f"dft_fwd_bf16_{seq}")
                inv_bf = _cast_bf16(jnp.asarray(inv_np), f"dft_inv_bf16_{seq}")
                hy.append(_hy_apply(proj, conv_w, fwd_bf, inv_bf, filters, bias, latent))
            cos, sin = _rope_tables(DEC_SEQ)
            att_l = _attention(proj, da_lambda[i], da_subln[i], lambda_init, True,
                               rope=(jnp.asarray(cos), jnp.asarray(sin)),
                               ctx_k=cache_k[:, i:i + 1], ctx_v=cache_v[:, i:i + 1])[0]
            att_c, new_k, new_v = _attention(proj, da_lambda[i], da_subln[i], lambda_init, False)
            x = _proj_residual(x, hy, (att_l, att_c), od_w_out[i].astype(BF16), mods, 2)
        x = _moe_layer(x, l, norm2[l], mods, moe_w_grp[l], moe_b_grp[l], moe_w_exp[l], moe_b_exp[l],
                       moe_w1, moe_w3, moe_w2, tril, triu, final_g=final_norm if l == DEPTH - 1 else None)

    y_sample, y_prompt = x
    return (y_prompt.reshape(BATCH, SEQ, D_MODEL), y_sample.reshape(DEC_BATCH, DEC_SEQ, D_MODEL),
            new_state, new_k, new_v)
```

```python
import functools
import math

import numpy as np
import jax
import jax.numpy as jnp
from jax import lax
from jax.experimental import pallas as pl
from jax.experimental.pallas import tpu as pltpu

F32 = jnp.float32
BF16 = jnp.bfloat16
I32 = jnp.int32
HIGHEST = lax.Precision.HIGHEST

D_MODEL = 1024
BATCH = 32
SEQ = 256
DEPTH = 2
DEC_BATCH = 8
DEC_SEQ = 1024
PAST_LEN = 256
GRID_W = 64
N_MOD = 6
EPS = 1e-6
GN_EPS = 1e-5

RET_DK = 128
RET_DV = 128
RET_HEADS = 4
RET_W = 512
RET_QK = 512
SC_W = 512
EVEN_IN = 2 * RET_QK + 2 * RET_W + 3 * SC_W

HY_W = 512
HY_ORDER = 2
HY_EMB = 33
HY_FFN = 64
HY_TARGET = 1e-2
HY_FAST_PCT = 0.3
HY_SLOW_PCT = 1.5
DA_DH = 64
DA_HEADS = 4
DA_W = 512
ODD_IN = 3 * HY_W + 3 * DA_W
ROPE_THETA = 10000.0

N_GROUPS = 4
EXP_PER_GROUP = 8
N_EXPERTS = 32
TOP_K = 2
D_EXPERT = 512

N_SAMPLE = DEC_BATCH * DEC_SEQ
N_PROMPT = BATCH * SEQ
N_TOK = N_SAMPLE + N_PROMPT
SEG = 1024
N_SEG = N_TOK // SEG
MOD_ROWS = 8

LANES = 128
SUBLANES = 8
VMEM_LIMIT = 56 * 1024 * 1024

TM = 256
MOE_BLK = 256
ROW_ALIGN = SUBLANES
R_LOC = (TOP_K * TM + N_EXPERTS * (ROW_ALIGN - 1) + LANES - 1) // LANES * LANES
CHUNKS = (256, 128, 64, 32, 16, 8)
MOE_ROWS = ((N_TOK * TOP_K + N_EXPERTS * (N_TOK // TM) * (ROW_ALIGN - 1) + N_EXPERTS * (MOE_BLK - 1))
            // MOE_BLK + 1) * MOE_BLK
MOE_NB = MOE_ROWS // MOE_BLK
HY_SLAB = 256
ATT_TQ = 256


def _cparams(sem, vmem=VMEM_LIMIT):
    return pltpu.CompilerParams(dimension_semantics=sem, vmem_limit_bytes=vmem)


def _silu(x):
    return x * (1.0 / (1.0 + jnp.exp(-x)))


def _split_bf16(a):
    hi = a.astype(BF16)
    return hi, (a - hi.astype(F32)).astype(BF16)


def _dot_3pass(a, b):
    ah, al = _split_bf16(a)
    bh, bl = _split_bf16(b)
    dot = functools.partial(jnp.dot, preferred_element_type=F32)
    return dot(ah, bh) + (dot(ah, bl) + dot(al, bh))


MODS_TN = 1536


def _mods_kernel(c_ref, w_ref, b_ref, o_ref):
    s = _silu(c_ref[...])
    o_ref[0] = jnp.dot(s, w_ref[0], preferred_element_type=F32, precision=HIGHEST) + b_ref[0]


def _mods(cvecs, ada_w, ada_b):
    n = N_MOD * D_MODEL
    return pl.pallas_call(
        _mods_kernel,
        grid=(DEPTH, n // MODS_TN),
        in_specs=[
            pl.BlockSpec((16, D_MODEL), lambda l, j: (0, 0)),
            pl.BlockSpec((1, D_MODEL, MODS_TN), lambda l, j: (l, 0, j)),
            pl.BlockSpec((1, 1, MODS_TN), lambda l, j: (l, 0, j)),
        ],
        out_specs=pl.BlockSpec((1, 16, MODS_TN), lambda l, j: (l, 0, j)),
        out_shape=jax.ShapeDtypeStruct((DEPTH, 16, n), F32),
        compiler_params=_cparams(("parallel", "parallel")),
        name="adaln_mods",
    )(cvecs, ada_w, ada_b.reshape(DEPTH, 1, n))


def _norm_mod(x, g, mod, shift_idx, scale_idx):
    ms = jnp.mean(x * x, axis=-1, keepdims=True)
    y = x * lax.rsqrt(ms + EPS) * g
    return y * (1.0 + mod[scale_idx:scale_idx + 1]) + mod[shift_idx:shift_idx + 1]


N_LAT_TILES = N_SAMPLE // TM
TM_WIDE = 512


def _group_specs(width, tm=TM):
    n_lat = N_SAMPLE // tm
    lat = pl.BlockSpec((tm, width), lambda i: (jnp.minimum(i, n_lat - 1), 0))
    ctx = pl.BlockSpec((tm, width), lambda i: (jnp.maximum(i - n_lat, 0), 0))
    return [lat, ctx]


def _group_pick(lat_ref, ctx_ref):
    n_lat = N_SAMPLE // lat_ref.shape[0]
    return jnp.where(pl.program_id(0) < n_lat, lat_ref[...], ctx_ref[...])


def _rows_specs(x, tm=TM):
    if isinstance(x, tuple):
        return _group_specs(D_MODEL, tm), list(x)
    return [pl.BlockSpec((tm, D_MODEL), lambda i: (i, 0))], [x]


def _rows_value(refs):
    return _group_pick(*refs) if len(refs) == 2 else refs[0][...]


def _nmm_kernel(*refs, n_x, shift_idx, scale_idx):
    g_ref, mod_ref, w_ref, o_ref = refs[n_x:]
    u = _norm_mod(_rows_value(refs[:n_x]), g_ref[...], mod_ref[0], shift_idx, scale_idx)
    o_ref[...] = jnp.dot(u.astype(BF16), w_ref[...], preferred_element_type=F32)


def _norm_mod_matmul(x, g, mods, w_bf16, shift_idx, scale_idx):
    n = w_bf16.shape[1]
    x_specs, x_args = _rows_specs(x)
    return pl.pallas_call(
        functools.partial(_nmm_kernel, n_x=len(x_args), shift_idx=shift_idx, scale_idx=scale_idx),
        grid=(N_TOK // TM,),
        in_specs=x_specs + [
            pl.BlockSpec((1, D_MODEL), lambda i: (0, 0)),
            pl.BlockSpec((1, MOD_ROWS, D_MODEL), lambda i: (i * TM // SEG, 0, 0)),
            pl.BlockSpec((D_MODEL, n), lambda i: (0, 0)),
        ],
        out_specs=pl.BlockSpec((TM, n), lambda i: (i, 0)),
        out_shape=jax.ShapeDtypeStruct((N_TOK, n), F32),
        compiler_params=_cparams(("parallel",)),
        name="norm_mod_inproj",
    )(*x_args, g.reshape(1, D_MODEL), mods, w_bf16)


def _proj_res_kernel(*refs, n_x, gate_idx):
    al_ref, ac_ref, bl_ref, bc_ref, wa_ref, wb_ref, mod_ref, o_ref = refs[n_x:]
    a = _group_pick(al_ref, ac_ref)
    b = _group_pick(bl_ref, bc_ref)
    y = jnp.dot(a.astype(BF16), wa_ref[...], preferred_element_type=F32)
    y = y + jnp.dot(b.astype(BF16), wb_ref[...], preferred_element_type=F32)
    o_ref[...] = _rows_value(refs[:n_x]) + mod_ref[0][gate_idx:gate_idx + 1] * y


def _proj_residual(x, a, b, w_bf16, mods, gate_idx):
    half = a[0].shape[1]
    tm = TM_WIDE
    x_specs, x_args = _rows_specs(x, tm)
    return pl.pallas_call(
        functools.partial(_proj_res_kernel, n_x=len(x_args), gate_idx=gate_idx),
        grid=(N_TOK // tm,),
        in_specs=x_specs + _group_specs(half, tm) + _group_specs(half, tm) + [
            pl.BlockSpec((half, D_MODEL), lambda i: (0, 0)),
            pl.BlockSpec((half, D_MODEL), lambda i: (1, 0)),
            pl.BlockSpec((1, MOD_ROWS, D_MODEL), lambda i: (i * tm // SEG, 0, 0)),
        ],
        out_specs=pl.BlockSpec((tm, D_MODEL), lambda i: (i, 0)),
        out_shape=jax.ShapeDtypeStruct((N_TOK, D_MODEL), F32),
        compiler_params=_cparams(("parallel",)),
        name="outproj_residual",
    )(*x_args, a[0], a[1], b[0], b[1], w_bf16, w_bf16, mods)


def _conv3_rows(z, w):
    n = z.shape[0]
    row = lax.broadcasted_iota(I32, z.shape, 0)
    zm = jnp.where(row == 0, 0.0, pltpu.roll(z, 1, 0))
    zp = jnp.where(row == n - 1, 0.0, pltpu.roll(z, n - 1, 0))
    return zm * w[0:1] + z * w[1:2] + zp * w[2:3]


def _log_gamma(dec):
    return jnp.log1p(-jnp.exp(dec))


def _even_kernel(*refs, seq, latent, tq, heads):
    (q_ref, k_ref, v_ref, g_ref, gb_ref, gc_ref, hx_ref, scw_ref, df_ref, db_ref) = refs[:10]
    if latent:
        s0_ref, ret_ref, sc_ref, mask_ref = refs[10:]
    else:
        ret_ref, sc_ref, st_ref, mask_ref = refs[10:]

    for hh in range(heads):
        lanes = slice(hh * RET_DK, (hh + 1) * RET_DK)
        lgf = _log_gamma(df_ref[hh])
        lgb = _log_gamma(db_ref[hh])

        @pl.when(pl.program_id(1) == 0)
        def _():
            for r in range(seq // tq):
                t = lax.broadcasted_iota(I32, (tq, seq), 0) + r * tq
                s = lax.broadcasted_iota(I32, (tq, seq), 1)
                d = (t - s).astype(F32)
                mf = jnp.where(d >= 0, jnp.exp(jnp.maximum(d, 0.0) * lgf), 0.0)
                mb = jnp.where(d <= 0, jnp.exp(jnp.maximum(-d, 0.0) * lgb), 0.0)
                mask_ref[hh, r * tq:(r + 1) * tq, :] = mf + mb

        k = k_ref[:, lanes] * (RET_DK ** -0.5)
        kb = k.astype(BF16)
        vb = v_ref[:, lanes].astype(BF16)
        if latent:
            s0f = s0_ref[0, 0, 0, hh].astype(BF16)
            s0b = s0_ref[0, 0, 1, hh].astype(BF16)
        for r in range(seq // tq):
            rows = slice(r * tq, (r + 1) * tq)
            qb = q_ref[rows, lanes].astype(BF16)
            s = lax.dot_general(qb, kb, (((1,), (1,)), ((), ())), preferred_element_type=F32)
            p = (s * mask_ref[hh, rows, :]).astype(BF16)
            o = jnp.dot(p, vb, preferred_element_type=F32)
            if latent:
                tpos = (lax.broadcasted_iota(I32, (tq, 1), 0) + r * tq).astype(F32)
                o = o + jnp.dot(qb, s0f, preferred_element_type=F32) * jnp.exp((tpos + 1.0) * lgf)
                o = o + jnp.dot(qb, s0b, preferred_element_type=F32) * jnp.exp((seq - tpos) * lgb)
            mu = jnp.mean(o, axis=-1, keepdims=True)
            var = jnp.mean(jnp.square(o - mu), axis=-1, keepdims=True)
            on = (o - mu) * lax.rsqrt(var + GN_EPS)
            ret_ref[rows, lanes] = _silu(g_ref[rows, lanes]) * on

        if not latent:
            spos = lax.broadcasted_iota(I32, (seq, 1), 0).astype(F32)
            kf = (k * jnp.exp((seq - 1.0 - spos) * lgf)).T.astype(BF16)
            kr = (k * jnp.exp(spos * lgb)).T.astype(BF16)
            st_ref[0, 0, 0, hh] = jnp.dot(kf, vb, preferred_element_type=F32)
            st_ref[0, 0, 1, hh] = jnp.dot(kr, vb, preferred_element_type=F32)

    sc_ref[...] = gb_ref[...] * _conv3_rows(gc_ref[...] * hx_ref[...], scw_ref[...])


def _even_mixer(proj, sc_w_pad, dec_f, dec_b, state=None):
    latent = state is not None
    seq = DEC_SEQ if latent else SEQ
    nb = DEC_BATCH if latent else BATCH
    row0 = 0 if latent else N_SAMPLE // seq
    tq = min(seq, 256)
    heads = 1 if latent else RET_HEADS
    steps = RET_HEADS // heads
    hw = heads * RET_DK

    def col(c0):
        return pl.BlockSpec((seq, hw), lambda h, b: (row0 + b, c0 * steps + h))

    in_specs = [col(0), col(1), col(2), col(3), col(4), col(5), col(6),
                pl.BlockSpec((SUBLANES, hw), lambda h, b: (0, h)),
                pl.BlockSpec((heads, 1, 1), lambda h, b: (h, 0, 0)),
                pl.BlockSpec((heads, 1, 1), lambda h, b: (h, 0, 0))]
    args = [proj] * 7 + [sc_w_pad, dec_f.reshape(RET_HEADS, 1, 1), dec_b.reshape(RET_HEADS, 1, 1)]
    out_spec = pl.BlockSpec((seq, hw), lambda h, b: (b, h))
    out_shape = [jax.ShapeDtypeStruct((nb * seq, RET_W), F32), jax.ShapeDtypeStruct((nb * seq, SC_W), F32)]
    out_specs = [out_spec, out_spec]
    st_spec = pl.BlockSpec((1, 1, 2, heads, RET_DK, RET_DV), lambda h, b: (b, 0, 0, h, 0, 0))
    if latent:
        in_specs.append(st_spec)
        args.append(state)
    else:
        out_shape.append(jax.ShapeDtypeStruct((BATCH, 1, 2, RET_HEADS, RET_DK, RET_DV), F32))
        out_specs.append(st_spec)
    return pl.pallas_call(
        functools.partial(_even_kernel, seq=seq, latent=latent, tq=tq, heads=heads),
        grid=(steps, nb),
        in_specs=in_specs,
        out_specs=out_specs,
        out_shape=out_shape,
        scratch_shapes=[pltpu.VMEM((heads, seq, seq), F32)],
        compiler_params=_cparams(("arbitrary", "arbitrary")),
        name="even_mixer_latent" if latent else "even_mixer_context",
    )(*args)


def _dft_tables(length):
    f = np.arange(length, dtype=np.float64)[:, None]
    t = np.arange(length, dtype=np.float64)[None, :]
    ang = np.pi * ((f * t) % (2 * length)) / length
    cos, sin = np.cos(ang), np.sin(ang)
    sin[0, :] = (-1.0) ** np.arange(length)
    fwd = np.concatenate([cos, sin], axis=0)
    wgt = np.full((2 * length,), 2.0)
    wgt[0] = 1.0
    wgt[length] = 1.0
    inv = (fwd * wgt[:, None] / (2.0 * length)).T
    return fwd.astype(np.float32), np.ascontiguousarray(inv).astype(np.float32)


def _hyena_consts(length):
    t = np.linspace(0.0, 1.0, length)[:, None]
    bands = (HY_EMB - 1) // 2
    w = 2.0 * np.pi * np.arange(length)[:, None] / length
    f = np.linspace(1e-4, bands - 1, bands)[None, :]
    z = np.concatenate([t, np.cos(f * w), -np.sin(f * w)], axis=-1)
    zpad = np.zeros((length, LANES))
    zpad[:, :HY_EMB] = z
    deltas = np.abs(np.linspace(math.log(HY_TARGET) / HY_SLOW_PCT, math.log(HY_TARGET) / HY_FAST_PCT, HY_W))
    decay = np.exp(-t * deltas[None, :])
    return zpad.astype(np.float32), decay.astype(np.float32)


def _cast_kernel(x_ref, o_ref):
    o_ref[...] = x_ref[...].astype(o_ref.dtype)


def _cast_bf16(x, name):
    rows = min(x.shape[0], 512)
    return pl.pallas_call(
        _cast_kernel,
        grid=(x.shape[0] // rows,),
        in_specs=[pl.BlockSpec((rows, x.shape[1]), lambda i: (i, 0))],
        out_specs=pl.BlockSpec((rows, x.shape[1]), lambda i: (i, 0)),
        out_shape=jax.ShapeDtypeStruct(x.shape, BF16),
        compiler_params=_cparams(("parallel",)),
        name=name,
    )(x)


def _hy_filter_kernel(z_ref, dec_ref, w1_ref, b1_ref, f1_ref, w2_ref, b2_ref, f2_ref, w3f_ref, w3b_ref,
                      fwd_ref, ka_ref, ka2_ref, kb_ref, *, seq):
    h = jnp.sin(f1_ref[...] * (jnp.dot(z_ref[...], w1_ref[...], preferred_element_type=F32,
                                       precision=HIGHEST) + b1_ref[...]))
    h = jnp.sin(f2_ref[...] * (jnp.dot(h, w2_ref[...], preferred_element_type=F32,
                                       precision=HIGHEST) + b2_ref[...]))
    dec = dec_ref[...]
    hf = jnp.dot(h, w3f_ref[...], preferred_element_type=F32, precision=HIGHEST) * dec
    hb = jnp.dot(h, w3b_ref[...], preferred_element_type=F32, precision=HIGHEST) * dec
    row = lax.broadcasted_iota(I32, hb.shape, 0)
    hb = jnp.where(row == 0, 0.0, hb)
    inv_norm = 1.0 / (jnp.sum(jnp.abs(hf), axis=0, keepdims=True)
                      + jnp.sum(jnp.abs(hb), axis=0, keepdims=True) + EPS)
    p = (hf + hb) * inv_norm
    q = (hf - hb) * inv_norm
    fp = _dot_3pass(fwd_ref[...], p)
    fq = _dot_3pass(fwd_ref[seq:, :], q)
    kr = fp[:seq]
    row = lax.broadcasted_iota(I32, kr.shape, 0)
    ka_ref[0] = kr
    ka2_ref[0] = jnp.where(row == 0, fp[seq:seq + 1], kr)
    kb_ref[0] = jnp.where(row == 0, 0.0, -fq)


def _hy_filters(seq, zfeat, decay, fwd, w1, b1, f1, w2, b2, f2, w3):
    ns = HY_W // HY_SLAB
    per_o = 2 * ns
    small = lambda shape: pl.BlockSpec(shape, lambda o, s: (0, 0))
    out_spec = pl.BlockSpec((1, seq, HY_SLAB), lambda o, s: (o, 0, s))
    out = jax.ShapeDtypeStruct((HY_ORDER, seq, HY_W), F32)
    return pl.pallas_call(
        functools.partial(_hy_filter_kernel, seq=seq),
        grid=(HY_ORDER, ns),
        in_specs=[
            small((seq, LANES)),
            pl.BlockSpec((seq, HY_SLAB), lambda o, s: (0, s)),
            small((LANES, LANES)), small((1, LANES)), small((1, LANES)),
            small((LANES, LANES)), small((1, LANES)), small((1, LANES)),
            pl.BlockSpec((LANES, HY_SLAB), lambda o, s: (0, o * per_o + s)),
            pl.BlockSpec((LANES, HY_SLAB), lambda o, s: (0, o * per_o + ns + s)),
            small((2 * seq, seq)),
        ],
        out_specs=[out_spec, out_spec, out_spec],
        out_shape=[out, out, out],
        compiler_params=_cparams(("parallel", "parallel")),
        name=f"hyena_filters_{seq}",
    )(zfeat, decay, w1, b1, f1, w2, b2, f2, w3, w3, fwd)


def _hy_apply_kernel(x1_ref, x2_ref, v_ref, c1_ref, c2_ref, cv_ref, fwd_ref, inv_ref,
                     ka_ref, ka2_ref, kb_ref, bias_ref, o_ref, *, seq):
    x1 = _conv3_rows(x1_ref[...], c1_ref[...])
    x2 = _conv3_rows(x2_ref[...], c2_ref[...])
    v = _conv3_rows(v_ref[...], cv_ref[...])

    def long_conv(sig, o):
        spec = jnp.dot(fwd_ref[...], sig.astype(BF16), preferred_element_type=F32)
        xr, xs = spec[:seq], spec[seq:]
        kb = kb_ref[o]
        yr = (xr * ka_ref[o] + xs * kb).astype(BF16)
        ys = (xs * ka2_ref[o] - xr * kb).astype(BF16)
        y = jnp.dot(inv_ref[:, :seq], yr, preferred_element_type=F32)
        y = y + jnp.dot(inv_ref[:, seq:], ys, preferred_element_type=F32)
        return y + sig * bias_ref[o:o + 1]

    z = x1 * long_conv(v, 0)
    o_ref[...] = x2 * long_conv(z, 1)


def _hy_apply(proj, conv_w_pad, fwd_bf, inv_bf, filters, bias_pad, latent):
    seq = DEC_SEQ if latent else SEQ
    nb = DEC_BATCH if latent else BATCH
    row0 = 0 if latent else N_SAMPLE // seq
    ns = HY_W // HY_SLAB
    const = lambda shape: pl.BlockSpec(shape, lambda s, b: (0,) * len(shape))
    col = lambda c0: pl.BlockSpec((seq, HY_SLAB), lambda s, b: (row0 + b, c0 + s))
    cw = lambda c0: pl.BlockSpec((SUBLANES, HY_SLAB), lambda s, b: (0, c0 + s))
    filt = pl.BlockSpec((HY_ORDER, seq, HY_SLAB), lambda s, b: (0, 0, s))
    in_specs = [col(0), col(ns), col(2 * ns), cw(0), cw(ns), cw(2 * ns),
                const((2 * seq, seq)), const((seq, 2 * seq)), filt, filt, filt,
                pl.BlockSpec((SUBLANES, HY_SLAB), lambda s, b: (0, s))]
    args = [proj] * 3 + [conv_w_pad] * 3 + [fwd_bf, inv_bf, *filters, bias_pad]
    return pl.pallas_call(
        functools.partial(_hy_apply_kernel, seq=seq),
        grid=(ns, nb),
        in_specs=in_specs,
        out_specs=pl.BlockSpec((seq, HY_SLAB), lambda s, b: (b, s)),
        out_shape=jax.ShapeDtypeStruct((nb * seq, HY_W), F32),
        compiler_params=_cparams(("arbitrary", "arbitrary")),
        name="hyena_latent" if latent else "hyena_context",
    )(*args)


def _rope_tables(length):
    rows = length // GRID_W
    row = np.repeat(np.arange(rows, dtype=np.float64), GRID_W)
    colp = np.tile(np.arange(GRID_W, dtype=np.float64), rows)
    half = DA_DH // 2
    freqs = ROPE_THETA ** (-np.arange(0, half, 2, dtype=np.float64) / half)
    ang = np.concatenate([row[:, None] * freqs[None, :]] * 2 + [colp[:, None] * freqs[None, :]] * 2, axis=1)
    ang = np.concatenate([ang, ang], axis=1)
    lane = np.arange(LANES)
    sign = np.where((lane % half) < half // 2, -1.0, 1.0)[None, :]
    return np.cos(ang).astype(np.float32), (np.sin(ang) * sign).astype(np.float32)


def _rope(x, cos, sin_signed):
    q = DA_DH // 4
    lane = lax.broadcasted_iota(I32, x.shape, 1)
    partner = jnp.where((lane % (2 * q)) < q, pltpu.roll(x, LANES - q, 1), pltpu.roll(x, q, 1))
    return x * cos + partner * sin_signed


def _attn_kernel(*refs, seq, latent, lambda_init, heads):
    q_ref, k_ref, v_ref, lam_ref, sub_ref = refs[:5]
    if latent:
        cos_ref, sin_ref, ck_ref, cv_ref, o_ref = refs[5:]
    else:
        o_ref, nk_ref, nv_ref = refs[5:]
    hw = 2 * DA_DH
    lv = lam_ref[...]
    lam = (jnp.exp(jnp.sum(lv[0:1] * lv[1:2], axis=1, keepdims=True))
           - jnp.exp(jnp.sum(lv[2:3] * lv[3:4], axis=1, keepdims=True)) + lambda_init)
    scale = DA_DH ** -0.5
    dn = (((1,), (1,)), ((), ()))
    tq = min(seq, ATT_TQ)
    for hh in range(heads):
        lanes = slice(hh * hw, (hh + 1) * hw)
        k = k_ref[:, lanes]
        v = v_ref[:, lanes]
        if latent:
            k = _rope(k, cos_ref[...], sin_ref[...])
            k_all = jnp.concatenate([k, jnp.concatenate([ck_ref[0, 0, 0, hh], ck_ref[0, 0, 1, hh]], axis=1)], axis=0)
            v_all = jnp.concatenate([v, cv_ref[0, 0, hh]], axis=0)
        else:
            k_all, v_all = k, v
            nk_ref[0, 0, 0, hh] = k[:, :DA_DH]
            nk_ref[0, 0, 1, hh] = k[:, DA_DH:]
            nv_ref[0, 0, hh] = v
        kb = k_all.astype(BF16)
        vb = v_all.astype(BF16)
        for r in range(seq // tq):
            rows = slice(r * tq, (r + 1) * tq)
            q = q_ref[rows, lanes]
            if latent:
                q = _rope(q, cos_ref[rows, :], sin_ref[rows, :])
            q = q * scale
            first = lax.broadcasted_iota(I32, q.shape, 1) < DA_DH
            q1 = jnp.where(first, q, 0.0).astype(BF16)
            q2 = jnp.where(first, 0.0, q).astype(BF16)
            s1 = lax.dot_general(q1, kb, dn, preferred_element_type=F32)
            s2 = lax.dot_general(q2, kb, dn, preferred_element_type=F32)
            e1 = jnp.exp(s1 - jnp.max(s1, axis=-1, keepdims=True))
            e2 = jnp.exp(s2 - jnp.max(s2, axis=-1, keepdims=True))
            o1 = jnp.dot(e1.astype(BF16), vb, preferred_element_type=F32)
            o2 = jnp.dot(e2.astype(BF16), vb, preferred_element_type=F32)
            o = (o1 * (1.0 / jnp.sum(e1, axis=-1, keepdims=True))
                 - o2 * (lam / jnp.sum(e2, axis=-1, keepdims=True)))
            o = o * lax.rsqrt(jnp.mean(o * o, axis=-1, keepdims=True) + EPS) * sub_ref[...]
            o_ref[rows, lanes] = o * (1.0 - lambda_init)


def _attention(proj, lam_vec, subln, lambda_init, latent, rope=None, ctx_k=None, ctx_v=None):
    seq = DEC_SEQ if latent else SEQ
    nb = DEC_BATCH if latent else BATCH
    row0 = 0 if latent else N_SAMPLE // seq
    hw = 2 * DA_DH
    heads = 1 if latent else DA_HEADS
    steps = DA_HEADS // heads
    col = lambda c: pl.BlockSpec((seq, heads * hw), lambda b, h: (row0 + b, c * steps + h))
    in_specs = [col(3), col(4), col(5),
                pl.BlockSpec((4, DA_DH), lambda b, h: (0, 0)),
                pl.BlockSpec((1, hw), lambda b, h: (0, 0))]
    args = [proj] * 3 + [lam_vec, subln.reshape(1, hw)]
    out_spec = pl.BlockSpec((seq, heads * hw), lambda b, h: (b, h))
    out_shape = [jax.ShapeDtypeStruct((nb * seq, DA_W), F32)]
    out_specs = [out_spec]
    k_spec = pl.BlockSpec((1, 1, 2, heads, SEQ, DA_DH), lambda b, h: (b, 0, 0, h, 0, 0))
    v_spec = pl.BlockSpec((1, 1, heads, SEQ, hw), lambda b, h: (b, 0, h, 0, 0))
    if latent:
        in_specs += [pl.BlockSpec((seq, hw), lambda b, h: (0, 0))] * 2 + [k_spec, v_spec]
        args += [rope[0], rope[1], ctx_k, ctx_v]
    else:
        out_shape += [jax.ShapeDtypeStruct((BATCH, 1, 2, DA_HEADS, SEQ, DA_DH), F32),
                      jax.ShapeDtypeStruct((BATCH, 1, DA_HEADS, SEQ, hw), F32)]
        out_specs += [k_spec, v_spec]
    return pl.pallas_call(
        functools.partial(_attn_kernel, seq=seq, latent=latent, lambda_init=lambda_init, heads=heads),
        grid=(nb, steps),
        in_specs=in_specs,
        out_specs=out_specs,
        out_shape=out_shape,
        compiler_params=_cparams(("arbitrary", "arbitrary")),
        name="diff_attn_latent" if latent else "diff_attn_context",
    )(*args)


HALF_D = D_MODEL // 2
HI16 = -65536


def _pack_bf16_pair(lo, hi):
    lo_bits = lax.bitcast_convert_type(lo.astype(BF16).astype(F32), I32)
    hi_bits = lax.bitcast_convert_type(hi.astype(BF16).astype(F32), I32)
    return (hi_bits & HI16) | lax.shift_right_logical(lo_bits, 16)


def _unpack_bf16_pair(p):
    lo = lax.bitcast_convert_type(lax.shift_left(p, 16), F32).astype(BF16)
    hi = lax.bitcast_convert_type(p & HI16, F32).astype(BF16)
    return lo, hi


GRP_EXPERT, GRP_REL, GRP_COUNT = range(3)


def _router_kernel(x_ref, g_ref, mod_ref, wr_ref, br_ref, tril_ref, triu_ref,
                   u_ref, loct_ref, info_ref, grp_ref, tot_ref, carry_ref):
    @pl.when(pl.program_id(0) == 0)
    def _():
        carry_ref[...] = jnp.zeros_like(carry_ref)

    u = _norm_mod(x_ref[...], g_ref[...], mod_ref[0], 3, 4)
    u_ref[...] = _pack_bf16_pair(u[:, :HALF_D], u[:, HALF_D:])
    logits = _dot_3pass(u, wr_ref[...]) + br_ref[...]
    lane = lax.broadcasted_iota(I32, logits.shape, 1)
    lane_f = lane.astype(F32)
    neg = -jnp.inf
    big = float(LANES)

    def first_max(vals):
        m = jnp.max(vals, axis=-1, keepdims=True)
        return m, jnp.min(jnp.where(vals == m, lane_f, big), axis=-1, keepdims=True)

    is_grp = lane < N_GROUPS
    gmax, gidx = first_max(jnp.where(is_grp, logits, neg))
    gsum = jnp.sum(jnp.where(is_grp, jnp.exp(logits - gmax), 0.0), axis=-1, keepdims=True)
    g_w = 1.0 / gsum
    lo = N_GROUPS + EXP_PER_GROUP * gidx
    in_grp = jnp.logical_and(lane_f >= lo, lane_f < lo + EXP_PER_GROUP)
    el = jnp.where(in_grp, logits, neg)
    v1, i1 = first_max(el)
    v2, i2 = first_max(jnp.where(lane_f == i1, neg, el))
    e = jnp.exp(v2 - v1)
    p1 = 1.0 / (1.0 + e)
    hit1 = lane_f == i1 - N_GROUPS
    hit2 = lane_f == i2 - N_GROUPS
    onehot = jnp.where(jnp.logical_or(hit1, hit2), 1.0, 0.0)
    cum = jnp.dot(tril_ref[...], onehot.astype(BF16), preferred_element_type=F32)
    n = cum[TM - 1:TM, :]
    n_pad = jnp.floor((n + (ROW_ALIGN - 1.0)) * (1.0 / ROW_ALIGN)) * ROW_ALIGN
    loc_off = jnp.dot(jnp.broadcast_to(n_pad, (SUBLANES, LANES)).astype(BF16), triu_ref[...],
                      preferred_element_type=F32)[0:1]
    base = cum - onehot + loc_off
    loc1 = jnp.sum(jnp.where(hit1, base, 0.0), axis=-1, keepdims=True)
    loc2 = jnp.sum(jnp.where(hit2, base, 0.0), axis=-1, keepdims=True)
    info = jnp.where(lane == 0, g_w * p1, jnp.where(lane == 1, g_w * (e * p1), jnp.where(lane == 2, loc1, loc2)))
    info_ref[...] = info
    loct_ref[0] = info.T[:SUBLANES]
    start8 = lax.broadcasted_iota(I32, (LANES, LANES), 0).astype(F32) * ROW_ALIGN
    owner = jnp.where(jnp.logical_and(start8 >= loc_off, start8 < loc_off + n_pad), 1.0, 0.0)
    lane128 = lax.broadcasted_iota(I32, (LANES, LANES), 1)
    c_exp = jnp.sum(owner * lane128.astype(F32), axis=-1, keepdims=True)
    c_rel = jnp.sum(owner * (carry_ref[...] + start8 - loc_off), axis=-1, keepdims=True)
    chunks = jnp.where(lane128 == 0, c_exp, jnp.where(lane128 == 1, c_rel, 0.0)).T
    n_chunks = jnp.sum(n_pad, axis=-1, keepdims=True) * (1.0 / ROW_ALIGN)
    row = lax.broadcasted_iota(I32, (SUBLANES, LANES), 0)
    grp = jnp.where(row == GRP_EXPERT, chunks[0:1], jnp.where(row == GRP_REL, chunks[1:2], n_chunks))
    grp_ref[0] = grp.astype(I32)
    carry_ref[...] = carry_ref[...] + n_pad
    tot_ref[...] = jnp.broadcast_to(carry_ref[...], tot_ref.shape).astype(I32)


def _router(x, g, mods, w_route, b_route, tril, triu):
    nt = N_TOK // TM
    return pl.pallas_call(
        _router_kernel,
        grid=(nt,),
        in_specs=[
            pl.BlockSpec((TM, D_MODEL), lambda i: (i, 0)),
            pl.BlockSpec((1, D_MODEL), lambda i: (0, 0)),
            pl.BlockSpec((1, MOD_ROWS, D_MODEL), lambda i: (i * TM // SEG, 0, 0)),
            pl.BlockSpec((D_MODEL, LANES), lambda i: (0, 0)),
            pl.BlockSpec((1, LANES), lambda i: (0, 0)),
            pl.BlockSpec((TM, TM), lambda i: (0, 0)),
            pl.BlockSpec((LANES, LANES), lambda i: (0, 0)),
        ],
        out_specs=[
            pl.BlockSpec((TM, HALF_D), lambda i: (i, 0)),
            pl.BlockSpec((1, SUBLANES, TM), lambda i: (i, 0, 0)),
            pl.BlockSpec((TM, LANES), lambda i: (i, 0)),
            pl.BlockSpec((1, SUBLANES, LANES), lambda i: (i, 0, 0)),
            pl.BlockSpec((SUBLANES, LANES), lambda i: (0, 0)),
        ],
        out_shape=[
            jax.ShapeDtypeStruct((N_TOK, HALF_D), I32),
            jax.ShapeDtypeStruct((nt, SUBLANES, TM), F32),
            jax.ShapeDtypeStruct((N_TOK, LANES), F32),
            jax.ShapeDtypeStruct((nt, SUBLANES, LANES), I32),
            jax.ShapeDtypeStruct((SUBLANES, LANES), I32),
        ],
        scratch_shapes=[pltpu.VMEM((1, LANES), F32)],
        compiler_params=_cparams(("arbitrary",)),
        name="moe_router",
    )(x, g.reshape(1, D_MODEL), mods, w_route, b_route, tril, triu)


def _for_chunks(n, fn):
    for size in CHUNKS:
        @pl.when((n & size) != 0)
        def _():
            fn(n & ~(2 * size - 1), size)


WAIT_SIZES = (512, 256, 128, 64, 32, 16, 8)


def _start_tile_copies(grp_ref, seg_ref, make):
    def body(c, carry):
        slot = seg_ref[0, grp_ref[0, GRP_EXPERT, c]] + grp_ref[0, GRP_REL, c]
        make(pl.ds(pl.multiple_of(c * ROW_ALIGN, ROW_ALIGN), ROW_ALIGN),
             pl.ds(pl.multiple_of(slot, ROW_ALIGN), ROW_ALIGN)).start()
        return carry

    lax.fori_loop(0, grp_ref[0, GRP_COUNT, 0], body, 0)


def _wait_tile_copies(grp_ref, make):
    rows = grp_ref[0, GRP_COUNT, 0] * ROW_ALIGN
    for size in WAIT_SIZES:
        @pl.when((rows & size) != 0)
        def _():
            make(pl.ds(0, size), pl.ds(0, size)).wait()


def _all_experts(body):
    lax.fori_loop(0, N_EXPERTS, lambda e, c: (body(e), c)[1], 0)


def _dispatch_kernel(grp_ref, grp1_ref, grp2_ref, seg_ref, loct_ref, u_ref, xb_ref, buf, zbuf, sem):
    i = pl.program_id(0)
    last = pl.num_programs(0) - 1
    cur = i % 2

    def maker(b):
        return lambda loc, slot: pltpu.make_async_copy(buf.at[b, loc], xb_ref.at[slot], sem.at[b])

    @pl.when(i >= 2)
    def _():
        _wait_tile_copies(grp2_ref, maker(cur))

    lo, hi = _unpack_bf16_pair(u_ref[...])
    row = lax.broadcasted_iota(I32, (R_LOC, TM), 0).astype(F32)
    sel = jnp.logical_or(row == loct_ref[0, 2:3, :], row == loct_ref[0, 3:4, :])
    pt = jnp.where(sel, 1.0, 0.0).astype(BF16)
    buf[cur] = _pack_bf16_pair(jnp.dot(pt, lo, preferred_element_type=F32),
                               jnp.dot(pt, hi, preferred_element_type=F32))
    _start_tile_copies(grp_ref, seg_ref, maker(cur))

    @pl.when(i == last)
    def _():
        @pl.when(i >= 1)
        def _():
            _wait_tile_copies(grp1_ref, maker(1 - cur))

        _wait_tile_copies(grp_ref, maker(cur))

        zbuf[...] = jnp.zeros_like(zbuf)

        def tail(e, act):
            total = seg_ref[1, e]
            dst = seg_ref[0, e] + total

            def one(off, size):
                act(pltpu.make_async_copy(zbuf.at[pl.ds(0, size)],
                                          xb_ref.at[pl.ds(pl.multiple_of(dst + off, ROW_ALIGN), size)], sem.at[2]))

            _for_chunks((-total) & (MOE_BLK - 1), one)

        _all_experts(lambda e: tail(e, lambda cp: cp.start()))
        _all_experts(lambda e: tail(e, lambda cp: cp.wait()))

        e_last = N_EXPERTS - 1
        used = (seg_ref[0, e_last] + seg_ref[1, e_last] + MOE_BLK - 1) // MOE_BLK

        def spare(j):
            return pltpu.make_async_copy(zbuf, xb_ref.at[pl.ds(pl.multiple_of(j * MOE_BLK, MOE_BLK), MOE_BLK)],
                                         sem.at[2])

        lax.fori_loop(used, MOE_NB, lambda j, c: (spare(j).start(), c)[1], 0)
        lax.fori_loop(used, MOE_NB, lambda j, c: (spare(j).wait(), c)[1], 0)


def _grp_spec(back):
    return pl.BlockSpec((1, SUBLANES, LANES), lambda i: (jnp.maximum(i - back, 0), 0, 0), memory_space=pltpu.SMEM)


def _dispatch(grp, seg, loct, u):
    return pl.pallas_call(
        _dispatch_kernel,
        grid=(N_TOK // TM,),
        in_specs=[
            _grp_spec(0), _grp_spec(1), _grp_spec(2),
            pl.BlockSpec(memory_space=pltpu.SMEM),
            pl.BlockSpec((1, SUBLANES, TM), lambda i: (i, 0, 0)),
            pl.BlockSpec((TM, HALF_D), lambda i: (i, 0)),
        ],
        out_specs=pl.BlockSpec(memory_space=pl.ANY),
        out_shape=jax.ShapeDtypeStruct((MOE_ROWS, HALF_D), I32),
        scratch_shapes=[pltpu.VMEM((2, R_LOC, HALF_D), I32), pltpu.VMEM((MOE_BLK, HALF_D), I32),
                        pltpu.SemaphoreType.DMA((3,))],
        compiler_params=_cparams(("arbitrary",)),
        name="moe_dispatch",
    )(grp, grp, grp, seg, loct, u)


def _expert_kernel(seg_ref, w1_ref, w3_ref, w2_ref, xb_ref, yb_ref, w1b, w3b, w2b, xbuf, ybuf, xsem, ysem):
    e = pl.program_id(0)
    last_e = pl.num_programs(0) - 1
    blk0 = seg_ref[0, e] // MOE_BLK
    nblk = (seg_ref[1, e] + MOE_BLK - 1) // MOE_BLK
    used = (seg_ref[0, N_EXPERTS - 1] + seg_ref[1, N_EXPERTS - 1] + MOE_BLK - 1) // MOE_BLK

    def rows(g):
        return pl.ds(pl.multiple_of(g * MOE_BLK, MOE_BLK), MOE_BLK)

    def fetch(g, b):
        return pltpu.make_async_copy(xb_ref.at[rows(g)], xbuf.at[b], xsem.at[b])

    def flush(g, b):
        return pltpu.make_async_copy(ybuf.at[b], yb_ref.at[rows(g)], ysem.at[b])

    @pl.when(e == 0)
    def _():
        fetch(0, 0).start()

    @pl.when(nblk > 0)
    def _():
        w1b[...] = w1_ref[0, 0].astype(BF16)
        w3b[...] = w3_ref[0, 0].astype(BF16)
        w2b[...] = w2_ref[0, 0].astype(BF16)

        def block(i, carry):
            g = blk0 + i
            b = g % 2
            fetch(g, b).wait()

            @pl.when(g + 1 < used)
            def _():
                fetch(g + 1, 1 - b).start()

            @pl.when(g >= 2)
            def _():
                flush(g - 2, b).wait()

            lo, hi = _unpack_bf16_pair(xbuf[b])

            def up(wb):
                return (jnp.dot(lo, wb[:HALF_D, :], preferred_element_type=F32)
                        + jnp.dot(hi, wb[HALF_D:, :], preferred_element_type=F32))

            hdn = (_silu(up(w1b)) * up(w3b)).astype(BF16)
            y = jnp.dot(hdn, w2b[...], preferred_element_type=F32)
            ybuf[b] = _pack_bf16_pair(y[:, :HALF_D], y[:, HALF_D:])
            flush(g, b).start()
            return carry

        lax.fori_loop(0, nblk, block, 0)

    @pl.when(e == last_e)
    def _():
        @pl.when(used >= 2)
        def _():
            flush(used - 2, used % 2).wait()

        flush(used - 1, (used - 1) % 2).wait()
        xbuf[0] = jnp.zeros((MOE_BLK, HALF_D), I32)

        def spare(g):
            return pltpu.make_async_copy(xbuf.at[0], yb_ref.at[rows(g)], xsem.at[0])

        lax.fori_loop(used, MOE_NB, lambda g, c: (spare(g).start(), c)[1], 0)
        lax.fori_loop(used, MOE_NB, lambda g, c: (spare(g).wait(), c)[1], 0)


def _experts(seg, xb, w1, w3, w2, layer):
    hbm = pl.BlockSpec(memory_space=pl.ANY)
    w_in = pl.BlockSpec((1, 1, D_MODEL, D_EXPERT), lambda e, s: (layer, e, 0, 0))
    w_out = pl.BlockSpec((1, 1, D_EXPERT, D_MODEL), lambda e, s: (layer, e, 0, 0))
    return pl.pallas_call(
        _expert_kernel,
        grid_spec=pltpu.PrefetchScalarGridSpec(
            num_scalar_prefetch=1,
            grid=(N_EXPERTS,),
            in_specs=[w_in, w_in, w_out, hbm],
            out_specs=hbm,
            scratch_shapes=[pltpu.VMEM((D_MODEL, D_EXPERT), BF16), pltpu.VMEM((D_MODEL, D_EXPERT), BF16),
                            pltpu.VMEM((D_EXPERT, D_MODEL), BF16),
                            pltpu.VMEM((2, MOE_BLK, HALF_D), I32), pltpu.VMEM((2, MOE_BLK, HALF_D), I32),
                            pltpu.SemaphoreType.DMA((2,)), pltpu.SemaphoreType.DMA((2,))],
        ),
        out_shape=jax.ShapeDtypeStruct((MOE_ROWS, HALF_D), I32),
        compiler_params=_cparams(("arbitrary",)),
        name="moe_experts",
    )(seg, w1, w3, w2, xb)


def _combine_kernel(*refs, final):
    grp_ref, nxt_ref, seg_ref, x_ref, info_ref, mod_ref, yb_ref = refs[:7]
    if final:
        fin_ref, olat_ref, octx_ref, ybuf, sem = refs[7:]
    else:
        o_ref, ybuf, sem = refs[7:]
    i = pl.program_id(0)
    cur = i % 2

    def maker(b):
        return lambda loc, slot: pltpu.make_async_copy(yb_ref.at[slot], ybuf.at[b, loc], sem.at[b])

    @pl.when(i == 0)
    def _():
        ybuf[...] = jnp.zeros_like(ybuf)
        _start_tile_copies(grp_ref, seg_ref, maker(0))

    @pl.when(i + 1 < pl.num_programs(0))
    def _():
        _start_tile_copies(nxt_ref, seg_ref, maker(1 - cur))

    _wait_tile_copies(grp_ref, maker(cur))
    info = info_ref[...]
    col = lax.broadcasted_iota(I32, (TM, R_LOC), 1).astype(F32)
    p = jnp.where(col == info[:, 2:3], info[:, 0:1], jnp.where(col == info[:, 3:4], info[:, 1:2], 0.0))
    p = p.astype(BF16)
    lo, hi = _unpack_bf16_pair(ybuf[cur])
    y = jnp.concatenate([jnp.dot(p, lo, preferred_element_type=F32),
                         jnp.dot(p, hi, preferred_element_type=F32)], axis=1)
    out = x_ref[...] + mod_ref[0][5:6] * y
    if not final:
        o_ref[...] = out
        return
    out = out * lax.rsqrt(jnp.mean(out * out, axis=-1, keepdims=True) + EPS) * fin_ref[...]

    @pl.when(i < N_LAT_TILES)
    def _():
        olat_ref[...] = out

    @pl.when(i >= N_LAT_TILES)
    def _():
        octx_ref[...] = out


def _combine(x, grp, seg, info, mods, yb, final_g=None):
    final = final_g is not None
    nt = N_TOK // TM
    in_specs = [
        _grp_spec(0),
        pl.BlockSpec((1, SUBLANES, LANES), lambda i: (jnp.minimum(i + 1, nt - 1), 0, 0), memory_space=pltpu.SMEM),
        pl.BlockSpec(memory_space=pltpu.SMEM),
        pl.BlockSpec((TM, D_MODEL), lambda i: (i, 0)),
        pl.BlockSpec((TM, LANES), lambda i: (i, 0)),
        pl.BlockSpec((1, MOD_ROWS, D_MODEL), lambda i: (i * TM // SEG, 0, 0)),
        pl.BlockSpec(memory_space=pl.ANY),
    ]
    args = [grp, grp, seg, x, info, mods, yb]
    if final:
        in_specs.append(pl.BlockSpec((1, D_MODEL), lambda i: (0, 0)))
        args.append(final_g.reshape(1, D_MODEL))
        out_specs = _group_specs(D_MODEL)
        out_shape = [jax.ShapeDtypeStruct((N_SAMPLE, D_MODEL), F32), jax.ShapeDtypeStruct((N_PROMPT, D_MODEL), F32)]
    else:
        out_specs = pl.BlockSpec((TM, D_MODEL), lambda i: (i, 0))
        out_shape = jax.ShapeDtypeStruct((N_TOK, D_MODEL), F32)
    return pl.pallas_call(
        functools.partial(_combine_kernel, final=final),
        grid=(nt,),
        in_specs=in_specs,
        out_specs=out_specs,
        out_shape=out_shape,
        scratch_shapes=[pltpu.VMEM((2, R_LOC, HALF_D), I32), pltpu.SemaphoreType.DMA((2,))],
        compiler_params=_cparams(("arbitrary",)),
        name="moe_combine_final" if final else "moe_combine",
    )(*args)


def _moe_layer(x, layer, norm2, mods, w_grp, b_grp, w_exp, b_exp, w1, w3, w2, tril, triu, final_g=None):
    w_route = jnp.zeros((D_MODEL, LANES), F32)
    w_route = w_route.at[:, :N_GROUPS].set(w_grp).at[:, N_GROUPS:N_GROUPS + N_EXPERTS].set(w_exp)
    b_route = jnp.zeros((1, LANES), F32)
    b_route = b_route.at[0, :N_GROUPS].set(b_grp).at[0, N_GROUPS:N_GROUPS + N_EXPERTS].set(b_exp)
    u, loct, info, grp, tot = _router(x, norm2, mods, w_route, b_route, tril, triu)

    total = tot[0, :N_EXPERTS]
    padded = (total + MOE_BLK - 1) // MOE_BLK * MOE_BLK
    pad_end = jnp.cumsum(padded)
    pad_start = pad_end - padded
    seg = jnp.zeros((2, LANES), I32).at[0, :N_EXPERTS].set(pad_start).at[1, :N_EXPERTS].set(total)
    xb = _dispatch(grp, seg, loct, u)
    yb = _experts(seg, xb, w1, w3, w2, layer)
    return _combine(x, grp, seg, info, mods, yb, final_g)


def _pad_rows(w, rows=SUBLANES):
    return jnp.zeros((rows, w.shape[1]), w.dtype).at[:w.shape[0]].set(w)


def _pad2(w, rows, cols):
    return jnp.zeros((rows, cols), w.dtype).at[:w.shape[0], :w.shape[1]].set(w)


def kernel(x_prompt, x_sample, c, c_ctx, state_ret, cache_k, cache_v, ada_w, ada_b, norm1, norm2, final_norm, ev_w_in, ev_w_out, ret_decay_fwd, ret_decay_bwd, sc_conv_w, od_w_in, od_w_out, hy_conv_w, hy_w1, hy_b1, hy_fr1, hy_w2, hy_b2, hy_fr2, hy_w3, hy_bias, da_lambda, da_subln, moe_w_grp, moe_b_grp, moe_w_exp, moe_b_exp, moe_w1, moe_w3, moe_w2):
    x = (x_sample.reshape(N_SAMPLE, D_MODEL), x_prompt.reshape(N_PROMPT, D_MODEL))

    cvecs = jnp.zeros((16, D_MODEL), F32).at[:DEC_BATCH].set(c).at[DEC_BATCH].set(c_ctx)
    m = _mods(cvecs, ada_w, ada_b).reshape(DEPTH, 16, N_MOD, D_MODEL)
    seg_mods = jnp.concatenate(
        [m[:, :DEC_BATCH], jnp.broadcast_to(m[:, DEC_BATCH:DEC_BATCH + 1], (DEPTH, N_SEG - DEC_BATCH, N_MOD, D_MODEL))],
        axis=1)
    seg_mods = jnp.pad(seg_mods, ((0, 0), (0, 0), (0, MOD_ROWS - N_MOD), (0, 0)))

    tril = jnp.asarray(np.tril(np.ones((TM, TM), np.float32)), BF16)
    triu = jnp.asarray(np.triu(np.ones((LANES, LANES), np.float32), 1), BF16)

    for l in range(DEPTH):
        mods = seg_mods[l]
        i = l // 2
        if l % 2 == 0:
            proj = _norm_mod_matmul(x, norm1[l], mods, ev_w_in[i].astype(BF16), 0, 1)
            scw = _pad_rows(sc_conv_w[i])
            ret_l, sc_l = _even_mixer(proj, scw, ret_decay_fwd[i], ret_decay_bwd[i], state=state_ret[:, i:i + 1])
            ret_c, sc_c, new_state = _even_mixer(proj, scw, ret_decay_fwd[i], ret_decay_bwd[i])
            x = _proj_residual(x, (ret_l, ret_c), (sc_l, sc_c), ev_w_out[i].astype(BF16), mods, 2)
        else:
            lambda_init = 0.8 - 0.6 * math.exp(-0.3 * l)
            proj = _norm_mod_matmul(x, norm1[l], mods, od_w_in[i].astype(BF16), 0, 1)
            conv_w = _pad_rows(hy_conv_w[i])
            bias = _pad_rows(hy_bias[i])
            w1 = _pad2(hy_w1[i], LANES, LANES)
            b1 = _pad2(hy_b1[i][None], 1, LANES)
            f1 = _pad2(hy_fr1[i][None], 1, LANES)
            w2 = _pad2(hy_w2[i], LANES, LANES)
            b2 = _pad2(hy_b2[i][None], 1, LANES)
            f2 = _pad2(hy_fr2[i][None], 1, LANES)
            w3 = _pad2(hy_w3[i], LANES, hy_w3.shape[2])
            hy = []
            for latent in (True, False):
                seq = DEC_SEQ if latent else SEQ
                fwd_np, inv_np = _dft_tables(seq)
                zfeat, decay = _hyena_consts(seq)
                fwd = jnp.asarray(fwd_np)
                filters = _hy_filters(seq, jnp.asarray(zfeat), jnp.asarray(decay), fwd, w1, b1, f1, w2, b2, f2, w3)
                fwd_bf = _cast_bf16(fwd, f"dft_fwd_bf16_{seq}")
                inv_bf = _cast_bf16(jnp.asarray(inv_np), f"dft_inv_bf16_{seq}")
                hy.append(_hy_apply(proj, conv_w, fwd_bf, inv_bf, filters, bias, latent))
            cos, sin = _rope_tables(DEC_SEQ)
            att_l = _attention(proj, da_lambda[i], da_subln[i], lambda_init, True,
                               rope=(jnp.asarray(cos), jnp.asarray(sin)),
                               ctx_k=cache_k[:, i:i + 1], ctx_v=cache_v[:, i:i + 1])[0]
            att_c, new_k, new_v = _attention(proj, da_lambda[i], da_subln[i], lambda_init, False)
            x = _proj_residual(x, hy, (att_l, att_c), od_w_out[i].astype(BF16), mods, 2)
        x = _moe_layer(x, l, norm2[l], mods, moe_w_grp[l], moe_b_grp[l], moe_w_exp[l], moe_b_exp[l],
                       moe_w1, moe_w3, moe_w2, tril, triu, final_g=final_norm if l == DEPTH - 1 else None)

    y_sample, y_prompt = x
    return (y_prompt.reshape(BATCH, SEQ, D_MODEL), y_sample.reshape(DEC_BATCH, DEC_SEQ, D_MODEL),
            new_state, new_k, new_v)
```

```python
import functools
import math

import numpy as np
import jax
import jax.numpy as jnp
from jax import lax
from jax.experimental import pallas as pl
from jax.experimental.pallas import tpu as pltpu

F32 = jnp.float32
BF16 = jnp.bfloat16
I32 = jnp.int32
HIGHEST = lax.Precision.HIGHEST

D_MODEL = 1024
BATCH = 32
SEQ = 256
DEPTH = 2
DEC_BATCH = 8
DEC_SEQ = 1024
PAST_LEN = 256
GRID_W = 64
N_MOD = 6
EPS = 1e-6
GN_EPS = 1e-5

RET_DK = 128
RET_DV = 128
RET_HEADS = 4
RET_W = 512
RET_QK = 512
SC_W = 512
EVEN_IN = 2 * RET_QK + 2 * RET_W + 3 * SC_W

HY_W = 512
HY_ORDER = 2
HY_EMB = 33
HY_FFN = 64
HY_TARGET = 1e-2
HY_FAST_PCT = 0.3
HY_SLOW_PCT = 1.5
DA_DH = 64
DA_HEADS = 4
DA_W = 512
ODD_IN = 3 * HY_W + 3 * DA_W
ROPE_THETA = 10000.0

N_GROUPS = 4
EXP_PER_GROUP = 8
N_EXPERTS = 32
TOP_K = 2
D_EXPERT = 512

N_SAMPLE = DEC_BATCH * DEC_SEQ
N_PROMPT = BATCH * SEQ
N_TOK = N_SAMPLE + N_PROMPT
SEG = 1024
N_SEG = N_TOK // SEG
MOD_ROWS = 8

LANES = 128
SUBLANES = 8
VMEM_LIMIT = 56 * 1024 * 1024

TM = 256
MOE_BLK = 512
ROW_ALIGN = SUBLANES
R_LOC = (TOP_K * TM + N_EXPERTS * (ROW_ALIGN - 1) + LANES - 1) // LANES * LANES
CHUNKS = (256, 128, 64, 32, 16, 8)
MOE_ROWS = ((N_TOK * TOP_K + N_EXPERTS * (N_TOK // TM) * (ROW_ALIGN - 1) + N_EXPERTS * (MOE_BLK - 1))
            // MOE_BLK + 1) * MOE_BLK
MOE_NB = MOE_ROWS // MOE_BLK
HY_SLAB = 256
ATT_TQ = 256


def _cparams(sem, vmem=VMEM_LIMIT):
    return pltpu.CompilerParams(dimension_semantics=sem, vmem_limit_bytes=vmem)


def _silu(x):
    return x * (1.0 / (1.0 + jnp.exp(-x)))


def _split_bf16(a):
    hi = a.astype(BF16)
    return hi, (a - hi.astype(F32)).astype(BF16)


def _dot_3pass(a, b):
    ah, al = _split_bf16(a)
    bh, bl = _split_bf16(b)
    dot = functools.partial(jnp.dot, preferred_element_type=F32)
    return dot(ah, bh) + (dot(ah, bl) + dot(al, bh))


MODS_TN = 1536


def _mods_kernel(c_ref, w_ref, b_ref, o_ref):
    s = _silu(c_ref[...])
    o_ref[0] = jnp.dot(s, w_ref[0], preferred_element_type=F32, precision=HIGHEST) + b_ref[0]


def _mods(cvecs, ada_w, ada_b):
    n = N_MOD * D_MODEL
    return pl.pallas_call(
        _mods_kernel,
        grid=(DEPTH, n // MODS_TN),
        in_specs=[
            pl.BlockSpec((16, D_MODEL), lambda l, j: (0, 0)),
            pl.BlockSpec((1, D_MODEL, MODS_TN), lambda l, j: (l, 0, j)),
            pl.BlockSpec((1, 1, MODS_TN), lambda l, j: (l, 0, j)),
        ],
        out_specs=pl.BlockSpec((1, 16, MODS_TN), lambda l, j: (l, 0, j)),
        out_shape=jax.ShapeDtypeStruct((DEPTH, 16, n), F32),
        compiler_params=_cparams(("parallel", "parallel")),
        name="adaln_mods",
    )(cvecs, ada_w, ada_b.reshape(DEPTH, 1, n))


def _norm_mod(x, g, mod, shift_idx, scale_idx):
    ms = jnp.mean(x * x, axis=-1, keepdims=True)
    y = x * lax.rsqrt(ms + EPS) * g
    return y * (1.0 + mod[scale_idx:scale_idx + 1]) + mod[shift_idx:shift_idx + 1]


N_LAT_TILES = N_SAMPLE // TM
TM_WIDE = 512


def _group_specs(width, tm=TM):
    n_lat = N_SAMPLE // tm
    lat = pl.BlockSpec((tm, width), lambda i: (jnp.minimum(i, n_lat - 1), 0))
    ctx = pl.BlockSpec((tm, width), lambda i: (jnp.maximum(i - n_lat, 0), 0))
    return [lat, ctx]


def _group_pick(lat_ref, ctx_ref):
    n_lat = N_SAMPLE // lat_ref.shape[0]
    return jnp.where(pl.program_id(0) < n_lat, lat_ref[...], ctx_ref[...])


def _rows_specs(x, tm=TM):
    if isinstance(x, tuple):
        return _group_specs(D_MODEL, tm), list(x)
    return [pl.BlockSpec((tm, D_MODEL), lambda i: (i, 0))], [x]


def _rows_value(refs):
    return _group_pick(*refs) if len(refs) == 2 else refs[0][...]


def _nmm_kernel(*refs, n_x, shift_idx, scale_idx):
    g_ref, mod_ref, w_ref, o_ref = refs[n_x:]
    u = _norm_mod(_rows_value(refs[:n_x]), g_ref[...], mod_ref[0], shift_idx, scale_idx)
    o_ref[...] = jnp.dot(u.astype(BF16), w_ref[...], preferred_element_type=F32)


def _norm_mod_matmul(x, g, mods, w_bf16, shift_idx, scale_idx):
    n = w_bf16.shape[1]
    x_specs, x_args = _rows_specs(x)
    return pl.pallas_call(
        functools.partial(_nmm_kernel, n_x=len(x_args), shift_idx=shift_idx, scale_idx=scale_idx),
        grid=(N_TOK // TM,),
        in_specs=x_specs + [
            pl.BlockSpec((1, D_MODEL), lambda i: (0, 0)),
            pl.BlockSpec((1, MOD_ROWS, D_MODEL), lambda i: (i * TM // SEG, 0, 0)),
            pl.BlockSpec((D_MODEL, n), lambda i: (0, 0)),
        ],
        out_specs=pl.BlockSpec((TM, n), lambda i: (i, 0)),
        out_shape=jax.ShapeDtypeStruct((N_TOK, n), F32),
        compiler_params=_cparams(("parallel",)),
        name="norm_mod_inproj",
    )(*x_args, g.reshape(1, D_MODEL), mods, w_bf16)


def _proj_res_kernel(*refs, n_x, gate_idx):
    al_ref, ac_ref, bl_ref, bc_ref, wa_ref, wb_ref, mod_ref, o_ref = refs[n_x:]
    a = _group_pick(al_ref, ac_ref)
    b = _group_pick(bl_ref, bc_ref)
    y = jnp.dot(a.astype(BF16), wa_ref[...], preferred_element_type=F32)
    y = y + jnp.dot(b.astype(BF16), wb_ref[...], preferred_element_type=F32)
    o_ref[...] = _rows_value(refs[:n_x]) + mod_ref[0][gate_idx:gate_idx + 1] * y


def _proj_residual(x, a, b, w_bf16, mods, gate_idx):
    half = a[0].shape[1]
    tm = TM_WIDE
    x_specs, x_args = _rows_specs(x, tm)
    return pl.pallas_call(
        functools.partial(_proj_res_kernel, n_x=len(x_args), gate_idx=gate_idx),
        grid=(N_TOK // tm,),
        in_specs=x_specs + _group_specs(half, tm) + _group_specs(half, tm) + [
            pl.BlockSpec((half, D_MODEL), lambda i: (0, 0)),
            pl.BlockSpec((half, D_MODEL), lambda i: (1, 0)),
            pl.BlockSpec((1, MOD_ROWS, D_MODEL), lambda i: (i * tm // SEG, 0, 0)),
        ],
        out_specs=pl.BlockSpec((tm, D_MODEL), lambda i: (i, 0)),
        out_shape=jax.ShapeDtypeStruct((N_TOK, D_MODEL), F32),
        compiler_params=_cparams(("parallel",)),
        name="outproj_residual",
    )(*x_args, a[0], a[1], b[0], b[1], w_bf16, w_bf16, mods)


def _conv3_rows(z, w):
    n = z.shape[0]
    row = lax.broadcasted_iota(I32, z.shape, 0)
    zm = jnp.where(row == 0, 0.0, pltpu.roll(z, 1, 0))
    zp = jnp.where(row == n - 1, 0.0, pltpu.roll(z, n - 1, 0))
    return zm * w[0:1] + z * w[1:2] + zp * w[2:3]


def _log_gamma(dec):
    return jnp.log1p(-jnp.exp(dec))


def _even_kernel(*refs, seq, latent, tq, heads):
    (q_ref, k_ref, v_ref, g_ref, gb_ref, gc_ref, hx_ref, scw_ref, df_ref, db_ref) = refs[:10]
    if latent:
        s0_ref, ret_ref, sc_ref, mask_ref = refs[10:]
    else:
        ret_ref, sc_ref, st_ref, mask_ref = refs[10:]

    for hh in range(heads):
        lanes = slice(hh * RET_DK, (hh + 1) * RET_DK)
        lgf = _log_gamma(df_ref[hh])
        lgb = _log_gamma(db_ref[hh])

        @pl.when(pl.program_id(1) == 0)
        def _():
            for r in range(seq // tq):
                t = lax.broadcasted_iota(I32, (tq, seq), 0) + r * tq
                s = lax.broadcasted_iota(I32, (tq, seq), 1)
                d = (t - s).astype(F32)
                mf = jnp.where(d >= 0, jnp.exp(jnp.maximum(d, 0.0) * lgf), 0.0)
                mb = jnp.where(d <= 0, jnp.exp(jnp.maximum(-d, 0.0) * lgb), 0.0)
                mask_ref[hh, r * tq:(r + 1) * tq, :] = mf + mb

        k = k_ref[:, lanes] * (RET_DK ** -0.5)
        kb = k.astype(BF16)
        vb = v_ref[:, lanes].astype(BF16)
        if latent:
            s0f = s0_ref[0, 0, 0, hh].astype(BF16)
            s0b = s0_ref[0, 0, 1, hh].astype(BF16)
        for r in range(seq // tq):
            rows = slice(r * tq, (r + 1) * tq)
            qb = q_ref[rows, lanes].astype(BF16)
            s = lax.dot_general(qb, kb, (((1,), (1,)), ((), ())), preferred_element_type=F32)
            p = (s * mask_ref[hh, rows, :]).astype(BF16)
            o = jnp.dot(p, vb, preferred_element_type=F32)
            if latent:
                tpos = (lax.broadcasted_iota(I32, (tq, 1), 0) + r * tq).astype(F32)
                o = o + jnp.dot(qb, s0f, preferred_element_type=F32) * jnp.exp((tpos + 1.0) * lgf)
                o = o + jnp.dot(qb, s0b, preferred_element_type=F32) * jnp.exp((seq - tpos) * lgb)
            mu = jnp.mean(o, axis=-1, keepdims=True)
            var = jnp.mean(jnp.square(o - mu), axis=-1, keepdims=True)
            on = (o - mu) * lax.rsqrt(var + GN_EPS)
            ret_ref[rows, lanes] = _silu(g_ref[rows, lanes]) * on

        if not latent:
            spos = lax.broadcasted_iota(I32, (seq, 1), 0).astype(F32)
            kf = (k * jnp.exp((seq - 1.0 - spos) * lgf)).T.astype(BF16)
            kr = (k * jnp.exp(spos * lgb)).T.astype(BF16)
            st_ref[0, 0, 0, hh] = jnp.dot(kf, vb, preferred_element_type=F32)
            st_ref[0, 0, 1, hh] = jnp.dot(kr, vb, preferred_element_type=F32)

    sc_ref[...] = gb_ref[...] * _conv3_rows(gc_ref[...] * hx_ref[...], scw_ref[...])


def _even_mixer(proj, sc_w_pad, dec_f, dec_b, state=None):
    latent = state is not None
    seq = DEC_SEQ if latent else SEQ
    nb = DEC_BATCH if latent else BATCH
    row0 = 0 if latent else N_SAMPLE // seq
    tq = min(seq, 256)
    heads = 1 if latent else RET_HEADS
    steps = RET_HEADS // heads
    hw = heads * RET_DK

    def col(c0):
        return pl.BlockSpec((seq, hw), lambda h, b: (row0 + b, c0 * steps + h))

    in_specs = [col(0), col(1), col(2), col(3), col(4), col(5), col(6),
                pl.BlockSpec((SUBLANES, hw), lambda h, b: (0, h)),
                pl.BlockSpec((heads, 1, 1), lambda h, b: (h, 0, 0)),
                pl.BlockSpec((heads, 1, 1), lambda h, b: (h, 0, 0))]
    args = [proj] * 7 + [sc_w_pad, dec_f.reshape(RET_HEADS, 1, 1), dec_b.reshape(RET_HEADS, 1, 1)]
    out_spec = pl.BlockSpec((seq, hw), lambda h, b: (b, h))
    out_shape = [jax.ShapeDtypeStruct((nb * seq, RET_W), F32), jax.ShapeDtypeStruct((nb * seq, SC_W), F32)]
    out_specs = [out_spec, out_spec]
    st_spec = pl.BlockSpec((1, 1, 2, heads, RET_DK, RET_DV), lambda h, b: (b, 0, 0, h, 0, 0))
    if latent:
        in_specs.append(st_spec)
        args.append(state)
    else:
        out_shape.append(jax.ShapeDtypeStruct((BATCH, 1, 2, RET_HEADS, RET_DK, RET_DV), F32))
        out_specs.append(st_spec)
    return pl.pallas_call(
        functools.partial(_even_kernel, seq=seq, latent=latent, tq=tq, heads=heads),
        grid=(steps, nb),
        in_specs=in_specs,
        out_specs=out_specs,
        out_shape=out_shape,
        scratch_shapes=[pltpu.VMEM((heads, seq, seq), F32)],
        compiler_params=_cparams(("arbitrary", "arbitrary")),
        name="even_mixer_latent" if latent else "even_mixer_context",
    )(*args)


def _dft_tables(length):
    f = np.arange(length, dtype=np.float64)[:, None]
    t = np.arange(length, dtype=np.float64)[None, :]
    ang = np.pi * ((f * t) % (2 * length)) / length
    cos, sin = np.cos(ang), np.sin(ang)
    sin[0, :] = (-1.0) ** np.arange(length)
    fwd = np.concatenate([cos, sin], axis=0)
    wgt = np.full((2 * length,), 2.0)
    wgt[0] = 1.0
    wgt[length] = 1.0
    inv = (fwd * wgt[:, None] / (2.0 * length)).T
    return fwd.astype(np.float32), np.ascontiguousarray(inv).astype(np.float32)


def _hyena_consts(length):
    t = np.linspace(0.0, 1.0, length)[:, None]
    bands = (HY_EMB - 1) // 2
    w = 2.0 * np.pi * np.arange(length)[:, None] / length
    f = np.linspace(1e-4, bands - 1, bands)[None, :]
    z = np.concatenate([t, np.cos(f * w), -np.sin(f * w)], axis=-1)
    zpad = np.zeros((length, LANES))
    zpad[:, :HY_EMB] = z
    deltas = np.abs(np.linspace(math.log(HY_TARGET) / HY_SLOW_PCT, math.log(HY_TARGET) / HY_FAST_PCT, HY_W))
    decay = np.exp(-t * deltas[None, :])
    return zpad.astype(np.float32), decay.astype(np.float32)


def _split_kernel(x_ref, hi_ref, lo_ref):
    hi, lo = _split_bf16(x_ref[...])
    hi_ref[...] = hi
    lo_ref[...] = lo


def _split_table(x, name):
    rows = min(x.shape[0], 512)
    spec = pl.BlockSpec((rows, x.shape[1]), lambda i: (i, 0))
    out = jax.ShapeDtypeStruct(x.shape, BF16)
    return pl.pallas_call(
        _split_kernel,
        grid=(x.shape[0] // rows,),
        in_specs=[spec],
        out_specs=[spec, spec],
        out_shape=[out, out],
        compiler_params=_cparams(("parallel",)),
        name=name,
    )(x)


def _dot_3pass_split(ah, al, b):
    bh, bl = _split_bf16(b)
    dot = functools.partial(jnp.dot, preferred_element_type=F32)
    return dot(ah, bh) + (dot(ah, bl) + dot(al, bh))


def _hy_filter_kernel(z_ref, dec_ref, w1_ref, b1_ref, f1_ref, w2_ref, b2_ref, f2_ref, w3f_ref, w3b_ref,
                      fwd_ref, fwd_lo_ref, ka_ref, ka2_ref, kb_ref, *, seq):
    h = jnp.sin(f1_ref[...] * (jnp.dot(z_ref[...], w1_ref[...], preferred_element_type=F32,
                                       precision=HIGHEST) + b1_ref[...]))
    h = jnp.sin(f2_ref[...] * (jnp.dot(h, w2_ref[...], preferred_element_type=F32,
                                       precision=HIGHEST) + b2_ref[...]))
    dec = dec_ref[...]
    hf = jnp.dot(h, w3f_ref[...], preferred_element_type=F32, precision=HIGHEST) * dec
    hb = jnp.dot(h, w3b_ref[...], preferred_element_type=F32, precision=HIGHEST) * dec
    row = lax.broadcasted_iota(I32, hb.shape, 0)
    hb = jnp.where(row == 0, 0.0, hb)
    inv_norm = 1.0 / (jnp.sum(jnp.abs(hf), axis=0, keepdims=True)
                      + jnp.sum(jnp.abs(hb), axis=0, keepdims=True) + EPS)
    p = (hf + hb) * inv_norm
    q = (hf - hb) * inv_norm
    fp = _dot_3pass_split(fwd_ref[...], fwd_lo_ref[...], p)
    fq = _dot_3pass_split(fwd_ref[seq:, :], fwd_lo_ref[seq:, :], q)
    kr = fp[:seq]
    row = lax.broadcasted_iota(I32, kr.shape, 0)
    ka_ref[0] = kr
    ka2_ref[0] = jnp.where(row == 0, fp[seq:seq + 1], kr)
    kb_ref[0] = jnp.where(row == 0, 0.0, -fq)


def _hy_filters(seq, zfeat, decay, fwd_hi, fwd_lo, w1, b1, f1, w2, b2, f2, w3):
    ns = HY_W // HY_SLAB
    per_o = 2 * ns
    small = lambda shape: pl.BlockSpec(shape, lambda o, s: (0, 0))
    out_spec = pl.BlockSpec((1, seq, HY_SLAB), lambda o, s: (o, 0, s))
    out = jax.ShapeDtypeStruct((HY_ORDER, seq, HY_W), F32)
    return pl.pallas_call(
        functools.partial(_hy_filter_kernel, seq=seq),
        grid=(HY_ORDER, ns),
        in_specs=[
            small((seq, LANES)),
            pl.BlockSpec((seq, HY_SLAB), lambda o, s: (0, s)),
            small((LANES, LANES)), small((1, LANES)), small((1, LANES)),
            small((LANES, LANES)), small((1, LANES)), small((1, LANES)),
            pl.BlockSpec((LANES, HY_SLAB), lambda o, s: (0, o * per_o + s)),
            pl.BlockSpec((LANES, HY_SLAB), lambda o, s: (0, o * per_o + ns + s)),
            small((2 * seq, seq)), small((2 * seq, seq)),
        ],
        out_specs=[out_spec, out_spec, out_spec],
        out_shape=[out, out, out],
        compiler_params=_cparams(("parallel", "parallel")),
        name=f"hyena_filters_{seq}",
    )(zfeat, decay, w1, b1, f1, w2, b2, f2, w3, w3, fwd_hi, fwd_lo)


def _hy_apply_kernel(x1_ref, x2_ref, v_ref, c1_ref, c2_ref, cv_ref, fwd_ref, inv_ref,
                     ka_ref, ka2_ref, kb_ref, bias_ref, o_ref, *, seq, nseq):
    def long_conv(sig, o):
        spec = jnp.dot(fwd_ref[...], sig.astype(BF16), preferred_element_type=F32)
        xr, xs = spec[:seq], spec[seq:]
        kb = kb_ref[o]
        yr = (xr * ka_ref[o] + xs * kb).astype(BF16)
        ys = (xs * ka2_ref[o] - xr * kb).astype(BF16)
        y = jnp.dot(inv_ref[:, :seq], yr, preferred_element_type=F32)
        y = y + jnp.dot(inv_ref[:, seq:], ys, preferred_element_type=F32)
        return y + sig * bias_ref[o:o + 1]

    for i in range(nseq):
        rows = slice(i * seq, (i + 1) * seq)
        x1 = _conv3_rows(x1_ref[rows, :], c1_ref[...])
        x2 = _conv3_rows(x2_ref[rows, :], c2_ref[...])
        v = _conv3_rows(v_ref[rows, :], cv_ref[...])
        z = x1 * long_conv(v, 0)
        o_ref[rows, :] = x2 * long_conv(z, 1)


HY_NSEQ = {True: 2, False: 4}


def _hy_apply(proj, conv_w_pad, fwd_bf, inv_bf, filters, bias_pad, latent):
    seq = DEC_SEQ if latent else SEQ
    nb = DEC_BATCH if latent else BATCH
    nseq = HY_NSEQ[latent]
    rows = nseq * seq
    row0 = 0 if latent else N_SAMPLE // rows
    ns = HY_W // HY_SLAB
    once = pl.Buffered(1)
    const = lambda shape: pl.BlockSpec(shape, lambda s, b: (0,) * len(shape), pipeline_mode=once)
    col = lambda c0: pl.BlockSpec((rows, HY_SLAB), lambda s, b: (row0 + b, c0 + s))
    cw = lambda c0: pl.BlockSpec((SUBLANES, HY_SLAB), lambda s, b: (0, c0 + s))
    filt = pl.BlockSpec((HY_ORDER, seq, HY_SLAB), lambda s, b: (0, 0, s), pipeline_mode=once)
    in_specs = [col(0), col(ns), col(2 * ns), cw(0), cw(ns), cw(2 * ns),
                const((2 * seq, seq)), const((seq, 2 * seq)), filt, filt, filt,
                pl.BlockSpec((SUBLANES, HY_SLAB), lambda s, b: (0, s))]
    args = [proj] * 3 + [conv_w_pad] * 3 + [fwd_bf, inv_bf, *filters, bias_pad]
    return pl.pallas_call(
        functools.partial(_hy_apply_kernel, seq=seq, nseq=nseq),
        grid=(ns, nb // nseq),
        in_specs=in_specs,
        out_specs=pl.BlockSpec((rows, HY_SLAB), lambda s, b: (b, s)),
        out_shape=jax.ShapeDtypeStruct((nb * seq, HY_W), F32),
        compiler_params=_cparams(("arbitrary", "arbitrary")),
        name="hyena_latent" if latent else "hyena_context",
    )(*args)


def _rope_tables(length):
    rows = length // GRID_W
    row = np.repeat(np.arange(rows, dtype=np.float64), GRID_W)
    colp = np.tile(np.arange(GRID_W, dtype=np.float64), rows)
    half = DA_DH // 2
    freqs = ROPE_THETA ** (-np.arange(0, half, 2, dtype=np.float64) / half)
    ang = np.concatenate([row[:, None] * freqs[None, :]] * 2 + [colp[:, None] * freqs[None, :]] * 2, axis=1)
    ang = np.concatenate([ang, ang], axis=1)
    lane = np.arange(LANES)
    sign = np.where((lane % half) < half // 2, -1.0, 1.0)[None, :]
    return np.cos(ang).astype(np.float32), (np.sin(ang) * sign).astype(np.float32)


def _rope(x, cos, sin_signed):
    q = DA_DH // 4
    lane = lax.broadcasted_iota(I32, x.shape, 1)
    partner = jnp.where((lane % (2 * q)) < q, pltpu.roll(x, LANES - q, 1), pltpu.roll(x, q, 1))
    return x * cos + partner * sin_signed


def _attn_kernel(*refs, seq, latent, lambda_init, heads):
    q_ref, k_ref, v_ref, lam_ref, sub_ref = refs[:5]
    if latent:
        cos_ref, sin_ref, ck_ref, cv_ref, o_ref = refs[5:]
    else:
        o_ref, nk_ref, nv_ref = refs[5:]
    hw = 2 * DA_DH
    lv = lam_ref[...]
    lam = (jnp.exp(jnp.sum(lv[0:1] * lv[1:2], axis=1, keepdims=True))
           - jnp.exp(jnp.sum(lv[2:3] * lv[3:4], axis=1, keepdims=True)) + lambda_init)
    scale = DA_DH ** -0.5
    dn = (((1,), (1,)), ((), ()))
    tq = min(seq, ATT_TQ)
    for hh in range(heads):
        lanes = slice(hh * hw, (hh + 1) * hw)
        k = k_ref[:, lanes]
        v = v_ref[:, lanes]
        if latent:
            k = _rope(k, cos_ref[...], sin_ref[...])
            k_all = jnp.concatenate([k, jnp.concatenate([ck_ref[0, 0, 0, hh], ck_ref[0, 0, 1, hh]], axis=1)], axis=0)
            v_all = jnp.concatenate([v, cv_ref[0, 0, hh]], axis=0)
        else:
            k_all, v_all = k, v
            nk_ref[0, 0, 0, hh] = k[:, :DA_DH]
            nk_ref[0, 0, 1, hh] = k[:, DA_DH:]
            nv_ref[0, 0, hh] = v
        kb = k_all.astype(BF16)
        vb = v_all.astype(BF16)
        for r in range(seq // tq):
            rows = slice(r * tq, (r + 1) * tq)
            q = q_ref[rows, lanes]
            if latent:
                q = _rope(q, cos_ref[rows, :], sin_ref[rows, :])
            q = q * scale
            first = lax.broadcasted_iota(I32, q.shape, 1) < DA_DH
            q1 = jnp.where(first, q, 0.0).astype(BF16)
            q2 = jnp.where(first, 0.0, q).astype(BF16)
            s1 = lax.dot_general(q1, kb, dn, preferred_element_type=F32)
            s2 = lax.dot_general(q2, kb, dn, preferred_element_type=F32)
            e1 = jnp.exp(s1 - jnp.max(s1, axis=-1, keepdims=True))
            e2 = jnp.exp(s2 - jnp.max(s2, axis=-1, keepdims=True))
            o1 = jnp.dot(e1.astype(BF16), vb, preferred_element_type=F32)
            o2 = jnp.dot(e2.astype(BF16), vb, preferred_element_type=F32)
            o = (o1 * (1.0 / jnp.sum(e1, axis=-1, keepdims=True))
                 - o2 * (lam / jnp.sum(e2, axis=-1, keepdims=True)))
            o = o * lax.rsqrt(jnp.mean(o * o, axis=-1, keepdims=True) + EPS) * sub_ref[...]
            o_ref[rows, lanes] = o * (1.0 - lambda_init)


def _attention(proj, lam_vec, subln, lambda_init, latent, rope=None, ctx_k=None, ctx_v=None):
    seq = DEC_SEQ if latent else SEQ
    nb = DEC_BATCH if latent else BATCH
    row0 = 0 if latent else N_SAMPLE // seq
    hw = 2 * DA_DH
    heads = 1 if latent else DA_HEADS
    steps = DA_HEADS // heads
    col = lambda c: pl.BlockSpec((seq, heads * hw), lambda b, h: (row0 + b, c * steps + h))
    in_specs = [col(3), col(4), col(5),
                pl.BlockSpec((4, DA_DH), lambda b, h: (0, 0)),
                pl.BlockSpec((1, hw), lambda b, h: (0, 0))]
    args = [proj] * 3 + [lam_vec, subln.reshape(1, hw)]
    out_spec = pl.BlockSpec((seq, heads * hw), lambda b, h: (b, h))
    out_shape = [jax.ShapeDtypeStruct((nb * seq, DA_W), F32)]
    out_specs = [out_spec]
    k_spec = pl.BlockSpec((1, 1, 2, heads, SEQ, DA_DH), lambda b, h: (b, 0, 0, h, 0, 0))
    v_spec = pl.BlockSpec((1, 1, heads, SEQ, hw), lambda b, h: (b, 0, h, 0, 0))
    if latent:
        in_specs += [pl.BlockSpec((seq, hw), lambda b, h: (0, 0))] * 2 + [k_spec, v_spec]
        args += [rope[0], rope[1], ctx_k, ctx_v]
    else:
        out_shape += [jax.ShapeDtypeStruct((BATCH, 1, 2, DA_HEADS, SEQ, DA_DH), F32),
                      jax.ShapeDtypeStruct((BATCH, 1, DA_HEADS, SEQ, hw), F32)]
        out_specs += [k_spec, v_spec]
    return pl.pallas_call(
        functools.partial(_attn_kernel, seq=seq, latent=latent, lambda_init=lambda_init, heads=heads),
        grid=(nb, steps),
        in_specs=in_specs,
        out_specs=out_specs,
        out_shape=out_shape,
        compiler_params=_cparams(("arbitrary", "arbitrary")),
        name="diff_attn_latent" if latent else "diff_attn_context",
    )(*args)


HALF_D = D_MODEL // 2
HI16 = -65536


def _pack_bf16_pair(lo, hi):
    lo_bits = lax.bitcast_convert_type(lo.astype(BF16).astype(F32), I32)
    hi_bits = lax.bitcast_convert_type(hi.astype(BF16).astype(F32), I32)
    return (hi_bits & HI16) | lax.shift_right_logical(lo_bits, 16)


def _unpack_bf16_pair(p):
    lo = lax.bitcast_convert_type(lax.shift_left(p, 16), F32).astype(BF16)
    hi = lax.bitcast_convert_type(p & HI16, F32).astype(BF16)
    return lo, hi


GRP_EXPERT, GRP_REL, GRP_COUNT = range(3)


def _router_kernel(x_ref, g_ref, mod_ref, wr_ref, br_ref, tril_ref, triu_ref,
                   u_ref, loct_ref, info_ref, grp_ref, tot_ref, carry_ref):
    @pl.when(pl.program_id(0) == 0)
    def _():
        carry_ref[...] = jnp.zeros_like(carry_ref)

    u = _norm_mod(x_ref[...], g_ref[...], mod_ref[0], 3, 4)
    u_ref[...] = _pack_bf16_pair(u[:, :HALF_D], u[:, HALF_D:])
    logits = _dot_3pass(u, wr_ref[...]) + br_ref[...]
    lane = lax.broadcasted_iota(I32, logits.shape, 1)
    lane_f = lane.astype(F32)
    neg = -jnp.inf
    big = float(LANES)

    def first_max(vals):
        m = jnp.max(vals, axis=-1, keepdims=True)
        return m, jnp.min(jnp.where(vals == m, lane_f, big), axis=-1, keepdims=True)

    is_grp = lane < N_GROUPS
    gmax, gidx = first_max(jnp.where(is_grp, logits, neg))
    gsum = jnp.sum(jnp.where(is_grp, jnp.exp(logits - gmax), 0.0), axis=-1, keepdims=True)
    g_w = 1.0 / gsum
    lo = N_GROUPS + EXP_PER_GROUP * gidx
    in_grp = jnp.logical_and(lane_f >= lo, lane_f < lo + EXP_PER_GROUP)
    el = jnp.where(in_grp, logits, neg)
    v1, i1 = first_max(el)
    v2, i2 = first_max(jnp.where(lane_f == i1, neg, el))
    e = jnp.exp(v2 - v1)
    p1 = 1.0 / (1.0 + e)
    hit1 = lane_f == i1 - N_GROUPS
    hit2 = lane_f == i2 - N_GROUPS
    onehot = jnp.where(jnp.logical_or(hit1, hit2), 1.0, 0.0)
    cum = jnp.dot(tril_ref[...], onehot.astype(BF16), preferred_element_type=F32)
    n = cum[TM - 1:TM, :]
    n_pad = jnp.floor((n + (ROW_ALIGN - 1.0)) * (1.0 / ROW_ALIGN)) * ROW_ALIGN
    loc_off = jnp.dot(jnp.broadcast_to(n_pad, (SUBLANES, LANES)).astype(BF16), triu_ref[...],
                      preferred_element_type=F32)[0:1]
    base = cum - onehot + loc_off
    loc1 = jnp.sum(jnp.where(hit1, base, 0.0), axis=-1, keepdims=True)
    loc2 = jnp.sum(jnp.where(hit2, base, 0.0), axis=-1, keepdims=True)
    info = jnp.where(lane == 0, g_w * p1, jnp.where(lane == 1, g_w * (e * p1), jnp.where(lane == 2, loc1, loc2)))
    info_ref[...] = info
    loct_ref[0] = info.T[:SUBLANES]
    start8 = lax.broadcasted_iota(I32, (LANES, LANES), 0).astype(F32) * ROW_ALIGN
    owner = jnp.where(jnp.logical_and(start8 >= loc_off, start8 < loc_off + n_pad), 1.0, 0.0)
    lane128 = lax.broadcasted_iota(I32, (LANES, LANES), 1)
    c_exp = jnp.sum(owner * lane128.astype(F32), axis=-1, keepdims=True)
    c_rel = jnp.sum(owner * (carry_ref[...] + start8 - loc_off), axis=-1, keepdims=True)
    chunks = jnp.where(lane128 == 0, c_exp, jnp.where(lane128 == 1, c_rel, 0.0)).T
    n_chunks = jnp.sum(n_pad, axis=-1, keepdims=True) * (1.0 / ROW_ALIGN)
    row = lax.broadcasted_iota(I32, (SUBLANES, LANES), 0)
    grp = jnp.where(row == GRP_EXPERT, chunks[0:1], jnp.where(row == GRP_REL, chunks[1:2], n_chunks))
    grp_ref[0] = grp.astype(I32)
    carry_ref[...] = carry_ref[...] + n_pad
    tot_ref[...] = jnp.broadcast_to(carry_ref[...], tot_ref.shape).astype(I32)


def _router(x, g, mods, w_route, b_route, tril, triu):
    nt = N_TOK // TM
    return pl.pallas_call(
        _router_kernel,
        grid=(nt,),
        in_specs=[
            pl.BlockSpec((TM, D_MODEL), lambda i: (i, 0)),
            pl.BlockSpec((1, D_MODEL), lambda i: (0, 0)),
            pl.BlockSpec((1, MOD_ROWS, D_MODEL), lambda i: (i * TM // SEG, 0, 0)),
            pl.BlockSpec((D_MODEL, LANES), lambda i: (0, 0)),
            pl.BlockSpec((1, LANES), lambda i: (0, 0)),
            pl.BlockSpec((TM, TM), lambda i: (0, 0)),
            pl.BlockSpec((LANES, LANES), lambda i: (0, 0)),
        ],
        out_specs=[
            pl.BlockSpec((TM, HALF_D), lambda i: (i, 0)),
            pl.BlockSpec((1, SUBLANES, TM), lambda i: (i, 0, 0)),
            pl.BlockSpec((TM, LANES), lambda i: (i, 0)),
            pl.BlockSpec((1, SUBLANES, LANES), lambda i: (i, 0, 0)),
            pl.BlockSpec((SUBLANES, LANES), lambda i: (0, 0)),
        ],
        out_shape=[
            jax.ShapeDtypeStruct((N_TOK, HALF_D), I32),
            jax.ShapeDtypeStruct((nt, SUBLANES, TM), F32),
            jax.ShapeDtypeStruct((N_TOK, LANES), F32),
            jax.ShapeDtypeStruct((nt, SUBLANES, LANES), I32),
            jax.ShapeDtypeStruct((SUBLANES, LANES), I32),
        ],
        scratch_shapes=[pltpu.VMEM((1, LANES), F32)],
        compiler_params=_cparams(("arbitrary",)),
        name="moe_router",
    )(x, g.reshape(1, D_MODEL), mods, w_route, b_route, tril, triu)


def _for_chunks(n, fn):
    for size in CHUNKS:
        @pl.when((n & size) != 0)
        def _():
            fn(n & ~(2 * size - 1), size)


WAIT_SIZES = (512, 256, 128, 64, 32, 16, 8)


def _start_tile_copies(grp_ref, seg_ref, make):
    def body(c, carry):
        slot = seg_ref[0, grp_ref[0, GRP_EXPERT, c]] + grp_ref[0, GRP_REL, c]
        make(pl.ds(pl.multiple_of(c * ROW_ALIGN, ROW_ALIGN), ROW_ALIGN),
             pl.ds(pl.multiple_of(slot, ROW_ALIGN), ROW_ALIGN)).start()
        return carry

    lax.fori_loop(0, grp_ref[0, GRP_COUNT, 0], body, 0)


def _wait_tile_copies(grp_ref, make):
    rows = grp_ref[0, GRP_COUNT, 0] * ROW_ALIGN
    for size in WAIT_SIZES:
        @pl.when((rows & size) != 0)
        def _():
            make(pl.ds(0, size), pl.ds(0, size)).wait()


def _all_experts(body):
    lax.fori_loop(0, N_EXPERTS, lambda e, c: (body(e), c)[1], 0)


def _dispatch_kernel(grp_ref, grp1_ref, grp2_ref, seg_ref, loct_ref, u_ref, xb_ref, buf, zbuf, sem):
    i = pl.program_id(0)
    last = pl.num_programs(0) - 1
    cur = i % 2

    def maker(b):
        return lambda loc, slot: pltpu.make_async_copy(buf.at[b, loc], xb_ref.at[slot], sem.at[b])

    @pl.when(i >= 2)
    def _():
        _wait_tile_copies(grp2_ref, maker(cur))

    lo, hi = _unpack_bf16_pair(u_ref[...])
    row = lax.broadcasted_iota(I32, (R_LOC, TM), 0).astype(F32)
    sel = jnp.logical_or(row == loct_ref[0, 2:3, :], row == loct_ref[0, 3:4, :])
    pt = jnp.where(sel, 1.0, 0.0).astype(BF16)
    buf[cur] = _pack_bf16_pair(jnp.dot(pt, lo, preferred_element_type=F32),
                               jnp.dot(pt, hi, preferred_element_type=F32))
    _start_tile_copies(grp_ref, seg_ref, maker(cur))

    @pl.when(i == last)
    def _():
        @pl.when(i >= 1)
        def _():
            _wait_tile_copies(grp1_ref, maker(1 - cur))

        _wait_tile_copies(grp_ref, maker(cur))

        zbuf[...] = jnp.zeros_like(zbuf)

        def tail(e, act):
            total = seg_ref[1, e]
            dst = seg_ref[0, e] + total

            def one(off, size):
                act(pltpu.make_async_copy(zbuf.at[pl.ds(0, size)],
                                          xb_ref.at[pl.ds(pl.multiple_of(dst + off, ROW_ALIGN), size)], sem.at[2]))

            _for_chunks((-total) & (MOE_BLK - 1), one)

        _all_experts(lambda e: tail(e, lambda cp: cp.start()))
        _all_experts(lambda e: tail(e, lambda cp: cp.wait()))

        e_last = N_EXPERTS - 1
        used = (seg_ref[0, e_last] + seg_ref[1, e_last] + MOE_BLK - 1) // MOE_BLK

        def spare(j):
            return pltpu.make_async_copy(zbuf, xb_ref.at[pl.ds(pl.multiple_of(j * MOE_BLK, MOE_BLK), MOE_BLK)],
                                         sem.at[2])

        lax.fori_loop(used, MOE_NB, lambda j, c: (spare(j).start(), c)[1], 0)
        lax.fori_loop(used, MOE_NB, lambda j, c: (spare(j).wait(), c)[1], 0)


def _grp_spec(back):
    return pl.BlockSpec((1, SUBLANES, LANES), lambda i: (jnp.maximum(i - back, 0), 0, 0), memory_space=pltpu.SMEM)


def _dispatch(grp, seg, loct, u):
    return pl.pallas_call(
        _dispatch_kernel,
        grid=(N_TOK // TM,),
        in_specs=[
            _grp_spec(0), _grp_spec(1), _grp_spec(2),
            pl.BlockSpec(memory_space=pltpu.SMEM),
            pl.BlockSpec((1, SUBLANES, TM), lambda i: (i, 0, 0)),
            pl.BlockSpec((TM, HALF_D), lambda i: (i, 0)),
        ],
        out_specs=pl.BlockSpec(memory_space=pl.ANY),
        out_shape=jax.ShapeDtypeStruct((MOE_ROWS, HALF_D), I32),
        scratch_shapes=[pltpu.VMEM((2, R_LOC, HALF_D), I32), pltpu.VMEM((MOE_BLK, HALF_D), I32),
                        pltpu.SemaphoreType.DMA((3,))],
        compiler_params=_cparams(("arbitrary",)),
        name="moe_dispatch",
    )(grp, grp, grp, seg, loct, u)


def _expert_kernel(seg_ref, w1_ref, w3_ref, w2_ref, xb_ref, yb_ref, w1b, w3b, w2b, xbuf, ybuf, xsem, ysem):
    e = pl.program_id(0)
    last_e = pl.num_programs(0) - 1
    blk0 = seg_ref[0, e] // MOE_BLK
    nblk = (seg_ref[1, e] + MOE_BLK - 1) // MOE_BLK
    used = (seg_ref[0, N_EXPERTS - 1] + seg_ref[1, N_EXPERTS - 1] + MOE_BLK - 1) // MOE_BLK

    def rows(g):
        return pl.ds(pl.multiple_of(g * MOE_BLK, MOE_BLK), MOE_BLK)

    def fetch(g, b):
        return pltpu.make_async_copy(xb_ref.at[rows(g)], xbuf.at[b], xsem.at[b])

    def flush(g, b):
        return pltpu.make_async_copy(ybuf.at[b], yb_ref.at[rows(g)], ysem.at[b])

    @pl.when(e == 0)
    def _():
        fetch(0, 0).start()

    @pl.when(nblk > 0)
    def _():
        w1b[...] = w1_ref[0, 0].astype(BF16)
        w3b[...] = w3_ref[0, 0].astype(BF16)
        w2b[...] = w2_ref[0, 0].astype(BF16)

        def block(i, carry):
            g = blk0 + i
            b = g % 2
            fetch(g, b).wait()

            @pl.when(g + 1 < used)
            def _():
                fetch(g + 1, 1 - b).start()

            @pl.when(g >= 2)
            def _():
                flush(g - 2, b).wait()

            lo, hi = _unpack_bf16_pair(xbuf[b])

            def up(wb):
                return (jnp.dot(lo, wb[:HALF_D, :], preferred_element_type=F32)
                        + jnp.dot(hi, wb[HALF_D:, :], preferred_element_type=F32))

            hdn = (_silu(up(w1b)) * up(w3b)).astype(BF16)
            y = jnp.dot(hdn, w2b[...], preferred_element_type=F32)
            ybuf[b] = _pack_bf16_pair(y[:, :HALF_D], y[:, HALF_D:])
            flush(g, b).start()
            return carry

        lax.fori_loop(0, nblk, block, 0)

    @pl.when(e == last_e)
    def _():
        @pl.when(used >= 2)
        def _():
            flush(used - 2, used % 2).wait()

        flush(used - 1, (used - 1) % 2).wait()
        xbuf[0] = jnp.zeros((MOE_BLK, HALF_D), I32)

        def spare(g):
            return pltpu.make_async_copy(xbuf.at[0], yb_ref.at[rows(g)], xsem.at[0])

        lax.fori_loop(used, MOE_NB, lambda g, c: (spare(g).start(), c)[1], 0)
        lax.fori_loop(used, MOE_NB, lambda g, c: (spare(g).wait(), c)[1], 0)


def _experts(seg, xb, w1, w3, w2, layer):
    hbm = pl.BlockSpec(memory_space=pl.ANY)
    w_in = pl.BlockSpec((1, 1, D_MODEL, D_EXPERT), lambda e, s: (layer, e, 0, 0))
    w_out = pl.BlockSpec((1, 1, D_EXPERT, D_MODEL), lambda e, s: (layer, e, 0, 0))
    return pl.pallas_call(
        _expert_kernel,
        grid_spec=pltpu.PrefetchScalarGridSpec(
            num_scalar_prefetch=1,
            grid=(N_EXPERTS,),
            in_specs=[w_in, w_in, w_out, hbm],
            out_specs=hbm,
            scratch_shapes=[pltpu.VMEM((D_MODEL, D_EXPERT), BF16), pltpu.VMEM((D_MODEL, D_EXPERT), BF16),
                            pltpu.VMEM((D_EXPERT, D_MODEL), BF16),
                            pltpu.VMEM((2, MOE_BLK, HALF_D), I32), pltpu.VMEM((2, MOE_BLK, HALF_D), I32),
                            pltpu.SemaphoreType.DMA((2,)), pltpu.SemaphoreType.DMA((2,))],
        ),
        out_shape=jax.ShapeDtypeStruct((MOE_ROWS, HALF_D), I32),
        compiler_params=_cparams(("arbitrary",)),
        name="moe_experts",
    )(seg, w1, w3, w2, xb)


def _combine_kernel(*refs, final):
    grp_ref, nxt_ref, seg_ref, x_ref, info_ref, mod_ref, yb_ref = refs[:7]
    if final:
        fin_ref, olat_ref, octx_ref, ybuf, sem = refs[7:]
    else:
        o_ref, ybuf, sem = refs[7:]
    i = pl.program_id(0)
    cur = i % 2

    def maker(b):
        return lambda loc, slot: pltpu.make_async_copy(yb_ref.at[slot], ybuf.at[b, loc], sem.at[b])

    @pl.when(i == 0)
    def _():
        ybuf[...] = jnp.zeros_like(ybuf)
        _start_tile_copies(grp_ref, seg_ref, maker(0))

    @pl.when(i + 1 < pl.num_programs(0))
    def _():
        _start_tile_copies(nxt_ref, seg_ref, maker(1 - cur))

    _wait_tile_copies(grp_ref, maker(cur))
    info = info_ref[...]
    col = lax.broadcasted_iota(I32, (TM, R_LOC), 1).astype(F32)
    p = jnp.where(col == info[:, 2:3], info[:, 0:1], jnp.where(col == info[:, 3:4], info[:, 1:2], 0.0))
    p = p.astype(BF16)
    lo, hi = _unpack_bf16_pair(ybuf[cur])
    y = jnp.concatenate([jnp.dot(p, lo, preferred_element_type=F32),
                         jnp.dot(p, hi, preferred_element_type=F32)], axis=1)
    out = x_ref[...] + mod_ref[0][5:6] * y
    if not final:
        o_ref[...] = out
        return
    out = out * lax.rsqrt(jnp.mean(out * out, axis=-1, keepdims=True) + EPS) * fin_ref[...]

    @pl.when(i < N_LAT_TILES)
    def _():
        olat_ref[...] = out

    @pl.when(i >= N_LAT_TILES)
    def _():
        octx_ref[...] = out


def _combine(x, grp, seg, info, mods, yb, final_g=None):
    final = final_g is not None
    nt = N_TOK // TM
    in_specs = [
        _grp_spec(0),
        pl.BlockSpec((1, SUBLANES, LANES), lambda i: (jnp.minimum(i + 1, nt - 1), 0, 0), memory_space=pltpu.SMEM),
        pl.BlockSpec(memory_space=pltpu.SMEM),
        pl.BlockSpec((TM, D_MODEL), lambda i: (i, 0)),
        pl.BlockSpec((TM, LANES), lambda i: (i, 0)),
        pl.BlockSpec((1, MOD_ROWS, D_MODEL), lambda i: (i * TM // SEG, 0, 0)),
        pl.BlockSpec(memory_space=pl.ANY),
    ]
    args = [grp, grp, seg, x, info, mods, yb]
    if final:
        in_specs.append(pl.BlockSpec((1, D_MODEL), lambda i: (0, 0)))
        args.append(final_g.reshape(1, D_MODEL))
        out_specs = _group_specs(D_MODEL)
        out_shape = [jax.ShapeDtypeStruct((N_SAMPLE, D_MODEL), F32), jax.ShapeDtypeStruct((N_PROMPT, D_MODEL), F32)]
    else:
        out_specs = pl.BlockSpec((TM, D_MODEL), lambda i: (i, 0))
        out_shape = jax.ShapeDtypeStruct((N_TOK, D_MODEL), F32)
    return pl.pallas_call(
        functools.partial(_combine_kernel, final=final),
        grid=(nt,),
        in_specs=in_specs,
        out_specs=out_specs,
        out_shape=out_shape,
        scratch_shapes=[pltpu.VMEM((2, R_LOC, HALF_D), I32), pltpu.SemaphoreType.DMA((2,))],
        compiler_params=_cparams(("arbitrary",)),
        name="moe_combine_final" if final else "moe_combine",
    )(*args)


def _moe_layer(x, layer, norm2, mods, w_grp, b_grp, w_exp, b_exp, w1, w3, w2, tril, triu, final_g=None):
    w_route = jnp.zeros((D_MODEL, LANES), F32)
    w_route = w_route.at[:, :N_GROUPS].set(w_grp).at[:, N_GROUPS:N_GROUPS + N_EXPERTS].set(w_exp)
    b_route = jnp.zeros((1, LANES), F32)
    b_route = b_route.at[0, :N_GROUPS].set(b_grp).at[0, N_GROUPS:N_GROUPS + N_EXPERTS].set(b_exp)
    u, loct, info, grp, tot = _router(x, norm2, mods, w_route, b_route, tril, triu)

    total = tot[0, :N_EXPERTS]
    padded = (total + MOE_BLK - 1) // MOE_BLK * MOE_BLK
    pad_end = jnp.cumsum(padded)
    pad_start = pad_end - padded
    seg = jnp.zeros((2, LANES), I32).at[0, :N_EXPERTS].set(pad_start).at[1, :N_EXPERTS].set(total)
    xb = _dispatch(grp, seg, loct, u)
    yb = _experts(seg, xb, w1, w3, w2, layer)
    return _combine(x, grp, seg, info, mods, yb, final_g)


def _pad_rows(w, rows=SUBLANES):
    return jnp.zeros((rows, w.shape[1]), w.dtype).at[:w.shape[0]].set(w)


def _pad2(w, rows, cols):
    return jnp.zeros((rows, cols), w.dtype).at[:w.shape[0], :w.shape[1]].set(w)


def kernel(x_prompt, x_sample, c, c_ctx, state_ret, cache_k, cache_v, ada_w, ada_b, norm1, norm2, final_norm, ev_w_in, ev_w_out, ret_decay_fwd, ret_decay_bwd, sc_conv_w, od_w_in, od_w_out, hy_conv_w, hy_w1, hy_b1, hy_fr1, hy_w2, hy_b2, hy_fr2, hy_w3, hy_bias, da_lambda, da_subln, moe_w_grp, moe_b_grp, moe_w_exp, moe_b_exp, moe_w1, moe_w3, moe_w2):
    x = (x_sample.reshape(N_SAMPLE, D_MODEL), x_prompt.reshape(N_PROMPT, D_MODEL))

    cvecs = jnp.zeros((16, D_MODEL), F32).at[:DEC_BATCH].set(c).at[DEC_BATCH].set(c_ctx)
    m = _mods(cvecs, ada_w, ada_b).reshape(DEPTH, 16, N_MOD, D_MODEL)
    seg_mods = jnp.concatenate(
        [m[:, :DEC_BATCH], jnp.broadcast_to(m[:, DEC_BATCH:DEC_BATCH + 1], (DEPTH, N_SEG - DEC_BATCH, N_MOD, D_MODEL))],
        axis=1)
    seg_mods = jnp.pad(seg_mods, ((0, 0), (0, 0), (0, MOD_ROWS - N_MOD), (0, 0)))

    tril = jnp.asarray(np.tril(np.ones((TM, TM), np.float32)), BF16)
    triu = jnp.asarray(np.triu(np.ones((LANES, LANES), np.float32), 1), BF16)

    for l in range(DEPTH):
        mods = seg_mods[l]
        i = l // 2
        if l % 2 == 0:
            proj = _norm_mod_matmul(x, norm1[l], mods, ev_w_in[i].astype(BF16), 0, 1)
            scw = _pad_rows(sc_conv_w[i])
            ret_l, sc_l = _even_mixer(proj, scw, ret_decay_fwd[i], ret_decay_bwd[i], state=state_ret[:, i:i + 1])
            ret_c, sc_c, new_state = _even_mixer(proj, scw, ret_decay_fwd[i], ret_decay_bwd[i])
            x = _proj_residual(x, (ret_l, ret_c), (sc_l, sc_c), ev_w_out[i].astype(BF16), mods, 2)
        else:
            lambda_init = 0.8 - 0.6 * math.exp(-0.3 * l)
            proj = _norm_mod_matmul(x, norm1[l], mods, od_w_in[i].astype(BF16), 0, 1)
            conv_w = _pad_rows(hy_conv_w[i])
            bias = _pad_rows(hy_bias[i])
            w1 = _pad2(hy_w1[i], LANES, LANES)
            b1 = _pad2(hy_b1[i][None], 1, LANES)
            f1 = _pad2(hy_fr1[i][None], 1, LANES)
            w2 = _pad2(hy_w2[i], LANES, LANES)
            b2 = _pad2(hy_b2[i][None], 1, LANES)
            f2 = _pad2(hy_fr2[i][None], 1, LANES)
            w3 = _pad2(hy_w3[i], LANES, hy_w3.shape[2])
            hy = []
            for latent in (True, False):
                seq = DEC_SEQ if latent else SEQ
                fwd_np, inv_np = _dft_tables(seq)
                zfeat, decay = _hyena_consts(seq)
                fwd_bf, fwd_lo = _split_table(jnp.asarray(fwd_np), f"dft_fwd_split_{seq}")
                inv_bf, _ = _split_table(jnp.asarray(inv_np), f"dft_inv_split_{seq}")
                filters = _hy_filters(seq, jnp.asarray(zfeat), jnp.asarray(decay), fwd_bf, fwd_lo,
                                      w1, b1, f1, w2, b2, f2, w3)
                hy.append(_hy_apply(proj, conv_w, fwd_bf, inv_bf, filters, bias, latent))
            cos, sin = _rope_tables(DEC_SEQ)
            att_l = _attention(proj, da_lambda[i], da_subln[i], lambda_init, True,
                               rope=(jnp.asarray(cos), jnp.asarray(sin)),
                               ctx_k=cache_k[:, i:i + 1], ctx_v=cache_v[:, i:i + 1])[0]
            att_c, new_k, new_v = _attention(proj, da_lambda[i], da_subln[i], lambda_init, False)
            x = _proj_residual(x, hy, (att_l, att_c), od_w_out[i].astype(BF16), mods, 2)
        x = _moe_layer(x, l, norm2[l], mods, moe_w_grp[l], moe_b_grp[l], moe_w_exp[l], moe_b_exp[l],
                       moe_w1, moe_w3, moe_w2, tril, triu, final_g=final_norm if l == DEPTH - 1 else None)

    y_sample, y_prompt = x
    return (y_prompt.reshape(BATCH, SEQ, D_MODEL), y_sample.reshape(DEC_BATCH, DEC_SEQ, D_MODEL),
            new_state, new_k, new_v)
```

```python
import functools
import math

import numpy as np
import jax
import jax.numpy as jnp
from jax import lax
from jax.experimental import pallas as pl
from jax.experimental.pallas import tpu as pltpu

F32 = jnp.float32
BF16 = jnp.bfloat16
I32 = jnp.int32
HIGHEST = lax.Precision.HIGHEST

D_MODEL = 1024
BATCH = 32
SEQ = 256
DEPTH = 2
DEC_BATCH = 8
DEC_SEQ = 1024
PAST_LEN = 256
GRID_W = 64
N_MOD = 6
EPS = 1e-6
GN_EPS = 1e-5

RET_DK = 128
RET_DV = 128
RET_HEADS = 4
RET_W = 512
RET_QK = 512
SC_W = 512
EVEN_IN = 2 * RET_QK + 2 * RET_W + 3 * SC_W

HY_W = 512
HY_ORDER = 2
HY_EMB = 33
HY_FFN = 64
HY_TARGET = 1e-2
HY_FAST_PCT = 0.3
HY_SLOW_PCT = 1.5
DA_DH = 64
DA_HEADS = 4
DA_W = 512
ODD_IN = 3 * HY_W + 3 * DA_W
ROPE_THETA = 10000.0

N_GROUPS = 4
EXP_PER_GROUP = 8
N_EXPERTS = 32
TOP_K = 2
D_EXPERT = 512

N_SAMPLE = DEC_BATCH * DEC_SEQ
N_PROMPT = BATCH * SEQ
N_TOK = N_SAMPLE + N_PROMPT
SEG = 1024
N_SEG = N_TOK // SEG
MOD_ROWS = 8

LANES = 128
SUBLANES = 8
VMEM_LIMIT = 56 * 1024 * 1024

TM = 256
MOE_BLK = 512
ROW_ALIGN = SUBLANES
R_LOC = (TOP_K * TM + N_EXPERTS * (ROW_ALIGN - 1) + LANES - 1) // LANES * LANES
CHUNKS = (256, 128, 64, 32, 16, 8)
MOE_ROWS = ((N_TOK * TOP_K + N_EXPERTS * (N_TOK // TM) * (ROW_ALIGN - 1) + N_EXPERTS * (MOE_BLK - 1))
            // MOE_BLK + 1) * MOE_BLK
MOE_NB = MOE_ROWS // MOE_BLK
HY_SLAB = 256
ATT_TQ = 256


def _cparams(sem, vmem=VMEM_LIMIT):
    return pltpu.CompilerParams(dimension_semantics=sem, vmem_limit_bytes=vmem)


def _silu(x):
    return x * (1.0 / (1.0 + jnp.exp(-x)))


def _split_bf16(a):
    hi = a.astype(BF16)
    return hi, (a - hi.astype(F32)).astype(BF16)


def _dot_3pass(a, b):
    ah, al = _split_bf16(a)
    bh, bl = _split_bf16(b)
    dot = functools.partial(jnp.dot, preferred_element_type=F32)
    return dot(ah, bh) + (dot(ah, bl) + dot(al, bh))


MODS_TN = 1536


def _mods_kernel(c_ref, w_ref, b_ref, o_ref):
    s = _silu(c_ref[...])
    o_ref[0] = jnp.dot(s, w_ref[0], preferred_element_type=F32, precision=HIGHEST) + b_ref[0]


def _mods(cvecs, ada_w, ada_b):
    n = N_MOD * D_MODEL
    return pl.pallas_call(
        _mods_kernel,
        grid=(DEPTH, n // MODS_TN),
        in_specs=[
            pl.BlockSpec((16, D_MODEL), lambda l, j: (0, 0)),
            pl.BlockSpec((1, D_MODEL, MODS_TN), lambda l, j: (l, 0, j)),
            pl.BlockSpec((1, 1, MODS_TN), lambda l, j: (l, 0, j)),
        ],
        out_specs=pl.BlockSpec((1, 16, MODS_TN), lambda l, j: (l, 0, j)),
        out_shape=jax.ShapeDtypeStruct((DEPTH, 16, n), F32),
        compiler_params=_cparams(("parallel", "parallel")),
        name="adaln_mods",
    )(cvecs, ada_w, ada_b.reshape(DEPTH, 1, n))


def _norm_mod(x, g, mod, shift_idx, scale_idx):
    ms = jnp.mean(x * x, axis=-1, keepdims=True)
    y = x * lax.rsqrt(ms + EPS) * g
    return y * (1.0 + mod[scale_idx:scale_idx + 1]) + mod[shift_idx:shift_idx + 1]


N_LAT_TILES = N_SAMPLE // TM
TM_WIDE = 512


def _group_specs(width, tm=TM):
    n_lat = N_SAMPLE // tm
    lat = pl.BlockSpec((tm, width), lambda i: (jnp.minimum(i, n_lat - 1), 0))
    ctx = pl.BlockSpec((tm, width), lambda i: (jnp.maximum(i - n_lat, 0), 0))
    return [lat, ctx]


def _group_pick(lat_ref, ctx_ref):
    n_lat = N_SAMPLE // lat_ref.shape[0]
    return jnp.where(pl.program_id(0) < n_lat, lat_ref[...], ctx_ref[...])


def _rows_specs(x, tm=TM):
    if isinstance(x, tuple):
        return _group_specs(D_MODEL, tm), list(x)
    return [pl.BlockSpec((tm, D_MODEL), lambda i: (i, 0))], [x]


def _rows_value(refs):
    return _group_pick(*refs) if len(refs) == 2 else refs[0][...]


def _nmm_kernel(*refs, n_x, shift_idx, scale_idx):
    g_ref, mod_ref, w_ref, o_ref = refs[n_x:]
    u = _norm_mod(_rows_value(refs[:n_x]), g_ref[...], mod_ref[0], shift_idx, scale_idx)
    o_ref[...] = jnp.dot(u.astype(BF16), w_ref[...], preferred_element_type=F32)


def _norm_mod_matmul(x, g, mods, w_bf16, shift_idx, scale_idx):
    n = w_bf16.shape[1]
    x_specs, x_args = _rows_specs(x)
    return pl.pallas_call(
        functools.partial(_nmm_kernel, n_x=len(x_args), shift_idx=shift_idx, scale_idx=scale_idx),
        grid=(N_TOK // TM,),
        in_specs=x_specs + [
            pl.BlockSpec((1, D_MODEL), lambda i: (0, 0)),
            pl.BlockSpec((1, MOD_ROWS, D_MODEL), lambda i: (i * TM // SEG, 0, 0)),
            pl.BlockSpec((D_MODEL, n), lambda i: (0, 0)),
        ],
        out_specs=pl.BlockSpec((TM, n), lambda i: (i, 0)),
        out_shape=jax.ShapeDtypeStruct((N_TOK, n), F32),
        compiler_params=_cparams(("parallel",)),
        name="norm_mod_inproj",
    )(*x_args, g.reshape(1, D_MODEL), mods, w_bf16)


def _conv3_rows(z, w):
    n = z.shape[0]
    row = lax.broadcasted_iota(I32, z.shape, 0)
    zm = jnp.where(row == 0, 0.0, pltpu.roll(z, 1, 0))
    zp = jnp.where(row == n - 1, 0.0, pltpu.roll(z, n - 1, 0))
    return zm * w[0:1] + z * w[1:2] + zp * w[2:3]


def _log_gamma(dec):
    return jnp.log1p(-jnp.exp(dec))


def _even_kernel(*refs, seq, latent, tq, heads):
    (q_ref, k_ref, v_ref, g_ref, gb_ref, gc_ref, hx_ref, scw_ref, df_ref, db_ref) = refs[:10]
    if latent:
        s0_ref, ret_ref, sc_ref, mask_ref = refs[10:]
    else:
        ret_ref, sc_ref, st_ref, mask_ref = refs[10:]

    for hh in range(heads):
        lanes = slice(hh * RET_DK, (hh + 1) * RET_DK)
        lgf = _log_gamma(df_ref[hh])
        lgb = _log_gamma(db_ref[hh])

        @pl.when(pl.program_id(1) == 0)
        def _():
            for r in range(seq // tq):
                t = lax.broadcasted_iota(I32, (tq, seq), 0) + r * tq
                s = lax.broadcasted_iota(I32, (tq, seq), 1)
                d = (t - s).astype(F32)
                mf = jnp.where(d >= 0, jnp.exp(jnp.maximum(d, 0.0) * lgf), 0.0)
                mb = jnp.where(d <= 0, jnp.exp(jnp.maximum(-d, 0.0) * lgb), 0.0)
                mask_ref[hh, r * tq:(r + 1) * tq, :] = mf + mb

        k = k_ref[:, lanes] * (RET_DK ** -0.5)
        kb = k.astype(BF16)
        vb = v_ref[:, lanes].astype(BF16)
        if latent:
            s0f = s0_ref[0, 0, 0, hh].astype(BF16)
            s0b = s0_ref[0, 0, 1, hh].astype(BF16)
        for r in range(seq // tq):
            rows = slice(r * tq, (r + 1) * tq)
            qb = q_ref[rows, lanes].astype(BF16)
            s = lax.dot_general(qb, kb, (((1,), (1,)), ((), ())), preferred_element_type=F32)
            p = (s * mask_ref[hh, rows, :]).astype(BF16)
            o = jnp.dot(p, vb, preferred_element_type=F32)
            if latent:
                tpos = (lax.broadcasted_iota(I32, (tq, 1), 0) + r * tq).astype(F32)
                o = o + jnp.dot(qb, s0f, preferred_element_type=F32) * jnp.exp((tpos + 1.0) * lgf)
                o = o + jnp.dot(qb, s0b, preferred_element_type=F32) * jnp.exp((seq - tpos) * lgb)
            mu = jnp.mean(o, axis=-1, keepdims=True)
            var = jnp.mean(jnp.square(o - mu), axis=-1, keepdims=True)
            on = (o - mu) * lax.rsqrt(var + GN_EPS)
            ret_ref[rows, lanes] = _silu(g_ref[rows, lanes]) * on

        if not latent:
            spos = lax.broadcasted_iota(I32, (seq, 1), 0).astype(F32)
            kf = (k * jnp.exp((seq - 1.0 - spos) * lgf)).T.astype(BF16)
            kr = (k * jnp.exp(spos * lgb)).T.astype(BF16)
            st_ref[0, 0, 0, hh] = jnp.dot(kf, vb, preferred_element_type=F32)
            st_ref[0, 0, 1, hh] = jnp.dot(kr, vb, preferred_element_type=F32)

    sc_ref[...] = gb_ref[...] * _conv3_rows(gc_ref[...] * hx_ref[...], scw_ref[...])


def _even_mixer(proj, sc_w_pad, dec_f, dec_b, state=None):
    latent = state is not None
    seq = DEC_SEQ if latent else SEQ
    nb = DEC_BATCH if latent else BATCH
    row0 = 0 if latent else N_SAMPLE // seq
    tq = min(seq, 256)
    heads = 1 if latent else RET_HEADS
    steps = RET_HEADS // heads
    hw = heads * RET_DK

    def col(c0):
        return pl.BlockSpec((seq, hw), lambda h, b: (row0 + b, c0 * steps + h))

    in_specs = [col(0), col(1), col(2), col(3), col(4), col(5), col(6),
                pl.BlockSpec((SUBLANES, hw), lambda h, b: (0, h)),
                pl.BlockSpec((heads, 1, 1), lambda h, b: (h, 0, 0)),
                pl.BlockSpec((heads, 1, 1), lambda h, b: (h, 0, 0))]
    args = [proj] * 7 + [sc_w_pad, dec_f.reshape(RET_HEADS, 1, 1), dec_b.reshape(RET_HEADS, 1, 1)]
    out_spec = pl.BlockSpec((seq, hw), lambda h, b: (b, h))
    out_shape = [jax.ShapeDtypeStruct((nb * seq, RET_W), F32), jax.ShapeDtypeStruct((nb * seq, SC_W), F32)]
    out_specs = [out_spec, out_spec]
    st_spec = pl.BlockSpec((1, 1, 2, heads, RET_DK, RET_DV), lambda h, b: (b, 0, 0, h, 0, 0))
    if latent:
        in_specs.append(st_spec)
        args.append(state)
    else:
        out_shape.append(jax.ShapeDtypeStruct((BATCH, 1, 2, RET_HEADS, RET_DK, RET_DV), F32))
        out_specs.append(st_spec)
    return pl.pallas_call(
        functools.partial(_even_kernel, seq=seq, latent=latent, tq=tq, heads=heads),
        grid=(steps, nb),
        in_specs=in_specs,
        out_specs=out_specs,
        out_shape=out_shape,
        scratch_shapes=[pltpu.VMEM((heads, seq, seq), F32)],
        compiler_params=_cparams(("arbitrary", "arbitrary")),
        name="even_mixer_latent" if latent else "even_mixer_context",
    )(*args)


def _dft_tables(length):
    f = np.arange(length, dtype=np.float64)[:, None]
    t = np.arange(length, dtype=np.float64)[None, :]
    ang = np.pi * ((f * t) % (2 * length)) / length
    cos, sin = np.cos(ang), np.sin(ang)
    sin[0, :] = (-1.0) ** np.arange(length)
    fwd = np.concatenate([cos, sin], axis=0)
    wgt = np.full((2 * length,), 2.0)
    wgt[0] = 1.0
    wgt[length] = 1.0
    inv = (fwd * wgt[:, None] / (2.0 * length)).T
    return fwd.astype(np.float32), np.ascontiguousarray(inv).astype(np.float32)


def _hyena_consts(length):
    t = np.linspace(0.0, 1.0, length)[:, None]
    bands = (HY_EMB - 1) // 2
    w = 2.0 * np.pi * np.arange(length)[:, None] / length
    f = np.linspace(1e-4, bands - 1, bands)[None, :]
    z = np.concatenate([t, np.cos(f * w), -np.sin(f * w)], axis=-1)
    zpad = np.zeros((length, LANES))
    zpad[:, :HY_EMB] = z
    deltas = np.abs(np.linspace(math.log(HY_TARGET) / HY_SLOW_PCT, math.log(HY_TARGET) / HY_FAST_PCT, HY_W))
    decay = np.exp(-t * deltas[None, :])
    return zpad.astype(np.float32), decay.astype(np.float32)


def _split_kernel(x_ref, hi_ref, lo_ref):
    hi, lo = _split_bf16(x_ref[...])
    hi_ref[...] = hi
    lo_ref[...] = lo


def _split_table(x, name):
    rows = min(x.shape[0], 512)
    spec = pl.BlockSpec((rows, x.shape[1]), lambda i: (i, 0))
    out = jax.ShapeDtypeStruct(x.shape, BF16)
    return pl.pallas_call(
        _split_kernel,
        grid=(x.shape[0] // rows,),
        in_specs=[spec],
        out_specs=[spec, spec],
        out_shape=[out, out],
        compiler_params=_cparams(("parallel",)),
        name=name,
    )(x)


def _dot_3pass_split(ah, al, b):
    bh, bl = _split_bf16(b)
    dot = functools.partial(jnp.dot, preferred_element_type=F32)
    return dot(ah, bh) + (dot(ah, bl) + dot(al, bh))


def _hy_filter_kernel(z_ref, dec_ref, w1_ref, b1_ref, f1_ref, w2_ref, b2_ref, f2_ref, w3f_ref, w3b_ref,
                      fwd_ref, fwd_lo_ref, ka_ref, ka2_ref, kb_ref, *, seq):
    h = jnp.sin(f1_ref[...] * (jnp.dot(z_ref[...], w1_ref[...], preferred_element_type=F32,
                                       precision=HIGHEST) + b1_ref[...]))
    h = jnp.sin(f2_ref[...] * (jnp.dot(h, w2_ref[...], preferred_element_type=F32,
                                       precision=HIGHEST) + b2_ref[...]))
    dec = dec_ref[...]
    hf = jnp.dot(h, w3f_ref[...], preferred_element_type=F32, precision=HIGHEST) * dec
    hb = jnp.dot(h, w3b_ref[...], preferred_element_type=F32, precision=HIGHEST) * dec
    row = lax.broadcasted_iota(I32, hb.shape, 0)
    hb = jnp.where(row == 0, 0.0, hb)
    inv_norm = 1.0 / (jnp.sum(jnp.abs(hf), axis=0, keepdims=True)
                      + jnp.sum(jnp.abs(hb), axis=0, keepdims=True) + EPS)
    p = (hf + hb) * inv_norm
    q = (hf - hb) * inv_norm
    fp = _dot_3pass_split(fwd_ref[...], fwd_lo_ref[...], p)
    fq = _dot_3pass_split(fwd_ref[seq:, :], fwd_lo_ref[seq:, :], q)
    kr = fp[:seq]
    row = lax.broadcasted_iota(I32, kr.shape, 0)
    ka_ref[0] = kr
    ka2_ref[0] = jnp.where(row == 0, fp[seq:seq + 1], kr)
    kb_ref[0] = jnp.where(row == 0, 0.0, -fq)


def _hy_filters(seq, zfeat, decay, fwd_hi, fwd_lo, w1, b1, f1, w2, b2, f2, w3):
    ns = HY_W // HY_SLAB
    per_o = 2 * ns
    small = lambda shape: pl.BlockSpec(shape, lambda o, s: (0, 0))
    out_spec = pl.BlockSpec((1, seq, HY_SLAB), lambda o, s: (o, 0, s))
    out = jax.ShapeDtypeStruct((HY_ORDER, seq, HY_W), F32)
    return pl.pallas_call(
        functools.partial(_hy_filter_kernel, seq=seq),
        grid=(HY_ORDER, ns),
        in_specs=[
            small((seq, LANES)),
            pl.BlockSpec((seq, HY_SLAB), lambda o, s: (0, s)),
            small((LANES, LANES)), small((1, LANES)), small((1, LANES)),
            small((LANES, LANES)), small((1, LANES)), small((1, LANES)),
            pl.BlockSpec((LANES, HY_SLAB), lambda o, s: (0, o * per_o + s)),
            pl.BlockSpec((LANES, HY_SLAB), lambda o, s: (0, o * per_o + ns + s)),
            small((2 * seq, seq)), small((2 * seq, seq)),
        ],
        out_specs=[out_spec, out_spec, out_spec],
        out_shape=[out, out, out],
        compiler_params=_cparams(("parallel", "parallel")),
        name=f"hyena_filters_{seq}",
    )(zfeat, decay, w1, b1, f1, w2, b2, f2, w3, w3, fwd_hi, fwd_lo)


def _hy_apply_kernel(x1_ref, x2_ref, v_ref, c1_ref, c2_ref, cv_ref, fwd_ref, inv_ref,
                     ka_ref, ka2_ref, kb_ref, bias_ref, o_ref, *, seq, nseq):
    def long_conv(sig, o):
        spec = jnp.dot(fwd_ref[...], sig.astype(BF16), preferred_element_type=F32)
        xr, xs = spec[:seq], spec[seq:]
        kb = kb_ref[o]
        yr = (xr * ka_ref[o] + xs * kb).astype(BF16)
        ys = (xs * ka2_ref[o] - xr * kb).astype(BF16)
        y = jnp.dot(inv_ref[:, :seq], yr, preferred_element_type=F32)
        y = y + jnp.dot(inv_ref[:, seq:], ys, preferred_element_type=F32)
        return y + sig * bias_ref[o:o + 1]

    for i in range(nseq):
        rows = slice(i * seq, (i + 1) * seq)
        x1 = _conv3_rows(x1_ref[rows, :], c1_ref[...])
        x2 = _conv3_rows(x2_ref[rows, :], c2_ref[...])
        v = _conv3_rows(v_ref[rows, :], cv_ref[...])
        z = x1 * long_conv(v, 0)
        o_ref[rows, :] = x2 * long_conv(z, 1)


HY_NSEQ = {True: 2, False: 4}


def _hy_apply(proj, conv_w_pad, fwd_bf, inv_bf, filters, bias_pad, latent):
    seq = DEC_SEQ if latent else SEQ
    nb = DEC_BATCH if latent else BATCH
    nseq = HY_NSEQ[latent]
    rows = nseq * seq
    row0 = 0 if latent else N_SAMPLE // rows
    ns = HY_W // HY_SLAB
    once = pl.Buffered(1)
    const = lambda shape: pl.BlockSpec(shape, lambda s, b: (0,) * len(shape), pipeline_mode=once)
    col = lambda c0: pl.BlockSpec((rows, HY_SLAB), lambda s, b: (row0 + b, c0 + s))
    cw = lambda c0: pl.BlockSpec((SUBLANES, HY_SLAB), lambda s, b: (0, c0 + s))
    filt = pl.BlockSpec((HY_ORDER, seq, HY_SLAB), lambda s, b: (0, 0, s), pipeline_mode=once)
    in_specs = [col(0), col(ns), col(2 * ns), cw(0), cw(ns), cw(2 * ns),
                const((2 * seq, seq)), const((seq, 2 * seq)), filt, filt, filt,
                pl.BlockSpec((SUBLANES, HY_SLAB), lambda s, b: (0, s))]
    args = [proj] * 3 + [conv_w_pad] * 3 + [fwd_bf, inv_bf, *filters, bias_pad]
    return pl.pallas_call(
        functools.partial(_hy_apply_kernel, seq=seq, nseq=nseq),
        grid=(ns, nb // nseq),
        in_specs=in_specs,
        out_specs=pl.BlockSpec((rows, HY_SLAB), lambda s, b: (b, s)),
        out_shape=jax.ShapeDtypeStruct((nb * seq, HY_W), F32),
        compiler_params=_cparams(("arbitrary", "arbitrary")),
        name="hyena_latent" if latent else "hyena_context",
    )(*args)


def _rope_tables(length):
    rows = length // GRID_W
    row = np.repeat(np.arange(rows, dtype=np.float64), GRID_W)
    colp = np.tile(np.arange(GRID_W, dtype=np.float64), rows)
    half = DA_DH // 2
    freqs = ROPE_THETA ** (-np.arange(0, half, 2, dtype=np.float64) / half)
    ang = np.concatenate([row[:, None] * freqs[None, :]] * 2 + [colp[:, None] * freqs[None, :]] * 2, axis=1)
    ang = np.concatenate([ang, ang], axis=1)
    lane = np.arange(LANES)
    sign = np.where((lane % half) < half // 2, -1.0, 1.0)[None, :]
    return np.cos(ang).astype(np.float32), (np.sin(ang) * sign).astype(np.float32)


def _rope(x, cos, sin_signed):
    q = DA_DH // 4
    lane = lax.broadcasted_iota(I32, x.shape, 1)
    partner = jnp.where((lane % (2 * q)) < q, pltpu.roll(x, LANES - q, 1), pltpu.roll(x, q, 1))
    return x * cos + partner * sin_signed


def _attn_kernel(*refs, seq, latent, lambda_init, heads):
    q_ref, k_ref, v_ref, lam_ref, sub_ref = refs[:5]
    if latent:
        cos_ref, sin_ref, ck_ref, cv_ref, o_ref = refs[5:]
    else:
        o_ref, nk_ref, nv_ref = refs[5:]
    hw = 2 * DA_DH
    lv = lam_ref[...]
    lam = (jnp.exp(jnp.sum(lv[0:1] * lv[1:2], axis=1, keepdims=True))
           - jnp.exp(jnp.sum(lv[2:3] * lv[3:4], axis=1, keepdims=True)) + lambda_init)
    scale = DA_DH ** -0.5
    dn = (((1,), (1,)), ((), ()))
    tq = min(seq, ATT_TQ)
    for hh in range(heads):
        lanes = slice(hh * hw, (hh + 1) * hw)
        k = k_ref[:, lanes]
        v = v_ref[:, lanes]
        if latent:
            k = _rope(k, cos_ref[...], sin_ref[...])
            k_all = jnp.concatenate([k, jnp.concatenate([ck_ref[0, 0, 0, hh], ck_ref[0, 0, 1, hh]], axis=1)], axis=0)
            v_all = jnp.concatenate([v, cv_ref[0, 0, hh]], axis=0)
        else:
            k_all, v_all = k, v
            nk_ref[0, 0, 0, hh] = k[:, :DA_DH]
            nk_ref[0, 0, 1, hh] = k[:, DA_DH:]
            nv_ref[0, 0, hh] = v
        kb = k_all.astype(BF16)
        vb = v_all.astype(BF16)
        for r in range(seq // tq):
            rows = slice(r * tq, (r + 1) * tq)
            q = q_ref[rows, lanes]
            if latent:
                q = _rope(q, cos_ref[rows, :], sin_ref[rows, :])
            q = q * scale
            first = lax.broadcasted_iota(I32, q.shape, 1) < DA_DH
            q1 = jnp.where(first, q, 0.0).astype(BF16)
            q2 = jnp.where(first, 0.0, q).astype(BF16)
            s1 = lax.dot_general(q1, kb, dn, preferred_element_type=F32)
            s2 = lax.dot_general(q2, kb, dn, preferred_element_type=F32)
            e1 = jnp.exp(s1 - jnp.max(s1, axis=-1, keepdims=True))
            e2 = jnp.exp(s2 - jnp.max(s2, axis=-1, keepdims=True))
            o1 = jnp.dot(e1.astype(BF16), vb, preferred_element_type=F32)
            o2 = jnp.dot(e2.astype(BF16), vb, preferred_element_type=F32)
            o = (o1 * (1.0 / jnp.sum(e1, axis=-1, keepdims=True))
                 - o2 * (lam / jnp.sum(e2, axis=-1, keepdims=True)))
            o = o * lax.rsqrt(jnp.mean(o * o, axis=-1, keepdims=True) + EPS) * sub_ref[...]
            o_ref[rows, lanes] = o * (1.0 - lambda_init)


def _attention(proj, lam_vec, subln, lambda_init, latent, rope=None, ctx_k=None, ctx_v=None):
    seq = DEC_SEQ if latent else SEQ
    nb = DEC_BATCH if latent else BATCH
    row0 = 0 if latent else N_SAMPLE // seq
    hw = 2 * DA_DH
    heads = 1 if latent else DA_HEADS
    steps = DA_HEADS // heads
    col = lambda c: pl.BlockSpec((seq, heads * hw), lambda b, h: (row0 + b, c * steps + h))
    in_specs = [col(3), col(4), col(5),
                pl.BlockSpec((4, DA_DH), lambda b, h: (0, 0)),
                pl.BlockSpec((1, hw), lambda b, h: (0, 0))]
    args = [proj] * 3 + [lam_vec, subln.reshape(1, hw)]
    out_spec = pl.BlockSpec((seq, heads * hw), lambda b, h: (b, h))
    out_shape = [jax.ShapeDtypeStruct((nb * seq, DA_W), F32)]
    out_specs = [out_spec]
    k_spec = pl.BlockSpec((1, 1, 2, heads, SEQ, DA_DH), lambda b, h: (b, 0, 0, h, 0, 0))
    v_spec = pl.BlockSpec((1, 1, heads, SEQ, hw), lambda b, h: (b, 0, h, 0, 0))
    if latent:
        in_specs += [pl.BlockSpec((seq, hw), lambda b, h: (0, 0))] * 2 + [k_spec, v_spec]
        args += [rope[0], rope[1], ctx_k, ctx_v]
    else:
        out_shape += [jax.ShapeDtypeStruct((BATCH, 1, 2, DA_HEADS, SEQ, DA_DH), F32),
                      jax.ShapeDtypeStruct((BATCH, 1, DA_HEADS, SEQ, hw), F32)]
        out_specs += [k_spec, v_spec]
    return pl.pallas_call(
        functools.partial(_attn_kernel, seq=seq, latent=latent, lambda_init=lambda_init, heads=heads),
        grid=(nb, steps),
        in_specs=in_specs,
        out_specs=out_specs,
        out_shape=out_shape,
        compiler_params=_cparams(("arbitrary", "arbitrary")),
        name="diff_attn_latent" if latent else "diff_attn_context",
    )(*args)


HALF_D = D_MODEL // 2
HI16 = -65536


def _pack_bf16_pair(lo, hi):
    lo_bits = lax.bitcast_convert_type(lo.astype(BF16).astype(F32), I32)
    hi_bits = lax.bitcast_convert_type(hi.astype(BF16).astype(F32), I32)
    return (hi_bits & HI16) | lax.shift_right_logical(lo_bits, 16)


def _unpack_bf16_pair(p):
    lo = lax.bitcast_convert_type(lax.shift_left(p, 16), F32).astype(BF16)
    hi = lax.bitcast_convert_type(p & HI16, F32).astype(BF16)
    return lo, hi


GRP_EXPERT, GRP_REL, GRP_COUNT = range(3)


def _route_rows(x, g, mod, wr, br, tril_ref, triu_ref, carry_ref):
    u = _norm_mod(x, g, mod, 3, 4)
    u_packed = _pack_bf16_pair(u[:, :HALF_D], u[:, HALF_D:])
    logits = _dot_3pass(u, wr) + br
    lane = lax.broadcasted_iota(I32, logits.shape, 1)
    lane_f = lane.astype(F32)
    neg = -jnp.inf
    big = float(LANES)

    def first_max(vals):
        m = jnp.max(vals, axis=-1, keepdims=True)
        return m, jnp.min(jnp.where(vals == m, lane_f, big), axis=-1, keepdims=True)

    is_grp = lane < N_GROUPS
    gmax, gidx = first_max(jnp.where(is_grp, logits, neg))
    gsum = jnp.sum(jnp.where(is_grp, jnp.exp(logits - gmax), 0.0), axis=-1, keepdims=True)
    g_w = 1.0 / gsum
    lo = N_GROUPS + EXP_PER_GROUP * gidx
    in_grp = jnp.logical_and(lane_f >= lo, lane_f < lo + EXP_PER_GROUP)
    el = jnp.where(in_grp, logits, neg)
    v1, i1 = first_max(el)
    v2, i2 = first_max(jnp.where(lane_f == i1, neg, el))
    e = jnp.exp(v2 - v1)
    p1 = 1.0 / (1.0 + e)
    hit1 = lane_f == i1 - N_GROUPS
    hit2 = lane_f == i2 - N_GROUPS
    onehot = jnp.where(jnp.logical_or(hit1, hit2), 1.0, 0.0)
    cum = jnp.dot(tril_ref[...], onehot.astype(BF16), preferred_element_type=F32)
    n = cum[TM - 1:TM, :]
    n_pad = jnp.floor((n + (ROW_ALIGN - 1.0)) * (1.0 / ROW_ALIGN)) * ROW_ALIGN
    loc_off = jnp.dot(jnp.broadcast_to(n_pad, (SUBLANES, LANES)).astype(BF16), triu_ref[...],
                      preferred_element_type=F32)[0:1]
    base = cum - onehot + loc_off
    loc1 = jnp.sum(jnp.where(hit1, base, 0.0), axis=-1, keepdims=True)
    loc2 = jnp.sum(jnp.where(hit2, base, 0.0), axis=-1, keepdims=True)
    info = jnp.where(lane == 0, g_w * p1, jnp.where(lane == 1, g_w * (e * p1), jnp.where(lane == 2, loc1, loc2)))
    start8 = lax.broadcasted_iota(I32, (LANES, LANES), 0).astype(F32) * ROW_ALIGN
    owner = jnp.where(jnp.logical_and(start8 >= loc_off, start8 < loc_off + n_pad), 1.0, 0.0)
    lane128 = lax.broadcasted_iota(I32, (LANES, LANES), 1)
    c_exp = jnp.sum(owner * lane128.astype(F32), axis=-1, keepdims=True)
    c_rel = jnp.sum(owner * (carry_ref[...] + start8 - loc_off), axis=-1, keepdims=True)
    chunks = jnp.where(lane128 == 0, c_exp, jnp.where(lane128 == 1, c_rel, 0.0)).T
    n_chunks = jnp.sum(n_pad, axis=-1, keepdims=True) * (1.0 / ROW_ALIGN)
    row = lax.broadcasted_iota(I32, (SUBLANES, LANES), 0)
    grp = jnp.where(row == GRP_EXPERT, chunks[0:1], jnp.where(row == GRP_REL, chunks[1:2], n_chunks))
    carry_ref[...] = carry_ref[...] + n_pad
    return u_packed, info, info.T[:SUBLANES], grp.astype(I32)


def _proj_res_router_kernel(*refs, n_x, gate_idx):
    (al_ref, ac_ref, bl_ref, bc_ref, wa_ref, wb_ref, mod_ref, g2_ref, wr_ref, br_ref, tril_ref, triu_ref,
     o_ref, u_ref, loct_ref, info_ref, grp_ref, tot_ref, carry_ref) = refs[n_x:]

    @pl.when(pl.program_id(0) == 0)
    def _():
        carry_ref[...] = jnp.zeros_like(carry_ref)

    a = _group_pick(al_ref, ac_ref)
    b = _group_pick(bl_ref, bc_ref)
    y = jnp.dot(a.astype(BF16), wa_ref[...], preferred_element_type=F32)
    y = y + jnp.dot(b.astype(BF16), wb_ref[...], preferred_element_type=F32)
    mod = mod_ref[0]
    x_new = _rows_value(refs[:n_x]) + mod[gate_idx:gate_idx + 1] * y
    o_ref[...] = x_new
    for t in range(TM_WIDE // TM):
        rows = slice(t * TM, (t + 1) * TM)
        u_packed, info, info_t, grp = _route_rows(x_new[rows], g2_ref[...], mod, wr_ref[...], br_ref[...],
                                                 tril_ref, triu_ref, carry_ref)
        u_ref[rows, :] = u_packed
        info_ref[rows, :] = info
        loct_ref[t] = info_t
        grp_ref[t] = grp
    tot_ref[...] = jnp.broadcast_to(carry_ref[...], tot_ref.shape).astype(I32)


def _proj_residual_router(x, a, b, w_bf16, mods, gate_idx, norm2, w_route, b_route, tril, triu):
    half = a[0].shape[1]
    tm = TM_WIDE
    sub = tm // TM
    nt = N_TOK // TM
    x_specs, x_args = _rows_specs(x, tm)
    const = lambda shape: pl.BlockSpec(shape, lambda i: (0,) * len(shape))
    return pl.pallas_call(
        functools.partial(_proj_res_router_kernel, n_x=len(x_args), gate_idx=gate_idx),
        grid=(N_TOK // tm,),
        in_specs=x_specs + _group_specs(half, tm) + _group_specs(half, tm) + [
            pl.BlockSpec((half, D_MODEL), lambda i: (0, 0)),
            pl.BlockSpec((half, D_MODEL), lambda i: (1, 0)),
            pl.BlockSpec((1, MOD_ROWS, D_MODEL), lambda i: (i * tm // SEG, 0, 0)),
            const((1, D_MODEL)), const((D_MODEL, LANES)), const((1, LANES)), const((TM, TM)), const((LANES, LANES)),
        ],
        out_specs=[
            pl.BlockSpec((tm, D_MODEL), lambda i: (i, 0)),
            pl.BlockSpec((tm, HALF_D), lambda i: (i, 0)),
            pl.BlockSpec((sub, SUBLANES, TM), lambda i: (i, 0, 0)),
            pl.BlockSpec((tm, LANES), lambda i: (i, 0)),
            pl.BlockSpec((sub, SUBLANES, LANES), lambda i: (i, 0, 0)),
            pl.BlockSpec((SUBLANES, LANES), lambda i: (0, 0)),
        ],
        out_shape=[
            jax.ShapeDtypeStruct((N_TOK, D_MODEL), F32),
            jax.ShapeDtypeStruct((N_TOK, HALF_D), I32),
            jax.ShapeDtypeStruct((nt, SUBLANES, TM), F32),
            jax.ShapeDtypeStruct((N_TOK, LANES), F32),
            jax.ShapeDtypeStruct((nt, SUBLANES, LANES), I32),
            jax.ShapeDtypeStruct((SUBLANES, LANES), I32),
        ],
        scratch_shapes=[pltpu.VMEM((1, LANES), F32)],
        compiler_params=_cparams(("arbitrary",)),
        name="outproj_residual_router",
    )(*x_args, a[0], a[1], b[0], b[1], w_bf16, w_bf16, mods, norm2.reshape(1, D_MODEL), w_route, b_route, tril, triu)


def _for_chunks(n, fn):
    for size in CHUNKS:
        @pl.when((n & size) != 0)
        def _():
            fn(n & ~(2 * size - 1), size)


WAIT_SIZES = (512, 256, 128, 64, 32, 16, 8)


def _start_tile_copies(grp_ref, seg_ref, make):
    def body(c, carry):
        slot = seg_ref[0, grp_ref[0, GRP_EXPERT, c]] + grp_ref[0, GRP_REL, c]
        make(pl.ds(pl.multiple_of(c * ROW_ALIGN, ROW_ALIGN), ROW_ALIGN),
             pl.ds(pl.multiple_of(slot, ROW_ALIGN), ROW_ALIGN)).start()
        return carry

    lax.fori_loop(0, grp_ref[0, GRP_COUNT, 0], body, 0)


def _wait_tile_copies(grp_ref, make):
    rows = grp_ref[0, GRP_COUNT, 0] * ROW_ALIGN
    for size in WAIT_SIZES:
        @pl.when((rows & size) != 0)
        def _():
            make(pl.ds(0, size), pl.ds(0, size)).wait()


def _all_experts(body):
    lax.fori_loop(0, N_EXPERTS, lambda e, c: (body(e), c)[1], 0)


def _dispatch_kernel(grp_ref, grp1_ref, grp2_ref, seg_ref, loct_ref, u_ref, xb_ref, buf, zbuf, sem):
    i = pl.program_id(0)
    last = pl.num_programs(0) - 1
    cur = i % 2

    def maker(b):
        return lambda loc, slot: pltpu.make_async_copy(buf.at[b, loc], xb_ref.at[slot], sem.at[b])

    @pl.when(i >= 2)
    def _():
        _wait_tile_copies(grp2_ref, maker(cur))

    lo, hi = _unpack_bf16_pair(u_ref[...])
    row = lax.broadcasted_iota(I32, (R_LOC, TM), 0).astype(F32)
    sel = jnp.logical_or(row == loct_ref[0, 2:3, :], row == loct_ref[0, 3:4, :])
    pt = jnp.where(sel, 1.0, 0.0).astype(BF16)
    buf[cur] = _pack_bf16_pair(jnp.dot(pt, lo, preferred_element_type=F32),
                               jnp.dot(pt, hi, preferred_element_type=F32))
    _start_tile_copies(grp_ref, seg_ref, maker(cur))

    @pl.when(i == last)
    def _():
        @pl.when(i >= 1)
        def _():
            _wait_tile_copies(grp1_ref, maker(1 - cur))

        _wait_tile_copies(grp_ref, maker(cur))

        zbuf[...] = jnp.zeros_like(zbuf)

        def tail(e, act):
            total = seg_ref[1, e]
            dst = seg_ref[0, e] + total

            def one(off, size):
                act(pltpu.make_async_copy(zbuf.at[pl.ds(0, size)],
                                          xb_ref.at[pl.ds(pl.multiple_of(dst + off, ROW_ALIGN), size)], sem.at[2]))

            _for_chunks((-total) & (MOE_BLK - 1), one)

        _all_experts(lambda e: tail(e, lambda cp: cp.start()))
        _all_experts(lambda e: tail(e, lambda cp: cp.wait()))

        e_last = N_EXPERTS - 1
        used = (seg_ref[0, e_last] + seg_ref[1, e_last] + MOE_BLK - 1) // MOE_BLK

        def spare(j):
            return pltpu.make_async_copy(zbuf, xb_ref.at[pl.ds(pl.multiple_of(j * MOE_BLK, MOE_BLK), MOE_BLK)],
                                         sem.at[2])

        lax.fori_loop(used, MOE_NB, lambda j, c: (spare(j).start(), c)[1], 0)
        lax.fori_loop(used, MOE_NB, lambda j, c: (spare(j).wait(), c)[1], 0)


def _grp_spec(back):
    return pl.BlockSpec((1, SUBLANES, LANES), lambda i: (jnp.maximum(i - back, 0), 0, 0), memory_space=pltpu.SMEM)


def _dispatch(grp, seg, loct, u):
    return pl.pallas_call(
        _dispatch_kernel,
        grid=(N_TOK // TM,),
        in_specs=[
            _grp_spec(0), _grp_spec(1), _grp_spec(2),
            pl.BlockSpec(memory_space=pltpu.SMEM),
            pl.BlockSpec((1, SUBLANES, TM), lambda i: (i, 0, 0)),
            pl.BlockSpec((TM, HALF_D), lambda i: (i, 0)),
        ],
        out_specs=pl.BlockSpec(memory_space=pl.ANY),
        out_shape=jax.ShapeDtypeStruct((MOE_ROWS, HALF_D), I32),
        scratch_shapes=[pltpu.VMEM((2, R_LOC, HALF_D), I32), pltpu.VMEM((MOE_BLK, HALF_D), I32),
                        pltpu.SemaphoreType.DMA((3,))],
        compiler_params=_cparams(("arbitrary",)),
        name="moe_dispatch",
    )(grp, grp, grp, seg, loct, u)


def _expert_kernel(seg_ref, w1_ref, w3_ref, w2_ref, xb_ref, yb_ref, w1b, w3b, w2b, xbuf, ybuf, xsem, ysem):
    e = pl.program_id(0)
    last_e = pl.num_programs(0) - 1
    blk0 = seg_ref[0, e] // MOE_BLK
    nblk = (seg_ref[1, e] + MOE_BLK - 1) // MOE_BLK
    used = (seg_ref[0, N_EXPERTS - 1] + seg_ref[1, N_EXPERTS - 1] + MOE_BLK - 1) // MOE_BLK

    def rows(g):
        return pl.ds(pl.multiple_of(g * MOE_BLK, MOE_BLK), MOE_BLK)

    def fetch(g, b):
        return pltpu.make_async_copy(xb_ref.at[rows(g)], xbuf.at[b], xsem.at[b])

    def flush(g, b):
        return pltpu.make_async_copy(ybuf.at[b], yb_ref.at[rows(g)], ysem.at[b])

    @pl.when(e == 0)
    def _():
        fetch(0, 0).start()

    @pl.when(nblk > 0)
    def _():
        w1b[...] = w1_ref[0, 0].astype(BF16)
        w3b[...] = w3_ref[0, 0].astype(BF16)
        w2b[...] = w2_ref[0, 0].astype(BF16)

        def block(i, carry):
            g = blk0 + i
            b = g % 2
            fetch(g, b).wait()

            @pl.when(g + 1 < used)
            def _():
                fetch(g + 1, 1 - b).start()

            @pl.when(g >= 2)
            def _():
                flush(g - 2, b).wait()

            def ffn(x_packed):
                lo, hi = _unpack_bf16_pair(x_packed)

                def up(wb):
                    return (jnp.dot(lo, wb[:HALF_D, :], preferred_element_type=F32)
                            + jnp.dot(hi, wb[HALF_D:, :], preferred_element_type=F32))

                hdn = (_silu(up(w1b)) * up(w3b)).astype(BF16)
                y = jnp.dot(hdn, w2b[...], preferred_element_type=F32)
                return _pack_bf16_pair(y[:, :HALF_D], y[:, HALF_D:])

            half = MOE_BLK // 2
            slots_here = seg_ref[1, e] - i * MOE_BLK

            @pl.when(slots_here > half)
            def _():
                ybuf[b] = ffn(xbuf[b])

            @pl.when(slots_here <= half)
            def _():
                ybuf[b, :half, :] = ffn(xbuf[b, :half, :])
                ybuf[b, half:, :] = jnp.zeros((half, HALF_D), I32)

            flush(g, b).start()
            return carry

        lax.fori_loop(0, nblk, block, 0)

    @pl.when(e == last_e)
    def _():
        @pl.when(used >= 2)
        def _():
            flush(used - 2, used % 2).wait()

        flush(used - 1, (used - 1) % 2).wait()
        xbuf[0] = jnp.zeros((MOE_BLK, HALF_D), I32)

        def spare(g):
            return pltpu.make_async_copy(xbuf.at[0], yb_ref.at[rows(g)], xsem.at[0])

        lax.fori_loop(used, MOE_NB, lambda g, c: (spare(g).start(), c)[1], 0)
        lax.fori_loop(used, MOE_NB, lambda g, c: (spare(g).wait(), c)[1], 0)


def _experts(seg, xb, w1, w3, w2, layer):
    hbm = pl.BlockSpec(memory_space=pl.ANY)
    w_in = pl.BlockSpec((1, 1, D_MODEL, D_EXPERT), lambda e, s: (layer, e, 0, 0))
    w_out = pl.BlockSpec((1, 1, D_EXPERT, D_MODEL), lambda e, s: (layer, e, 0, 0))
    return pl.pallas_call(
        _expert_kernel,
        grid_spec=pltpu.PrefetchScalarGridSpec(
            num_scalar_prefetch=1,
            grid=(N_EXPERTS,),
            in_specs=[w_in, w_in, w_out, hbm],
            out_specs=hbm,
            scratch_shapes=[pltpu.VMEM((D_MODEL, D_EXPERT), BF16), pltpu.VMEM((D_MODEL, D_EXPERT), BF16),
                            pltpu.VMEM((D_EXPERT, D_MODEL), BF16),
                            pltpu.VMEM((2, MOE_BLK, HALF_D), I32), pltpu.VMEM((2, MOE_BLK, HALF_D), I32),
                            pltpu.SemaphoreType.DMA((2,)), pltpu.SemaphoreType.DMA((2,))],
        ),
        out_shape=jax.ShapeDtypeStruct((MOE_ROWS, HALF_D), I32),
        compiler_params=_cparams(("arbitrary",)),
        name="moe_experts",
    )(seg, w1, w3, w2, xb)


def _combine_kernel(*refs, final):
    grp_ref, nxt_ref, seg_ref, x_ref, info_ref, mod_ref, yb_ref = refs[:7]
    if final:
        fin_ref, olat_ref, octx_ref, ybuf, sem = refs[7:]
    else:
        o_ref, ybuf, sem = refs[7:]
    i = pl.program_id(0)
    cur = i % 2

    def maker(b):
        return lambda loc, slot: pltpu.make_async_copy(yb_ref.at[slot], ybuf.at[b, loc], sem.at[b])

    @pl.when(i == 0)
    def _():
        ybuf[...] = jnp.zeros_like(ybuf)
        _start_tile_copies(grp_ref, seg_ref, maker(0))

    @pl.when(i + 1 < pl.num_programs(0))
    def _():
        _start_tile_copies(nxt_ref, seg_ref, maker(1 - cur))

    _wait_tile_copies(grp_ref, maker(cur))
    info = info_ref[...]
    col = lax.broadcasted_iota(I32, (TM, R_LOC), 1).astype(F32)
    p = jnp.where(col == info[:, 2:3], info[:, 0:1], jnp.where(col == info[:, 3:4], info[:, 1:2], 0.0))
    p = p.astype(BF16)
    lo, hi = _unpack_bf16_pair(ybuf[cur])
    y = jnp.concatenate([jnp.dot(p, lo, preferred_element_type=F32),
                         jnp.dot(p, hi, preferred_element_type=F32)], axis=1)
    out = x_ref[...] + mod_ref[0][5:6] * y
    if not final:
        o_ref[...] = out
        return
    out = out * lax.rsqrt(jnp.mean(out * out, axis=-1, keepdims=True) + EPS) * fin_ref[...]

    @pl.when(i < N_LAT_TILES)
    def _():
        olat_ref[...] = out

    @pl.when(i >= N_LAT_TILES)
    def _():
        octx_ref[...] = out


def _combine(x, grp, seg, info, mods, yb, final_g=None):
    final = final_g is not None
    nt = N_TOK // TM
    in_specs = [
        _grp_spec(0),
        pl.BlockSpec((1, SUBLANES, LANES), lambda i: (jnp.minimum(i + 1, nt - 1), 0, 0), memory_space=pltpu.SMEM),
        pl.BlockSpec(memory_space=pltpu.SMEM),
        pl.BlockSpec((TM, D_MODEL), lambda i: (i, 0)),
        pl.BlockSpec((TM, LANES), lambda i: (i, 0)),
        pl.BlockSpec((1, MOD_ROWS, D_MODEL), lambda i: (i * TM // SEG, 0, 0)),
        pl.BlockSpec(memory_space=pl.ANY),
    ]
    args = [grp, grp, seg, x, info, mods, yb]
    if final:
        in_specs.append(pl.BlockSpec((1, D_MODEL), lambda i: (0, 0)))
        args.append(final_g.reshape(1, D_MODEL))
        out_specs = _group_specs(D_MODEL)
        out_shape = [jax.ShapeDtypeStruct((N_SAMPLE, D_MODEL), F32), jax.ShapeDtypeStruct((N_PROMPT, D_MODEL), F32)]
    else:
        out_specs = pl.BlockSpec((TM, D_MODEL), lambda i: (i, 0))
        out_shape = jax.ShapeDtypeStruct((N_TOK, D_MODEL), F32)
    return pl.pallas_call(
        functools.partial(_combine_kernel, final=final),
        grid=(nt,),
        in_specs=in_specs,
        out_specs=out_specs,
        out_shape=out_shape,
        scratch_shapes=[pltpu.VMEM((2, R_LOC, HALF_D), I32), pltpu.SemaphoreType.DMA((2,))],
        compiler_params=_cparams(("arbitrary",)),
        name="moe_combine_final" if final else "moe_combine",
    )(*args)


def _route_weights(w_grp, b_grp, w_exp, b_exp):
    w_route = jnp.zeros((D_MODEL, LANES), F32)
    w_route = w_route.at[:, :N_GROUPS].set(w_grp).at[:, N_GROUPS:N_GROUPS + N_EXPERTS].set(w_exp)
    b_route = jnp.zeros((1, LANES), F32)
    b_route = b_route.at[0, :N_GROUPS].set(b_grp).at[0, N_GROUPS:N_GROUPS + N_EXPERTS].set(b_exp)
    return w_route, b_route


def _moe_layer(x, route, layer, mods, w1, w3, w2, final_g=None):
    u, loct, info, grp, tot = route
    total = tot[0, :N_EXPERTS]
    padded = (total + MOE_BLK - 1) // MOE_BLK * MOE_BLK
    pad_end = jnp.cumsum(padded)
    pad_start = pad_end - padded
    seg = jnp.zeros((2, LANES), I32).at[0, :N_EXPERTS].set(pad_start).at[1, :N_EXPERTS].set(total)
    xb = _dispatch(grp, seg, loct, u)
    yb = _experts(seg, xb, w1, w3, w2, layer)
    return _combine(x, grp, seg, info, mods, yb, final_g)


def _pad_rows(w, rows=SUBLANES):
    return jnp.zeros((rows, w.shape[1]), w.dtype).at[:w.shape[0]].set(w)


def _pad2(w, rows, cols):
    return jnp.zeros((rows, cols), w.dtype).at[:w.shape[0], :w.shape[1]].set(w)


def kernel(x_prompt, x_sample, c, c_ctx, state_ret, cache_k, cache_v, ada_w, ada_b, norm1, norm2, final_norm, ev_w_in, ev_w_out, ret_decay_fwd, ret_decay_bwd, sc_conv_w, od_w_in, od_w_out, hy_conv_w, hy_w1, hy_b1, hy_fr1, hy_w2, hy_b2, hy_fr2, hy_w3, hy_bias, da_lambda, da_subln, moe_w_grp, moe_b_grp, moe_w_exp, moe_b_exp, moe_w1, moe_w3, moe_w2):
    x = (x_sample.reshape(N_SAMPLE, D_MODEL), x_prompt.reshape(N_PROMPT, D_MODEL))

    cvecs = jnp.zeros((16, D_MODEL), F32).at[:DEC_BATCH].set(c).at[DEC_BATCH].set(c_ctx)
    m = _mods(cvecs, ada_w, ada_b).reshape(DEPTH, 16, N_MOD, D_MODEL)
    seg_mods = jnp.concatenate(
        [m[:, :DEC_BATCH], jnp.broadcast_to(m[:, DEC_BATCH:DEC_BATCH + 1], (DEPTH, N_SEG - DEC_BATCH, N_MOD, D_MODEL))],
        axis=1)
    seg_mods = jnp.pad(seg_mods, ((0, 0), (0, 0), (0, MOD_ROWS - N_MOD), (0, 0)))

    tril = jnp.asarray(np.tril(np.ones((TM, TM), np.float32)), BF16)
    triu = jnp.asarray(np.triu(np.ones((LANES, LANES), np.float32), 1), BF16)

    for l in range(DEPTH):
        mods = seg_mods[l]
        i = l // 2
        if l % 2 == 0:
            proj = _norm_mod_matmul(x, norm1[l], mods, ev_w_in[i].astype(BF16), 0, 1)
            scw = _pad_rows(sc_conv_w[i])
            ret_l, sc_l = _even_mixer(proj, scw, ret_decay_fwd[i], ret_decay_bwd[i], state=state_ret[:, i:i + 1])
            ret_c, sc_c, new_state = _even_mixer(proj, scw, ret_decay_fwd[i], ret_decay_bwd[i])
            mixed, w_out = ((ret_l, ret_c), (sc_l, sc_c)), ev_w_out[i]
        else:
            lambda_init = 0.8 - 0.6 * math.exp(-0.3 * l)
            proj = _norm_mod_matmul(x, norm1[l], mods, od_w_in[i].astype(BF16), 0, 1)
            conv_w = _pad_rows(hy_conv_w[i])
            bias = _pad_rows(hy_bias[i])
            w1 = _pad2(hy_w1[i], LANES, LANES)
            b1 = _pad2(hy_b1[i][None], 1, LANES)
            f1 = _pad2(hy_fr1[i][None], 1, LANES)
            w2 = _pad2(hy_w2[i], LANES, LANES)
            b2 = _pad2(hy_b2[i][None], 1, LANES)
            f2 = _pad2(hy_fr2[i][None], 1, LANES)
            w3 = _pad2(hy_w3[i], LANES, hy_w3.shape[2])
            hy = []
            for latent in (True, False):
                seq = DEC_SEQ if latent else SEQ
                fwd_np, inv_np = _dft_tables(seq)
                zfeat, decay = _hyena_consts(seq)
                fwd_bf, fwd_lo = _split_table(jnp.asarray(fwd_np), f"dft_fwd_split_{seq}")
                inv_bf, _ = _split_table(jnp.asarray(inv_np), f"dft_inv_split_{seq}")
                filters = _hy_filters(seq, jnp.asarray(zfeat), jnp.asarray(decay), fwd_bf, fwd_lo,
                                      w1, b1, f1, w2, b2, f2, w3)
                hy.append(_hy_apply(proj, conv_w, fwd_bf, inv_bf, filters, bias, latent))
            cos, sin = _rope_tables(DEC_SEQ)
            att_l = _attention(proj, da_lambda[i], da_subln[i], lambda_init, True,
                               rope=(jnp.asarray(cos), jnp.asarray(sin)),
                               ctx_k=cache_k[:, i:i + 1], ctx_v=cache_v[:, i:i + 1])[0]
            att_c, new_k, new_v = _attention(proj, da_lambda[i], da_subln[i], lambda_init, False)
            mixed, w_out = (tuple(hy), (att_l, att_c)), od_w_out[i]
        w_route, b_route = _route_weights(moe_w_grp[l], moe_b_grp[l], moe_w_exp[l], moe_b_exp[l])
        x, *route = _proj_residual_router(x, mixed[0], mixed[1], w_out.astype(BF16), mods, 2,
                                          norm2[l], w_route, b_route, tril, triu)
        x = _moe_layer(x, route, l, mods, moe_w1, moe_w3, moe_w2,
                       final_g=final_norm if l == DEPTH - 1 else None)

    y_sample, y_prompt = x
    return (y_prompt.reshape(BATCH, SEQ, D_MODEL), y_sample.reshape(DEC_BATCH, DEC_SEQ, D_MODEL),
            new_state, new_k, new_v)
```

```python
import functools
import math

import numpy as np
import jax
import jax.numpy as jnp
from jax import lax
from jax.experimental import pallas as pl
from jax.experimental.pallas import tpu as pltpu

F32 = jnp.float32
BF16 = jnp.bfloat16
I32 = jnp.int32
HIGHEST = lax.Precision.HIGHEST

D_MODEL = 1024
BATCH = 32
SEQ = 256
DEPTH = 2
DEC_BATCH = 8
DEC_SEQ = 1024
PAST_LEN = 256
GRID_W = 64
N_MOD = 6
EPS = 1e-6
GN_EPS = 1e-5

RET_DK = 128
RET_DV = 128
RET_HEADS = 4
RET_W = 512
RET_QK = 512
SC_W = 512
EVEN_IN = 2 * RET_QK + 2 * RET_W + 3 * SC_W

HY_W = 512
HY_ORDER = 2
HY_EMB = 33
HY_FFN = 64
HY_TARGET = 1e-2
HY_FAST_PCT = 0.3
HY_SLOW_PCT = 1.5
DA_DH = 64
DA_HEADS = 4
DA_W = 512
ODD_IN = 3 * HY_W + 3 * DA_W
ROPE_THETA = 10000.0

N_GROUPS = 4
EXP_PER_GROUP = 8
N_EXPERTS = 32
TOP_K = 2
D_EXPERT = 512

N_SAMPLE = DEC_BATCH * DEC_SEQ
N_PROMPT = BATCH * SEQ
N_TOK = N_SAMPLE + N_PROMPT
SEG = 1024
N_SEG = N_TOK // SEG
MOD_ROWS = 8

LANES = 128
SUBLANES = 8
VMEM_LIMIT = 56 * 1024 * 1024

TM = 256
MOE_BLK = 512
ROW_ALIGN = SUBLANES
R_LOC = (TOP_K * TM + N_EXPERTS * (ROW_ALIGN - 1) + LANES - 1) // LANES * LANES
CHUNKS = (256, 128, 64, 32, 16, 8)
MOE_ROWS = ((N_TOK * TOP_K + N_EXPERTS * (N_TOK // TM) * (ROW_ALIGN - 1) + N_EXPERTS * (MOE_BLK - 1))
            // MOE_BLK + 1) * MOE_BLK
MOE_NB = MOE_ROWS // MOE_BLK
HY_SLAB = 256
ATT_TQ = 256


def _cparams(sem, vmem=VMEM_LIMIT):
    return pltpu.CompilerParams(dimension_semantics=sem, vmem_limit_bytes=vmem)


def _silu(x):
    return x * (1.0 / (1.0 + jnp.exp(-x)))


def _split_bf16(a):
    hi = a.astype(BF16)
    return hi, (a - hi.astype(F32)).astype(BF16)


def _dot_3pass(a, b):
    ah, al = _split_bf16(a)
    bh, bl = _split_bf16(b)
    dot = functools.partial(jnp.dot, preferred_element_type=F32)
    return dot(ah, bh) + (dot(ah, bl) + dot(al, bh))


MODS_TN = 1536


def _mods_kernel(c_ref, w_ref, b_ref, o_ref):
    s = _silu(c_ref[...])
    o_ref[0] = jnp.dot(s, w_ref[0], preferred_element_type=F32, precision=HIGHEST) + b_ref[0]


def _mods(cvecs, ada_w, ada_b):
    n = N_MOD * D_MODEL
    return pl.pallas_call(
        _mods_kernel,
        grid=(DEPTH, n // MODS_TN),
        in_specs=[
            pl.BlockSpec((16, D_MODEL), lambda l, j: (0, 0)),
            pl.BlockSpec((1, D_MODEL, MODS_TN), lambda l, j: (l, 0, j)),
            pl.BlockSpec((1, 1, MODS_TN), lambda l, j: (l, 0, j)),
        ],
        out_specs=pl.BlockSpec((1, 16, MODS_TN), lambda l, j: (l, 0, j)),
        out_shape=jax.ShapeDtypeStruct((DEPTH, 16, n), F32),
        compiler_params=_cparams(("parallel", "parallel")),
        name="adaln_mods",
    )(cvecs, ada_w, ada_b.reshape(DEPTH, 1, n))


def _norm_mod(x, g, mod, shift_idx, scale_idx):
    ms = jnp.mean(x * x, axis=-1, keepdims=True)
    y = x * lax.rsqrt(ms + EPS) * g
    return y * (1.0 + mod[scale_idx:scale_idx + 1]) + mod[shift_idx:shift_idx + 1]


N_LAT_TILES = N_SAMPLE // TM
TM_WIDE = 512


def _group_specs(width, tm=TM):
    n_lat = N_SAMPLE // tm
    lat = pl.BlockSpec((tm, width), lambda i: (jnp.minimum(i, n_lat - 1), 0))
    ctx = pl.BlockSpec((tm, width), lambda i: (jnp.maximum(i - n_lat, 0), 0))
    return [lat, ctx]


def _group_pick(lat_ref, ctx_ref):
    n_lat = N_SAMPLE // lat_ref.shape[0]
    return jnp.where(pl.program_id(0) < n_lat, lat_ref[...], ctx_ref[...])


def _rows_specs(x, tm=TM):
    if isinstance(x, tuple):
        return _group_specs(D_MODEL, tm), list(x)
    return [pl.BlockSpec((tm, D_MODEL), lambda i: (i, 0))], [x]


def _rows_value(refs):
    return _group_pick(*refs) if len(refs) == 2 else refs[0][...]


def _nmm_kernel(*refs, n_x, shift_idx, scale_idx):
    g_ref, mod_ref, w_ref, o_ref = refs[n_x:]
    u = _norm_mod(_rows_value(refs[:n_x]), g_ref[...], mod_ref[0], shift_idx, scale_idx)
    o_ref[...] = jnp.dot(u.astype(BF16), w_ref[...], preferred_element_type=F32)


def _norm_mod_matmul(x, g, mods, w_bf16, shift_idx, scale_idx):
    n = w_bf16.shape[1]
    x_specs, x_args = _rows_specs(x)
    return pl.pallas_call(
        functools.partial(_nmm_kernel, n_x=len(x_args), shift_idx=shift_idx, scale_idx=scale_idx),
        grid=(N_TOK // TM,),
        in_specs=x_specs + [
            pl.BlockSpec((1, D_MODEL), lambda i: (0, 0)),
            pl.BlockSpec((1, MOD_ROWS, D_MODEL), lambda i: (i * TM // SEG, 0, 0)),
            pl.BlockSpec((D_MODEL, n), lambda i: (0, 0)),
        ],
        out_specs=pl.BlockSpec((TM, n), lambda i: (i, 0)),
        out_shape=jax.ShapeDtypeStruct((N_TOK, n), F32),
        compiler_params=_cparams(("parallel",)),
        name="norm_mod_inproj",
    )(*x_args, g.reshape(1, D_MODEL), mods, w_bf16)


def _conv3_rows(z, w):
    n = z.shape[0]
    row = lax.broadcasted_iota(I32, z.shape, 0)
    zm = jnp.where(row == 0, 0.0, pltpu.roll(z, 1, 0))
    zp = jnp.where(row == n - 1, 0.0, pltpu.roll(z, n - 1, 0))
    return zm * w[0:1] + z * w[1:2] + zp * w[2:3]


def _log_gamma(dec):
    return jnp.log1p(-jnp.exp(dec))


def _even_kernel(*refs, seq, latent, tq, heads):
    (q_ref, k_ref, v_ref, g_ref, gb_ref, gc_ref, hx_ref, scw_ref, df_ref, db_ref) = refs[:10]
    if latent:
        s0_ref, ret_ref, sc_ref, mask_ref = refs[10:]
    else:
        ret_ref, sc_ref, st_ref, mask_ref = refs[10:]

    for hh in range(heads):
        lanes = slice(hh * RET_DK, (hh + 1) * RET_DK)
        lgf = _log_gamma(df_ref[hh])
        lgb = _log_gamma(db_ref[hh])

        @pl.when(pl.program_id(1) == 0)
        def _():
            for r in range(seq // tq):
                t = lax.broadcasted_iota(I32, (tq, seq), 0) + r * tq
                s = lax.broadcasted_iota(I32, (tq, seq), 1)
                d = (t - s).astype(F32)
                mf = jnp.where(d >= 0, jnp.exp(jnp.maximum(d, 0.0) * lgf), 0.0)
                mb = jnp.where(d <= 0, jnp.exp(jnp.maximum(-d, 0.0) * lgb), 0.0)
                mask_ref[hh, r * tq:(r + 1) * tq, :] = mf + mb

        k = k_ref[:, lanes] * (RET_DK ** -0.5)
        kb = k.astype(BF16)
        vb = v_ref[:, lanes].astype(BF16)
        if latent:
            s0f = s0_ref[0, 0, 0, hh].astype(BF16)
            s0b = s0_ref[0, 0, 1, hh].astype(BF16)
        for r in range(seq // tq):
            rows = slice(r * tq, (r + 1) * tq)
            qb = q_ref[rows, lanes].astype(BF16)
            s = lax.dot_general(qb, kb, (((1,), (1,)), ((), ())), preferred_element_type=F32)
            p = (s * mask_ref[hh, rows, :]).astype(BF16)
            o = jnp.dot(p, vb, preferred_element_type=F32)
            if latent:
                tpos = (lax.broadcasted_iota(I32, (tq, 1), 0) + r * tq).astype(F32)
                o = o + jnp.dot(qb, s0f, preferred_element_type=F32) * jnp.exp((tpos + 1.0) * lgf)
                o = o + jnp.dot(qb, s0b, preferred_element_type=F32) * jnp.exp((seq - tpos) * lgb)
            mu = jnp.mean(o, axis=-1, keepdims=True)
            var = jnp.mean(jnp.square(o - mu), axis=-1, keepdims=True)
            on = (o - mu) * lax.rsqrt(var + GN_EPS)
            ret_ref[rows, lanes] = _silu(g_ref[rows, lanes]) * on

        if not latent:
            spos = lax.broadcasted_iota(I32, (seq, 1), 0).astype(F32)
            kf = (k * jnp.exp((seq - 1.0 - spos) * lgf)).T.astype(BF16)
            kr = (k * jnp.exp(spos * lgb)).T.astype(BF16)
            st_ref[0, 0, 0, hh] = jnp.dot(kf, vb, preferred_element_type=F32)
            st_ref[0, 0, 1, hh] = jnp.dot(kr, vb, preferred_element_type=F32)

    sc_ref[...] = gb_ref[...] * _conv3_rows(gc_ref[...] * hx_ref[...], scw_ref[...])


def _even_mixer(proj, sc_w_pad, dec_f, dec_b, state=None):
    latent = state is not None
    seq = DEC_SEQ if latent else SEQ
    nb = DEC_BATCH if latent else BATCH
    row0 = 0 if latent else N_SAMPLE // seq
    tq = min(seq, 256)
    heads = 1 if latent else RET_HEADS
    steps = RET_HEADS // heads
    hw = heads * RET_DK

    def col(c0):
        return pl.BlockSpec((seq, hw), lambda h, b: (row0 + b, c0 * steps + h))

    in_specs = [col(0), col(1), col(2), col(3), col(4), col(5), col(6),
                pl.BlockSpec((SUBLANES, hw), lambda h, b: (0, h)),
                pl.BlockSpec((heads, 1, 1), lambda h, b: (h, 0, 0)),
                pl.BlockSpec((heads, 1, 1), lambda h, b: (h, 0, 0))]
    args = [proj] * 7 + [sc_w_pad, dec_f.reshape(RET_HEADS, 1, 1), dec_b.reshape(RET_HEADS, 1, 1)]
    out_spec = pl.BlockSpec((seq, hw), lambda h, b: (b, h))
    out_shape = [jax.ShapeDtypeStruct((nb * seq, RET_W), F32), jax.ShapeDtypeStruct((nb * seq, SC_W), F32)]
    out_specs = [out_spec, out_spec]
    st_spec = pl.BlockSpec((1, 1, 2, heads, RET_DK, RET_DV), lambda h, b: (b, 0, 0, h, 0, 0))
    if latent:
        in_specs.append(st_spec)
        args.append(state)
    else:
        out_shape.append(jax.ShapeDtypeStruct((BATCH, 1, 2, RET_HEADS, RET_DK, RET_DV), F32))
        out_specs.append(st_spec)
    return pl.pallas_call(
        functools.partial(_even_kernel, seq=seq, latent=latent, tq=tq, heads=heads),
        grid=(steps, nb),
        in_specs=in_specs,
        out_specs=out_specs,
        out_shape=out_shape,
        scratch_shapes=[pltpu.VMEM((heads, seq, seq), F32)],
        compiler_params=_cparams(("arbitrary", "arbitrary")),
        name="even_mixer_latent" if latent else "even_mixer_context",
    )(*args)


def _dft_tables(length):
    f = np.arange(length, dtype=np.float64)[:, None]
    t = np.arange(length, dtype=np.float64)[None, :]
    ang = np.pi * ((f * t) % (2 * length)) / length
    cos, sin = np.cos(ang), np.sin(ang)
    sin[0, :] = (-1.0) ** np.arange(length)
    fwd = np.concatenate([cos, sin], axis=0)
    wgt = np.full((2 * length,), 2.0)
    wgt[0] = 1.0
    wgt[length] = 1.0
    inv = (fwd * wgt[:, None] / (2.0 * length)).T
    return fwd.astype(np.float32), np.ascontiguousarray(inv).astype(np.float32)


def _hyena_consts(length):
    t = np.linspace(0.0, 1.0, length)[:, None]
    bands = (HY_EMB - 1) // 2
    w = 2.0 * np.pi * np.arange(length)[:, None] / length
    f = np.linspace(1e-4, bands - 1, bands)[None, :]
    z = np.concatenate([t, np.cos(f * w), -np.sin(f * w)], axis=-1)
    zpad = np.zeros((length, LANES))
    zpad[:, :HY_EMB] = z
    deltas = np.abs(np.linspace(math.log(HY_TARGET) / HY_SLOW_PCT, math.log(HY_TARGET) / HY_FAST_PCT, HY_W))
    decay = np.exp(-t * deltas[None, :])
    return zpad.astype(np.float32), decay.astype(np.float32)


def _split_kernel(x_ref, hi_ref, lo_ref):
    hi, lo = _split_bf16(x_ref[...])
    hi_ref[...] = hi
    lo_ref[...] = lo


def _split_table(x, name):
    rows = min(x.shape[0], 1024)
    spec = pl.BlockSpec((rows, x.shape[1]), lambda i: (i, 0))
    out = jax.ShapeDtypeStruct(x.shape, BF16)
    return pl.pallas_call(
        _split_kernel,
        grid=(x.shape[0] // rows,),
        in_specs=[spec],
        out_specs=[spec, spec],
        out_shape=[out, out],
        compiler_params=_cparams(("parallel",)),
        name=name,
    )(x)


def _dot_3pass_split(ah, al, b):
    bh, bl = _split_bf16(b)
    dot = functools.partial(jnp.dot, preferred_element_type=F32)
    return dot(ah, bh) + (dot(ah, bl) + dot(al, bh))


def _hy_filter_kernel(z_ref, dec_ref, w1_ref, b1_ref, f1_ref, w2_ref, b2_ref, f2_ref, w3f_ref, w3b_ref,
                      fwd_ref, fwd_lo_ref, ka_ref, ka2_ref, kb_ref, *, seq):
    h = jnp.sin(f1_ref[...] * (jnp.dot(z_ref[...], w1_ref[...], preferred_element_type=F32,
                                       precision=HIGHEST) + b1_ref[...]))
    h = jnp.sin(f2_ref[...] * (jnp.dot(h, w2_ref[...], preferred_element_type=F32,
                                       precision=HIGHEST) + b2_ref[...]))
    dec = dec_ref[...]
    hf = jnp.dot(h, w3f_ref[...], preferred_element_type=F32, precision=HIGHEST) * dec
    hb = jnp.dot(h, w3b_ref[...], preferred_element_type=F32, precision=HIGHEST) * dec
    row = lax.broadcasted_iota(I32, hb.shape, 0)
    hb = jnp.where(row == 0, 0.0, hb)
    inv_norm = 1.0 / (jnp.sum(jnp.abs(hf), axis=0, keepdims=True)
                      + jnp.sum(jnp.abs(hb), axis=0, keepdims=True) + EPS)
    p = (hf + hb) * inv_norm
    q = (hf - hb) * inv_norm
    fp = _dot_3pass_split(fwd_ref[...], fwd_lo_ref[...], p)
    fq = _dot_3pass_split(fwd_ref[seq:, :], fwd_lo_ref[seq:, :], q)
    kr = fp[:seq]
    row = lax.broadcasted_iota(I32, kr.shape, 0)
    ka_ref[0] = kr
    ka2_ref[0] = jnp.where(row == 0, fp[seq:seq + 1], kr)
    kb_ref[0] = jnp.where(row == 0, 0.0, -fq)


def _hy_filters(seq, zfeat, decay, fwd_hi, fwd_lo, w1, b1, f1, w2, b2, f2, w3):
    ns = HY_W // HY_SLAB
    per_o = 2 * ns
    small = lambda shape: pl.BlockSpec(shape, lambda o, s: (0, 0))
    out_spec = pl.BlockSpec((1, seq, HY_SLAB), lambda o, s: (o, 0, s))
    out = jax.ShapeDtypeStruct((HY_ORDER, seq, HY_W), F32)
    return pl.pallas_call(
        functools.partial(_hy_filter_kernel, seq=seq),
        grid=(HY_ORDER, ns),
        in_specs=[
            small((seq, LANES)),
            pl.BlockSpec((seq, HY_SLAB), lambda o, s: (0, s)),
            small((LANES, LANES)), small((1, LANES)), small((1, LANES)),
            small((LANES, LANES)), small((1, LANES)), small((1, LANES)),
            pl.BlockSpec((LANES, HY_SLAB), lambda o, s: (0, o * per_o + s)),
            pl.BlockSpec((LANES, HY_SLAB), lambda o, s: (0, o * per_o + ns + s)),
            small((2 * seq, seq)), small((2 * seq, seq)),
        ],
        out_specs=[out_spec, out_spec, out_spec],
        out_shape=[out, out, out],
        compiler_params=_cparams(("parallel", "parallel")),
        name=f"hyena_filters_{seq}",
    )(zfeat, decay, w1, b1, f1, w2, b2, f2, w3, w3, fwd_hi, fwd_lo)


def _hy_apply_kernel(x1_ref, x2_ref, v_ref, c1_ref, c2_ref, cv_ref, fwd_ref, inv_ref,
                     ka_ref, ka2_ref, kb_ref, bias_ref, o_ref, *, seq, nseq):
    def long_conv(sig, o):
        spec = jnp.dot(fwd_ref[...], sig.astype(BF16), preferred_element_type=F32)
        xr, xs = spec[:seq], spec[seq:]
        kb = kb_ref[o]
        yr = (xr * ka_ref[o] + xs * kb).astype(BF16)
        ys = (xs * ka2_ref[o] - xr * kb).astype(BF16)
        y = jnp.dot(inv_ref[:, :seq], yr, preferred_element_type=F32)
        y = y + jnp.dot(inv_ref[:, seq:], ys, preferred_element_type=F32)
        return y + sig * bias_ref[o:o + 1]

    for i in range(nseq):
        rows = slice(i * seq, (i + 1) * seq)
        x1 = _conv3_rows(x1_ref[rows, :], c1_ref[...])
        x2 = _conv3_rows(x2_ref[rows, :], c2_ref[...])
        v = _conv3_rows(v_ref[rows, :], cv_ref[...])
        z = x1 * long_conv(v, 0)
        o_ref[rows, :] = x2 * long_conv(z, 1)


HY_NSEQ = {True: 2, False: 4}


def _hy_apply(proj, conv_w_pad, fwd_bf, inv_bf, filters, bias_pad, latent):
    seq = DEC_SEQ if latent else SEQ
    nb = DEC_BATCH if latent else BATCH
    nseq = HY_NSEQ[latent]
    rows = nseq * seq
    row0 = 0 if latent else N_SAMPLE // rows
    ns = HY_W // HY_SLAB
    once = pl.Buffered(1)
    const = lambda shape: pl.BlockSpec(shape, lambda s, b: (0,) * len(shape), pipeline_mode=once)
    col = lambda c0: pl.BlockSpec((rows, HY_SLAB), lambda s, b: (row0 + b, c0 + s))
    cw = lambda c0: pl.BlockSpec((SUBLANES, HY_SLAB), lambda s, b: (0, c0 + s))
    filt = pl.BlockSpec((HY_ORDER, seq, HY_SLAB), lambda s, b: (0, 0, s), pipeline_mode=once)
    in_specs = [col(0), col(ns), col(2 * ns), cw(0), cw(ns), cw(2 * ns),
                const((2 * seq, seq)), const((seq, 2 * seq)), filt, filt, filt,
                pl.BlockSpec((SUBLANES, HY_SLAB), lambda s, b: (0, s))]
    args = [proj] * 3 + [conv_w_pad] * 3 + [fwd_bf, inv_bf, *filters, bias_pad]
    return pl.pallas_call(
        functools.partial(_hy_apply_kernel, seq=seq, nseq=nseq),
        grid=(ns, nb // nseq),
        in_specs=in_specs,
        out_specs=pl.BlockSpec((rows, HY_SLAB), lambda s, b: (b, s)),
        out_shape=jax.ShapeDtypeStruct((nb * seq, HY_W), F32),
        compiler_params=_cparams(("arbitrary", "arbitrary")),
        name="hyena_latent" if latent else "hyena_context",
    )(*args)


def _rope_tables(length):
    rows = length // GRID_W
    row = np.repeat(np.arange(rows, dtype=np.float64), GRID_W)
    colp = np.tile(np.arange(GRID_W, dtype=np.float64), rows)
    half = DA_DH // 2
    freqs = ROPE_THETA ** (-np.arange(0, half, 2, dtype=np.float64) / half)
    ang = np.concatenate([row[:, None] * freqs[None, :]] * 2 + [colp[:, None] * freqs[None, :]] * 2, axis=1)
    ang = np.concatenate([ang, ang], axis=1)
    lane = np.arange(LANES)
    sign = np.where((lane % half) < half // 2, -1.0, 1.0)[None, :]
    return np.cos(ang).astype(np.float32), (np.sin(ang) * sign).astype(np.float32)


def _rope(x, cos, sin_signed):
    q = DA_DH // 4
    lane = lax.broadcasted_iota(I32, x.shape, 1)
    partner = jnp.where((lane % (2 * q)) < q, pltpu.roll(x, LANES - q, 1), pltpu.roll(x, q, 1))
    return x * cos + partner * sin_signed


def _attn_kernel(*refs, seq, latent, lambda_init, heads):
    q_ref, k_ref, v_ref, lam_ref, sub_ref = refs[:5]
    if latent:
        cos_ref, sin_ref, ck_ref, cv_ref, o_ref = refs[5:]
    else:
        o_ref, nk_ref, nv_ref = refs[5:]
    hw = 2 * DA_DH
    lv = lam_ref[...]
    lam = (jnp.exp(jnp.sum(lv[0:1] * lv[1:2], axis=1, keepdims=True))
           - jnp.exp(jnp.sum(lv[2:3] * lv[3:4], axis=1, keepdims=True)) + lambda_init)
    scale = DA_DH ** -0.5
    dn = (((1,), (1,)), ((), ()))
    tq = min(seq, ATT_TQ)
    for hh in range(heads):
        lanes = slice(hh * hw, (hh + 1) * hw)
        k = k_ref[:, lanes]
        v = v_ref[:, lanes]
        if latent:
            k = _rope(k, cos_ref[...], sin_ref[...])
            k_all = jnp.concatenate([k, jnp.concatenate([ck_ref[0, 0, 0, hh], ck_ref[0, 0, 1, hh]], axis=1)], axis=0)
            v_all = jnp.concatenate([v, cv_ref[0, 0, hh]], axis=0)
        else:
            k_all, v_all = k, v
            nk_ref[0, 0, 0, hh] = k[:, :DA_DH]
            nk_ref[0, 0, 1, hh] = k[:, DA_DH:]
            nv_ref[0, 0, hh] = v
        kb = k_all.astype(BF16)
        vb = v_all.astype(BF16)
        for r in range(seq // tq):
            rows = slice(r * tq, (r + 1) * tq)
            q = q_ref[rows, lanes]
            if latent:
                q = _rope(q, cos_ref[rows, :], sin_ref[rows, :])
            q = q * scale
            first = lax.broadcasted_iota(I32, q.shape, 1) < DA_DH
            q1 = jnp.where(first, q, 0.0).astype(BF16)
            q2 = jnp.where(first, 0.0, q).astype(BF16)
            s1 = lax.dot_general(q1, kb, dn, preferred_element_type=F32)
            s2 = lax.dot_general(q2, kb, dn, preferred_element_type=F32)
            e1 = jnp.exp(s1 - jnp.max(s1, axis=-1, keepdims=True))
            e2 = jnp.exp(s2 - jnp.max(s2, axis=-1, keepdims=True))
            o1 = jnp.dot(e1.astype(BF16), vb, preferred_element_type=F32)
            o2 = jnp.dot(e2.astype(BF16), vb, preferred_element_type=F32)
            o = (o1 * (1.0 / jnp.sum(e1, axis=-1, keepdims=True))
                 - o2 * (lam / jnp.sum(e2, axis=-1, keepdims=True)))
            o = o * lax.rsqrt(jnp.mean(o * o, axis=-1, keepdims=True) + EPS) * sub_ref[...]
            o_ref[rows, lanes] = o * (1.0 - lambda_init)


def _attention(proj, lam_vec, subln, lambda_init, latent, rope=None, ctx_k=None, ctx_v=None):
    seq = DEC_SEQ if latent else SEQ
    nb = DEC_BATCH if latent else BATCH
    row0 = 0 if latent else N_SAMPLE // seq
    hw = 2 * DA_DH
    heads = 1 if latent else DA_HEADS
    steps = DA_HEADS // heads
    col = lambda c: pl.BlockSpec((seq, heads * hw), lambda b, h: (row0 + b, c * steps + h))
    in_specs = [col(3), col(4), col(5),
                pl.BlockSpec((4, DA_DH), lambda b, h: (0, 0)),
                pl.BlockSpec((1, hw), lambda b, h: (0, 0))]
    args = [proj] * 3 + [lam_vec, subln.reshape(1, hw)]
    out_spec = pl.BlockSpec((seq, heads * hw), lambda b, h: (b, h))
    out_shape = [jax.ShapeDtypeStruct((nb * seq, DA_W), F32)]
    out_specs = [out_spec]
    k_spec = pl.BlockSpec((1, 1, 2, heads, SEQ, DA_DH), lambda b, h: (b, 0, 0, h, 0, 0))
    v_spec = pl.BlockSpec((1, 1, heads, SEQ, hw), lambda b, h: (b, 0, h, 0, 0))
    if latent:
        in_specs += [pl.BlockSpec((seq, hw), lambda b, h: (0, 0))] * 2 + [k_spec, v_spec]
        args += [rope[0], rope[1], ctx_k, ctx_v]
    else:
        out_shape += [jax.ShapeDtypeStruct((BATCH, 1, 2, DA_HEADS, SEQ, DA_DH), F32),
                      jax.ShapeDtypeStruct((BATCH, 1, DA_HEADS, SEQ, hw), F32)]
        out_specs += [k_spec, v_spec]
    return pl.pallas_call(
        functools.partial(_attn_kernel, seq=seq, latent=latent, lambda_init=lambda_init, heads=heads),
        grid=(nb, steps),
        in_specs=in_specs,
        out_specs=out_specs,
        out_shape=out_shape,
        compiler_params=_cparams(("arbitrary", "arbitrary")),
        name="diff_attn_latent" if latent else "diff_attn_context",
    )(*args)


HALF_D = D_MODEL // 2
HI16 = -65536


def _pack_bf16_pair(lo, hi):
    lo_bits = lax.bitcast_convert_type(lo.astype(BF16).astype(F32), I32)
    hi_bits = lax.bitcast_convert_type(hi.astype(BF16).astype(F32), I32)
    return (hi_bits & HI16) | lax.shift_right_logical(lo_bits, 16)


def _pack_exact_bf16_pair(lo, hi):
    return ((lax.bitcast_convert_type(hi, I32) & HI16)
            | lax.shift_right_logical(lax.bitcast_convert_type(lo, I32), 16))


def _unpack_bf16_pair(p):
    lo = lax.bitcast_convert_type(lax.shift_left(p, 16), F32).astype(BF16)
    hi = lax.bitcast_convert_type(p & HI16, F32).astype(BF16)
    return lo, hi


GRP_EXPERT, GRP_REL, GRP_COUNT = range(3)


def _route_rows(x, g, mod, wr, br, tril_ref, triu_ref, carry_ref):
    u = _norm_mod(x, g, mod, 3, 4)
    u_packed = _pack_bf16_pair(u[:, :HALF_D], u[:, HALF_D:])
    logits = _dot_3pass(u, wr) + br
    lane = lax.broadcasted_iota(I32, logits.shape, 1)
    lane_f = lane.astype(F32)
    neg = -jnp.inf
    big = float(LANES)

    def first_max(vals):
        m = jnp.max(vals, axis=-1, keepdims=True)
        return m, jnp.min(jnp.where(vals == m, lane_f, big), axis=-1, keepdims=True)

    is_grp = lane < N_GROUPS
    gmax, gidx = first_max(jnp.where(is_grp, logits, neg))
    gsum = jnp.sum(jnp.where(is_grp, jnp.exp(logits - gmax), 0.0), axis=-1, keepdims=True)
    g_w = 1.0 / gsum
    lo = N_GROUPS + EXP_PER_GROUP * gidx
    in_grp = jnp.logical_and(lane_f >= lo, lane_f < lo + EXP_PER_GROUP)
    el = jnp.where(in_grp, logits, neg)
    v1, i1 = first_max(el)
    v2, i2 = first_max(jnp.where(lane_f == i1, neg, el))
    e = jnp.exp(v2 - v1)
    p1 = 1.0 / (1.0 + e)
    hit1 = lane_f == i1 - N_GROUPS
    hit2 = lane_f == i2 - N_GROUPS
    onehot = jnp.where(jnp.logical_or(hit1, hit2), 1.0, 0.0)
    cum = jnp.dot(tril_ref[...], onehot.astype(BF16), preferred_element_type=F32)
    n = cum[TM - 1:TM, :]
    n_pad = jnp.floor((n + (ROW_ALIGN - 1.0)) * (1.0 / ROW_ALIGN)) * ROW_ALIGN
    loc_off = jnp.dot(jnp.broadcast_to(n_pad, (SUBLANES, LANES)).astype(BF16), triu_ref[...],
                      preferred_element_type=F32)[0:1]
    base = cum - onehot + loc_off
    loc1 = jnp.sum(jnp.where(hit1, base, 0.0), axis=-1, keepdims=True)
    loc2 = jnp.sum(jnp.where(hit2, base, 0.0), axis=-1, keepdims=True)
    info = jnp.where(lane == 0, g_w * p1, jnp.where(lane == 1, g_w * (e * p1), jnp.where(lane == 2, loc1, loc2)))
    start8 = lax.broadcasted_iota(I32, (LANES, LANES), 0).astype(F32) * ROW_ALIGN
    owner = jnp.where(jnp.logical_and(start8 >= loc_off, start8 < loc_off + n_pad), 1.0, 0.0)
    lane128 = lax.broadcasted_iota(I32, (LANES, LANES), 1)
    c_exp = jnp.sum(owner * lane128.astype(F32), axis=-1, keepdims=True)
    c_rel = jnp.sum(owner * (carry_ref[...] + start8 - loc_off), axis=-1, keepdims=True)
    chunks = jnp.where(lane128 == 0, c_exp, jnp.where(lane128 == 1, c_rel, 0.0)).T
    n_chunks = jnp.sum(n_pad, axis=-1, keepdims=True) * (1.0 / ROW_ALIGN)
    row = lax.broadcasted_iota(I32, (SUBLANES, LANES), 0)
    grp = jnp.where(row == GRP_EXPERT, chunks[0:1], jnp.where(row == GRP_REL, chunks[1:2], n_chunks))
    carry_ref[...] = carry_ref[...] + n_pad
    return u_packed, info, info.T[:SUBLANES], grp.astype(I32)


def _proj_res_router_kernel(*refs, n_x, gate_idx):
    (al_ref, ac_ref, bl_ref, bc_ref, wa_ref, wb_ref, mod_ref, g2_ref, wr_ref, br_ref, tril_ref, triu_ref,
     o_ref, u_ref, loct_ref, info_ref, grp_ref, tot_ref, carry_ref) = refs[n_x:]

    @pl.when(pl.program_id(0) == 0)
    def _():
        carry_ref[...] = jnp.zeros_like(carry_ref)

    a = _group_pick(al_ref, ac_ref)
    b = _group_pick(bl_ref, bc_ref)
    y = jnp.dot(a.astype(BF16), wa_ref[...], preferred_element_type=F32)
    y = y + jnp.dot(b.astype(BF16), wb_ref[...], preferred_element_type=F32)
    mod = mod_ref[0]
    x_new = _rows_value(refs[:n_x]) + mod[gate_idx:gate_idx + 1] * y
    o_ref[...] = x_new
    for t in range(TM_WIDE // TM):
        rows = slice(t * TM, (t + 1) * TM)
        u_packed, info, info_t, grp = _route_rows(x_new[rows], g2_ref[...], mod, wr_ref[...], br_ref[...],
                                                 tril_ref, triu_ref, carry_ref)
        u_ref[rows, :] = u_packed
        info_ref[rows, :] = info
        loct_ref[t] = info_t
        grp_ref[t] = grp
    tot_ref[...] = jnp.broadcast_to(carry_ref[...], tot_ref.shape).astype(I32)


def _proj_residual_router(x, a, b, w_bf16, mods, gate_idx, norm2, w_route, b_route, tril, triu):
    half = a[0].shape[1]
    tm = TM_WIDE
    sub = tm // TM
    nt = N_TOK // TM
    x_specs, x_args = _rows_specs(x, tm)
    const = lambda shape: pl.BlockSpec(shape, lambda i: (0,) * len(shape))
    return pl.pallas_call(
        functools.partial(_proj_res_router_kernel, n_x=len(x_args), gate_idx=gate_idx),
        grid=(N_TOK // tm,),
        in_specs=x_specs + _group_specs(half, tm) + _group_specs(half, tm) + [
            pl.BlockSpec((half, D_MODEL), lambda i: (0, 0)),
            pl.BlockSpec((half, D_MODEL), lambda i: (1, 0)),
            pl.BlockSpec((1, MOD_ROWS, D_MODEL), lambda i: (i * tm // SEG, 0, 0)),
            const((1, D_MODEL)), const((D_MODEL, LANES)), const((1, LANES)), const((TM, TM)), const((LANES, LANES)),
        ],
        out_specs=[
            pl.BlockSpec((tm, D_MODEL), lambda i: (i, 0)),
            pl.BlockSpec((tm, HALF_D), lambda i: (i, 0)),
            pl.BlockSpec((sub, SUBLANES, TM), lambda i: (i, 0, 0)),
            pl.BlockSpec((tm, LANES), lambda i: (i, 0)),
            pl.BlockSpec((sub, SUBLANES, LANES), lambda i: (i, 0, 0)),
            pl.BlockSpec((SUBLANES, LANES), lambda i: (0, 0)),
        ],
        out_shape=[
            jax.ShapeDtypeStruct((N_TOK, D_MODEL), F32),
            jax.ShapeDtypeStruct((N_TOK, HALF_D), I32),
            jax.ShapeDtypeStruct((nt, SUBLANES, TM), F32),
            jax.ShapeDtypeStruct((N_TOK, LANES), F32),
            jax.ShapeDtypeStruct((nt, SUBLANES, LANES), I32),
            jax.ShapeDtypeStruct((SUBLANES, LANES), I32),
        ],
        scratch_shapes=[pltpu.VMEM((1, LANES), F32)],
        compiler_params=_cparams(("arbitrary",)),
        name="outproj_residual_router",
    )(*x_args, a[0], a[1], b[0], b[1], w_bf16, w_bf16, mods, norm2.reshape(1, D_MODEL), w_route, b_route, tril, triu)


def _for_chunks(n, fn):
    for size in CHUNKS:
        @pl.when((n & size) != 0)
        def _():
            fn(n & ~(2 * size - 1), size)


N_COPIES = R_LOC // ROW_ALIGN
DUMP_ROWS = 2 * R_LOC


def _start_tile_copies(grp_ref, seg_ref, make, dummy, n=None):
    if n is None:
        n = grp_ref[0, GRP_COUNT, 0]
    for c in range(N_COPIES):
        real = c < n
        slot = seg_ref[0, grp_ref[0, GRP_EXPERT, c]] + grp_ref[0, GRP_REL, c]
        d_loc, d_slot = dummy(c)
        loc = jnp.where(real, c * ROW_ALIGN, d_loc)
        slot = jnp.where(real, slot, d_slot)
        make(pl.ds(pl.multiple_of(loc, ROW_ALIGN), ROW_ALIGN), pl.ds(pl.multiple_of(slot, ROW_ALIGN), ROW_ALIGN)).start()


def _wait_tile_copies(make):
    make(pl.ds(0, R_LOC), pl.ds(0, R_LOC)).wait()


def _all_experts(body):
    lax.fori_loop(0, N_EXPERTS, lambda e, c: (body(e), c)[1], 0)


def _dispatch_kernel(grp_ref, seg_ref, loct_ref, u_ref, xb_ref, buf, zbuf, sem):
    i = pl.program_id(0)
    last = pl.num_programs(0) - 1
    cur = i % 2

    def maker(b):
        return lambda loc, slot: pltpu.make_async_copy(buf.at[b, loc], xb_ref.at[slot], sem.at[b])

    @pl.when(i >= 2)
    def _():
        _wait_tile_copies(maker(cur))

    lo, hi = _unpack_bf16_pair(u_ref[...])
    row = lax.broadcasted_iota(I32, (R_LOC, TM), 0).astype(F32)
    sel = jnp.logical_or(row == loct_ref[0, 2:3, :], row == loct_ref[0, 3:4, :])
    pt = jnp.where(sel, 1.0, 0.0).astype(BF16)
    buf[cur] = _pack_exact_bf16_pair(jnp.dot(pt, lo, preferred_element_type=F32),
                                     jnp.dot(pt, hi, preferred_element_type=F32))
    _start_tile_copies(grp_ref, seg_ref, maker(cur),
                       lambda c: (c * ROW_ALIGN, MOE_ROWS + cur * R_LOC + c * ROW_ALIGN))

    @pl.when(i == last)
    def _():
        @pl.when(i >= 1)
        def _():
            _wait_tile_copies(maker(1 - cur))

        _wait_tile_copies(maker(cur))

        zbuf[...] = jnp.zeros_like(zbuf)

        def tail(e, act):
            total = seg_ref[1, e]
            dst = seg_ref[0, e] + total

            def one(off, size):
                act(pltpu.make_async_copy(zbuf.at[pl.ds(0, size)],
                                          xb_ref.at[pl.ds(pl.multiple_of(dst + off, ROW_ALIGN), size)], sem.at[2]))

            _for_chunks((-total) & (MOE_BLK - 1), one)

        _all_experts(lambda e: tail(e, lambda cp: cp.start()))
        _all_experts(lambda e: tail(e, lambda cp: cp.wait()))

        e_last = N_EXPERTS - 1
        used = (seg_ref[0, e_last] + seg_ref[1, e_last] + MOE_BLK - 1) // MOE_BLK

        def spare(j):
            return pltpu.make_async_copy(zbuf, xb_ref.at[pl.ds(pl.multiple_of(j * MOE_BLK, MOE_BLK), MOE_BLK)],
                                         sem.at[2])

        lax.fori_loop(used, (MOE_ROWS + DUMP_ROWS) // MOE_BLK, lambda j, c: (spare(j).start(), c)[1], 0)
        lax.fori_loop(used, (MOE_ROWS + DUMP_ROWS) // MOE_BLK, lambda j, c: (spare(j).wait(), c)[1], 0)


def _dispatch(grp, seg, loct, u):
    return pl.pallas_call(
        _dispatch_kernel,
        grid=(N_TOK // TM,),
        in_specs=[
            pl.BlockSpec((1, SUBLANES, LANES), lambda i: (i, 0, 0), memory_space=pltpu.SMEM),
            pl.BlockSpec(memory_space=pltpu.SMEM),
            pl.BlockSpec((1, SUBLANES, TM), lambda i: (i, 0, 0)),
            pl.BlockSpec((TM, HALF_D), lambda i: (i, 0)),
        ],
        out_specs=pl.BlockSpec(memory_space=pl.ANY),
        out_shape=jax.ShapeDtypeStruct((MOE_ROWS + DUMP_ROWS, HALF_D), I32),
        scratch_shapes=[pltpu.VMEM((2, R_LOC, HALF_D), I32), pltpu.VMEM((MOE_BLK, HALF_D), I32),
                        pltpu.SemaphoreType.DMA((3,))],
        compiler_params=_cparams(("arbitrary",)),
        name="moe_dispatch",
    )(grp, seg, loct, u)


def _expert_kernel(seg_ref, w1_ref, w3_ref, w2_ref, xb_ref, yb_ref, w1b, w3b, w2b, xbuf, ybuf, xsem, ysem):
    e = pl.program_id(0)
    last_e = pl.num_programs(0) - 1
    blk0 = seg_ref[0, e] // MOE_BLK
    nblk = (seg_ref[1, e] + MOE_BLK - 1) // MOE_BLK
    used = (seg_ref[0, N_EXPERTS - 1] + seg_ref[1, N_EXPERTS - 1] + MOE_BLK - 1) // MOE_BLK

    def rows(g):
        return pl.ds(pl.multiple_of(g * MOE_BLK, MOE_BLK), MOE_BLK)

    def fetch(g, b):
        return pltpu.make_async_copy(xb_ref.at[rows(g)], xbuf.at[b], xsem.at[b])

    def flush(g, b):
        return pltpu.make_async_copy(ybuf.at[b], yb_ref.at[rows(g)], ysem.at[b])

    @pl.when(e == 0)
    def _():
        fetch(0, 0).start()

    @pl.when(nblk > 0)
    def _():
        w1b[...] = w1_ref[0, 0].astype(BF16)
        w3b[...] = w3_ref[0, 0].astype(BF16)
        w2b[...] = w2_ref[0, 0].astype(BF16)

        def block(i, carry):
            g = blk0 + i
            b = g % 2
            fetch(g, b).wait()

            @pl.when(g + 1 < used)
            def _():
                fetch(g + 1, 1 - b).start()

            @pl.when(g >= 2)
            def _():
                flush(g - 2, b).wait()

            def ffn(x_packed):
                lo, hi = _unpack_bf16_pair(x_packed)

                def up(wb):
                    return (jnp.dot(lo, wb[:HALF_D, :], preferred_element_type=F32)
                            + jnp.dot(hi, wb[HALF_D:, :], preferred_element_type=F32))

                hdn = (_silu(up(w1b)) * up(w3b)).astype(BF16)
                y = jnp.dot(hdn, w2b[...], preferred_element_type=F32)
                return _pack_bf16_pair(y[:, :HALF_D], y[:, HALF_D:])

            half = MOE_BLK // 2
            slots_here = seg_ref[1, e] - i * MOE_BLK

            @pl.when(slots_here > half)
            def _():
                ybuf[b] = ffn(xbuf[b])

            @pl.when(slots_here <= half)
            def _():
                ybuf[b, :half, :] = ffn(xbuf[b, :half, :])
                ybuf[b, half:, :] = jnp.zeros((half, HALF_D), I32)

            flush(g, b).start()
            return carry

        lax.fori_loop(0, nblk, block, 0)

    @pl.when(e == last_e)
    def _():
        @pl.when(used >= 2)
        def _():
            flush(used - 2, used % 2).wait()

        flush(used - 1, (used - 1) % 2).wait()
        xbuf[0] = jnp.zeros((MOE_BLK, HALF_D), I32)

        def spare(g):
            return pltpu.make_async_copy(xbuf.at[0], yb_ref.at[rows(g)], xsem.at[0])

        lax.fori_loop(used, MOE_NB, lambda g, c: (spare(g).start(), c)[1], 0)
        lax.fori_loop(used, MOE_NB, lambda g, c: (spare(g).wait(), c)[1], 0)


def _experts(seg, xb, w1, w3, w2, layer):
    hbm = pl.BlockSpec(memory_space=pl.ANY)
    w_in = pl.BlockSpec((1, 1, D_MODEL, D_EXPERT), lambda e, s: (layer, e, 0, 0))
    w_out = pl.BlockSpec((1, 1, D_EXPERT, D_MODEL), lambda e, s: (layer, e, 0, 0))
    return pl.pallas_call(
        _expert_kernel,
        grid_spec=pltpu.PrefetchScalarGridSpec(
            num_scalar_prefetch=1,
            grid=(N_EXPERTS,),
            in_specs=[w_in, w_in, w_out, hbm],
            out_specs=hbm,
            scratch_shapes=[pltpu.VMEM((D_MODEL, D_EXPERT), BF16), pltpu.VMEM((D_MODEL, D_EXPERT), BF16),
                            pltpu.VMEM((D_EXPERT, D_MODEL), BF16),
                            pltpu.VMEM((2, MOE_BLK, HALF_D), I32), pltpu.VMEM((2, MOE_BLK, HALF_D), I32),
                            pltpu.SemaphoreType.DMA((2,)), pltpu.SemaphoreType.DMA((2,))],
        ),
        out_shape=jax.ShapeDtypeStruct((MOE_ROWS, HALF_D), I32),
        compiler_params=_cparams(("arbitrary",)),
        name="moe_experts",
    )(seg, w1, w3, w2, xb)


def _combine_kernel(*refs, final):
    grp_ref, nxt_ref, seg_ref, x_ref, info_ref, mod_ref, yb_ref = refs[:7]
    if final:
        fin_ref, olat_ref, octx_ref, ybuf, sem = refs[7:]
    else:
        g1n_ref, modn_ref, wn_ref, o_ref, proj_ref, ybuf, sem = refs[7:]
    i = pl.program_id(0)
    cur = i % 2

    def maker(b):
        return lambda loc, slot: pltpu.make_async_copy(yb_ref.at[slot], ybuf.at[b, loc], sem.at[b])

    def dummy(c):
        return R_LOC + c * ROW_ALIGN, c * ROW_ALIGN

    @pl.when(i == 0)
    def _():
        ybuf[...] = jnp.zeros_like(ybuf)
        _start_tile_copies(grp_ref, seg_ref, maker(0), dummy)

    has_next = i + 1 < pl.num_programs(0)
    _start_tile_copies(nxt_ref, seg_ref, maker(1 - cur), dummy, jnp.where(has_next, nxt_ref[0, GRP_COUNT, 0], 0))
    _wait_tile_copies(maker(cur))

    @pl.when(jnp.logical_not(has_next))
    def _():
        _wait_tile_copies(maker(1 - cur))

    info = info_ref[...]
    col = lax.broadcasted_iota(I32, (TM, R_LOC), 1).astype(F32)
    p = jnp.where(col == info[:, 2:3], info[:, 0:1], jnp.where(col == info[:, 3:4], info[:, 1:2], 0.0))
    p = p.astype(BF16)
    lo, hi = _unpack_bf16_pair(ybuf[cur, :R_LOC, :])
    y = jnp.concatenate([jnp.dot(p, lo, preferred_element_type=F32),
                         jnp.dot(p, hi, preferred_element_type=F32)], axis=1)
    out = x_ref[...] + mod_ref[0][5:6] * y
    if not final:
        o_ref[...] = out
        u_next = _norm_mod(out, g1n_ref[...], modn_ref[0], 0, 1)
        proj_ref[...] = jnp.dot(u_next.astype(BF16), wn_ref[...], preferred_element_type=F32)
        return
    out = out * lax.rsqrt(jnp.mean(out * out, axis=-1, keepdims=True) + EPS) * fin_ref[...]

    @pl.when(i < N_LAT_TILES)
    def _():
        olat_ref[...] = out

    @pl.when(i >= N_LAT_TILES)
    def _():
        octx_ref[...] = out


def _combine(x, grp, seg, info, mods, yb, final_g=None, next_proj=None):
    final = final_g is not None
    nt = N_TOK // TM
    in_specs = [
        pl.BlockSpec((1, SUBLANES, LANES), lambda i: (i, 0, 0), memory_space=pltpu.SMEM),
        pl.BlockSpec((1, SUBLANES, LANES), lambda i: (jnp.minimum(i + 1, nt - 1), 0, 0), memory_space=pltpu.SMEM),
        pl.BlockSpec(memory_space=pltpu.SMEM),
        pl.BlockSpec((TM, D_MODEL), lambda i: (i, 0)),
        pl.BlockSpec((TM, LANES), lambda i: (i, 0)),
        pl.BlockSpec((1, MOD_ROWS, D_MODEL), lambda i: (i * TM // SEG, 0, 0)),
        pl.BlockSpec(memory_space=pl.ANY),
    ]
    args = [grp, grp, seg, x, info, mods, yb]
    if final:
        in_specs.append(pl.BlockSpec((1, D_MODEL), lambda i: (0, 0)))
        args.append(final_g.reshape(1, D_MODEL))
        out_specs = _group_specs(D_MODEL)
        out_shape = [jax.ShapeDtypeStruct((N_SAMPLE, D_MODEL), F32), jax.ShapeDtypeStruct((N_PROMPT, D_MODEL), F32)]
    else:
        g1n, modn, wn = next_proj
        n = wn.shape[1]
        in_specs += [pl.BlockSpec((1, D_MODEL), lambda i: (0, 0)),
                     pl.BlockSpec((1, MOD_ROWS, D_MODEL), lambda i: (i * TM // SEG, 0, 0)),
                     pl.BlockSpec((D_MODEL, n), lambda i: (0, 0), pipeline_mode=pl.Buffered(1))]
        args += [g1n.reshape(1, D_MODEL), modn, wn]
        out_specs = [pl.BlockSpec((TM, D_MODEL), lambda i: (i, 0)), pl.BlockSpec((TM, n), lambda i: (i, 0))]
        out_shape = [jax.ShapeDtypeStruct((N_TOK, D_MODEL), F32), jax.ShapeDtypeStruct((N_TOK, n), F32)]
    return pl.pallas_call(
        functools.partial(_combine_kernel, final=final),
        grid=(nt,),
        in_specs=in_specs,
        out_specs=out_specs,
        out_shape=out_shape,
        scratch_shapes=[pltpu.VMEM((2, 2 * R_LOC, HALF_D), I32), pltpu.SemaphoreType.DMA((2,))],
        compiler_params=_cparams(("arbitrary",)),
        name="moe_combine_final" if final else "moe_combine",
    )(*args)


def _route_weights(w_grp, b_grp, w_exp, b_exp):
    w_route = jnp.zeros((D_MODEL, LANES), F32)
    w_route = w_route.at[:, :N_GROUPS].set(w_grp).at[:, N_GROUPS:N_GROUPS + N_EXPERTS].set(w_exp)
    b_route = jnp.zeros((1, LANES), F32)
    b_route = b_route.at[0, :N_GROUPS].set(b_grp).at[0, N_GROUPS:N_GROUPS + N_EXPERTS].set(b_exp)
    return w_route, b_route


def _moe_layer(x, route, layer, mods, w1, w3, w2, final_g=None, next_proj=None):
    u, loct, info, grp, tot = route
    total = tot[0, :N_EXPERTS]
    padded = (total + MOE_BLK - 1) // MOE_BLK * MOE_BLK
    pad_end = jnp.cumsum(padded)
    pad_start = pad_end - padded
    seg = jnp.zeros((2, LANES), I32).at[0, :N_EXPERTS].set(pad_start).at[1, :N_EXPERTS].set(total)
    xb = _dispatch(grp, seg, loct, u)
    yb = _experts(seg, xb, w1, w3, w2, layer)
    return _combine(x, grp, seg, info, mods, yb, final_g, next_proj)


def _pad_rows(w, rows=SUBLANES):
    return jnp.zeros((rows, w.shape[1]), w.dtype).at[:w.shape[0]].set(w)


def _pad2(w, rows, cols):
    return jnp.zeros((rows, cols), w.dtype).at[:w.shape[0], :w.shape[1]].set(w)


def kernel(x_prompt, x_sample, c, c_ctx, state_ret, cache_k, cache_v, ada_w, ada_b, norm1, norm2, final_norm, ev_w_in, ev_w_out, ret_decay_fwd, ret_decay_bwd, sc_conv_w, od_w_in, od_w_out, hy_conv_w, hy_w1, hy_b1, hy_fr1, hy_w2, hy_b2, hy_fr2, hy_w3, hy_bias, da_lambda, da_subln, moe_w_grp, moe_b_grp, moe_w_exp, moe_b_exp, moe_w1, moe_w3, moe_w2):
    x = (x_sample.reshape(N_SAMPLE, D_MODEL), x_prompt.reshape(N_PROMPT, D_MODEL))

    cvecs = jnp.zeros((16, D_MODEL), F32).at[:DEC_BATCH].set(c).at[DEC_BATCH].set(c_ctx)
    m = _mods(cvecs, ada_w, ada_b).reshape(DEPTH, 16, N_MOD, D_MODEL)
    seg_mods = jnp.concatenate(
        [m[:, :DEC_BATCH], jnp.broadcast_to(m[:, DEC_BATCH:DEC_BATCH + 1], (DEPTH, N_SEG - DEC_BATCH, N_MOD, D_MODEL))],
        axis=1)
    seg_mods = jnp.pad(seg_mods, ((0, 0), (0, 0), (0, MOD_ROWS - N_MOD), (0, 0)))

    tril = jnp.asarray(np.tril(np.ones((TM, TM), np.float32)), BF16)
    triu = jnp.asarray(np.triu(np.ones((LANES, LANES), np.float32), 1), BF16)

    def w_in(layer):
        w = ev_w_in if layer % 2 == 0 else od_w_in
        return w[layer // 2].astype(BF16)

    proj = None
    for l in range(DEPTH):
        mods = seg_mods[l]
        i = l // 2
        if l % 2 == 0:
            if proj is None:
                proj = _norm_mod_matmul(x, norm1[l], mods, w_in(l), 0, 1)
            scw = _pad_rows(sc_conv_w[i])
            ret_l, sc_l = _even_mixer(proj, scw, ret_decay_fwd[i], ret_decay_bwd[i], state=state_ret[:, i:i + 1])
            ret_c, sc_c, new_state = _even_mixer(proj, scw, ret_decay_fwd[i], ret_decay_bwd[i])
            mixed, w_out = ((ret_l, ret_c), (sc_l, sc_c)), ev_w_out[i]
        else:
            lambda_init = 0.8 - 0.6 * math.exp(-0.3 * l)
            if proj is None:
                proj = _norm_mod_matmul(x, norm1[l], mods, w_in(l), 0, 1)
            conv_w = _pad_rows(hy_conv_w[i])
            bias = _pad_rows(hy_bias[i])
            w1 = _pad2(hy_w1[i], LANES, LANES)
            b1 = _pad2(hy_b1[i][None], 1, LANES)
            f1 = _pad2(hy_fr1[i][None], 1, LANES)
            w2 = _pad2(hy_w2[i], LANES, LANES)
            b2 = _pad2(hy_b2[i][None], 1, LANES)
            f2 = _pad2(hy_fr2[i][None], 1, LANES)
            w3 = _pad2(hy_w3[i], LANES, hy_w3.shape[2])
            hy = []
            for latent in (True, False):
                seq = DEC_SEQ if latent else SEQ
                fwd_np, inv_np = _dft_tables(seq)
                zfeat, decay = _hyena_consts(seq)
                fwd_bf, fwd_lo = _split_table(jnp.asarray(fwd_np), f"dft_fwd_split_{seq}")
                inv_bf, _ = _split_table(jnp.asarray(inv_np), f"dft_inv_split_{seq}")
                filters = _hy_filters(seq, jnp.asarray(zfeat), jnp.asarray(decay), fwd_bf, fwd_lo,
                                      w1, b1, f1, w2, b2, f2, w3)
                hy.append(_hy_apply(proj, conv_w, fwd_bf, inv_bf, filters, bias, latent))
            cos, sin = _rope_tables(DEC_SEQ)
            att_l = _attention(proj, da_lambda[i], da_subln[i], lambda_init, True,
                               rope=(jnp.asarray(cos), jnp.asarray(sin)),
                               ctx_k=cache_k[:, i:i + 1], ctx_v=cache_v[:, i:i + 1])[0]
            att_c, new_k, new_v = _attention(proj, da_lambda[i], da_subln[i], lambda_init, False)
            mixed, w_out = (tuple(hy), (att_l, att_c)), od_w_out[i]
        w_route, b_route = _route_weights(moe_w_grp[l], moe_b_grp[l], moe_w_exp[l], moe_b_exp[l])
        x, *route = _proj_residual_router(x, mixed[0], mixed[1], w_out.astype(BF16), mods, 2,
                                          norm2[l], w_route, b_route, tril, triu)
        if l == DEPTH - 1:
            x = _moe_layer(x, route, l, mods, moe_w1, moe_w3, moe_w2, final_g=final_norm)
        else:
            x, proj = _moe_layer(x, route, l, mods, moe_w1, moe_w3, moe_w2,
                                 next_proj=(norm1[l + 1], seg_mods[l + 1], w_in(l + 1)))

    y_sample, y_prompt = x
    return (y_prompt.reshape(BATCH, SEQ, D_MODEL), y_sample.reshape(DEC_BATCH, DEC_SEQ, D_MODEL),
            new_state, new_k, new_v)
```

```python
import functools
import math

import numpy as np
import jax
import jax.numpy as jnp
from jax import lax
from jax.experimental import pallas as pl
from jax.experimental.pallas import tpu as pltpu

F32 = jnp.float32
BF16 = jnp.bfloat16
I32 = jnp.int32
HIGHEST = lax.Precision.HIGHEST

D_MODEL = 1024
BATCH = 32
SEQ = 256
DEPTH = 2
DEC_BATCH = 8
DEC_SEQ = 1024
PAST_LEN = 256
GRID_W = 64
N_MOD = 6
EPS = 1e-6
GN_EPS = 1e-5

RET_DK = 128
RET_DV = 128
RET_HEADS = 4
RET_W = 512
RET_QK = 512
SC_W = 512
EVEN_IN = 2 * RET_QK + 2 * RET_W + 3 * SC_W

HY_W = 512
HY_ORDER = 2
HY_EMB = 33
HY_FFN = 64
HY_TARGET = 1e-2
HY_FAST_PCT = 0.3
HY_SLOW_PCT = 1.5
DA_DH = 64
DA_HEADS = 4
DA_W = 512
ODD_IN = 3 * HY_W + 3 * DA_W
ROPE_THETA = 10000.0

N_GROUPS = 4
EXP_PER_GROUP = 8
N_EXPERTS = 32
TOP_K = 2
D_EXPERT = 512

N_SAMPLE = DEC_BATCH * DEC_SEQ
N_PROMPT = BATCH * SEQ
N_TOK = N_SAMPLE + N_PROMPT
SEG = 1024
N_SEG = N_TOK // SEG
MOD_ROWS = 8

LANES = 128
SUBLANES = 8
VMEM_LIMIT = 56 * 1024 * 1024

TM = 256
MOE_BLK = 512
ROW_ALIGN = SUBLANES
R_LOC = (TOP_K * TM + N_EXPERTS * (ROW_ALIGN - 1) + LANES - 1) // LANES * LANES
CHUNKS = (256, 128, 64, 32, 16, 8)
MOE_ROWS = ((N_TOK * TOP_K + N_EXPERTS * (N_TOK // TM) * (ROW_ALIGN - 1) + N_EXPERTS * (MOE_BLK - 1))
            // MOE_BLK + 1) * MOE_BLK
MOE_NB = MOE_ROWS // MOE_BLK
HY_SLAB = 256
ATT_TQ = 256


def _cparams(sem, vmem=VMEM_LIMIT):
    return pltpu.CompilerParams(dimension_semantics=sem, vmem_limit_bytes=vmem)


def _silu(x):
    return x * (1.0 / (1.0 + jnp.exp(-x)))


def _split_bf16(a):
    hi = a.astype(BF16)
    return hi, (a - hi.astype(F32)).astype(BF16)


def _dot_3pass(a, b):
    ah, al = _split_bf16(a)
    bh, bl = _split_bf16(b)
    dot = functools.partial(jnp.dot, preferred_element_type=F32)
    return dot(ah, bh) + (dot(ah, bl) + dot(al, bh))


MODS_TN = 1536


def _mods_kernel(c_ref, w_ref, b_ref, o_ref):
    s = _silu(c_ref[...])
    o_ref[0] = jnp.dot(s, w_ref[0], preferred_element_type=F32, precision=HIGHEST) + b_ref[0]


def _mods(cvecs, ada_w, ada_b):
    n = N_MOD * D_MODEL
    return pl.pallas_call(
        _mods_kernel,
        grid=(DEPTH, n // MODS_TN),
        in_specs=[
            pl.BlockSpec((16, D_MODEL), lambda l, j: (0, 0)),
            pl.BlockSpec((1, D_MODEL, MODS_TN), lambda l, j: (l, 0, j)),
            pl.BlockSpec((1, 1, MODS_TN), lambda l, j: (l, 0, j)),
        ],
        out_specs=pl.BlockSpec((1, 16, MODS_TN), lambda l, j: (l, 0, j)),
        out_shape=jax.ShapeDtypeStruct((DEPTH, 16, n), F32),
        compiler_params=_cparams(("parallel", "parallel")),
        name="adaln_mods",
    )(cvecs, ada_w, ada_b.reshape(DEPTH, 1, n))


def _norm_mod(x, g, mod, shift_idx, scale_idx):
    ms = jnp.mean(x * x, axis=-1, keepdims=True)
    y = x * lax.rsqrt(ms + EPS) * g
    return y * (1.0 + mod[scale_idx:scale_idx + 1]) + mod[shift_idx:shift_idx + 1]


N_LAT_TILES = N_SAMPLE // TM
TM_WIDE = 512


def _group_specs(width, tm=TM):
    n_lat = N_SAMPLE // tm
    lat = pl.BlockSpec((tm, width), lambda i: (jnp.minimum(i, n_lat - 1), 0))
    ctx = pl.BlockSpec((tm, width), lambda i: (jnp.maximum(i - n_lat, 0), 0))
    return [lat, ctx]


def _group_pick(lat_ref, ctx_ref):
    n_lat = N_SAMPLE // lat_ref.shape[0]
    return jnp.where(pl.program_id(0) < n_lat, lat_ref[...], ctx_ref[...])


def _rows_specs(x, tm=TM):
    if isinstance(x, tuple):
        return _group_specs(D_MODEL, tm), list(x)
    return [pl.BlockSpec((tm, D_MODEL), lambda i: (i, 0))], [x]


def _rows_value(refs):
    return _group_pick(*refs) if len(refs) == 2 else refs[0][...]


def _nmm_kernel(*refs, n_x, shift_idx, scale_idx):
    g_ref, mod_ref, w_ref, o_ref = refs[n_x:]
    u = _norm_mod(_rows_value(refs[:n_x]), g_ref[...], mod_ref[0], shift_idx, scale_idx)
    o_ref[...] = jnp.dot(u.astype(BF16), w_ref[...], preferred_element_type=F32)


def _norm_mod_matmul(x, g, mods, w_bf16, shift_idx, scale_idx):
    n = w_bf16.shape[1]
    x_specs, x_args = _rows_specs(x)
    return pl.pallas_call(
        functools.partial(_nmm_kernel, n_x=len(x_args), shift_idx=shift_idx, scale_idx=scale_idx),
        grid=(N_TOK // TM,),
        in_specs=x_specs + [
            pl.BlockSpec((1, D_MODEL), lambda i: (0, 0)),
            pl.BlockSpec((1, MOD_ROWS, D_MODEL), lambda i: (i * TM // SEG, 0, 0)),
            pl.BlockSpec((D_MODEL, n), lambda i: (0, 0)),
        ],
        out_specs=pl.BlockSpec((TM, n), lambda i: (i, 0)),
        out_shape=jax.ShapeDtypeStruct((N_TOK, n), F32),
        compiler_params=_cparams(("parallel",)),
        name="norm_mod_inproj",
    )(*x_args, g.reshape(1, D_MODEL), mods, w_bf16)


def _conv3_rows(z, w):
    n = z.shape[0]
    row = lax.broadcasted_iota(I32, z.shape, 0)
    zm = jnp.where(row == 0, 0.0, pltpu.roll(z, 1, 0))
    zp = jnp.where(row == n - 1, 0.0, pltpu.roll(z, n - 1, 0))
    return zm * w[0:1] + z * w[1:2] + zp * w[2:3]


CTX_NSEQ = 4


def _log_gamma(dec):
    return jnp.log1p(-jnp.exp(dec))


def _even_kernel(*refs, seq, latent, tq, heads, nseq):
    (q_ref, k_ref, v_ref, g_ref, gb_ref, gc_ref, hx_ref, scw_ref, df_ref, db_ref) = refs[:10]
    if latent:
        s0_ref, ret_ref, sc_ref, mask_ref = refs[10:]
    else:
        ret_ref, sc_ref, st_ref, mask_ref = refs[10:]

    decays = []
    for hh in range(heads):
        lgf = _log_gamma(df_ref[hh])
        lgb = _log_gamma(db_ref[hh])
        decays.append((lgf, lgb))

        @pl.when(pl.program_id(1) == 0)
        def _():
            for r in range(seq // tq):
                t = lax.broadcasted_iota(I32, (tq, seq), 0) + r * tq
                s = lax.broadcasted_iota(I32, (tq, seq), 1)
                d = (t - s).astype(F32)
                mf = jnp.where(d >= 0, jnp.exp(jnp.maximum(d, 0.0) * lgf), 0.0)
                mb = jnp.where(d <= 0, jnp.exp(jnp.maximum(-d, 0.0) * lgb), 0.0)
                mask_ref[hh, r * tq:(r + 1) * tq, :] = mf + mb

    for sq in range(nseq):
        whole = slice(sq * seq, (sq + 1) * seq)
        for hh in range(heads):
            lanes = slice(hh * RET_DK, (hh + 1) * RET_DK)
            lgf, lgb = decays[hh]
            k = k_ref[whole, lanes] * (RET_DK ** -0.5)
            kb = k.astype(BF16)
            vb = v_ref[whole, lanes].astype(BF16)
            if latent:
                s0f = s0_ref[sq, 0, 0, hh].astype(BF16)
                s0b = s0_ref[sq, 0, 1, hh].astype(BF16)
            for r in range(seq // tq):
                rows = slice(sq * seq + r * tq, sq * seq + (r + 1) * tq)
                qb = q_ref[rows, lanes].astype(BF16)
                s = lax.dot_general(qb, kb, (((1,), (1,)), ((), ())), preferred_element_type=F32)
                p = (s * mask_ref[hh, r * tq:(r + 1) * tq, :]).astype(BF16)
                o = jnp.dot(p, vb, preferred_element_type=F32)
                if latent:
                    tpos = (lax.broadcasted_iota(I32, (tq, 1), 0) + r * tq).astype(F32)
                    o = o + jnp.dot(qb, s0f, preferred_element_type=F32) * jnp.exp((tpos + 1.0) * lgf)
                    o = o + jnp.dot(qb, s0b, preferred_element_type=F32) * jnp.exp((seq - tpos) * lgb)
                mu = jnp.mean(o, axis=-1, keepdims=True)
                var = jnp.mean(jnp.square(o - mu), axis=-1, keepdims=True)
                on = (o - mu) * lax.rsqrt(var + GN_EPS)
                ret_ref[rows, lanes] = _silu(g_ref[rows, lanes]) * on

            if not latent:
                spos = lax.broadcasted_iota(I32, (seq, 1), 0).astype(F32)
                kf = (k * jnp.exp((seq - 1.0 - spos) * lgf)).T.astype(BF16)
                kr = (k * jnp.exp(spos * lgb)).T.astype(BF16)
                st_ref[sq, 0, 0, hh] = jnp.dot(kf, vb, preferred_element_type=F32)
                st_ref[sq, 0, 1, hh] = jnp.dot(kr, vb, preferred_element_type=F32)

        sc_ref[whole, :] = gb_ref[whole, :] * _conv3_rows(gc_ref[whole, :] * hx_ref[whole, :], scw_ref[...])


def _even_mixer(proj, sc_w_pad, dec_f, dec_b, state=None):
    latent = state is not None
    seq = DEC_SEQ if latent else SEQ
    nb = DEC_BATCH if latent else BATCH
    tq = min(seq, 256)
    heads = 1 if latent else RET_HEADS
    nseq = 1 if latent else CTX_NSEQ
    rows = nseq * seq
    row0 = 0 if latent else N_SAMPLE // rows
    steps = RET_HEADS // heads
    hw = heads * RET_DK

    def col(c0):
        return pl.BlockSpec((rows, hw), lambda h, b: (row0 + b, c0 * steps + h))

    in_specs = [col(0), col(1), col(2), col(3), col(4), col(5), col(6),
                pl.BlockSpec((SUBLANES, hw), lambda h, b: (0, h)),
                pl.BlockSpec((heads, 1, 1), lambda h, b: (h, 0, 0)),
                pl.BlockSpec((heads, 1, 1), lambda h, b: (h, 0, 0))]
    args = [proj] * 7 + [sc_w_pad, dec_f.reshape(RET_HEADS, 1, 1), dec_b.reshape(RET_HEADS, 1, 1)]
    out_spec = pl.BlockSpec((rows, hw), lambda h, b: (b, h))
    out_shape = [jax.ShapeDtypeStruct((nb * seq, RET_W), F32), jax.ShapeDtypeStruct((nb * seq, SC_W), F32)]
    out_specs = [out_spec, out_spec]
    st_spec = pl.BlockSpec((nseq, 1, 2, heads, RET_DK, RET_DV), lambda h, b: (b, 0, 0, h, 0, 0))
    if latent:
        in_specs.append(st_spec)
        args.append(state)
    else:
        out_shape.append(jax.ShapeDtypeStruct((BATCH, 1, 2, RET_HEADS, RET_DK, RET_DV), F32))
        out_specs.append(st_spec)
    return pl.pallas_call(
        functools.partial(_even_kernel, seq=seq, latent=latent, tq=tq, heads=heads, nseq=nseq),
        grid=(steps, nb // nseq),
        in_specs=in_specs,
        out_specs=out_specs,
        out_shape=out_shape,
        scratch_shapes=[pltpu.VMEM((heads, seq, seq), F32)],
        compiler_params=_cparams(("arbitrary", "arbitrary")),
        name="even_mixer_latent" if latent else "even_mixer_context",
    )(*args)


def _dft_tables(length):
    f = np.arange(length, dtype=np.float64)[:, None]
    t = np.arange(length, dtype=np.float64)[None, :]
    ang = np.pi * ((f * t) % (2 * length)) / length
    cos, sin = np.cos(ang), np.sin(ang)
    sin[0, :] = (-1.0) ** np.arange(length)
    fwd = np.concatenate([cos, sin], axis=0)
    wgt = np.full((2 * length,), 2.0)
    wgt[0] = 1.0
    wgt[length] = 1.0
    inv = (fwd * wgt[:, None] / (2.0 * length)).T
    return fwd.astype(np.float32), np.ascontiguousarray(inv).astype(np.float32)


def _hyena_consts(length):
    t = np.linspace(0.0, 1.0, length)[:, None]
    bands = (HY_EMB - 1) // 2
    w = 2.0 * np.pi * np.arange(length)[:, None] / length
    f = np.linspace(1e-4, bands - 1, bands)[None, :]
    z = np.concatenate([t, np.cos(f * w), -np.sin(f * w)], axis=-1)
    zpad = np.zeros((length, LANES))
    zpad[:, :HY_EMB] = z
    deltas = np.abs(np.linspace(math.log(HY_TARGET) / HY_SLOW_PCT, math.log(HY_TARGET) / HY_FAST_PCT, HY_W))
    decay = np.exp(-t * deltas[None, :])
    return zpad.astype(np.float32), decay.astype(np.float32)


def _split_kernel(x_ref, hi_ref, lo_ref):
    hi, lo = _split_bf16(x_ref[...])
    hi_ref[...] = hi
    lo_ref[...] = lo


def _split_table(x, name):
    rows = min(x.shape[0], 1024)
    spec = pl.BlockSpec((rows, x.shape[1]), lambda i: (i, 0))
    out = jax.ShapeDtypeStruct(x.shape, BF16)
    return pl.pallas_call(
        _split_kernel,
        grid=(x.shape[0] // rows,),
        in_specs=[spec],
        out_specs=[spec, spec],
        out_shape=[out, out],
        compiler_params=_cparams(("parallel",)),
        name=name,
    )(x)


def _dot_3pass_split(ah, al, b):
    bh, bl = _split_bf16(b)
    dot = functools.partial(jnp.dot, preferred_element_type=F32)
    return dot(ah, bh) + (dot(ah, bl) + dot(al, bh))


def _hy_filter_kernel(z_ref, dec_ref, w1_ref, b1_ref, f1_ref, w2_ref, b2_ref, f2_ref, w3f_ref, w3b_ref,
                      fwd_ref, fwd_lo_ref, ka_ref, ka2_ref, kb_ref, *, seq):
    h = jnp.sin(f1_ref[...] * (jnp.dot(z_ref[...], w1_ref[...], preferred_element_type=F32,
                                       precision=HIGHEST) + b1_ref[...]))
    h = jnp.sin(f2_ref[...] * (jnp.dot(h, w2_ref[...], preferred_element_type=F32,
                                       precision=HIGHEST) + b2_ref[...]))
    dec = dec_ref[...]
    hf = jnp.dot(h, w3f_ref[...], preferred_element_type=F32, precision=HIGHEST) * dec
    hb = jnp.dot(h, w3b_ref[...], preferred_element_type=F32, precision=HIGHEST) * dec
    row = lax.broadcasted_iota(I32, hb.shape, 0)
    hb = jnp.where(row == 0, 0.0, hb)
    inv_norm = 1.0 / (jnp.sum(jnp.abs(hf), axis=0, keepdims=True)
                      + jnp.sum(jnp.abs(hb), axis=0, keepdims=True) + EPS)
    p = (hf + hb) * inv_norm
    q = (hf - hb) * inv_norm
    fp = _dot_3pass_split(fwd_ref[...], fwd_lo_ref[...], p)
    fq = _dot_3pass_split(fwd_ref[seq:, :], fwd_lo_ref[seq:, :], q)
    kr = fp[:seq]
    row = lax.broadcasted_iota(I32, kr.shape, 0)
    ka_ref[0] = kr
    ka2_ref[0] = jnp.where(row == 0, fp[seq:seq + 1], kr)
    kb_ref[0] = jnp.where(row == 0, 0.0, -fq)


def _hy_filters(seq, zfeat, decay, fwd_hi, fwd_lo, w1, b1, f1, w2, b2, f2, w3):
    ns = HY_W // HY_SLAB
    per_o = 2 * ns
    small = lambda shape: pl.BlockSpec(shape, lambda o, s: (0, 0))
    out_spec = pl.BlockSpec((1, seq, HY_SLAB), lambda o, s: (o, 0, s))
    out = jax.ShapeDtypeStruct((HY_ORDER, seq, HY_W), F32)
    return pl.pallas_call(
        functools.partial(_hy_filter_kernel, seq=seq),
        grid=(HY_ORDER, ns),
        in_specs=[
            small((seq, LANES)),
            pl.BlockSpec((seq, HY_SLAB), lambda o, s: (0, s)),
            small((LANES, LANES)), small((1, LANES)), small((1, LANES)),
            small((LANES, LANES)), small((1, LANES)), small((1, LANES)),
            pl.BlockSpec((LANES, HY_SLAB), lambda o, s: (0, o * per_o + s)),
            pl.BlockSpec((LANES, HY_SLAB), lambda o, s: (0, o * per_o + ns + s)),
            small((2 * seq, seq)), small((2 * seq, seq)),
        ],
        out_specs=[out_spec, out_spec, out_spec],
        out_shape=[out, out, out],
        compiler_params=_cparams(("parallel", "parallel")),
        name=f"hyena_filters_{seq}",
    )(zfeat, decay, w1, b1, f1, w2, b2, f2, w3, w3, fwd_hi, fwd_lo)


def _hy_apply_kernel(x1_ref, x2_ref, v_ref, c1_ref, c2_ref, cv_ref, fwd_ref, inv_ref,
                     ka_ref, ka2_ref, kb_ref, bias_ref, o_ref, *, seq, nseq):
    def long_conv(sig, o):
        spec = jnp.dot(fwd_ref[...], sig.astype(BF16), preferred_element_type=F32)
        xr, xs = spec[:seq], spec[seq:]
        kb = kb_ref[o]
        yr = (xr * ka_ref[o] + xs * kb).astype(BF16)
        ys = (xs * ka2_ref[o] - xr * kb).astype(BF16)
        y = jnp.dot(inv_ref[:, :seq], yr, preferred_element_type=F32)
        y = y + jnp.dot(inv_ref[:, seq:], ys, preferred_element_type=F32)
        return y + sig * bias_ref[o:o + 1]

    for i in range(nseq):
        rows = slice(i * seq, (i + 1) * seq)
        x1 = _conv3_rows(x1_ref[rows, :], c1_ref[...])
        x2 = _conv3_rows(x2_ref[rows, :], c2_ref[...])
        v = _conv3_rows(v_ref[rows, :], cv_ref[...])
        z = x1 * long_conv(v, 0)
        o_ref[rows, :] = x2 * long_conv(z, 1)


HY_NSEQ = {True: 2, False: 4}


def _hy_apply(proj, conv_w_pad, fwd_bf, inv_bf, filters, bias_pad, latent):
    seq = DEC_SEQ if latent else SEQ
    nb = DEC_BATCH if latent else BATCH
    nseq = HY_NSEQ[latent]
    rows = nseq * seq
    row0 = 0 if latent else N_SAMPLE // rows
    ns = HY_W // HY_SLAB
    once = pl.Buffered(1)
    const = lambda shape: pl.BlockSpec(shape, lambda s, b: (0,) * len(shape), pipeline_mode=once)
    col = lambda c0: pl.BlockSpec((rows, HY_SLAB), lambda s, b: (row0 + b, c0 + s))
    cw = lambda c0: pl.BlockSpec((SUBLANES, HY_SLAB), lambda s, b: (0, c0 + s))
    filt = pl.BlockSpec((HY_ORDER, seq, HY_SLAB), lambda s, b: (0, 0, s), pipeline_mode=once)
    in_specs = [col(0), col(ns), col(2 * ns), cw(0), cw(ns), cw(2 * ns),
                const((2 * seq, seq)), const((seq, 2 * seq)), filt, filt, filt,
                pl.BlockSpec((SUBLANES, HY_SLAB), lambda s, b: (0, s))]
    args = [proj] * 3 + [conv_w_pad] * 3 + [fwd_bf, inv_bf, *filters, bias_pad]
    return pl.pallas_call(
        functools.partial(_hy_apply_kernel, seq=seq, nseq=nseq),
        grid=(ns, nb // nseq),
        in_specs=in_specs,
        out_specs=pl.BlockSpec((rows, HY_SLAB), lambda s, b: (b, s)),
        out_shape=jax.ShapeDtypeStruct((nb * seq, HY_W), F32),
        compiler_params=_cparams(("arbitrary", "arbitrary")),
        name="hyena_latent" if latent else "hyena_context",
    )(*args)


def _rope_tables(length):
    rows = length // GRID_W
    row = np.repeat(np.arange(rows, dtype=np.float64), GRID_W)
    colp = np.tile(np.arange(GRID_W, dtype=np.float64), rows)
    half = DA_DH // 2
    freqs = ROPE_THETA ** (-np.arange(0, half, 2, dtype=np.float64) / half)
    ang = np.concatenate([row[:, None] * freqs[None, :]] * 2 + [colp[:, None] * freqs[None, :]] * 2, axis=1)
    ang = np.concatenate([ang, ang], axis=1)
    lane = np.arange(LANES)
    sign = np.where((lane % half) < half // 2, -1.0, 1.0)[None, :]
    return np.cos(ang).astype(np.float32), (np.sin(ang) * sign).astype(np.float32)


def _rope(x, cos, sin_signed):
    q = DA_DH // 4
    lane = lax.broadcasted_iota(I32, x.shape, 1)
    partner = jnp.where((lane % (2 * q)) < q, pltpu.roll(x, LANES - q, 1), pltpu.roll(x, q, 1))
    return x * cos + partner * sin_signed


def _attn_kernel(*refs, seq, latent, lambda_init, heads, nseq):
    q_ref, k_ref, v_ref, lam_ref, sub_ref = refs[:5]
    if latent:
        cos_ref, sin_ref, ck_ref, cv_ref, o_ref = refs[5:]
    else:
        o_ref, nk_ref, nv_ref = refs[5:]
    hw = 2 * DA_DH
    lv = lam_ref[...]
    lam = (jnp.exp(jnp.sum(lv[0:1] * lv[1:2], axis=1, keepdims=True))
           - jnp.exp(jnp.sum(lv[2:3] * lv[3:4], axis=1, keepdims=True)) + lambda_init)
    scale = DA_DH ** -0.5
    dn = (((1,), (1,)), ((), ()))
    tq = min(seq, ATT_TQ)
    for sq, hh in [(sq, hh) for sq in range(nseq) for hh in range(heads)]:
        whole = slice(sq * seq, (sq + 1) * seq)
        lanes = slice(hh * hw, (hh + 1) * hw)
        k = k_ref[whole, lanes]
        v = v_ref[whole, lanes]
        if latent:
            k = _rope(k, cos_ref[...], sin_ref[...])
            k_all = jnp.concatenate([k, jnp.concatenate([ck_ref[sq, 0, 0, hh], ck_ref[sq, 0, 1, hh]], axis=1)], axis=0)
            v_all = jnp.concatenate([v, cv_ref[sq, 0, hh]], axis=0)
        else:
            k_all, v_all = k, v
            nk_ref[sq, 0, 0, hh] = k[:, :DA_DH]
            nk_ref[sq, 0, 1, hh] = k[:, DA_DH:]
            nv_ref[sq, 0, hh] = v
        kb = k_all.astype(BF16)
        vb = v_all.astype(BF16)
        for r in range(seq // tq):
            rows = slice(sq * seq + r * tq, sq * seq + (r + 1) * tq)
            q = q_ref[rows, lanes]
            if latent:
                q = _rope(q, cos_ref[r * tq:(r + 1) * tq, :], sin_ref[r * tq:(r + 1) * tq, :])
            q = q * scale
            first = lax.broadcasted_iota(I32, q.shape, 1) < DA_DH
            q1 = jnp.where(first, q, 0.0).astype(BF16)
            q2 = jnp.where(first, 0.0, q).astype(BF16)
            s1 = lax.dot_general(q1, kb, dn, preferred_element_type=F32)
            s2 = lax.dot_general(q2, kb, dn, preferred_element_type=F32)
            e1 = jnp.exp(s1 - jnp.max(s1, axis=-1, keepdims=True))
            e2 = jnp.exp(s2 - jnp.max(s2, axis=-1, keepdims=True))
            o1 = jnp.dot(e1.astype(BF16), vb, preferred_element_type=F32)
            o2 = jnp.dot(e2.astype(BF16), vb, preferred_element_type=F32)
            o = (o1 * (1.0 / jnp.sum(e1, axis=-1, keepdims=True))
                 - o2 * (lam / jnp.sum(e2, axis=-1, keepdims=True)))
            o = o * lax.rsqrt(jnp.mean(o * o, axis=-1, keepdims=True) + EPS) * sub_ref[...]
            o_ref[rows, lanes] = o * (1.0 - lambda_init)


def _attention(proj, lam_vec, subln, lambda_init, latent, rope=None, ctx_k=None, ctx_v=None):
    seq = DEC_SEQ if latent else SEQ
    nb = DEC_BATCH if latent else BATCH
    hw = 2 * DA_DH
    heads = 1 if latent else DA_HEADS
    nseq = 1 if latent else CTX_NSEQ
    rows = nseq * seq
    row0 = 0 if latent else N_SAMPLE // rows
    steps = DA_HEADS // heads
    col = lambda c: pl.BlockSpec((rows, heads * hw), lambda b, h: (row0 + b, c * steps + h))
    in_specs = [col(3), col(4), col(5),
                pl.BlockSpec((4, DA_DH), lambda b, h: (0, 0)),
                pl.BlockSpec((1, hw), lambda b, h: (0, 0))]
    args = [proj] * 3 + [lam_vec, subln.reshape(1, hw)]
    out_spec = pl.BlockSpec((rows, heads * hw), lambda b, h: (b, h))
    out_shape = [jax.ShapeDtypeStruct((nb * seq, DA_W), F32)]
    out_specs = [out_spec]
    k_spec = pl.BlockSpec((nseq, 1, 2, heads, SEQ, DA_DH), lambda b, h: (b, 0, 0, h, 0, 0))
    v_spec = pl.BlockSpec((nseq, 1, heads, SEQ, hw), lambda b, h: (b, 0, h, 0, 0))
    if latent:
        in_specs += [pl.BlockSpec((seq, hw), lambda b, h: (0, 0))] * 2 + [k_spec, v_spec]
        args += [rope[0], rope[1], ctx_k, ctx_v]
    else:
        out_shape += [jax.ShapeDtypeStruct((BATCH, 1, 2, DA_HEADS, SEQ, DA_DH), F32),
                      jax.ShapeDtypeStruct((BATCH, 1, DA_HEADS, SEQ, hw), F32)]
        out_specs += [k_spec, v_spec]
    return pl.pallas_call(
        functools.partial(_attn_kernel, seq=seq, latent=latent, lambda_init=lambda_init, heads=heads, nseq=nseq),
        grid=(nb // nseq, steps),
        in_specs=in_specs,
        out_specs=out_specs,
        out_shape=out_shape,
        compiler_params=_cparams(("arbitrary", "arbitrary")),
        name="diff_attn_latent" if latent else "diff_attn_context",
    )(*args)


HALF_D = D_MODEL // 2
HI16 = -65536


def _pack_bf16_pair(lo, hi):
    lo_bits = lax.bitcast_convert_type(lo.astype(BF16).astype(F32), I32)
    hi_bits = lax.bitcast_convert_type(hi.astype(BF16).astype(F32), I32)
    return (hi_bits & HI16) | lax.shift_right_logical(lo_bits, 16)


def _pack_exact_bf16_pair(lo, hi):
    return ((lax.bitcast_convert_type(hi, I32) & HI16)
            | lax.shift_right_logical(lax.bitcast_convert_type(lo, I32), 16))


def _unpack_bf16_pair(p):
    lo = lax.bitcast_convert_type(lax.shift_left(p, 16), F32).astype(BF16)
    hi = lax.bitcast_convert_type(p & HI16, F32).astype(BF16)
    return lo, hi


GRP_EXPERT, GRP_REL, GRP_COUNT = range(3)


def _route_rows(x, g, mod, wr, br, tril_ref, triu_ref, carry_ref):
    u = _norm_mod(x, g, mod, 3, 4)
    u_packed = _pack_bf16_pair(u[:, :HALF_D], u[:, HALF_D:])
    logits = _dot_3pass(u, wr) + br
    lane = lax.broadcasted_iota(I32, logits.shape, 1)
    lane_f = lane.astype(F32)
    neg = -jnp.inf
    big = float(LANES)

    def first_max(vals):
        m = jnp.max(vals, axis=-1, keepdims=True)
        return m, jnp.min(jnp.where(vals == m, lane_f, big), axis=-1, keepdims=True)

    is_grp = lane < N_GROUPS
    gmax, gidx = first_max(jnp.where(is_grp, logits, neg))
    gsum = jnp.sum(jnp.where(is_grp, jnp.exp(logits - gmax), 0.0), axis=-1, keepdims=True)
    g_w = 1.0 / gsum
    lo = N_GROUPS + EXP_PER_GROUP * gidx
    in_grp = jnp.logical_and(lane_f >= lo, lane_f < lo + EXP_PER_GROUP)
    el = jnp.where(in_grp, logits, neg)
    v1, i1 = first_max(el)
    v2, i2 = first_max(jnp.where(lane_f == i1, neg, el))
    e = jnp.exp(v2 - v1)
    p1 = 1.0 / (1.0 + e)
    hit1 = lane_f == i1 - N_GROUPS
    hit2 = lane_f == i2 - N_GROUPS
    onehot = jnp.where(jnp.logical_or(hit1, hit2), 1.0, 0.0)
    cum = jnp.dot(tril_ref[...], onehot.astype(BF16), preferred_element_type=F32)
    n = cum[TM - 1:TM, :]
    n_pad = jnp.floor((n + (ROW_ALIGN - 1.0)) * (1.0 / ROW_ALIGN)) * ROW_ALIGN
    loc_off = jnp.dot(jnp.broadcast_to(n_pad, (SUBLANES, LANES)).astype(BF16), triu_ref[...],
                      preferred_element_type=F32)[0:1]
    base = cum - onehot + loc_off
    loc1 = jnp.sum(jnp.where(hit1, base, 0.0), axis=-1, keepdims=True)
    loc2 = jnp.sum(jnp.where(hit2, base, 0.0), axis=-1, keepdims=True)
    info = jnp.where(lane == 0, g_w * p1, jnp.where(lane == 1, g_w * (e * p1), jnp.where(lane == 2, loc1, loc2)))
    start8 = lax.broadcasted_iota(I32, (LANES, LANES), 0).astype(F32) * ROW_ALIGN
    owner = jnp.where(jnp.logical_and(start8 >= loc_off, start8 < loc_off + n_pad), 1.0, 0.0)
    lane128 = lax.broadcasted_iota(I32, (LANES, LANES), 1)
    c_exp = jnp.sum(owner * lane128.astype(F32), axis=-1, keepdims=True)
    c_rel = jnp.sum(owner * (carry_ref[...] + start8 - loc_off), axis=-1, keepdims=True)
    chunks = jnp.where(lane128 == 0, c_exp, jnp.where(lane128 == 1, c_rel, 0.0)).T
    n_chunks = jnp.sum(n_pad, axis=-1, keepdims=True) * (1.0 / ROW_ALIGN)
    row = lax.broadcasted_iota(I32, (SUBLANES, LANES), 0)
    grp = jnp.where(row == GRP_EXPERT, chunks[0:1], jnp.where(row == GRP_REL, chunks[1:2], n_chunks))
    carry_ref[...] = carry_ref[...] + n_pad
    return u_packed, info, info.T[:SUBLANES], grp.astype(I32)


def _proj_res_router_kernel(*refs, n_x, gate_idx):
    (al_ref, ac_ref, bl_ref, bc_ref, wa_ref, wb_ref, mod_ref, g2_ref, wr_ref, br_ref, tril_ref, triu_ref,
     o_ref, u_ref, loct_ref, info_ref, grp_ref, tot_ref, carry_ref) = refs[n_x:]

    @pl.when(pl.program_id(0) == 0)
    def _():
        carry_ref[...] = jnp.zeros_like(carry_ref)

    a = _group_pick(al_ref, ac_ref)
    b = _group_pick(bl_ref, bc_ref)
    y = jnp.dot(a.astype(BF16), wa_ref[...], preferred_element_type=F32)
    y = y + jnp.dot(b.astype(BF16), wb_ref[...], preferred_element_type=F32)
    mod = mod_ref[0]
    x_new = _rows_value(refs[:n_x]) + mod[gate_idx:gate_idx + 1] * y
    o_ref[...] = x_new
    for t in range(TM_WIDE // TM):
        rows = slice(t * TM, (t + 1) * TM)
        u_packed, info, info_t, grp = _route_rows(x_new[rows], g2_ref[...], mod, wr_ref[...], br_ref[...],
                                                 tril_ref, triu_ref, carry_ref)
        u_ref[rows, :] = u_packed
        info_ref[rows, :] = info
        loct_ref[t] = info_t
        grp_ref[t] = grp
    tot_ref[...] = jnp.broadcast_to(carry_ref[...], tot_ref.shape).astype(I32)


def _proj_residual_router(x, a, b, w_bf16, mods, gate_idx, norm2, w_route, b_route, tril, triu):
    half = a[0].shape[1]
    tm = TM_WIDE
    sub = tm // TM
    nt = N_TOK // TM
    x_specs, x_args = _rows_specs(x, tm)
    const = lambda shape: pl.BlockSpec(shape, lambda i: (0,) * len(shape))
    return pl.pallas_call(
        functools.partial(_proj_res_router_kernel, n_x=len(x_args), gate_idx=gate_idx),
        grid=(N_TOK // tm,),
        in_specs=x_specs + _group_specs(half, tm) + _group_specs(half, tm) + [
            pl.BlockSpec((half, D_MODEL), lambda i: (0, 0)),
            pl.BlockSpec((half, D_MODEL), lambda i: (1, 0)),
            pl.BlockSpec((1, MOD_ROWS, D_MODEL), lambda i: (i * tm // SEG, 0, 0)),
            const((1, D_MODEL)), const((D_MODEL, LANES)), const((1, LANES)), const((TM, TM)), const((LANES, LANES)),
        ],
        out_specs=[
            pl.BlockSpec((tm, D_MODEL), lambda i: (i, 0)),
            pl.BlockSpec((tm, HALF_D), lambda i: (i, 0)),
            pl.BlockSpec((sub, SUBLANES, TM), lambda i: (i, 0, 0)),
            pl.BlockSpec((tm, LANES), lambda i: (i, 0)),
            pl.BlockSpec((sub, SUBLANES, LANES), lambda i: (i, 0, 0)),
            pl.BlockSpec((SUBLANES, LANES), lambda i: (0, 0)),
        ],
        out_shape=[
            jax.ShapeDtypeStruct((N_TOK, D_MODEL), F32),
            jax.ShapeDtypeStruct((N_TOK, HALF_D), I32),
            jax.ShapeDtypeStruct((nt, SUBLANES, TM), F32),
            jax.ShapeDtypeStruct((N_TOK, LANES), F32),
            jax.ShapeDtypeStruct((nt, SUBLANES, LANES), I32),
            jax.ShapeDtypeStruct((SUBLANES, LANES), I32),
        ],
        scratch_shapes=[pltpu.VMEM((1, LANES), F32)],
        compiler_params=_cparams(("arbitrary",)),
        name="outproj_residual_router",
    )(*x_args, a[0], a[1], b[0], b[1], w_bf16, w_bf16, mods, norm2.reshape(1, D_MODEL), w_route, b_route, tril, triu)


def _for_chunks(n, fn):
    for size in CHUNKS:
        @pl.when((n & size) != 0)
        def _():
            fn(n & ~(2 * size - 1), size)


N_COPIES = R_LOC // ROW_ALIGN
DUMP_ROWS = 2 * R_LOC


def _start_tile_copies(grp_ref, seg_ref, make, dummy, n=None):
    if n is None:
        n = grp_ref[0, GRP_COUNT, 0]
    for c in range(N_COPIES):
        real = c < n
        slot = seg_ref[0, grp_ref[0, GRP_EXPERT, c]] + grp_ref[0, GRP_REL, c]
        d_loc, d_slot = dummy(c)
        loc = jnp.where(real, c * ROW_ALIGN, d_loc)
        slot = jnp.where(real, slot, d_slot)
        make(pl.ds(pl.multiple_of(loc, ROW_ALIGN), ROW_ALIGN), pl.ds(pl.multiple_of(slot, ROW_ALIGN), ROW_ALIGN)).start()


def _wait_tile_copies(make):
    make(pl.ds(0, R_LOC), pl.ds(0, R_LOC)).wait()


def _all_experts(body):
    lax.fori_loop(0, N_EXPERTS, lambda e, c: (body(e), c)[1], 0)


def _dispatch_kernel(grp_ref, seg_ref, loct_ref, u_ref, xb_ref, buf, zbuf, sem):
    i = pl.program_id(0)
    last = pl.num_programs(0) - 1
    cur = i % 2

    def maker(b):
        return lambda loc, slot: pltpu.make_async_copy(buf.at[b, loc], xb_ref.at[slot], sem.at[b])

    @pl.when(i >= 2)
    def _():
        _wait_tile_copies(maker(cur))

    lo, hi = _unpack_bf16_pair(u_ref[...])
    row = lax.broadcasted_iota(I32, (R_LOC, TM), 0).astype(F32)
    sel = jnp.logical_or(row == loct_ref[0, 2:3, :], row == loct_ref[0, 3:4, :])
    pt = jnp.where(sel, 1.0, 0.0).astype(BF16)
    buf[cur] = _pack_exact_bf16_pair(jnp.dot(pt, lo, preferred_element_type=F32),
                                     jnp.dot(pt, hi, preferred_element_type=F32))
    _start_tile_copies(grp_ref, seg_ref, maker(cur),
                       lambda c: (c * ROW_ALIGN, MOE_ROWS + cur * R_LOC + c * ROW_ALIGN))

    @pl.when(i == last)
    def _():
        @pl.when(i >= 1)
        def _():
            _wait_tile_copies(maker(1 - cur))

        _wait_tile_copies(maker(cur))

        zbuf[...] = jnp.zeros_like(zbuf)

        def tail(e, act):
            total = seg_ref[1, e]
            dst = seg_ref[0, e] + total

            def one(off, size):
                act(pltpu.make_async_copy(zbuf.at[pl.ds(0, size)],
                                          xb_ref.at[pl.ds(pl.multiple_of(dst + off, ROW_ALIGN), size)], sem.at[2]))

            _for_chunks((-total) & (MOE_BLK - 1), one)

        _all_experts(lambda e: tail(e, lambda cp: cp.start()))
        _all_experts(lambda e: tail(e, lambda cp: cp.wait()))

        e_last = N_EXPERTS - 1
        used = (seg_ref[0, e_last] + seg_ref[1, e_last] + MOE_BLK - 1) // MOE_BLK

        def spare(j):
            return pltpu.make_async_copy(zbuf, xb_ref.at[pl.ds(pl.multiple_of(j * MOE_BLK, MOE_BLK), MOE_BLK)],
                                         sem.at[2])

        lax.fori_loop(used, (MOE_ROWS + DUMP_ROWS) // MOE_BLK, lambda j, c: (spare(j).start(), c)[1], 0)
        lax.fori_loop(used, (MOE_ROWS + DUMP_ROWS) // MOE_BLK, lambda j, c: (spare(j).wait(), c)[1], 0)


def _dispatch(grp, seg, loct, u):
    return pl.pallas_call(
        _dispatch_kernel,
        grid=(N_TOK // TM,),
        in_specs=[
            pl.BlockSpec((1, SUBLANES, LANES), lambda i: (i, 0, 0), memory_space=pltpu.SMEM),
            pl.BlockSpec(memory_space=pltpu.SMEM),
            pl.BlockSpec((1, SUBLANES, TM), lambda i: (i, 0, 0)),
            pl.BlockSpec((TM, HALF_D), lambda i: (i, 0)),
        ],
        out_specs=pl.BlockSpec(memory_space=pl.ANY),
        out_shape=jax.ShapeDtypeStruct((MOE_ROWS + DUMP_ROWS, HALF_D), I32),
        scratch_shapes=[pltpu.VMEM((2, R_LOC, HALF_D), I32), pltpu.VMEM((MOE_BLK, HALF_D), I32),
                        pltpu.SemaphoreType.DMA((3,))],
        compiler_params=_cparams(("arbitrary",)),
        name="moe_dispatch",
    )(grp, seg, loct, u)


def _expert_kernel(seg_ref, w1_ref, w3_ref, w2_ref, xb_ref, yb_ref, w1b, w3b, w2b, xbuf, ybuf, xsem, ysem):
    e = pl.program_id(0)
    last_e = pl.num_programs(0) - 1
    blk0 = seg_ref[0, e] // MOE_BLK
    nblk = (seg_ref[1, e] + MOE_BLK - 1) // MOE_BLK
    used = (seg_ref[0, N_EXPERTS - 1] + seg_ref[1, N_EXPERTS - 1] + MOE_BLK - 1) // MOE_BLK

    def rows(g):
        return pl.ds(pl.multiple_of(g * MOE_BLK, MOE_BLK), MOE_BLK)

    def fetch(g, b):
        return pltpu.make_async_copy(xb_ref.at[rows(g)], xbuf.at[b], xsem.at[b])

    def flush(g, b):
        return pltpu.make_async_copy(ybuf.at[b], yb_ref.at[rows(g)], ysem.at[b])

    row_queue = 1

    @pl.when(e == 0)
    def _():
        fetch(0, 0).start(priority=row_queue)

    @pl.when(nblk > 0)
    def _():
        w1b[...] = w1_ref[0, 0].astype(BF16)
        w3b[...] = w3_ref[0, 0].astype(BF16)
        w2b[...] = w2_ref[0, 0].astype(BF16)

        def block(i, carry):
            g = blk0 + i
            b = g % 2
            fetch(g, b).wait()

            @pl.when(g + 1 < used)
            def _():
                fetch(g + 1, 1 - b).start(priority=row_queue)

            @pl.when(g >= 2)
            def _():
                flush(g - 2, b).wait()

            def ffn(x_packed):
                lo, hi = _unpack_bf16_pair(x_packed)

                def up(wb):
                    return (jnp.dot(lo, wb[:HALF_D, :], preferred_element_type=F32)
                            + jnp.dot(hi, wb[HALF_D:, :], preferred_element_type=F32))

                hdn = (_silu(up(w1b)) * up(w3b)).astype(BF16)
                y = jnp.dot(hdn, w2b[...], preferred_element_type=F32)
                return _pack_bf16_pair(y[:, :HALF_D], y[:, HALF_D:])

            half = MOE_BLK // 2
            slots_here = seg_ref[1, e] - i * MOE_BLK

            @pl.when(slots_here > half)
            def _():
                ybuf[b] = ffn(xbuf[b])

            @pl.when(slots_here <= half)
            def _():
                ybuf[b, :half, :] = ffn(xbuf[b, :half, :])
                ybuf[b, half:, :] = jnp.zeros((half, HALF_D), I32)

            flush(g, b).start(priority=row_queue)
            return carry

        lax.fori_loop(0, nblk, block, 0)

    @pl.when(e == last_e)
    def _():
        @pl.when(used >= 2)
        def _():
            flush(used - 2, used % 2).wait()

        flush(used - 1, (used - 1) % 2).wait()
        xbuf[0] = jnp.zeros((MOE_BLK, HALF_D), I32)

        def spare(g):
            return pltpu.make_async_copy(xbuf.at[0], yb_ref.at[rows(g)], xsem.at[0])

        lax.fori_loop(used, MOE_NB, lambda g, c: (spare(g).start(), c)[1], 0)
        lax.fori_loop(used, MOE_NB, lambda g, c: (spare(g).wait(), c)[1], 0)


def _experts(seg, xb, w1, w3, w2, layer):
    hbm = pl.BlockSpec(memory_space=pl.ANY)
    w_in = pl.BlockSpec((1, 1, D_MODEL, D_EXPERT), lambda e, s: (layer, e, 0, 0))
    w_out = pl.BlockSpec((1, 1, D_EXPERT, D_MODEL), lambda e, s: (layer, e, 0, 0))
    return pl.pallas_call(
        _expert_kernel,
        grid_spec=pltpu.PrefetchScalarGridSpec(
            num_scalar_prefetch=1,
            grid=(N_EXPERTS,),
            in_specs=[w_in, w_in, w_out, hbm],
            out_specs=hbm,
            scratch_shapes=[pltpu.VMEM((D_MODEL, D_EXPERT), BF16), pltpu.VMEM((D_MODEL, D_EXPERT), BF16),
                            pltpu.VMEM((D_EXPERT, D_MODEL), BF16),
                            pltpu.VMEM((2, MOE_BLK, HALF_D), I32), pltpu.VMEM((2, MOE_BLK, HALF_D), I32),
                            pltpu.SemaphoreType.DMA((2,)), pltpu.SemaphoreType.DMA((2,))],
        ),
        out_shape=jax.ShapeDtypeStruct((MOE_ROWS, HALF_D), I32),
        compiler_params=_cparams(("arbitrary",)),
        name="moe_experts",
    )(seg, w1, w3, w2, xb)


def _combine_kernel(*refs, final):
    grp_ref, nxt_ref, seg_ref, x_ref, info_ref, mod_ref, yb_ref = refs[:7]
    if final:
        fin_ref, olat_ref, octx_ref, ybuf, sem = refs[7:]
    else:
        g1n_ref, modn_ref, wn_ref, o_ref, proj_ref, ybuf, sem = refs[7:]
    i = pl.program_id(0)
    cur = i % 2

    def maker(b):
        return lambda loc, slot: pltpu.make_async_copy(yb_ref.at[slot], ybuf.at[b, loc], sem.at[b])

    def dummy(c):
        return R_LOC + c * ROW_ALIGN, c * ROW_ALIGN

    @pl.when(i == 0)
    def _():
        ybuf[...] = jnp.zeros_like(ybuf)
        _start_tile_copies(grp_ref, seg_ref, maker(0), dummy)

    has_next = i + 1 < pl.num_programs(0)
    _start_tile_copies(nxt_ref, seg_ref, maker(1 - cur), dummy, jnp.where(has_next, nxt_ref[0, GRP_COUNT, 0], 0))
    _wait_tile_copies(maker(cur))

    @pl.when(jnp.logical_not(has_next))
    def _():
        _wait_tile_copies(maker(1 - cur))

    info = info_ref[...]
    col = lax.broadcasted_iota(I32, (TM, R_LOC), 1).astype(F32)
    p = jnp.where(col == info[:, 2:3], info[:, 0:1], jnp.where(col == info[:, 3:4], info[:, 1:2], 0.0))
    p = p.astype(BF16)
    lo, hi = _unpack_bf16_pair(ybuf[cur, :R_LOC, :])
    y = jnp.concatenate([jnp.dot(p, lo, preferred_element_type=F32),
                         jnp.dot(p, hi, preferred_element_type=F32)], axis=1)
    out = x_ref[...] + mod_ref[0][5:6] * y
    if not final:
        o_ref[...] = out
        u_next = _norm_mod(out, g1n_ref[...], modn_ref[0], 0, 1)
        proj_ref[...] = jnp.dot(u_next.astype(BF16), wn_ref[...], preferred_element_type=F32)
        return
    out = out * lax.rsqrt(jnp.mean(out * out, axis=-1, keepdims=True) + EPS) * fin_ref[...]

    @pl.when(i < N_LAT_TILES)
    def _():
        olat_ref[...] = out

    @pl.when(i >= N_LAT_TILES)
    def _():
        octx_ref[...] = out


def _combine(x, grp, seg, info, mods, yb, final_g=None, next_proj=None):
    final = final_g is not None
    nt = N_TOK // TM
    in_specs = [
        pl.BlockSpec((1, SUBLANES, LANES), lambda i: (i, 0, 0), memory_space=pltpu.SMEM),
        pl.BlockSpec((1, SUBLANES, LANES), lambda i: (jnp.minimum(i + 1, nt - 1), 0, 0), memory_space=pltpu.SMEM),
        pl.BlockSpec(memory_space=pltpu.SMEM),
        pl.BlockSpec((TM, D_MODEL), lambda i: (i, 0)),
        pl.BlockSpec((TM, LANES), lambda i: (i, 0)),
        pl.BlockSpec((1, MOD_ROWS, D_MODEL), lambda i: (i * TM // SEG, 0, 0)),
        pl.BlockSpec(memory_space=pl.ANY),
    ]
    args = [grp, grp, seg, x, info, mods, yb]
    if final:
        in_specs.append(pl.BlockSpec((1, D_MODEL), lambda i: (0, 0)))
        args.append(final_g.reshape(1, D_MODEL))
        out_specs = _group_specs(D_MODEL)
        out_shape = [jax.ShapeDtypeStruct((N_SAMPLE, D_MODEL), F32), jax.ShapeDtypeStruct((N_PROMPT, D_MODEL), F32)]
    else:
        g1n, modn, wn = next_proj
        n = wn.shape[1]
        in_specs += [pl.BlockSpec((1, D_MODEL), lambda i: (0, 0)),
                     pl.BlockSpec((1, MOD_ROWS, D_MODEL), lambda i: (i * TM // SEG, 0, 0)),
                     pl.BlockSpec((D_MODEL, n), lambda i: (0, 0), pipeline_mode=pl.Buffered(1))]
        args += [g1n.reshape(1, D_MODEL), modn, wn]
        out_specs = [pl.BlockSpec((TM, D_MODEL), lambda i: (i, 0)), pl.BlockSpec((TM, n), lambda i: (i, 0))]
        out_shape = [jax.ShapeDtypeStruct((N_TOK, D_MODEL), F32), jax.ShapeDtypeStruct((N_TOK, n), F32)]
    return pl.pallas_call(
        functools.partial(_combine_kernel, final=final),
        grid=(nt,),
        in_specs=in_specs,
        out_specs=out_specs,
        out_shape=out_shape,
        scratch_shapes=[pltpu.VMEM((2, 2 * R_LOC, HALF_D), I32), pltpu.SemaphoreType.DMA((2,))],
        compiler_params=_cparams(("arbitrary",)),
        name="moe_combine_final" if final else "moe_combine",
    )(*args)


def _route_weights(w_grp, b_grp, w_exp, b_exp):
    w_route = jnp.zeros((D_MODEL, LANES), F32)
    w_route = w_route.at[:, :N_GROUPS].set(w_grp).at[:, N_GROUPS:N_GROUPS + N_EXPERTS].set(w_exp)
    b_route = jnp.zeros((1, LANES), F32)
    b_route = b_route.at[0, :N_GROUPS].set(b_grp).at[0, N_GROUPS:N_GROUPS + N_EXPERTS].set(b_exp)
    return w_route, b_route


def _moe_layer(x, route, layer, mods, w1, w3, w2, final_g=None, next_proj=None):
    u, loct, info, grp, tot = route
    total = tot[0, :N_EXPERTS]
    padded = (total + MOE_BLK - 1) // MOE_BLK * MOE_BLK
    pad_end = jnp.cumsum(padded)
    pad_start = pad_end - padded
    seg = jnp.zeros((2, LANES), I32).at[0, :N_EXPERTS].set(pad_start).at[1, :N_EXPERTS].set(total)
    xb = _dispatch(grp, seg, loct, u)
    yb = _experts(seg, xb, w1, w3, w2, layer)
    return _combine(x, grp, seg, info, mods, yb, final_g, next_proj)


def _pad_rows(w, rows=SUBLANES):
    return jnp.zeros((rows, w.shape[1]), w.dtype).at[:w.shape[0]].set(w)


def _pad2(w, rows, cols):
    return jnp.zeros((rows, cols), w.dtype).at[:w.shape[0], :w.shape[1]].set(w)


def kernel(x_prompt, x_sample, c, c_ctx, state_ret, cache_k, cache_v, ada_w, ada_b, norm1, norm2, final_norm, ev_w_in, ev_w_out, ret_decay_fwd, ret_decay_bwd, sc_conv_w, od_w_in, od_w_out, hy_conv_w, hy_w1, hy_b1, hy_fr1, hy_w2, hy_b2, hy_fr2, hy_w3, hy_bias, da_lambda, da_subln, moe_w_grp, moe_b_grp, moe_w_exp, moe_b_exp, moe_w1, moe_w3, moe_w2):
    x = (x_sample.reshape(N_SAMPLE, D_MODEL), x_prompt.reshape(N_PROMPT, D_MODEL))

    cvecs = jnp.zeros((16, D_MODEL), F32).at[:DEC_BATCH].set(c).at[DEC_BATCH].set(c_ctx)
    m = _mods(cvecs, ada_w, ada_b).reshape(DEPTH, 16, N_MOD, D_MODEL)
    seg_mods = jnp.concatenate(
        [m[:, :DEC_BATCH], jnp.broadcast_to(m[:, DEC_BATCH:DEC_BATCH + 1], (DEPTH, N_SEG - DEC_BATCH, N_MOD, D_MODEL))],
        axis=1)
    seg_mods = jnp.pad(seg_mods, ((0, 0), (0, 0), (0, MOD_ROWS - N_MOD), (0, 0)))

    tril = jnp.asarray(np.tril(np.ones((TM, TM), np.float32)), BF16)
    triu = jnp.asarray(np.triu(np.ones((LANES, LANES), np.float32), 1), BF16)

    def w_in(layer):
        w = ev_w_in if layer % 2 == 0 else od_w_in
        return w[layer // 2].astype(BF16)

    proj = None
    for l in range(DEPTH):
        mods = seg_mods[l]
        i = l // 2
        if l % 2 == 0:
            if proj is None:
                proj = _norm_mod_matmul(x, norm1[l], mods, w_in(l), 0, 1)
            scw = _pad_rows(sc_conv_w[i])
            ret_l, sc_l = _even_mixer(proj, scw, ret_decay_fwd[i], ret_decay_bwd[i], state=state_ret[:, i:i + 1])
            ret_c, sc_c, new_state = _even_mixer(proj, scw, ret_decay_fwd[i], ret_decay_bwd[i])
            mixed, w_out = ((ret_l, ret_c), (sc_l, sc_c)), ev_w_out[i]
        else:
            lambda_init = 0.8 - 0.6 * math.exp(-0.3 * l)
            if proj is None:
                proj = _norm_mod_matmul(x, norm1[l], mods, w_in(l), 0, 1)
            conv_w = _pad_rows(hy_conv_w[i])
            bias = _pad_rows(hy_bias[i])
            w1 = _pad2(hy_w1[i], LANES, LANES)
            b1 = _pad2(hy_b1[i][None], 1, LANES)
            f1 = _pad2(hy_fr1[i][None], 1, LANES)
            w2 = _pad2(hy_w2[i], LANES, LANES)
            b2 = _pad2(hy_b2[i][None], 1, LANES)
            f2 = _pad2(hy_fr2[i][None], 1, LANES)
            w3 = _pad2(hy_w3[i], LANES, hy_w3.shape[2])
            hy = []
            for latent in (True, False):
                seq = DEC_SEQ if latent else SEQ
                fwd_np, inv_np = _dft_tables(seq)
                zfeat, decay = _hyena_consts(seq)
                fwd_bf, fwd_lo = _split_table(jnp.asarray(fwd_np), f"dft_fwd_split_{seq}")
                inv_bf, _ = _split_table(jnp.asarray(inv_np), f"dft_inv_split_{seq}")
                filters = _hy_filters(seq, jnp.asarray(zfeat), jnp.asarray(decay), fwd_bf, fwd_lo,
                                      w1, b1, f1, w2, b2, f2, w3)
                hy.append(_hy_apply(proj, conv_w, fwd_bf, inv_bf, filters, bias, latent))
            cos, sin = _rope_tables(DEC_SEQ)
            att_l = _attention(proj, da_lambda[i], da_subln[i], lambda_init, True,
                               rope=(jnp.asarray(cos), jnp.asarray(sin)),
                               ctx_k=cache_k[:, i:i + 1], ctx_v=cache_v[:, i:i + 1])[0]
            att_c, new_k, new_v = _attention(proj, da_lambda[i], da_subln[i], lambda_init, False)
            mixed, w_out = (tuple(hy), (att_l, att_c)), od_w_out[i]
        w_route, b_route = _route_weights(moe_w_grp[l], moe_b_grp[l], moe_w_exp[l], moe_b_exp[l])
        x, *route = _proj_residual_router(x, mixed[0], mixed[1], w_out.astype(BF16), mods, 2,
                                          norm2[l], w_route, b_route, tril, triu)
        if l == DEPTH - 1:
            x = _moe_layer(x, route, l, mods, moe_w1, moe_w3, moe_w2, final_g=final_norm)
        else:
            x, proj = _moe_layer(x, route, l, mods, moe_w1, moe_w3, moe_w2,
                                 next_proj=(norm1[l + 1], seg_mods[l + 1], w_in(l + 1)))

    y_sample, y_prompt = x
    return (y_prompt.reshape(BATCH, SEQ, D_MODEL), y_sample.reshape(DEC_BATCH, DEC_SEQ, D_MODEL),
            new_state, new_k, new_v)
```

```python
import functools
import math

import numpy as np
import jax
import jax.numpy as jnp
from jax import lax
from jax.experimental import pallas as pl
from jax.experimental.pallas import tpu as pltpu

F32 = jnp.float32
BF16 = jnp.bfloat16
I32 = jnp.int32
HIGHEST = lax.Precision.HIGHEST

D_MODEL = 1024
BATCH = 32
SEQ = 256
DEPTH = 2
DEC_BATCH = 8
DEC_SEQ = 1024
PAST_LEN = 256
GRID_W = 64
N_MOD = 6
EPS = 1e-6
GN_EPS = 1e-5

RET_DK = 128
RET_DV = 128
RET_HEADS = 4
RET_W = 512
RET_QK = 512
SC_W = 512
EVEN_IN = 2 * RET_QK + 2 * RET_W + 3 * SC_W

HY_W = 512
HY_ORDER = 2
HY_EMB = 33
HY_FFN = 64
HY_TARGET = 1e-2
HY_FAST_PCT = 0.3
HY_SLOW_PCT = 1.5
DA_DH = 64
DA_HEADS = 4
DA_W = 512
ODD_IN = 3 * HY_W + 3 * DA_W
ROPE_THETA = 10000.0

N_GROUPS = 4
EXP_PER_GROUP = 8
N_EXPERTS = 32
TOP_K = 2
D_EXPERT = 512

N_SAMPLE = DEC_BATCH * DEC_SEQ
N_PROMPT = BATCH * SEQ
N_TOK = N_SAMPLE + N_PROMPT
SEG = 1024
N_SEG = N_TOK // SEG
MOD_ROWS = 8

LANES = 128
SUBLANES = 8
VMEM_LIMIT = 56 * 1024 * 1024

TM = 256
MOE_BLK = 512
ROW_ALIGN = SUBLANES
R_LOC = (TOP_K * TM + N_EXPERTS * (ROW_ALIGN - 1) + LANES - 1) // LANES * LANES
CHUNKS = (256, 128, 64, 32, 16, 8)
MOE_ROWS = ((N_TOK * TOP_K + N_EXPERTS * (N_TOK // TM) * (ROW_ALIGN - 1) + N_EXPERTS * (MOE_BLK - 1))
            // MOE_BLK + 1) * MOE_BLK
MOE_NB = MOE_ROWS // MOE_BLK
HY_SLAB = 512
ATT_TQ = 256


def _cparams(sem, vmem=VMEM_LIMIT):
    return pltpu.CompilerParams(dimension_semantics=sem, vmem_limit_bytes=vmem)


def _silu(x):
    return x * (1.0 / (1.0 + jnp.exp(-x)))


def _split_bf16(a):
    hi = a.astype(BF16)
    return hi, (a - hi.astype(F32)).astype(BF16)


def _dot_3pass(a, b):
    ah, al = _split_bf16(a)
    bh, bl = _split_bf16(b)
    dot = functools.partial(jnp.dot, preferred_element_type=F32)
    return dot(ah, bh) + (dot(ah, bl) + dot(al, bh))


MODS_TN = 1536


def _mods_kernel(c_ref, w_ref, b_ref, o_ref):
    s = _silu(c_ref[...])
    o_ref[0] = jnp.dot(s, w_ref[0], preferred_element_type=F32, precision=HIGHEST) + b_ref[0]


def _mods(cvecs, ada_w, ada_b):
    n = N_MOD * D_MODEL
    return pl.pallas_call(
        _mods_kernel,
        grid=(DEPTH, n // MODS_TN),
        in_specs=[
            pl.BlockSpec((16, D_MODEL), lambda l, j: (0, 0)),
            pl.BlockSpec((1, D_MODEL, MODS_TN), lambda l, j: (l, 0, j)),
            pl.BlockSpec((1, 1, MODS_TN), lambda l, j: (l, 0, j)),
        ],
        out_specs=pl.BlockSpec((1, 16, MODS_TN), lambda l, j: (l, 0, j)),
        out_shape=jax.ShapeDtypeStruct((DEPTH, 16, n), F32),
        compiler_params=_cparams(("parallel", "parallel")),
        name="adaln_mods",
    )(cvecs, ada_w, ada_b.reshape(DEPTH, 1, n))


def _norm_mod(x, g, mod, shift_idx, scale_idx):
    ms = jnp.mean(x * x, axis=-1, keepdims=True)
    y = x * lax.rsqrt(ms + EPS) * g
    return y * (1.0 + mod[scale_idx:scale_idx + 1]) + mod[shift_idx:shift_idx + 1]


N_LAT_TILES = N_SAMPLE // TM
TM_WIDE = 512


def _group_specs(width, tm=TM):
    n_lat = N_SAMPLE // tm
    lat = pl.BlockSpec((tm, width), lambda i: (jnp.minimum(i, n_lat - 1), 0))
    ctx = pl.BlockSpec((tm, width), lambda i: (jnp.maximum(i - n_lat, 0), 0))
    return [lat, ctx]


def _group_pick(lat_ref, ctx_ref):
    n_lat = N_SAMPLE // lat_ref.shape[0]
    return jnp.where(pl.program_id(0) < n_lat, lat_ref[...], ctx_ref[...])


def _rows_specs(x, tm=TM):
    if isinstance(x, tuple):
        return _group_specs(D_MODEL, tm), list(x)
    return [pl.BlockSpec((tm, D_MODEL), lambda i: (i, 0))], [x]


def _rows_value(refs):
    return _group_pick(*refs) if len(refs) == 2 else refs[0][...]


def _nmm_kernel(*refs, n_x, shift_idx, scale_idx):
    g_ref, mod_ref, w_ref, o_ref = refs[n_x:]
    u = _norm_mod(_rows_value(refs[:n_x]), g_ref[...], mod_ref[0], shift_idx, scale_idx)
    o_ref[...] = jnp.dot(u.astype(BF16), w_ref[...], preferred_element_type=F32)


def _norm_mod_matmul(x, g, mods, w_bf16, shift_idx, scale_idx):
    n = w_bf16.shape[1]
    x_specs, x_args = _rows_specs(x)
    return pl.pallas_call(
        functools.partial(_nmm_kernel, n_x=len(x_args), shift_idx=shift_idx, scale_idx=scale_idx),
        grid=(N_TOK // TM,),
        in_specs=x_specs + [
            pl.BlockSpec((1, D_MODEL), lambda i: (0, 0)),
            pl.BlockSpec((1, MOD_ROWS, D_MODEL), lambda i: (i * TM // SEG, 0, 0)),
            pl.BlockSpec((D_MODEL, n), lambda i: (0, 0)),
        ],
        out_specs=pl.BlockSpec((TM, n), lambda i: (i, 0)),
        out_shape=jax.ShapeDtypeStruct((N_TOK, n), F32),
        compiler_params=_cparams(("parallel",)),
        name="norm_mod_inproj",
    )(*x_args, g.reshape(1, D_MODEL), mods, w_bf16)


def _conv3_rows(z, w):
    n = z.shape[0]
    row = lax.broadcasted_iota(I32, z.shape, 0)
    zm = jnp.where(row == 0, 0.0, pltpu.roll(z, 1, 0))
    zp = jnp.where(row == n - 1, 0.0, pltpu.roll(z, n - 1, 0))
    return zm * w[0:1] + z * w[1:2] + zp * w[2:3]


CTX_NSEQ = 4


def _log_gamma(dec):
    return jnp.log1p(-jnp.exp(dec))


def _even_kernel(*refs, seq, latent, tq, heads, nseq):
    (q_ref, k_ref, v_ref, g_ref, gb_ref, gc_ref, hx_ref, scw_ref, df_ref, db_ref) = refs[:10]
    if latent:
        s0_ref, ret_ref, sc_ref, mask_ref = refs[10:]
    else:
        ret_ref, sc_ref, st_ref, mask_ref = refs[10:]

    decays = []
    for hh in range(heads):
        lgf = _log_gamma(df_ref[hh])
        lgb = _log_gamma(db_ref[hh])
        decays.append((lgf, lgb))

        @pl.when(pl.program_id(1) == 0)
        def _():
            for r in range(seq // tq):
                t = lax.broadcasted_iota(I32, (tq, seq), 0) + r * tq
                s = lax.broadcasted_iota(I32, (tq, seq), 1)
                d = (t - s).astype(F32)
                mf = jnp.where(d >= 0, jnp.exp(jnp.maximum(d, 0.0) * lgf), 0.0)
                mb = jnp.where(d <= 0, jnp.exp(jnp.maximum(-d, 0.0) * lgb), 0.0)
                mask_ref[hh, r * tq:(r + 1) * tq, :] = mf + mb

    for sq in range(nseq):
        whole = slice(sq * seq, (sq + 1) * seq)
        for hh in range(heads):
            lanes = slice(hh * RET_DK, (hh + 1) * RET_DK)
            lgf, lgb = decays[hh]
            k = k_ref[whole, lanes] * (RET_DK ** -0.5)
            kb = k.astype(BF16)
            vb = v_ref[whole, lanes].astype(BF16)
            if latent:
                s0f = s0_ref[sq, 0, 0, hh].astype(BF16)
                s0b = s0_ref[sq, 0, 1, hh].astype(BF16)
            for r in range(seq // tq):
                rows = slice(sq * seq + r * tq, sq * seq + (r + 1) * tq)
                qb = q_ref[rows, lanes].astype(BF16)
                s = lax.dot_general(qb, kb, (((1,), (1,)), ((), ())), preferred_element_type=F32)
                p = (s * mask_ref[hh, r * tq:(r + 1) * tq, :]).astype(BF16)
                o = jnp.dot(p, vb, preferred_element_type=F32)
                if latent:
                    tpos = (lax.broadcasted_iota(I32, (tq, 1), 0) + r * tq).astype(F32)
                    o = o + jnp.dot(qb, s0f, preferred_element_type=F32) * jnp.exp((tpos + 1.0) * lgf)
                    o = o + jnp.dot(qb, s0b, preferred_element_type=F32) * jnp.exp((seq - tpos) * lgb)
                mu = jnp.mean(o, axis=-1, keepdims=True)
                var = jnp.mean(jnp.square(o - mu), axis=-1, keepdims=True)
                on = (o - mu) * lax.rsqrt(var + GN_EPS)
                ret_ref[rows, lanes] = _silu(g_ref[rows, lanes]) * on

            if not latent:
                spos = lax.broadcasted_iota(I32, (seq, 1), 0).astype(F32)
                kf = (k * jnp.exp((seq - 1.0 - spos) * lgf)).T.astype(BF16)
                kr = (k * jnp.exp(spos * lgb)).T.astype(BF16)
                st_ref[sq, 0, 0, hh] = jnp.dot(kf, vb, preferred_element_type=F32)
                st_ref[sq, 0, 1, hh] = jnp.dot(kr, vb, preferred_element_type=F32)

        sc_ref[whole, :] = gb_ref[whole, :] * _conv3_rows(gc_ref[whole, :] * hx_ref[whole, :], scw_ref[...])


def _even_mixer(proj, sc_w_pad, dec_f, dec_b, state=None):
    latent = state is not None
    seq = DEC_SEQ if latent else SEQ
    nb = DEC_BATCH if latent else BATCH
    tq = min(seq, 256)
    heads = 2 if latent else RET_HEADS
    nseq = 1 if latent else CTX_NSEQ
    rows = nseq * seq
    row0 = 0 if latent else N_SAMPLE // rows
    steps = RET_HEADS // heads
    hw = heads * RET_DK

    def col(c0):
        return pl.BlockSpec((rows, hw), lambda h, b: (row0 + b, c0 * steps + h))

    in_specs = [col(0), col(1), col(2), col(3), col(4), col(5), col(6),
                pl.BlockSpec((SUBLANES, hw), lambda h, b: (0, h)),
                pl.BlockSpec((heads, 1, 1), lambda h, b: (h, 0, 0)),
                pl.BlockSpec((heads, 1, 1), lambda h, b: (h, 0, 0))]
    args = [proj] * 7 + [sc_w_pad, dec_f.reshape(RET_HEADS, 1, 1), dec_b.reshape(RET_HEADS, 1, 1)]
    out_spec = pl.BlockSpec((rows, hw), lambda h, b: (b, h))
    out_shape = [jax.ShapeDtypeStruct((nb * seq, RET_W), F32), jax.ShapeDtypeStruct((nb * seq, SC_W), F32)]
    out_specs = [out_spec, out_spec]
    st_spec = pl.BlockSpec((nseq, 1, 2, heads, RET_DK, RET_DV), lambda h, b: (b, 0, 0, h, 0, 0))
    if latent:
        in_specs.append(st_spec)
        args.append(state)
    else:
        out_shape.append(jax.ShapeDtypeStruct((BATCH, 1, 2, RET_HEADS, RET_DK, RET_DV), F32))
        out_specs.append(st_spec)
    return pl.pallas_call(
        functools.partial(_even_kernel, seq=seq, latent=latent, tq=tq, heads=heads, nseq=nseq),
        grid=(steps, nb // nseq),
        in_specs=in_specs,
        out_specs=out_specs,
        out_shape=out_shape,
        scratch_shapes=[pltpu.VMEM((heads, seq, seq), F32)],
        compiler_params=_cparams(("arbitrary", "arbitrary")),
        name="even_mixer_latent" if latent else "even_mixer_context",
    )(*args)


def _dft_tables(length):
    f = np.arange(length, dtype=np.float64)[:, None]
    t = np.arange(length, dtype=np.float64)[None, :]
    ang = np.pi * ((f * t) % (2 * length)) / length
    cos, sin = np.cos(ang), np.sin(ang)
    sin[0, :] = (-1.0) ** np.arange(length)
    fwd = np.concatenate([cos, sin], axis=0)
    wgt = np.full((2 * length,), 2.0)
    wgt[0] = 1.0
    wgt[length] = 1.0
    inv = (fwd * wgt[:, None] / (2.0 * length)).T
    return fwd.astype(np.float32), np.ascontiguousarray(inv).astype(np.float32)


def _hyena_consts(length):
    t = np.linspace(0.0, 1.0, length)[:, None]
    bands = (HY_EMB - 1) // 2
    w = 2.0 * np.pi * np.arange(length)[:, None] / length
    f = np.linspace(1e-4, bands - 1, bands)[None, :]
    z = np.concatenate([t, np.cos(f * w), -np.sin(f * w)], axis=-1)
    zpad = np.zeros((length, LANES))
    zpad[:, :HY_EMB] = z
    deltas = np.abs(np.linspace(math.log(HY_TARGET) / HY_SLOW_PCT, math.log(HY_TARGET) / HY_FAST_PCT, HY_W))
    decay = np.exp(-t * deltas[None, :])
    return zpad.astype(np.float32), decay.astype(np.float32)


def _split_kernel(x_ref, hi_ref, lo_ref):
    hi, lo = _split_bf16(x_ref[...])
    hi_ref[...] = hi
    lo_ref[...] = lo


def _split_table(x, name):
    rows = min(x.shape[0], 1024)
    spec = pl.BlockSpec((rows, x.shape[1]), lambda i: (i, 0))
    out = jax.ShapeDtypeStruct(x.shape, BF16)
    return pl.pallas_call(
        _split_kernel,
        grid=(x.shape[0] // rows,),
        in_specs=[spec],
        out_specs=[spec, spec],
        out_shape=[out, out],
        compiler_params=_cparams(("parallel",)),
        name=name,
    )(x)


def _dot_3pass_split(ah, al, b):
    bh, bl = _split_bf16(b)
    dot = functools.partial(jnp.dot, preferred_element_type=F32)
    return dot(ah, bh) + (dot(ah, bl) + dot(al, bh))


def _hy_filter_kernel(z_ref, dec_ref, w1_ref, b1_ref, f1_ref, w2_ref, b2_ref, f2_ref, w3f_ref, w3b_ref,
                      fwd_ref, fwd_lo_ref, ka_ref, ka2_ref, kb_ref, *, seq):
    h = jnp.sin(f1_ref[...] * (jnp.dot(z_ref[...], w1_ref[...], preferred_element_type=F32,
                                       precision=HIGHEST) + b1_ref[...]))
    h = jnp.sin(f2_ref[...] * (jnp.dot(h, w2_ref[...], preferred_element_type=F32,
                                       precision=HIGHEST) + b2_ref[...]))
    dec = dec_ref[...]
    hf = jnp.dot(h, w3f_ref[...], preferred_element_type=F32, precision=HIGHEST) * dec
    hb = jnp.dot(h, w3b_ref[...], preferred_element_type=F32, precision=HIGHEST) * dec
    row = lax.broadcasted_iota(I32, hb.shape, 0)
    hb = jnp.where(row == 0, 0.0, hb)
    inv_norm = 1.0 / (jnp.sum(jnp.abs(hf), axis=0, keepdims=True)
                      + jnp.sum(jnp.abs(hb), axis=0, keepdims=True) + EPS)
    p = (hf + hb) * inv_norm
    q = (hf - hb) * inv_norm
    fp = _dot_3pass_split(fwd_ref[...], fwd_lo_ref[...], p)
    fq = _dot_3pass_split(fwd_ref[seq:, :], fwd_lo_ref[seq:, :], q)
    kr = fp[:seq]
    row = lax.broadcasted_iota(I32, kr.shape, 0)
    ka_ref[0] = kr
    ka2_ref[0] = jnp.where(row == 0, fp[seq:seq + 1], kr)
    kb_ref[0] = jnp.where(row == 0, 0.0, -fq)


def _hy_filters(seq, zfeat, decay, fwd_hi, fwd_lo, w1, b1, f1, w2, b2, f2, w3):
    ns = HY_W // HY_SLAB
    per_o = 2 * ns
    small = lambda shape: pl.BlockSpec(shape, lambda o, s: (0, 0))
    out_spec = pl.BlockSpec((1, seq, HY_SLAB), lambda o, s: (o, 0, s))
    out = jax.ShapeDtypeStruct((HY_ORDER, seq, HY_W), F32)
    return pl.pallas_call(
        functools.partial(_hy_filter_kernel, seq=seq),
        grid=(HY_ORDER, ns),
        in_specs=[
            small((seq, LANES)),
            pl.BlockSpec((seq, HY_SLAB), lambda o, s: (0, s)),
            small((LANES, LANES)), small((1, LANES)), small((1, LANES)),
            small((LANES, LANES)), small((1, LANES)), small((1, LANES)),
            pl.BlockSpec((LANES, HY_SLAB), lambda o, s: (0, o * per_o + s)),
            pl.BlockSpec((LANES, HY_SLAB), lambda o, s: (0, o * per_o + ns + s)),
            small((2 * seq, seq)), small((2 * seq, seq)),
        ],
        out_specs=[out_spec, out_spec, out_spec],
        out_shape=[out, out, out],
        compiler_params=_cparams(("parallel", "parallel")),
        name=f"hyena_filters_{seq}",
    )(zfeat, decay, w1, b1, f1, w2, b2, f2, w3, w3, fwd_hi, fwd_lo)


def _hy_apply_kernel(x1_ref, x2_ref, v_ref, c1_ref, c2_ref, cv_ref, fwd_ref, inv_ref,
                     ka_ref, ka2_ref, kb_ref, bias_ref, o_ref, *, seq, nseq):
    def long_conv(sig, o):
        spec = jnp.dot(fwd_ref[...], sig.astype(BF16), preferred_element_type=F32)
        xr, xs = spec[:seq], spec[seq:]
        kb = kb_ref[o]
        yr = (xr * ka_ref[o] + xs * kb).astype(BF16)
        ys = (xs * ka2_ref[o] - xr * kb).astype(BF16)
        y = jnp.dot(inv_ref[:, :seq], yr, preferred_element_type=F32)
        y = y + jnp.dot(inv_ref[:, seq:], ys, preferred_element_type=F32)
        return y + sig * bias_ref[o:o + 1]

    for i in range(nseq):
        rows = slice(i * seq, (i + 1) * seq)
        x1 = _conv3_rows(x1_ref[rows, :], c1_ref[...])
        x2 = _conv3_rows(x2_ref[rows, :], c2_ref[...])
        v = _conv3_rows(v_ref[rows, :], cv_ref[...])
        z = x1 * long_conv(v, 0)
        o_ref[rows, :] = x2 * long_conv(z, 1)


HY_NSEQ = {True: 1, False: 4}


def _hy_apply(proj, conv_w_pad, fwd_bf, inv_bf, filters, bias_pad, latent):
    seq = DEC_SEQ if latent else SEQ
    nb = DEC_BATCH if latent else BATCH
    nseq = HY_NSEQ[latent]
    rows = nseq * seq
    row0 = 0 if latent else N_SAMPLE // rows
    ns = HY_W // HY_SLAB
    once = pl.Buffered(1)
    const = lambda shape: pl.BlockSpec(shape, lambda s, b: (0,) * len(shape), pipeline_mode=once)
    col = lambda c0: pl.BlockSpec((rows, HY_SLAB), lambda s, b: (row0 + b, c0 + s))
    cw = lambda c0: pl.BlockSpec((SUBLANES, HY_SLAB), lambda s, b: (0, c0 + s))
    filt = pl.BlockSpec((HY_ORDER, seq, HY_SLAB), lambda s, b: (0, 0, s), pipeline_mode=once)
    in_specs = [col(0), col(ns), col(2 * ns), cw(0), cw(ns), cw(2 * ns),
                const((2 * seq, seq)), const((seq, 2 * seq)), filt, filt, filt,
                pl.BlockSpec((SUBLANES, HY_SLAB), lambda s, b: (0, s))]
    args = [proj] * 3 + [conv_w_pad] * 3 + [fwd_bf, inv_bf, *filters, bias_pad]
    return pl.pallas_call(
        functools.partial(_hy_apply_kernel, seq=seq, nseq=nseq),
        grid=(ns, nb // nseq),
        in_specs=in_specs,
        out_specs=pl.BlockSpec((rows, HY_SLAB), lambda s, b: (b, s)),
        out_shape=jax.ShapeDtypeStruct((nb * seq, HY_W), F32),
        compiler_params=_cparams(("arbitrary", "arbitrary")),
        name="hyena_latent" if latent else "hyena_context",
    )(*args)


def _rope_tables(length):
    rows = length // GRID_W
    row = np.repeat(np.arange(rows, dtype=np.float64), GRID_W)
    colp = np.tile(np.arange(GRID_W, dtype=np.float64), rows)
    half = DA_DH // 2
    freqs = ROPE_THETA ** (-np.arange(0, half, 2, dtype=np.float64) / half)
    ang = np.concatenate([row[:, None] * freqs[None, :]] * 2 + [colp[:, None] * freqs[None, :]] * 2, axis=1)
    ang = np.concatenate([ang, ang], axis=1)
    lane = np.arange(LANES)
    sign = np.where((lane % half) < half // 2, -1.0, 1.0)[None, :]
    return np.cos(ang).astype(np.float32), (np.sin(ang) * sign).astype(np.float32)


def _rope(x, cos, sin_signed):
    q = DA_DH // 4
    lane = lax.broadcasted_iota(I32, x.shape, 1)
    partner = jnp.where((lane % (2 * q)) < q, pltpu.roll(x, LANES - q, 1), pltpu.roll(x, q, 1))
    return x * cos + partner * sin_signed


def _attn_kernel(*refs, seq, latent, lambda_init, heads, nseq):
    q_ref, k_ref, v_ref, lam_ref, sub_ref = refs[:5]
    if latent:
        cos_ref, sin_ref, ck_ref, cv_ref, o_ref = refs[5:]
    else:
        o_ref, nk_ref, nv_ref = refs[5:]
    hw = 2 * DA_DH
    lv = lam_ref[...]
    lam = (jnp.exp(jnp.sum(lv[0:1] * lv[1:2], axis=1, keepdims=True))
           - jnp.exp(jnp.sum(lv[2:3] * lv[3:4], axis=1, keepdims=True)) + lambda_init)
    scale = DA_DH ** -0.5
    dn = (((1,), (1,)), ((), ()))
    tq = min(seq, ATT_TQ)
    for sq, hh in [(sq, hh) for sq in range(nseq) for hh in range(heads)]:
        whole = slice(sq * seq, (sq + 1) * seq)
        lanes = slice(hh * hw, (hh + 1) * hw)
        k = k_ref[whole, lanes]
        v = v_ref[whole, lanes]
        if latent:
            k = _rope(k, cos_ref[...], sin_ref[...])
            k_all = jnp.concatenate([k, jnp.concatenate([ck_ref[sq, 0, 0, hh], ck_ref[sq, 0, 1, hh]], axis=1)], axis=0)
            v_all = jnp.concatenate([v, cv_ref[sq, 0, hh]], axis=0)
        else:
            k_all, v_all = k, v
            nk_ref[sq, 0, 0, hh] = k[:, :DA_DH]
            nk_ref[sq, 0, 1, hh] = k[:, DA_DH:]
            nv_ref[sq, 0, hh] = v
        kb = k_all.astype(BF16)
        vb = v_all.astype(BF16)
        for r in range(seq // tq):
            rows = slice(sq * seq + r * tq, sq * seq + (r + 1) * tq)
            q = q_ref[rows, lanes]
            if latent:
                q = _rope(q, cos_ref[r * tq:(r + 1) * tq, :], sin_ref[r * tq:(r + 1) * tq, :])
            q = q * scale
            first = lax.broadcasted_iota(I32, q.shape, 1) < DA_DH
            q1 = jnp.where(first, q, 0.0).astype(BF16)
            q2 = jnp.where(first, 0.0, q).astype(BF16)
            s1 = lax.dot_general(q1, kb, dn, preferred_element_type=F32)
            s2 = lax.dot_general(q2, kb, dn, preferred_element_type=F32)
            e1 = jnp.exp(s1 - jnp.max(s1, axis=-1, keepdims=True))
            e2 = jnp.exp(s2 - jnp.max(s2, axis=-1, keepdims=True))
            o1 = jnp.dot(e1.astype(BF16), vb, preferred_element_type=F32)
            o2 = jnp.dot(e2.astype(BF16), vb, preferred_element_type=F32)
            o = (o1 * (1.0 / jnp.sum(e1, axis=-1, keepdims=True))
                 - o2 * (lam / jnp.sum(e2, axis=-1, keepdims=True)))
            o = o * lax.rsqrt(jnp.mean(o * o, axis=-1, keepdims=True) + EPS) * sub_ref[...]
            o_ref[rows, lanes] = o * (1.0 - lambda_init)


def _attention(proj, lam_vec, subln, lambda_init, latent, rope=None, ctx_k=None, ctx_v=None):
    seq = DEC_SEQ if latent else SEQ
    nb = DEC_BATCH if latent else BATCH
    hw = 2 * DA_DH
    heads = 2 if latent else DA_HEADS
    nseq = 1 if latent else CTX_NSEQ
    rows = nseq * seq
    row0 = 0 if latent else N_SAMPLE // rows
    steps = DA_HEADS // heads
    col = lambda c: pl.BlockSpec((rows, heads * hw), lambda b, h: (row0 + b, c * steps + h))
    in_specs = [col(3), col(4), col(5),
                pl.BlockSpec((4, DA_DH), lambda b, h: (0, 0)),
                pl.BlockSpec((1, hw), lambda b, h: (0, 0))]
    args = [proj] * 3 + [lam_vec, subln.reshape(1, hw)]
    out_spec = pl.BlockSpec((rows, heads * hw), lambda b, h: (b, h))
    out_shape = [jax.ShapeDtypeStruct((nb * seq, DA_W), F32)]
    out_specs = [out_spec]
    k_spec = pl.BlockSpec((nseq, 1, 2, heads, SEQ, DA_DH), lambda b, h: (b, 0, 0, h, 0, 0))
    v_spec = pl.BlockSpec((nseq, 1, heads, SEQ, hw), lambda b, h: (b, 0, h, 0, 0))
    if latent:
        in_specs += [pl.BlockSpec((seq, hw), lambda b, h: (0, 0))] * 2 + [k_spec, v_spec]
        args += [rope[0], rope[1], ctx_k, ctx_v]
    else:
        out_shape += [jax.ShapeDtypeStruct((BATCH, 1, 2, DA_HEADS, SEQ, DA_DH), F32),
                      jax.ShapeDtypeStruct((BATCH, 1, DA_HEADS, SEQ, hw), F32)]
        out_specs += [k_spec, v_spec]
    return pl.pallas_call(
        functools.partial(_attn_kernel, seq=seq, latent=latent, lambda_init=lambda_init, heads=heads, nseq=nseq),
        grid=(nb // nseq, steps),
        in_specs=in_specs,
        out_specs=out_specs,
        out_shape=out_shape,
        compiler_params=_cparams(("arbitrary", "arbitrary")),
        name="diff_attn_latent" if latent else "diff_attn_context",
    )(*args)


HALF_D = D_MODEL // 2
HI16 = -65536


def _pack_bf16_pair(lo, hi):
    lo_bits = lax.bitcast_convert_type(lo.astype(BF16).astype(F32), I32)
    hi_bits = lax.bitcast_convert_type(hi.astype(BF16).astype(F32), I32)
    return (hi_bits & HI16) | lax.shift_right_logical(lo_bits, 16)


def _pack_exact_bf16_pair(lo, hi):
    return ((lax.bitcast_convert_type(hi, I32) & HI16)
            | lax.shift_right_logical(lax.bitcast_convert_type(lo, I32), 16))


def _unpack_bf16_pair(p):
    lo = lax.bitcast_convert_type(lax.shift_left(p, 16), F32).astype(BF16)
    hi = lax.bitcast_convert_type(p & HI16, F32).astype(BF16)
    return lo, hi


GRP_EXPERT, GRP_REL, GRP_COUNT = range(3)


def _route_rows(x, g, mod, wr, br, tril_ref, triu_ref, carry_ref):
    u = _norm_mod(x, g, mod, 3, 4)
    u_packed = _pack_bf16_pair(u[:, :HALF_D], u[:, HALF_D:])
    logits = _dot_3pass(u, wr) + br
    lane = lax.broadcasted_iota(I32, logits.shape, 1)
    lane_f = lane.astype(F32)
    neg = -jnp.inf
    big = float(LANES)

    def first_max(vals):
        m = jnp.max(vals, axis=-1, keepdims=True)
        return m, jnp.min(jnp.where(vals == m, lane_f, big), axis=-1, keepdims=True)

    is_grp = lane < N_GROUPS
    gmax, gidx = first_max(jnp.where(is_grp, logits, neg))
    gsum = jnp.sum(jnp.where(is_grp, jnp.exp(logits - gmax), 0.0), axis=-1, keepdims=True)
    g_w = 1.0 / gsum
    lo = N_GROUPS + EXP_PER_GROUP * gidx
    in_grp = jnp.logical_and(lane_f >= lo, lane_f < lo + EXP_PER_GROUP)
    el = jnp.where(in_grp, logits, neg)
    v1, i1 = first_max(el)
    v2, i2 = first_max(jnp.where(lane_f == i1, neg, el))
    e = jnp.exp(v2 - v1)
    p1 = 1.0 / (1.0 + e)
    hit1 = lane_f == i1 - N_GROUPS
    hit2 = lane_f == i2 - N_GROUPS
    onehot = jnp.where(jnp.logical_or(hit1, hit2), 1.0, 0.0)
    cum = jnp.dot(tril_ref[...], onehot.astype(BF16), preferred_element_type=F32)
    n = cum[TM - 1:TM, :]
    n_pad = jnp.floor((n + (ROW_ALIGN - 1.0)) * (1.0 / ROW_ALIGN)) * ROW_ALIGN
    loc_off = jnp.dot(jnp.broadcast_to(n_pad, (SUBLANES, LANES)).astype(BF16), triu_ref[...],
                      preferred_element_type=F32)[0:1]
    base = cum - onehot + loc_off
    loc1 = jnp.sum(jnp.where(hit1, base, 0.0), axis=-1, keepdims=True)
    loc2 = jnp.sum(jnp.where(hit2, base, 0.0), axis=-1, keepdims=True)
    info = jnp.where(lane == 0, g_w * p1, jnp.where(lane == 1, g_w * (e * p1), jnp.where(lane == 2, loc1, loc2)))
    start8 = lax.broadcasted_iota(I32, (LANES, LANES), 0).astype(F32) * ROW_ALIGN
    owner = jnp.where(jnp.logical_and(start8 >= loc_off, start8 < loc_off + n_pad), 1.0, 0.0)
    lane128 = lax.broadcasted_iota(I32, (LANES, LANES), 1)
    c_exp = jnp.sum(owner * lane128.astype(F32), axis=-1, keepdims=True)
    c_rel = jnp.sum(owner * (carry_ref[...] + start8 - loc_off), axis=-1, keepdims=True)
    chunks = jnp.where(lane128 == 0, c_exp, jnp.where(lane128 == 1, c_rel, 0.0)).T
    n_chunks = jnp.sum(n_pad, axis=-1, keepdims=True) * (1.0 / ROW_ALIGN)
    row = lax.broadcasted_iota(I32, (SUBLANES, LANES), 0)
    grp = jnp.where(row == GRP_EXPERT, chunks[0:1], jnp.where(row == GRP_REL, chunks[1:2], n_chunks))
    carry_ref[...] = carry_ref[...] + n_pad
    return u_packed, info, info.T[:SUBLANES], grp.astype(I32)


def _proj_res_router_kernel(*refs, n_x, gate_idx):
    (al_ref, ac_ref, bl_ref, bc_ref, wa_ref, wb_ref, mod_ref, g2_ref, wr_ref, br_ref, tril_ref, triu_ref,
     o_ref, u_ref, loct_ref, info_ref, grp_ref, tot_ref, carry_ref) = refs[n_x:]

    @pl.when(pl.program_id(0) == 0)
    def _():
        carry_ref[...] = jnp.zeros_like(carry_ref)

    a = _group_pick(al_ref, ac_ref)
    b = _group_pick(bl_ref, bc_ref)
    y = jnp.dot(a.astype(BF16), wa_ref[...], preferred_element_type=F32)
    y = y + jnp.dot(b.astype(BF16), wb_ref[...], preferred_element_type=F32)
    mod = mod_ref[0]
    x_new = _rows_value(refs[:n_x]) + mod[gate_idx:gate_idx + 1] * y
    o_ref[...] = x_new
    for t in range(TM_WIDE // TM):
        rows = slice(t * TM, (t + 1) * TM)
        u_packed, info, info_t, grp = _route_rows(x_new[rows], g2_ref[...], mod, wr_ref[...], br_ref[...],
                                                 tril_ref, triu_ref, carry_ref)
        u_ref[rows, :] = u_packed
        info_ref[rows, :] = info
        loct_ref[t] = info_t
        grp_ref[t] = grp
    tot_ref[...] = jnp.broadcast_to(carry_ref[...], tot_ref.shape).astype(I32)


def _proj_residual_router(x, a, b, w_bf16, mods, gate_idx, norm2, w_route, b_route, tril, triu):
    half = a[0].shape[1]
    tm = TM_WIDE
    sub = tm // TM
    nt = N_TOK // TM
    x_specs, x_args = _rows_specs(x, tm)
    const = lambda shape: pl.BlockSpec(shape, lambda i: (0,) * len(shape))
    return pl.pallas_call(
        functools.partial(_proj_res_router_kernel, n_x=len(x_args), gate_idx=gate_idx),
        grid=(N_TOK // tm,),
        in_specs=x_specs + _group_specs(half, tm) + _group_specs(half, tm) + [
            pl.BlockSpec((half, D_MODEL), lambda i: (0, 0)),
            pl.BlockSpec((half, D_MODEL), lambda i: (1, 0)),
            pl.BlockSpec((1, MOD_ROWS, D_MODEL), lambda i: (i * tm // SEG, 0, 0)),
            const((1, D_MODEL)), const((D_MODEL, LANES)), const((1, LANES)), const((TM, TM)), const((LANES, LANES)),
        ],
        out_specs=[
            pl.BlockSpec((tm, D_MODEL), lambda i: (i, 0)),
            pl.BlockSpec((tm, HALF_D), lambda i: (i, 0)),
            pl.BlockSpec((sub, SUBLANES, TM), lambda i: (i, 0, 0)),
            pl.BlockSpec((tm, LANES), lambda i: (i, 0)),
            pl.BlockSpec((sub, SUBLANES, LANES), lambda i: (i, 0, 0)),
            pl.BlockSpec((SUBLANES, LANES), lambda i: (0, 0)),
        ],
        out_shape=[
            jax.ShapeDtypeStruct((N_TOK, D_MODEL), F32),
            jax.ShapeDtypeStruct((N_TOK, HALF_D), I32),
            jax.ShapeDtypeStruct((nt, SUBLANES, TM), F32),
            jax.ShapeDtypeStruct((N_TOK, LANES), F32),
            jax.ShapeDtypeStruct((nt, SUBLANES, LANES), I32),
            jax.ShapeDtypeStruct((SUBLANES, LANES), I32),
        ],
        scratch_shapes=[pltpu.VMEM((1, LANES), F32)],
        compiler_params=_cparams(("arbitrary",)),
        name="outproj_residual_router",
    )(*x_args, a[0], a[1], b[0], b[1], w_bf16, w_bf16, mods, norm2.reshape(1, D_MODEL), w_route, b_route, tril, triu)


def _for_chunks(n, fn):
    for size in CHUNKS:
        @pl.when((n & size) != 0)
        def _():
            fn(n & ~(2 * size - 1), size)


N_COPIES = R_LOC // ROW_ALIGN
DUMP_ROWS = 2 * R_LOC


def _start_tile_copies(grp_ref, seg_ref, make, dummy, n=None):
    if n is None:
        n = grp_ref[0, GRP_COUNT, 0]
    for c in range(N_COPIES):
        real = c < n
        slot = seg_ref[0, grp_ref[0, GRP_EXPERT, c]] + grp_ref[0, GRP_REL, c]
        d_loc, d_slot = dummy(c)
        loc = jnp.where(real, c * ROW_ALIGN, d_loc)
        slot = jnp.where(real, slot, d_slot)
        make(pl.ds(pl.multiple_of(loc, ROW_ALIGN), ROW_ALIGN), pl.ds(pl.multiple_of(slot, ROW_ALIGN), ROW_ALIGN)).start()


def _wait_tile_copies(make):
    make(pl.ds(0, R_LOC), pl.ds(0, R_LOC)).wait()


def _all_experts(body):
    lax.fori_loop(0, N_EXPERTS, lambda e, c: (body(e), c)[1], 0)


def _dispatch_kernel(grp_ref, seg_ref, loct_ref, u_ref, xb_ref, buf, zbuf, sem):
    i = pl.program_id(0)
    last = pl.num_programs(0) - 1
    cur = i % 2

    def maker(b):
        return lambda loc, slot: pltpu.make_async_copy(buf.at[b, loc], xb_ref.at[slot], sem.at[b])

    @pl.when(i >= 2)
    def _():
        _wait_tile_copies(maker(cur))

    lo, hi = _unpack_bf16_pair(u_ref[...])
    row = lax.broadcasted_iota(I32, (R_LOC, TM), 0).astype(F32)
    sel = jnp.logical_or(row == loct_ref[0, 2:3, :], row == loct_ref[0, 3:4, :])
    pt = jnp.where(sel, 1.0, 0.0).astype(BF16)
    buf[cur] = _pack_exact_bf16_pair(jnp.dot(pt, lo, preferred_element_type=F32),
                                     jnp.dot(pt, hi, preferred_element_type=F32))
    _start_tile_copies(grp_ref, seg_ref, maker(cur),
                       lambda c: (c * ROW_ALIGN, MOE_ROWS + cur * R_LOC + c * ROW_ALIGN))

    @pl.when(i == last)
    def _():
        @pl.when(i >= 1)
        def _():
            _wait_tile_copies(maker(1 - cur))

        _wait_tile_copies(maker(cur))

        zbuf[...] = jnp.zeros_like(zbuf)

        def tail(e, act):
            total = seg_ref[1, e]
            dst = seg_ref[0, e] + total

            def one(off, size):
                act(pltpu.make_async_copy(zbuf.at[pl.ds(0, size)],
                                          xb_ref.at[pl.ds(pl.multiple_of(dst + off, ROW_ALIGN), size)], sem.at[2]))

            _for_chunks((-total) & (MOE_BLK - 1), one)

        _all_experts(lambda e: tail(e, lambda cp: cp.start()))
        _all_experts(lambda e: tail(e, lambda cp: cp.wait()))

        e_last = N_EXPERTS - 1
        used = (seg_ref[0, e_last] + seg_ref[1, e_last] + MOE_BLK - 1) // MOE_BLK

        def spare(j):
            return pltpu.make_async_copy(zbuf, xb_ref.at[pl.ds(pl.multiple_of(j * MOE_BLK, MOE_BLK), MOE_BLK)],
                                         sem.at[2])

        lax.fori_loop(used, (MOE_ROWS + DUMP_ROWS) // MOE_BLK, lambda j, c: (spare(j).start(), c)[1], 0)
        lax.fori_loop(used, (MOE_ROWS + DUMP_ROWS) // MOE_BLK, lambda j, c: (spare(j).wait(), c)[1], 0)


def _dispatch(grp, seg, loct, u):
    return pl.pallas_call(
        _dispatch_kernel,
        grid=(N_TOK // TM,),
        in_specs=[
            pl.BlockSpec((1, SUBLANES, LANES), lambda i: (i, 0, 0), memory_space=pltpu.SMEM),
            pl.BlockSpec(memory_space=pltpu.SMEM),
            pl.BlockSpec((1, SUBLANES, TM), lambda i: (i, 0, 0)),
            pl.BlockSpec((TM, HALF_D), lambda i: (i, 0)),
        ],
        out_specs=pl.BlockSpec(memory_space=pl.ANY),
        out_shape=jax.ShapeDtypeStruct((MOE_ROWS + DUMP_ROWS, HALF_D), I32),
        scratch_shapes=[pltpu.VMEM((2, R_LOC, HALF_D), I32), pltpu.VMEM((MOE_BLK, HALF_D), I32),
                        pltpu.SemaphoreType.DMA((3,))],
        compiler_params=_cparams(("arbitrary",)),
        name="moe_dispatch",
    )(grp, seg, loct, u)


def _expert_kernel(seg_ref, w1_ref, w3_ref, w2_ref, xb_ref, yb_ref, w1b, w3b, w2b, xbuf, ybuf, xsem, ysem):
    e = pl.program_id(0)
    last_e = pl.num_programs(0) - 1
    blk0 = seg_ref[0, e] // MOE_BLK
    nblk = (seg_ref[1, e] + MOE_BLK - 1) // MOE_BLK
    used = (seg_ref[0, N_EXPERTS - 1] + seg_ref[1, N_EXPERTS - 1] + MOE_BLK - 1) // MOE_BLK

    def rows(g):
        return pl.ds(pl.multiple_of(g * MOE_BLK, MOE_BLK), MOE_BLK)

    def fetch(g, b):
        return pltpu.make_async_copy(xb_ref.at[rows(g)], xbuf.at[b], xsem.at[b])

    def flush(g, b):
        return pltpu.make_async_copy(ybuf.at[b], yb_ref.at[rows(g)], ysem.at[b])

    row_queue = 1

    @pl.when(e == 0)
    def _():
        fetch(0, 0).start(priority=row_queue)

    @pl.when(nblk > 0)
    def _():
        w1b[...] = w1_ref[0, 0].astype(BF16)
        w3b[...] = w3_ref[0, 0].astype(BF16)
        w2b[...] = w2_ref[0, 0].astype(BF16)

        def block(i, carry):
            g = blk0 + i
            b = g % 2
            fetch(g, b).wait()

            @pl.when(g + 1 < used)
            def _():
                fetch(g + 1, 1 - b).start(priority=row_queue)

            @pl.when(g >= 2)
            def _():
                flush(g - 2, b).wait()

            def ffn(x_packed):
                lo, hi = _unpack_bf16_pair(x_packed)

                def up(wb):
                    return (jnp.dot(lo, wb[:HALF_D, :], preferred_element_type=F32)
                            + jnp.dot(hi, wb[HALF_D:, :], preferred_element_type=F32))

                hdn = (_silu(up(w1b)) * up(w3b)).astype(BF16)
                y = jnp.dot(hdn, w2b[...], preferred_element_type=F32)
                return _pack_bf16_pair(y[:, :HALF_D], y[:, HALF_D:])

            half = MOE_BLK // 2
            slots_here = seg_ref[1, e] - i * MOE_BLK

            @pl.when(slots_here > half)
            def _():
                ybuf[b] = ffn(xbuf[b])

            @pl.when(slots_here <= half)
            def _():
                ybuf[b, :half, :] = ffn(xbuf[b, :half, :])
                ybuf[b, half:, :] = jnp.zeros((half, HALF_D), I32)

            flush(g, b).start(priority=row_queue)
            return carry

        lax.fori_loop(0, nblk, block, 0)

    @pl.when(e == last_e)
    def _():
        @pl.when(used >= 2)
        def _():
            flush(used - 2, used % 2).wait()

        flush(used - 1, (used - 1) % 2).wait()
        xbuf[0] = jnp.zeros((MOE_BLK, HALF_D), I32)

        def spare(g):
            return pltpu.make_async_copy(xbuf.at[0], yb_ref.at[rows(g)], xsem.at[0])

        lax.fori_loop(used, MOE_NB, lambda g, c: (spare(g).start(), c)[1], 0)
        lax.fori_loop(used, MOE_NB, lambda g, c: (spare(g).wait(), c)[1], 0)


def _experts(seg, xb, w1, w3, w2, layer):
    hbm = pl.BlockSpec(memory_space=pl.ANY)
    w_in = pl.BlockSpec((1, 1, D_MODEL, D_EXPERT), lambda e, s: (layer, e, 0, 0))
    w_out = pl.BlockSpec((1, 1, D_EXPERT, D_MODEL), lambda e, s: (layer, e, 0, 0))
    return pl.pallas_call(
        _expert_kernel,
        grid_spec=pltpu.PrefetchScalarGridSpec(
            num_scalar_prefetch=1,
            grid=(N_EXPERTS,),
            in_specs=[w_in, w_in, w_out, hbm],
            out_specs=hbm,
            scratch_shapes=[pltpu.VMEM((D_MODEL, D_EXPERT), BF16), pltpu.VMEM((D_MODEL, D_EXPERT), BF16),
                            pltpu.VMEM((D_EXPERT, D_MODEL), BF16),
                            pltpu.VMEM((2, MOE_BLK, HALF_D), I32), pltpu.VMEM((2, MOE_BLK, HALF_D), I32),
                            pltpu.SemaphoreType.DMA((2,)), pltpu.SemaphoreType.DMA((2,))],
        ),
        out_shape=jax.ShapeDtypeStruct((MOE_ROWS, HALF_D), I32),
        compiler_params=_cparams(("arbitrary",)),
        name="moe_experts",
    )(seg, w1, w3, w2, xb)


def _combine_kernel(*refs, final):
    grp_ref, nxt_ref, seg_ref, x_ref, info_ref, mod_ref, yb_ref = refs[:7]
    if final:
        fin_ref, olat_ref, octx_ref, ybuf, sem = refs[7:]
    else:
        g1n_ref, modn_ref, wn_ref, o_ref, proj_ref, ybuf, sem = refs[7:]
    i = pl.program_id(0)
    cur = i % 2

    def maker(b):
        return lambda loc, slot: pltpu.make_async_copy(yb_ref.at[slot], ybuf.at[b, loc], sem.at[b])

    def dummy(c):
        return R_LOC + c * ROW_ALIGN, c * ROW_ALIGN

    @pl.when(i == 0)
    def _():
        ybuf[...] = jnp.zeros_like(ybuf)
        _start_tile_copies(grp_ref, seg_ref, maker(0), dummy)

    has_next = i + 1 < pl.num_programs(0)
    _start_tile_copies(nxt_ref, seg_ref, maker(1 - cur), dummy, jnp.where(has_next, nxt_ref[0, GRP_COUNT, 0], 0))
    _wait_tile_copies(maker(cur))

    @pl.when(jnp.logical_not(has_next))
    def _():
        _wait_tile_copies(maker(1 - cur))

    info = info_ref[...]
    col = lax.broadcasted_iota(I32, (TM, R_LOC), 1).astype(F32)
    p = jnp.where(col == info[:, 2:3], info[:, 0:1], jnp.where(col == info[:, 3:4], info[:, 1:2], 0.0))
    p = p.astype(BF16)
    lo, hi = _unpack_bf16_pair(ybuf[cur, :R_LOC, :])
    y = jnp.concatenate([jnp.dot(p, lo, preferred_element_type=F32),
                         jnp.dot(p, hi, preferred_element_type=F32)], axis=1)
    out = x_ref[...] + mod_ref[0][5:6] * y
    if not final:
        o_ref[...] = out
        u_next = _norm_mod(out, g1n_ref[...], modn_ref[0], 0, 1)
        proj_ref[...] = jnp.dot(u_next.astype(BF16), wn_ref[...], preferred_element_type=F32)
        return
    out = out * lax.rsqrt(jnp.mean(out * out, axis=-1, keepdims=True) + EPS) * fin_ref[...]

    @pl.when(i < N_LAT_TILES)
    def _():
        olat_ref[...] = out

    @pl.when(i >= N_LAT_TILES)
    def _():
        octx_ref[...] = out


def _combine(x, grp, seg, info, mods, yb, final_g=None, next_proj=None):
    final = final_g is not None
    nt = N_TOK // TM
    in_specs = [
        pl.BlockSpec((1, SUBLANES, LANES), lambda i: (i, 0, 0), memory_space=pltpu.SMEM),
        pl.BlockSpec((1, SUBLANES, LANES), lambda i: (jnp.minimum(i + 1, nt - 1), 0, 0), memory_space=pltpu.SMEM),
        pl.BlockSpec(memory_space=pltpu.SMEM),
        pl.BlockSpec((TM, D_MODEL), lambda i: (i, 0)),
        pl.BlockSpec((TM, LANES), lambda i: (i, 0)),
        pl.BlockSpec((1, MOD_ROWS, D_MODEL), lambda i: (i * TM // SEG, 0, 0)),
        pl.BlockSpec(memory_space=pl.ANY),
    ]
    args = [grp, grp, seg, x, info, mods, yb]
    if final:
        in_specs.append(pl.BlockSpec((1, D_MODEL), lambda i: (0, 0)))
        args.append(final_g.reshape(1, D_MODEL))
        out_specs = _group_specs(D_MODEL)
        out_shape = [jax.ShapeDtypeStruct((N_SAMPLE, D_MODEL), F32), jax.ShapeDtypeStruct((N_PROMPT, D_MODEL), F32)]
    else:
        g1n, modn, wn = next_proj
        n = wn.shape[1]
        in_specs += [pl.BlockSpec((1, D_MODEL), lambda i: (0, 0)),
                     pl.BlockSpec((1, MOD_ROWS, D_MODEL), lambda i: (i * TM // SEG, 0, 0)),
                     pl.BlockSpec((D_MODEL, n), lambda i: (0, 0), pipeline_mode=pl.Buffered(1))]
        args += [g1n.reshape(1, D_MODEL), modn, wn]
        out_specs = [pl.BlockSpec((TM, D_MODEL), lambda i: (i, 0)), pl.BlockSpec((TM, n), lambda i: (i, 0))]
        out_shape = [jax.ShapeDtypeStruct((N_TOK, D_MODEL), F32), jax.ShapeDtypeStruct((N_TOK, n), F32)]
    return pl.pallas_call(
        functools.partial(_combine_kernel, final=final),
        grid=(nt,),
        in_specs=in_specs,
        out_specs=out_specs,
        out_shape=out_shape,
        scratch_shapes=[pltpu.VMEM((2, 2 * R_LOC, HALF_D), I32), pltpu.SemaphoreType.DMA((2,))],
        compiler_params=_cparams(("arbitrary",)),
        name="moe_combine_final" if final else "moe_combine",
    )(*args)


def _route_weights(w_grp, b_grp, w_exp, b_exp):
    w_route = jnp.zeros((D_MODEL, LANES), F32)
    w_route = w_route.at[:, :N_GROUPS].set(w_grp).at[:, N_GROUPS:N_GROUPS + N_EXPERTS].set(w_exp)
    b_route = jnp.zeros((1, LANES), F32)
    b_route = b_route.at[0, :N_GROUPS].set(b_grp).at[0, N_GROUPS:N_GROUPS + N_EXPERTS].set(b_exp)
    return w_route, b_route


def _moe_layer(x, route, layer, mods, w1, w3, w2, final_g=None, next_proj=None):
    u, loct, info, grp, tot = route
    total = tot[0, :N_EXPERTS]
    padded = (total + MOE_BLK - 1) // MOE_BLK * MOE_BLK
    pad_end = jnp.cumsum(padded)
    pad_start = pad_end - padded
    seg = jnp.zeros((2, LANES), I32).at[0, :N_EXPERTS].set(pad_start).at[1, :N_EXPERTS].set(total)
    xb = _dispatch(grp, seg, loct, u)
    yb = _experts(seg, xb, w1, w3, w2, layer)
    return _combine(x, grp, seg, info, mods, yb, final_g, next_proj)


def _pad_rows(w, rows=SUBLANES):
    return jnp.zeros((rows, w.shape[1]), w.dtype).at[:w.shape[0]].set(w)


def _pad2(w, rows, cols):
    return jnp.zeros((rows, cols), w.dtype).at[:w.shape[0], :w.shape[1]].set(w)


def kernel(x_prompt, x_sample, c, c_ctx, state_ret, cache_k, cache_v, ada_w, ada_b, norm1, norm2, final_norm, ev_w_in, ev_w_out, ret_decay_fwd, ret_decay_bwd, sc_conv_w, od_w_in, od_w_out, hy_conv_w, hy_w1, hy_b1, hy_fr1, hy_w2, hy_b2, hy_fr2, hy_w3, hy_bias, da_lambda, da_subln, moe_w_grp, moe_b_grp, moe_w_exp, moe_b_exp, moe_w1, moe_w3, moe_w2):
    x = (x_sample.reshape(N_SAMPLE, D_MODEL), x_prompt.reshape(N_PROMPT, D_MODEL))

    cvecs = jnp.zeros((16, D_MODEL), F32).at[:DEC_BATCH].set(c).at[DEC_BATCH].set(c_ctx)
    m = _mods(cvecs, ada_w, ada_b).reshape(DEPTH, 16, N_MOD, D_MODEL)
    seg_mods = jnp.concatenate(
        [m[:, :DEC_BATCH], jnp.broadcast_to(m[:, DEC_BATCH:DEC_BATCH + 1], (DEPTH, N_SEG - DEC_BATCH, N_MOD, D_MODEL))],
        axis=1)
    seg_mods = jnp.pad(seg_mods, ((0, 0), (0, 0), (0, MOD_ROWS - N_MOD), (0, 0)))

    tril = jnp.asarray(np.tril(np.ones((TM, TM), np.float32)), BF16)
    triu = jnp.asarray(np.triu(np.ones((LANES, LANES), np.float32), 1), BF16)

    def w_in(layer):
        w = ev_w_in if layer % 2 == 0 else od_w_in
        return w[layer // 2].astype(BF16)

    proj = None
    for l in range(DEPTH):
        mods = seg_mods[l]
        i = l // 2
        if l % 2 == 0:
            if proj is None:
                proj = _norm_mod_matmul(x, norm1[l], mods, w_in(l), 0, 1)
            scw = _pad_rows(sc_conv_w[i])
            ret_l, sc_l = _even_mixer(proj, scw, ret_decay_fwd[i], ret_decay_bwd[i], state=state_ret[:, i:i + 1])
            ret_c, sc_c, new_state = _even_mixer(proj, scw, ret_decay_fwd[i], ret_decay_bwd[i])
            mixed, w_out = ((ret_l, ret_c), (sc_l, sc_c)), ev_w_out[i]
        else:
            lambda_init = 0.8 - 0.6 * math.exp(-0.3 * l)
            if proj is None:
                proj = _norm_mod_matmul(x, norm1[l], mods, w_in(l), 0, 1)
            conv_w = _pad_rows(hy_conv_w[i])
            bias = _pad_rows(hy_bias[i])
            w1 = _pad2(hy_w1[i], LANES, LANES)
            b1 = _pad2(hy_b1[i][None], 1, LANES)
            f1 = _pad2(hy_fr1[i][None], 1, LANES)
            w2 = _pad2(hy_w2[i], LANES, LANES)
            b2 = _pad2(hy_b2[i][None], 1, LANES)
            f2 = _pad2(hy_fr2[i][None], 1, LANES)
            w3 = _pad2(hy_w3[i], LANES, hy_w3.shape[2])
            hy = []
            for latent in (True, False):
                seq = DEC_SEQ if latent else SEQ
                fwd_np, inv_np = _dft_tables(seq)
                zfeat, decay = _hyena_consts(seq)
                fwd_bf, fwd_lo = _split_table(jnp.asarray(fwd_np), f"dft_fwd_split_{seq}")
                inv_bf, _ = _split_table(jnp.asarray(inv_np), f"dft_inv_split_{seq}")
                filters = _hy_filters(seq, jnp.asarray(zfeat), jnp.asarray(decay), fwd_bf, fwd_lo,
                                      w1, b1, f1, w2, b2, f2, w3)
                hy.append(_hy_apply(proj, conv_w, fwd_bf, inv_bf, filters, bias, latent))
            cos, sin = _rope_tables(DEC_SEQ)
            att_l = _attention(proj, da_lambda[i], da_subln[i], lambda_init, True,
                               rope=(jnp.asarray(cos), jnp.asarray(sin)),
                               ctx_k=cache_k[:, i:i + 1], ctx_v=cache_v[:, i:i + 1])[0]
            att_c, new_k, new_v = _attention(proj, da_lambda[i], da_subln[i], lambda_init, False)
            mixed, w_out = (tuple(hy), (att_l, att_c)), od_w_out[i]
        w_route, b_route = _route_weights(moe_w_grp[l], moe_b_grp[l], moe_w_exp[l], moe_b_exp[l])
        x, *route = _proj_residual_router(x, mixed[0], mixed[1], w_out.astype(BF16), mods, 2,
                                          norm2[l], w_route, b_route, tril, triu)
        if l == DEPTH - 1:
            x = _moe_layer(x, route, l, mods, moe_w1, moe_w3, moe_w2, final_g=final_norm)
        else:
            x, proj = _moe_layer(x, route, l, mods, moe_w1, moe_w3, moe_w2,
                                 next_proj=(norm1[l + 1], seg_mods[l + 1], w_in(l + 1)))

    y_sample, y_prompt = x
    return (y_prompt.reshape(BATCH, SEQ, D_MODEL), y_sample.reshape(DEC_BATCH, DEC_SEQ, D_MODEL),
            new_state, new_k, new_v)
```

```python
import functools
import math

import numpy as np
import jax
import jax.numpy as jnp
from jax import lax
from jax.experimental import pallas as pl
from jax.experimental.pallas import tpu as pltpu

F32 = jnp.float32
BF16 = jnp.bfloat16
I32 = jnp.int32
HIGHEST = lax.Precision.HIGHEST

D_MODEL = 1024
BATCH = 32
SEQ = 256
DEPTH = 2
DEC_BATCH = 8
DEC_SEQ = 1024
PAST_LEN = 256
GRID_W = 64
N_MOD = 6
EPS = 1e-6
GN_EPS = 1e-5

RET_DK = 128
RET_DV = 128
RET_HEADS = 4
RET_W = 512
RET_QK = 512
SC_W = 512
EVEN_IN = 2 * RET_QK + 2 * RET_W + 3 * SC_W

HY_W = 512
HY_ORDER = 2
HY_EMB = 33
HY_FFN = 64
HY_TARGET = 1e-2
HY_FAST_PCT = 0.3
HY_SLOW_PCT = 1.5
DA_DH = 64
DA_HEADS = 4
DA_W = 512
ODD_IN = 3 * HY_W + 3 * DA_W
ROPE_THETA = 10000.0

N_GROUPS = 4
EXP_PER_GROUP = 8
N_EXPERTS = 32
TOP_K = 2
D_EXPERT = 512

N_SAMPLE = DEC_BATCH * DEC_SEQ
N_PROMPT = BATCH * SEQ
N_TOK = N_SAMPLE + N_PROMPT
SEG = 1024
N_SEG = N_TOK // SEG
MOD_ROWS = 8

LANES = 128
SUBLANES = 8
VMEM_LIMIT = 56 * 1024 * 1024

TM = 256
MOE_BLK = 512
ROW_ALIGN = SUBLANES
R_LOC = (TOP_K * TM + N_EXPERTS * (ROW_ALIGN - 1) + LANES - 1) // LANES * LANES
CHUNKS = (256, 128, 64, 32, 16, 8)
MOE_ROWS = ((N_TOK * TOP_K + N_EXPERTS * (N_TOK // TM) * (ROW_ALIGN - 1) + N_EXPERTS * (MOE_BLK - 1))
            // MOE_BLK + 1) * MOE_BLK
MOE_NB = MOE_ROWS // MOE_BLK
HY_SLAB = 512
ATT_TQ = 256


def _cparams(sem, vmem=VMEM_LIMIT):
    return pltpu.CompilerParams(dimension_semantics=sem, vmem_limit_bytes=vmem)


def _silu(x):
    return x * (1.0 / (1.0 + jnp.exp(-x)))


def _split_bf16(a):
    hi = a.astype(BF16)
    return hi, (a - hi.astype(F32)).astype(BF16)


def _dot_3pass(a, b):
    ah, al = _split_bf16(a)
    bh, bl = _split_bf16(b)
    dot = functools.partial(jnp.dot, preferred_element_type=F32)
    return dot(ah, bh) + (dot(ah, bl) + dot(al, bh))


MODS_TN = 1536


def _mods_kernel(c_ref, w_ref, b_ref, o_ref):
    s = _silu(c_ref[...])
    o_ref[0] = jnp.dot(s, w_ref[0], preferred_element_type=F32, precision=HIGHEST) + b_ref[0]


def _mods(cvecs, ada_w, ada_b):
    n = N_MOD * D_MODEL
    return pl.pallas_call(
        _mods_kernel,
        grid=(DEPTH, n // MODS_TN),
        in_specs=[
            pl.BlockSpec((16, D_MODEL), lambda l, j: (0, 0)),
            pl.BlockSpec((1, D_MODEL, MODS_TN), lambda l, j: (l, 0, j)),
            pl.BlockSpec((1, 1, MODS_TN), lambda l, j: (l, 0, j)),
        ],
        out_specs=pl.BlockSpec((1, 16, MODS_TN), lambda l, j: (l, 0, j)),
        out_shape=jax.ShapeDtypeStruct((DEPTH, 16, n), F32),
        compiler_params=_cparams(("parallel", "parallel")),
        name="adaln_mods",
    )(cvecs, ada_w, ada_b.reshape(DEPTH, 1, n))


def _norm_mod(x, g, mod, shift_idx, scale_idx):
    ms = jnp.mean(x * x, axis=-1, keepdims=True)
    y = x * lax.rsqrt(ms + EPS) * g
    return y * (1.0 + mod[scale_idx:scale_idx + 1]) + mod[shift_idx:shift_idx + 1]


N_LAT_TILES = N_SAMPLE // TM
TM_WIDE = 512
TM_ROUTE = 1024


def _group_specs(width, tm=TM):
    n_lat = N_SAMPLE // tm
    lat = pl.BlockSpec((tm, width), lambda i: (jnp.minimum(i, n_lat - 1), 0))
    ctx = pl.BlockSpec((tm, width), lambda i: (jnp.maximum(i - n_lat, 0), 0))
    return [lat, ctx]


def _group_pick(lat_ref, ctx_ref):
    n_lat = N_SAMPLE // lat_ref.shape[0]
    return jnp.where(pl.program_id(0) < n_lat, lat_ref[...], ctx_ref[...])


def _rows_specs(x, tm=TM):
    if isinstance(x, tuple):
        return _group_specs(D_MODEL, tm), list(x)
    return [pl.BlockSpec((tm, D_MODEL), lambda i: (i, 0))], [x]


def _rows_value(refs):
    return _group_pick(*refs) if len(refs) == 2 else refs[0][...]


def _nmm_kernel(*refs, n_x, shift_idx, scale_idx):
    g_ref, mod_ref, w_ref, o_ref = refs[n_x:]
    u = _norm_mod(_rows_value(refs[:n_x]), g_ref[...], mod_ref[0], shift_idx, scale_idx)
    o_ref[...] = jnp.dot(u.astype(BF16), w_ref[...], preferred_element_type=F32)


def _norm_mod_matmul(x, g, mods, w_bf16, shift_idx, scale_idx):
    n = w_bf16.shape[1]
    tm = TM_WIDE
    x_specs, x_args = _rows_specs(x, tm)
    return pl.pallas_call(
        functools.partial(_nmm_kernel, n_x=len(x_args), shift_idx=shift_idx, scale_idx=scale_idx),
        grid=(N_TOK // tm,),
        in_specs=x_specs + [
            pl.BlockSpec((1, D_MODEL), lambda i: (0, 0)),
            pl.BlockSpec((1, MOD_ROWS, D_MODEL), lambda i: (i * tm // SEG, 0, 0)),
            pl.BlockSpec((D_MODEL, n), lambda i: (0, 0), pipeline_mode=pl.Buffered(1)),
        ],
        out_specs=pl.BlockSpec((tm, n), lambda i: (i, 0)),
        out_shape=jax.ShapeDtypeStruct((N_TOK, n), F32),
        compiler_params=_cparams(("parallel",)),
        name="norm_mod_inproj",
    )(*x_args, g.reshape(1, D_MODEL), mods, w_bf16)


def _conv3_rows(z, w):
    n = z.shape[0]
    row = lax.broadcasted_iota(I32, z.shape, 0)
    zm = jnp.where(row == 0, 0.0, pltpu.roll(z, 1, 0))
    zp = jnp.where(row == n - 1, 0.0, pltpu.roll(z, n - 1, 0))
    return zm * w[0:1] + z * w[1:2] + zp * w[2:3]


CTX_NSEQ = 4


def _log_gamma(dec):
    return jnp.log1p(-jnp.exp(dec))


def _even_kernel(*refs, seq, latent, tq, heads, nseq):
    (q_ref, k_ref, v_ref, g_ref, gb_ref, gc_ref, hx_ref, scw_ref, df_ref, db_ref) = refs[:10]
    if latent:
        s0_ref, ret_ref, sc_ref, mask_ref = refs[10:]
    else:
        ret_ref, sc_ref, st_ref, mask_ref = refs[10:]

    decays = []
    for hh in range(heads):
        lgf = _log_gamma(df_ref[hh])
        lgb = _log_gamma(db_ref[hh])
        decays.append((lgf, lgb))

        @pl.when(pl.program_id(1) == 0)
        def _():
            for r in range(seq // tq):
                t = lax.broadcasted_iota(I32, (tq, seq), 0) + r * tq
                s = lax.broadcasted_iota(I32, (tq, seq), 1)
                d = (t - s).astype(F32)
                mf = jnp.where(d >= 0, jnp.exp(jnp.maximum(d, 0.0) * lgf), 0.0)
                mb = jnp.where(d <= 0, jnp.exp(jnp.maximum(-d, 0.0) * lgb), 0.0)
                mask_ref[hh, r * tq:(r + 1) * tq, :] = mf + mb

    for sq in range(nseq):
        whole = slice(sq * seq, (sq + 1) * seq)
        for hh in range(heads):
            lanes = slice(hh * RET_DK, (hh + 1) * RET_DK)
            lgf, lgb = decays[hh]
            k = k_ref[whole, lanes] * (RET_DK ** -0.5)
            kb = k.astype(BF16)
            vb = v_ref[whole, lanes].astype(BF16)
            if latent:
                s0f = s0_ref[sq, 0, 0, hh].astype(BF16)
                s0b = s0_ref[sq, 0, 1, hh].astype(BF16)
            for r in range(seq // tq):
                rows = slice(sq * seq + r * tq, sq * seq + (r + 1) * tq)
                qb = q_ref[rows, lanes].astype(BF16)
                s = lax.dot_general(qb, kb, (((1,), (1,)), ((), ())), preferred_element_type=F32)
                p = (s * mask_ref[hh, r * tq:(r + 1) * tq, :]).astype(BF16)
                o = jnp.dot(p, vb, preferred_element_type=F32)
                if latent:
                    tpos = (lax.broadcasted_iota(I32, (tq, 1), 0) + r * tq).astype(F32)
                    o = o + jnp.dot(qb, s0f, preferred_element_type=F32) * jnp.exp((tpos + 1.0) * lgf)
                    o = o + jnp.dot(qb, s0b, preferred_element_type=F32) * jnp.exp((seq - tpos) * lgb)
                mu = jnp.mean(o, axis=-1, keepdims=True)
                var = jnp.mean(jnp.square(o - mu), axis=-1, keepdims=True)
                on = (o - mu) * lax.rsqrt(var + GN_EPS)
                ret_ref[rows, lanes] = _silu(g_ref[rows, lanes]) * on

            if not latent:
                spos = lax.broadcasted_iota(I32, (seq, 1), 0).astype(F32)
                kf = (k * jnp.exp((seq - 1.0 - spos) * lgf)).T.astype(BF16)
                kr = (k * jnp.exp(spos * lgb)).T.astype(BF16)
                st_ref[sq, 0, 0, hh] = jnp.dot(kf, vb, preferred_element_type=F32)
                st_ref[sq, 0, 1, hh] = jnp.dot(kr, vb, preferred_element_type=F32)

        sc_ref[whole, :] = gb_ref[whole, :] * _conv3_rows(gc_ref[whole, :] * hx_ref[whole, :], scw_ref[...])


def _even_mixer(proj, sc_w_pad, dec_f, dec_b, state=None):
    latent = state is not None
    seq = DEC_SEQ if latent else SEQ
    nb = DEC_BATCH if latent else BATCH
    tq = min(seq, 256)
    heads = 2 if latent else RET_HEADS
    nseq = 1 if latent else CTX_NSEQ
    rows = nseq * seq
    row0 = 0 if latent else N_SAMPLE // rows
    steps = RET_HEADS // heads
    hw = heads * RET_DK

    def col(c0):
        return pl.BlockSpec((rows, hw), lambda h, b: (row0 + b, c0 * steps + h))

    in_specs = [col(0), col(1), col(2), col(3), col(4), col(5), col(6),
                pl.BlockSpec((SUBLANES, hw), lambda h, b: (0, h)),
                pl.BlockSpec((heads, 1, 1), lambda h, b: (h, 0, 0)),
                pl.BlockSpec((heads, 1, 1), lambda h, b: (h, 0, 0))]
    args = [proj] * 7 + [sc_w_pad, dec_f.reshape(RET_HEADS, 1, 1), dec_b.reshape(RET_HEADS, 1, 1)]
    out_spec = pl.BlockSpec((rows, hw), lambda h, b: (b, h))
    out_shape = [jax.ShapeDtypeStruct((nb * seq, RET_W), F32), jax.ShapeDtypeStruct((nb * seq, SC_W), F32)]
    out_specs = [out_spec, out_spec]
    st_spec = pl.BlockSpec((nseq, 1, 2, heads, RET_DK, RET_DV), lambda h, b: (b, 0, 0, h, 0, 0))
    if latent:
        in_specs.append(st_spec)
        args.append(state)
    else:
        out_shape.append(jax.ShapeDtypeStruct((BATCH, 1, 2, RET_HEADS, RET_DK, RET_DV), F32))
        out_specs.append(st_spec)
    return pl.pallas_call(
        functools.partial(_even_kernel, seq=seq, latent=latent, tq=tq, heads=heads, nseq=nseq),
        grid=(steps, nb // nseq),
        in_specs=in_specs,
        out_specs=out_specs,
        out_shape=out_shape,
        scratch_shapes=[pltpu.VMEM((heads, seq, seq), F32)],
        compiler_params=_cparams(("arbitrary", "arbitrary")),
        name="even_mixer_latent" if latent else "even_mixer_context",
    )(*args)


def _dft_tables(length):
    f = np.arange(length, dtype=np.float64)[:, None]
    t = np.arange(length, dtype=np.float64)[None, :]
    ang = np.pi * ((f * t) % (2 * length)) / length
    cos, sin = np.cos(ang), np.sin(ang)
    sin[0, :] = (-1.0) ** np.arange(length)
    fwd = np.concatenate([cos, sin], axis=0)
    wgt = np.full((2 * length,), 2.0)
    wgt[0] = 1.0
    wgt[length] = 1.0
    inv = (fwd * wgt[:, None] / (2.0 * length)).T
    return fwd.astype(np.float32), np.ascontiguousarray(inv).astype(np.float32)


def _hyena_consts(length):
    t = np.linspace(0.0, 1.0, length)[:, None]
    bands = (HY_EMB - 1) // 2
    w = 2.0 * np.pi * np.arange(length)[:, None] / length
    f = np.linspace(1e-4, bands - 1, bands)[None, :]
    z = np.concatenate([t, np.cos(f * w), -np.sin(f * w)], axis=-1)
    zpad = np.zeros((length, LANES))
    zpad[:, :HY_EMB] = z
    deltas = np.abs(np.linspace(math.log(HY_TARGET) / HY_SLOW_PCT, math.log(HY_TARGET) / HY_FAST_PCT, HY_W))
    decay = np.exp(-t * deltas[None, :])
    return zpad.astype(np.float32), decay.astype(np.float32)


def _split_kernel(x_ref, hi_ref, lo_ref):
    hi, lo = _split_bf16(x_ref[...])
    hi_ref[...] = hi
    lo_ref[...] = lo


def _split_table(x, name):
    rows = min(x.shape[0], 1024)
    spec = pl.BlockSpec((rows, x.shape[1]), lambda i: (i, 0))
    out = jax.ShapeDtypeStruct(x.shape, BF16)
    return pl.pallas_call(
        _split_kernel,
        grid=(x.shape[0] // rows,),
        in_specs=[spec],
        out_specs=[spec, spec],
        out_shape=[out, out],
        compiler_params=_cparams(("parallel",)),
        name=name,
    )(x)


def _dot_3pass_split(ah, al, b):
    bh, bl = _split_bf16(b)
    dot = functools.partial(jnp.dot, preferred_element_type=F32)
    return dot(ah, bh) + (dot(ah, bl) + dot(al, bh))


def _hy_filter_kernel(z_ref, dec_ref, w1_ref, b1_ref, f1_ref, w2_ref, b2_ref, f2_ref, w3f_ref, w3b_ref,
                      fwd_ref, fwd_lo_ref, ka_ref, ka2_ref, kb_ref, *, seq):
    h = jnp.sin(f1_ref[...] * (jnp.dot(z_ref[...], w1_ref[...], preferred_element_type=F32,
                                       precision=HIGHEST) + b1_ref[...]))
    h = jnp.sin(f2_ref[...] * (jnp.dot(h, w2_ref[...], preferred_element_type=F32,
                                       precision=HIGHEST) + b2_ref[...]))
    dec = dec_ref[...]
    hf = jnp.dot(h, w3f_ref[...], preferred_element_type=F32, precision=HIGHEST) * dec
    hb = jnp.dot(h, w3b_ref[...], preferred_element_type=F32, precision=HIGHEST) * dec
    row = lax.broadcasted_iota(I32, hb.shape, 0)
    hb = jnp.where(row == 0, 0.0, hb)
    inv_norm = 1.0 / (jnp.sum(jnp.abs(hf), axis=0, keepdims=True)
                      + jnp.sum(jnp.abs(hb), axis=0, keepdims=True) + EPS)
    p = (hf + hb) * inv_norm
    q = (hf - hb) * inv_norm
    fp = _dot_3pass_split(fwd_ref[...], fwd_lo_ref[...], p)
    fq = _dot_3pass_split(fwd_ref[seq:, :], fwd_lo_ref[seq:, :], q)
    kr = fp[:seq]
    row = lax.broadcasted_iota(I32, kr.shape, 0)
    ka_ref[0] = kr
    ka2_ref[0] = jnp.where(row == 0, fp[seq:seq + 1], kr)
    kb_ref[0] = jnp.where(row == 0, 0.0, -fq)


def _hy_filters(seq, zfeat, decay, fwd_hi, fwd_lo, w1, b1, f1, w2, b2, f2, w3):
    ns = HY_W // HY_SLAB
    per_o = 2 * ns
    small = lambda shape: pl.BlockSpec(shape, lambda o, s: (0, 0))
    out_spec = pl.BlockSpec((1, seq, HY_SLAB), lambda o, s: (o, 0, s))
    out = jax.ShapeDtypeStruct((HY_ORDER, seq, HY_W), F32)
    return pl.pallas_call(
        functools.partial(_hy_filter_kernel, seq=seq),
        grid=(HY_ORDER, ns),
        in_specs=[
            small((seq, LANES)),
            pl.BlockSpec((seq, HY_SLAB), lambda o, s: (0, s)),
            small((LANES, LANES)), small((1, LANES)), small((1, LANES)),
            small((LANES, LANES)), small((1, LANES)), small((1, LANES)),
            pl.BlockSpec((LANES, HY_SLAB), lambda o, s: (0, o * per_o + s)),
            pl.BlockSpec((LANES, HY_SLAB), lambda o, s: (0, o * per_o + ns + s)),
            small((2 * seq, seq)), small((2 * seq, seq)),
        ],
        out_specs=[out_spec, out_spec, out_spec],
        out_shape=[out, out, out],
        compiler_params=_cparams(("parallel", "parallel")),
        name=f"hyena_filters_{seq}",
    )(zfeat, decay, w1, b1, f1, w2, b2, f2, w3, w3, fwd_hi, fwd_lo)


def _hy_apply_kernel(x1_ref, x2_ref, v_ref, c1_ref, c2_ref, cv_ref, fwd_ref, inv_ref,
                     ka_ref, ka2_ref, kb_ref, bias_ref, o_ref, *, seq, nseq):
    def long_conv(sig, o):
        spec = jnp.dot(fwd_ref[...], sig.astype(BF16), preferred_element_type=F32)
        xr, xs = spec[:seq], spec[seq:]
        kb = kb_ref[o]
        yr = (xr * ka_ref[o] + xs * kb).astype(BF16)
        ys = (xs * ka2_ref[o] - xr * kb).astype(BF16)
        y = jnp.dot(inv_ref[:, :seq], yr, preferred_element_type=F32)
        y = y + jnp.dot(inv_ref[:, seq:], ys, preferred_element_type=F32)
        return y + sig * bias_ref[o:o + 1]

    for i in range(nseq):
        rows = slice(i * seq, (i + 1) * seq)
        x1 = _conv3_rows(x1_ref[rows, :], c1_ref[...])
        x2 = _conv3_rows(x2_ref[rows, :], c2_ref[...])
        v = _conv3_rows(v_ref[rows, :], cv_ref[...])
        z = x1 * long_conv(v, 0)
        o_ref[rows, :] = x2 * long_conv(z, 1)


HY_NSEQ = {True: 1, False: 4}


def _hy_apply(proj, conv_w_pad, fwd_bf, inv_bf, filters, bias_pad, latent):
    seq = DEC_SEQ if latent else SEQ
    nb = DEC_BATCH if latent else BATCH
    nseq = HY_NSEQ[latent]
    rows = nseq * seq
    row0 = 0 if latent else N_SAMPLE // rows
    ns = HY_W // HY_SLAB
    once = pl.Buffered(1)
    const = lambda shape: pl.BlockSpec(shape, lambda s, b: (0,) * len(shape), pipeline_mode=once)
    col = lambda c0: pl.BlockSpec((rows, HY_SLAB), lambda s, b: (row0 + b, c0 + s))
    cw = lambda c0: pl.BlockSpec((SUBLANES, HY_SLAB), lambda s, b: (0, c0 + s))
    filt = pl.BlockSpec((HY_ORDER, seq, HY_SLAB), lambda s, b: (0, 0, s), pipeline_mode=once)
    in_specs = [col(0), col(ns), col(2 * ns), cw(0), cw(ns), cw(2 * ns),
                const((2 * seq, seq)), const((seq, 2 * seq)), filt, filt, filt,
                pl.BlockSpec((SUBLANES, HY_SLAB), lambda s, b: (0, s))]
    args = [proj] * 3 + [conv_w_pad] * 3 + [fwd_bf, inv_bf, *filters, bias_pad]
    return pl.pallas_call(
        functools.partial(_hy_apply_kernel, seq=seq, nseq=nseq),
        grid=(ns, nb // nseq),
        in_specs=in_specs,
        out_specs=pl.BlockSpec((rows, HY_SLAB), lambda s, b: (b, s)),
        out_shape=jax.ShapeDtypeStruct((nb * seq, HY_W), F32),
        compiler_params=_cparams(("arbitrary", "arbitrary")),
        name="hyena_latent" if latent else "hyena_context",
    )(*args)


def _rope_tables(length):
    rows = length // GRID_W
    row = np.repeat(np.arange(rows, dtype=np.float64), GRID_W)
    colp = np.tile(np.arange(GRID_W, dtype=np.float64), rows)
    half = DA_DH // 2
    freqs = ROPE_THETA ** (-np.arange(0, half, 2, dtype=np.float64) / half)
    ang = np.concatenate([row[:, None] * freqs[None, :]] * 2 + [colp[:, None] * freqs[None, :]] * 2, axis=1)
    ang = np.concatenate([ang, ang], axis=1)
    lane = np.arange(LANES)
    sign = np.where((lane % half) < half // 2, -1.0, 1.0)[None, :]
    return np.cos(ang).astype(np.float32), (np.sin(ang) * sign).astype(np.float32)


def _rope(x, cos, sin_signed):
    q = DA_DH // 4
    lane = lax.broadcasted_iota(I32, x.shape, 1)
    partner = jnp.where((lane % (2 * q)) < q, pltpu.roll(x, LANES - q, 1), pltpu.roll(x, q, 1))
    return x * cos + partner * sin_signed


def _attn_kernel(*refs, seq, latent, lambda_init, heads, nseq):
    q_ref, k_ref, v_ref, lam_ref, sub_ref = refs[:5]
    if latent:
        cos_ref, sin_ref, ck_ref, cv_ref, o_ref = refs[5:]
    else:
        o_ref, nk_ref, nv_ref = refs[5:]
    hw = 2 * DA_DH
    lv = lam_ref[...]
    lam = (jnp.exp(jnp.sum(lv[0:1] * lv[1:2], axis=1, keepdims=True))
           - jnp.exp(jnp.sum(lv[2:3] * lv[3:4], axis=1, keepdims=True)) + lambda_init)
    scale = DA_DH ** -0.5
    dn = (((1,), (1,)), ((), ()))
    tq = min(seq, ATT_TQ)
    for sq, hh in [(sq, hh) for sq in range(nseq) for hh in range(heads)]:
        whole = slice(sq * seq, (sq + 1) * seq)
        lanes = slice(hh * hw, (hh + 1) * hw)
        k = k_ref[whole, lanes]
        v = v_ref[whole, lanes]
        if latent:
            k = _rope(k, cos_ref[...], sin_ref[...])
            k_all = jnp.concatenate([k, jnp.concatenate([ck_ref[sq, 0, 0, hh], ck_ref[sq, 0, 1, hh]], axis=1)], axis=0)
            v_all = jnp.concatenate([v, cv_ref[sq, 0, hh]], axis=0)
        else:
            k_all, v_all = k, v
            nk_ref[sq, 0, 0, hh] = k[:, :DA_DH]
            nk_ref[sq, 0, 1, hh] = k[:, DA_DH:]
            nv_ref[sq, 0, hh] = v
        kb = k_all.astype(BF16)
        vb = v_all.astype(BF16)
        for r in range(seq // tq):
            rows = slice(sq * seq + r * tq, sq * seq + (r + 1) * tq)
            q = q_ref[rows, lanes]
            if latent:
                q = _rope(q, cos_ref[r * tq:(r + 1) * tq, :], sin_ref[r * tq:(r + 1) * tq, :])
            q = q * scale
            first = lax.broadcasted_iota(I32, q.shape, 1) < DA_DH
            q1 = jnp.where(first, q, 0.0).astype(BF16)
            q2 = jnp.where(first, 0.0, q).astype(BF16)
            s1 = lax.dot_general(q1, kb, dn, preferred_element_type=F32)
            s2 = lax.dot_general(q2, kb, dn, preferred_element_type=F32)
            e1 = jnp.exp(s1 - jnp.max(s1, axis=-1, keepdims=True))
            e2 = jnp.exp(s2 - jnp.max(s2, axis=-1, keepdims=True))
            o1 = jnp.dot(e1.astype(BF16), vb, preferred_element_type=F32)
            o2 = jnp.dot(e2.astype(BF16), vb, preferred_element_type=F32)
            o = (o1 * (1.0 / jnp.sum(e1, axis=-1, keepdims=True))
                 - o2 * (lam / jnp.sum(e2, axis=-1, keepdims=True)))
            o = o * lax.rsqrt(jnp.mean(o * o, axis=-1, keepdims=True) + EPS) * sub_ref[...]
            o_ref[rows, lanes] = o * (1.0 - lambda_init)


def _attention(proj, lam_vec, subln, lambda_init, latent, rope=None, ctx_k=None, ctx_v=None):
    seq = DEC_SEQ if latent else SEQ
    nb = DEC_BATCH if latent else BATCH
    hw = 2 * DA_DH
    heads = DA_HEADS
    nseq = 1 if latent else CTX_NSEQ
    rows = nseq * seq
    row0 = 0 if latent else N_SAMPLE // rows
    steps = DA_HEADS // heads
    col = lambda c: pl.BlockSpec((rows, heads * hw), lambda b, h: (row0 + b, c * steps + h))
    in_specs = [col(3), col(4), col(5),
                pl.BlockSpec((4, DA_DH), lambda b, h: (0, 0)),
                pl.BlockSpec((1, hw), lambda b, h: (0, 0))]
    args = [proj] * 3 + [lam_vec, subln.reshape(1, hw)]
    out_spec = pl.BlockSpec((rows, heads * hw), lambda b, h: (b, h))
    out_shape = [jax.ShapeDtypeStruct((nb * seq, DA_W), F32)]
    out_specs = [out_spec]
    k_spec = pl.BlockSpec((nseq, 1, 2, heads, SEQ, DA_DH), lambda b, h: (b, 0, 0, h, 0, 0))
    v_spec = pl.BlockSpec((nseq, 1, heads, SEQ, hw), lambda b, h: (b, 0, h, 0, 0))
    if latent:
        in_specs += [pl.BlockSpec((seq, hw), lambda b, h: (0, 0))] * 2 + [k_spec, v_spec]
        args += [rope[0], rope[1], ctx_k, ctx_v]
    else:
        out_shape += [jax.ShapeDtypeStruct((BATCH, 1, 2, DA_HEADS, SEQ, DA_DH), F32),
                      jax.ShapeDtypeStruct((BATCH, 1, DA_HEADS, SEQ, hw), F32)]
        out_specs += [k_spec, v_spec]
    return pl.pallas_call(
        functools.partial(_attn_kernel, seq=seq, latent=latent, lambda_init=lambda_init, heads=heads, nseq=nseq),
        grid=(nb // nseq, steps),
        in_specs=in_specs,
        out_specs=out_specs,
        out_shape=out_shape,
        compiler_params=_cparams(("arbitrary", "arbitrary")),
        name="diff_attn_latent" if latent else "diff_attn_context",
    )(*args)


HALF_D = D_MODEL // 2
HI16 = -65536


def _pack_bf16_pair(lo, hi):
    lo_bits = lax.bitcast_convert_type(lo.astype(BF16).astype(F32), I32)
    hi_bits = lax.bitcast_convert_type(hi.astype(BF16).astype(F32), I32)
    return (hi_bits & HI16) | lax.shift_right_logical(lo_bits, 16)


def _pack_exact_bf16_pair(lo, hi):
    return ((lax.bitcast_convert_type(hi, I32) & HI16)
            | lax.shift_right_logical(lax.bitcast_convert_type(lo, I32), 16))


def _unpack_bf16_pair(p):
    lo = lax.bitcast_convert_type(lax.shift_left(p, 16), F32).astype(BF16)
    hi = lax.bitcast_convert_type(p & HI16, F32).astype(BF16)
    return lo, hi


GRP_EXPERT, GRP_REL, GRP_COUNT = range(3)


def _route_rows(x, g, mod, wr, br, tril_ref, triu_ref, carry_ref):
    u = _norm_mod(x, g, mod, 3, 4)
    u_packed = _pack_bf16_pair(u[:, :HALF_D], u[:, HALF_D:])
    logits = _dot_3pass(u, wr) + br
    lane = lax.broadcasted_iota(I32, logits.shape, 1)
    lane_f = lane.astype(F32)
    neg = -jnp.inf
    big = float(LANES)

    def first_max(vals):
        m = jnp.max(vals, axis=-1, keepdims=True)
        return m, jnp.min(jnp.where(vals == m, lane_f, big), axis=-1, keepdims=True)

    is_grp = lane < N_GROUPS
    gmax, gidx = first_max(jnp.where(is_grp, logits, neg))
    gsum = jnp.sum(jnp.where(is_grp, jnp.exp(logits - gmax), 0.0), axis=-1, keepdims=True)
    g_w = 1.0 / gsum
    lo = N_GROUPS + EXP_PER_GROUP * gidx
    in_grp = jnp.logical_and(lane_f >= lo, lane_f < lo + EXP_PER_GROUP)
    el = jnp.where(in_grp, logits, neg)
    v1, i1 = first_max(el)
    v2, i2 = first_max(jnp.where(lane_f == i1, neg, el))
    e = jnp.exp(v2 - v1)
    p1 = 1.0 / (1.0 + e)
    hit1 = lane_f == i1 - N_GROUPS
    hit2 = lane_f == i2 - N_GROUPS
    onehot = jnp.where(jnp.logical_or(hit1, hit2), 1.0, 0.0)
    cum = jnp.dot(tril_ref[...], onehot.astype(BF16), preferred_element_type=F32)
    n = cum[TM - 1:TM, :]
    n_pad = jnp.floor((n + (ROW_ALIGN - 1.0)) * (1.0 / ROW_ALIGN)) * ROW_ALIGN
    loc_off = jnp.dot(jnp.broadcast_to(n_pad, (SUBLANES, LANES)).astype(BF16), triu_ref[...],
                      preferred_element_type=F32)[0:1]
    base = cum - onehot + loc_off
    loc1 = jnp.sum(jnp.where(hit1, base, 0.0), axis=-1, keepdims=True)
    loc2 = jnp.sum(jnp.where(hit2, base, 0.0), axis=-1, keepdims=True)
    info = jnp.where(lane == 0, g_w * p1, jnp.where(lane == 1, g_w * (e * p1), jnp.where(lane == 2, loc1, loc2)))
    start8 = lax.broadcasted_iota(I32, (LANES, LANES), 0).astype(F32) * ROW_ALIGN
    owner = jnp.where(jnp.logical_and(start8 >= loc_off, start8 < loc_off + n_pad), 1.0, 0.0)
    lane128 = lax.broadcasted_iota(I32, (LANES, LANES), 1)
    c_exp = jnp.sum(owner * lane128.astype(F32), axis=-1, keepdims=True)
    c_rel = jnp.sum(owner * (carry_ref[...] + start8 - loc_off), axis=-1, keepdims=True)
    chunks = jnp.where(lane128 == 0, c_exp, jnp.where(lane128 == 1, c_rel, 0.0)).T
    n_chunks = jnp.sum(n_pad, axis=-1, keepdims=True) * (1.0 / ROW_ALIGN)
    row = lax.broadcasted_iota(I32, (SUBLANES, LANES), 0)
    grp = jnp.where(row == GRP_EXPERT, chunks[0:1], jnp.where(row == GRP_REL, chunks[1:2], n_chunks))
    carry_ref[...] = carry_ref[...] + n_pad
    return u_packed, info, info.T[:SUBLANES], grp.astype(I32)


def _proj_res_router_kernel(*refs, n_x, gate_idx):
    (al_ref, ac_ref, bl_ref, bc_ref, wa_ref, wb_ref, mod_ref, g2_ref, wr_ref, br_ref, tril_ref, triu_ref,
     o_ref, u_ref, loct_ref, info_ref, grp_ref, tot_ref, carry_ref) = refs[n_x:]

    @pl.when(pl.program_id(0) == 0)
    def _():
        carry_ref[...] = jnp.zeros_like(carry_ref)

    a = _group_pick(al_ref, ac_ref)
    b = _group_pick(bl_ref, bc_ref)
    y = jnp.dot(a.astype(BF16), wa_ref[...], preferred_element_type=F32)
    y = y + jnp.dot(b.astype(BF16), wb_ref[...], preferred_element_type=F32)
    mod = mod_ref[0]
    x_new = _rows_value(refs[:n_x]) + mod[gate_idx:gate_idx + 1] * y
    o_ref[...] = x_new
    for t in range(TM_ROUTE // TM):
        rows = slice(t * TM, (t + 1) * TM)
        u_packed, info, info_t, grp = _route_rows(x_new[rows], g2_ref[...], mod, wr_ref[...], br_ref[...],
                                                 tril_ref, triu_ref, carry_ref)
        u_ref[rows, :] = u_packed
        info_ref[rows, :] = info
        loct_ref[t] = info_t
        grp_ref[t] = grp
    tot_ref[...] = jnp.broadcast_to(carry_ref[...], tot_ref.shape).astype(I32)


def _proj_residual_router(x, a, b, w_bf16, mods, gate_idx, norm2, w_route, b_route, tril, triu):
    half = a[0].shape[1]
    tm = TM_ROUTE
    sub = tm // TM
    nt = N_TOK // TM
    x_specs, x_args = _rows_specs(x, tm)
    const = lambda shape: pl.BlockSpec(shape, lambda i: (0,) * len(shape))
    return pl.pallas_call(
        functools.partial(_proj_res_router_kernel, n_x=len(x_args), gate_idx=gate_idx),
        grid=(N_TOK // tm,),
        in_specs=x_specs + _group_specs(half, tm) + _group_specs(half, tm) + [
            pl.BlockSpec((half, D_MODEL), lambda i: (0, 0)),
            pl.BlockSpec((half, D_MODEL), lambda i: (1, 0)),
            pl.BlockSpec((1, MOD_ROWS, D_MODEL), lambda i: (i * tm // SEG, 0, 0)),
            const((1, D_MODEL)), const((D_MODEL, LANES)), const((1, LANES)), const((TM, TM)), const((LANES, LANES)),
        ],
        out_specs=[
            pl.BlockSpec((tm, D_MODEL), lambda i: (i, 0)),
            pl.BlockSpec((tm, HALF_D), lambda i: (i, 0)),
            pl.BlockSpec((sub, SUBLANES, TM), lambda i: (i, 0, 0)),
            pl.BlockSpec((tm, LANES), lambda i: (i, 0)),
            pl.BlockSpec((sub, SUBLANES, LANES), lambda i: (i, 0, 0)),
            pl.BlockSpec((SUBLANES, LANES), lambda i: (0, 0)),
        ],
        out_shape=[
            jax.ShapeDtypeStruct((N_TOK, D_MODEL), F32),
            jax.ShapeDtypeStruct((N_TOK, HALF_D), I32),
            jax.ShapeDtypeStruct((nt, SUBLANES, TM), F32),
            jax.ShapeDtypeStruct((N_TOK, LANES), F32),
            jax.ShapeDtypeStruct((nt, SUBLANES, LANES), I32),
            jax.ShapeDtypeStruct((SUBLANES, LANES), I32),
        ],
        scratch_shapes=[pltpu.VMEM((1, LANES), F32)],
        compiler_params=_cparams(("arbitrary",)),
        name="outproj_residual_router",
    )(*x_args, a[0], a[1], b[0], b[1], w_bf16, w_bf16, mods, norm2.reshape(1, D_MODEL), w_route, b_route, tril, triu)


def _for_chunks(n, fn):
    for size in CHUNKS:
        @pl.when((n & size) != 0)
        def _():
            fn(n & ~(2 * size - 1), size)


N_COPIES = R_LOC // ROW_ALIGN
DUMP_ROWS = 2 * R_LOC


def _start_tile_copies(grp_ref, seg_ref, make, dummy, n=None):
    if n is None:
        n = grp_ref[0, GRP_COUNT, 0]
    for c in range(N_COPIES):
        real = c < n
        slot = seg_ref[0, grp_ref[0, GRP_EXPERT, c]] + grp_ref[0, GRP_REL, c]
        d_loc, d_slot = dummy(c)
        loc = jnp.where(real, c * ROW_ALIGN, d_loc)
        slot = jnp.where(real, slot, d_slot)
        make(pl.ds(pl.multiple_of(loc, ROW_ALIGN), ROW_ALIGN), pl.ds(pl.multiple_of(slot, ROW_ALIGN), ROW_ALIGN)).start()


def _wait_tile_copies(make):
    make(pl.ds(0, R_LOC), pl.ds(0, R_LOC)).wait()


def _all_experts(body):
    lax.fori_loop(0, N_EXPERTS, lambda e, c: (body(e), c)[1], 0)


def _dispatch_kernel(grp_ref, seg_ref, loct_ref, u_ref, xb_ref, buf, zbuf, sem):
    i = pl.program_id(0)
    last = pl.num_programs(0) - 1
    cur = i % 2

    def maker(b):
        return lambda loc, slot: pltpu.make_async_copy(buf.at[b, loc], xb_ref.at[slot], sem.at[b])

    @pl.when(i >= 2)
    def _():
        _wait_tile_copies(maker(cur))

    lo, hi = _unpack_bf16_pair(u_ref[...])
    row = lax.broadcasted_iota(I32, (R_LOC, TM), 0).astype(F32)
    sel = jnp.logical_or(row == loct_ref[0, 2:3, :], row == loct_ref[0, 3:4, :])
    pt = jnp.where(sel, 1.0, 0.0).astype(BF16)
    buf[cur] = _pack_exact_bf16_pair(jnp.dot(pt, lo, preferred_element_type=F32),
                                     jnp.dot(pt, hi, preferred_element_type=F32))
    _start_tile_copies(grp_ref, seg_ref, maker(cur),
                       lambda c: (c * ROW_ALIGN, MOE_ROWS + cur * R_LOC + c * ROW_ALIGN))

    @pl.when(i == last)
    def _():
        @pl.when(i >= 1)
        def _():
            _wait_tile_copies(maker(1 - cur))

        _wait_tile_copies(maker(cur))

        zbuf[...] = jnp.zeros_like(zbuf)

        def tail(e, act):
            total = seg_ref[1, e]
            dst = seg_ref[0, e] + total

            def one(off, size):
                act(pltpu.make_async_copy(zbuf.at[pl.ds(0, size)],
                                          xb_ref.at[pl.ds(pl.multiple_of(dst + off, ROW_ALIGN), size)], sem.at[2]))

            _for_chunks((-total) & (MOE_BLK - 1), one)

        _all_experts(lambda e: tail(e, lambda cp: cp.start()))
        _all_experts(lambda e: tail(e, lambda cp: cp.wait()))

        e_last = N_EXPERTS - 1
        used = (seg_ref[0, e_last] + seg_ref[1, e_last] + MOE_BLK - 1) // MOE_BLK

        def spare(j):
            return pltpu.make_async_copy(zbuf, xb_ref.at[pl.ds(pl.multiple_of(j * MOE_BLK, MOE_BLK), MOE_BLK)],
                                         sem.at[2])

        lax.fori_loop(used, (MOE_ROWS + DUMP_ROWS) // MOE_BLK, lambda j, c: (spare(j).start(), c)[1], 0)
        lax.fori_loop(used, (MOE_ROWS + DUMP_ROWS) // MOE_BLK, lambda j, c: (spare(j).wait(), c)[1], 0)


def _dispatch(grp, seg, loct, u):
    return pl.pallas_call(
        _dispatch_kernel,
        grid=(N_TOK // TM,),
        in_specs=[
            pl.BlockSpec((1, SUBLANES, LANES), lambda i: (i, 0, 0), memory_space=pltpu.SMEM),
            pl.BlockSpec(memory_space=pltpu.SMEM),
            pl.BlockSpec((1, SUBLANES, TM), lambda i: (i, 0, 0)),
            pl.BlockSpec((TM, HALF_D), lambda i: (i, 0)),
        ],
        out_specs=pl.BlockSpec(memory_space=pl.ANY),
        out_shape=jax.ShapeDtypeStruct((MOE_ROWS + DUMP_ROWS, HALF_D), I32),
        scratch_shapes=[pltpu.VMEM((2, R_LOC, HALF_D), I32), pltpu.VMEM((MOE_BLK, HALF_D), I32),
                        pltpu.SemaphoreType.DMA((3,))],
        compiler_params=_cparams(("arbitrary",)),
        name="moe_dispatch",
    )(grp, seg, loct, u)


def _expert_kernel(seg_ref, w1_ref, w3_ref, w2_ref, xb_ref, yb_ref, w1b, w3b, w2b, xbuf, ybuf, xsem, ysem):
    e = pl.program_id(0)
    last_e = pl.num_programs(0) - 1
    blk0 = seg_ref[0, e] // MOE_BLK
    nblk = (seg_ref[1, e] + MOE_BLK - 1) // MOE_BLK
    used = (seg_ref[0, N_EXPERTS - 1] + seg_ref[1, N_EXPERTS - 1] + MOE_BLK - 1) // MOE_BLK

    def rows(g):
        return pl.ds(pl.multiple_of(g * MOE_BLK, MOE_BLK), MOE_BLK)

    def fetch(g, b):
        return pltpu.make_async_copy(xb_ref.at[rows(g)], xbuf.at[b], xsem.at[b])

    def flush(g, b):
        return pltpu.make_async_copy(ybuf.at[b], yb_ref.at[rows(g)], ysem.at[b])

    row_queue = 1

    @pl.when(e == 0)
    def _():
        fetch(0, 0).start(priority=row_queue)

    @pl.when(nblk > 0)
    def _():
        w1b[...] = w1_ref[0, 0].astype(BF16)
        w3b[...] = w3_ref[0, 0].astype(BF16)
        w2b[...] = w2_ref[0, 0].astype(BF16)

        def block(i, carry):
            g = blk0 + i
            b = g % 2
            fetch(g, b).wait()

            @pl.when(g + 1 < used)
            def _():
                fetch(g + 1, 1 - b).start(priority=row_queue)

            @pl.when(g >= 2)
            def _():
                flush(g - 2, b).wait()

            def ffn(x_packed):
                lo, hi = _unpack_bf16_pair(x_packed)

                def up(wb):
                    return (jnp.dot(lo, wb[:HALF_D, :], preferred_element_type=F32)
                            + jnp.dot(hi, wb[HALF_D:, :], preferred_element_type=F32))

                hdn = (_silu(up(w1b)) * up(w3b)).astype(BF16)
                y = jnp.dot(hdn, w2b[...], preferred_element_type=F32)
                return _pack_bf16_pair(y[:, :HALF_D], y[:, HALF_D:])

            half = MOE_BLK // 2
            slots_here = seg_ref[1, e] - i * MOE_BLK

            @pl.when(slots_here > half)
            def _():
                ybuf[b] = ffn(xbuf[b])

            @pl.when(slots_here <= half)
            def _():
                ybuf[b, :half, :] = ffn(xbuf[b, :half, :])
                ybuf[b, half:, :] = jnp.zeros((half, HALF_D), I32)

            flush(g, b).start(priority=row_queue)
            return carry

        lax.fori_loop(0, nblk, block, 0)

    @pl.when(e == last_e)
    def _():
        @pl.when(used >= 2)
        def _():
            flush(used - 2, used % 2).wait()

        flush(used - 1, (used - 1) % 2).wait()
        xbuf[0] = jnp.zeros((MOE_BLK, HALF_D), I32)

        def spare(g):
            return pltpu.make_async_copy(xbuf.at[0], yb_ref.at[rows(g)], xsem.at[0])

        lax.fori_loop(used, MOE_NB, lambda g, c: (spare(g).start(), c)[1], 0)
        lax.fori_loop(used, MOE_NB, lambda g, c: (spare(g).wait(), c)[1], 0)


def _experts(seg, xb, w1, w3, w2, layer):
    hbm = pl.BlockSpec(memory_space=pl.ANY)
    w_in = pl.BlockSpec((1, 1, D_MODEL, D_EXPERT), lambda e, s: (layer, e, 0, 0))
    w_out = pl.BlockSpec((1, 1, D_EXPERT, D_MODEL), lambda e, s: (layer, e, 0, 0))
    return pl.pallas_call(
        _expert_kernel,
        grid_spec=pltpu.PrefetchScalarGridSpec(
            num_scalar_prefetch=1,
            grid=(N_EXPERTS,),
            in_specs=[w_in, w_in, w_out, hbm],
            out_specs=hbm,
            scratch_shapes=[pltpu.VMEM((D_MODEL, D_EXPERT), BF16), pltpu.VMEM((D_MODEL, D_EXPERT), BF16),
                            pltpu.VMEM((D_EXPERT, D_MODEL), BF16),
                            pltpu.VMEM((2, MOE_BLK, HALF_D), I32), pltpu.VMEM((2, MOE_BLK, HALF_D), I32),
                            pltpu.SemaphoreType.DMA((2,)), pltpu.SemaphoreType.DMA((2,))],
        ),
        out_shape=jax.ShapeDtypeStruct((MOE_ROWS, HALF_D), I32),
        compiler_params=_cparams(("arbitrary",)),
        name="moe_experts",
    )(seg, w1, w3, w2, xb)


def _combine_kernel(*refs, final):
    grp_ref, nxt_ref, seg_ref, x_ref, info_ref, mod_ref, yb_ref = refs[:7]
    if final:
        fin_ref, olat_ref, octx_ref, ybuf, sem = refs[7:]
    else:
        g1n_ref, modn_ref, wn_ref, o_ref, proj_ref, ybuf, sem = refs[7:]
    i = pl.program_id(0)
    cur = i % 2

    def maker(b):
        return lambda loc, slot: pltpu.make_async_copy(yb_ref.at[slot], ybuf.at[b, loc], sem.at[b])

    def dummy(c):
        return R_LOC + c * ROW_ALIGN, c * ROW_ALIGN

    @pl.when(i == 0)
    def _():
        ybuf[...] = jnp.zeros_like(ybuf)
        _start_tile_copies(grp_ref, seg_ref, maker(0), dummy)

    has_next = i + 1 < pl.num_programs(0)
    _start_tile_copies(nxt_ref, seg_ref, maker(1 - cur), dummy, jnp.where(has_next, nxt_ref[0, GRP_COUNT, 0], 0))
    _wait_tile_copies(maker(cur))

    @pl.when(jnp.logical_not(has_next))
    def _():
        _wait_tile_copies(maker(1 - cur))

    info = info_ref[...]
    col = lax.broadcasted_iota(I32, (TM, R_LOC), 1).astype(F32)
    p = jnp.where(col == info[:, 2:3], info[:, 0:1], jnp.where(col == info[:, 3:4], info[:, 1:2], 0.0))
    p = p.astype(BF16)
    lo, hi = _unpack_bf16_pair(ybuf[cur, :R_LOC, :])
    y = jnp.concatenate([jnp.dot(p, lo, preferred_element_type=F32),
                         jnp.dot(p, hi, preferred_element_type=F32)], axis=1)
    out = x_ref[...] + mod_ref[0][5:6] * y
    if not final:
        o_ref[...] = out
        u_next = _norm_mod(out, g1n_ref[...], modn_ref[0], 0, 1)
        proj_ref[...] = jnp.dot(u_next.astype(BF16), wn_ref[...], preferred_element_type=F32)
        return
    out = out * lax.rsqrt(jnp.mean(out * out, axis=-1, keepdims=True) + EPS) * fin_ref[...]

    @pl.when(i < N_LAT_TILES)
    def _():
        olat_ref[...] = out

    @pl.when(i >= N_LAT_TILES)
    def _():
        octx_ref[...] = out


def _combine(x, grp, seg, info, mods, yb, final_g=None, next_proj=None):
    final = final_g is not None
    nt = N_TOK // TM
    in_specs = [
        pl.BlockSpec((1, SUBLANES, LANES), lambda i: (i, 0, 0), memory_space=pltpu.SMEM),
        pl.BlockSpec((1, SUBLANES, LANES), lambda i: (jnp.minimum(i + 1, nt - 1), 0, 0), memory_space=pltpu.SMEM),
        pl.BlockSpec(memory_space=pltpu.SMEM),
        pl.BlockSpec((TM, D_MODEL), lambda i: (i, 0)),
        pl.BlockSpec((TM, LANES), lambda i: (i, 0)),
        pl.BlockSpec((1, MOD_ROWS, D_MODEL), lambda i: (i * TM // SEG, 0, 0)),
        pl.BlockSpec(memory_space=pl.ANY),
    ]
    args = [grp, grp, seg, x, info, mods, yb]
    if final:
        in_specs.append(pl.BlockSpec((1, D_MODEL), lambda i: (0, 0)))
        args.append(final_g.reshape(1, D_MODEL))
        out_specs = _group_specs(D_MODEL)
        out_shape = [jax.ShapeDtypeStruct((N_SAMPLE, D_MODEL), F32), jax.ShapeDtypeStruct((N_PROMPT, D_MODEL), F32)]
    else:
        g1n, modn, wn = next_proj
        n = wn.shape[1]
        in_specs += [pl.BlockSpec((1, D_MODEL), lambda i: (0, 0)),
                     pl.BlockSpec((1, MOD_ROWS, D_MODEL), lambda i: (i * TM // SEG, 0, 0)),
                     pl.BlockSpec((D_MODEL, n), lambda i: (0, 0), pipeline_mode=pl.Buffered(1))]
        args += [g1n.reshape(1, D_MODEL), modn, wn]
        out_specs = [pl.BlockSpec((TM, D_MODEL), lambda i: (i, 0)), pl.BlockSpec((TM, n), lambda i: (i, 0))]
        out_shape = [jax.ShapeDtypeStruct((N_TOK, D_MODEL), F32), jax.ShapeDtypeStruct((N_TOK, n), F32)]
    return pl.pallas_call(
        functools.partial(_combine_kernel, final=final),
        grid=(nt,),
        in_specs=in_specs,
        out_specs=out_specs,
        out_shape=out_shape,
        scratch_shapes=[pltpu.VMEM((2, 2 * R_LOC, HALF_D), I32), pltpu.SemaphoreType.DMA((2,))],
        compiler_params=_cparams(("arbitrary",)),
        name="moe_combine_final" if final else "moe_combine",
    )(*args)


def _route_weights(w_grp, b_grp, w_exp, b_exp):
    w_route = jnp.zeros((D_MODEL, LANES), F32)
    w_route = w_route.at[:, :N_GROUPS].set(w_grp).at[:, N_GROUPS:N_GROUPS + N_EXPERTS].set(w_exp)
    b_route = jnp.zeros((1, LANES), F32)
    b_route = b_route.at[0, :N_GROUPS].set(b_grp).at[0, N_GROUPS:N_GROUPS + N_EXPERTS].set(b_exp)
    return w_route, b_route


def _moe_layer(x, route, layer, mods, w1, w3, w2, final_g=None, next_proj=None):
    u, loct, info, grp, tot = route
    total = tot[0, :N_EXPERTS]
    padded = (total + MOE_BLK - 1) // MOE_BLK * MOE_BLK
    pad_end = jnp.cumsum(padded)
    pad_start = pad_end - padded
    seg = jnp.zeros((2, LANES), I32).at[0, :N_EXPERTS].set(pad_start).at[1, :N_EXPERTS].set(total)
    xb = _dispatch(grp, seg, loct, u)
    yb = _experts(seg, xb, w1, w3, w2, layer)
    return _combine(x, grp, seg, info, mods, yb, final_g, next_proj)


def _pad_rows(w, rows=SUBLANES):
    return jnp.zeros((rows, w.shape[1]), w.dtype).at[:w.shape[0]].set(w)


def _pad2(w, rows, cols):
    return jnp.zeros((rows, cols), w.dtype).at[:w.shape[0], :w.shape[1]].set(w)


def kernel(x_prompt, x_sample, c, c_ctx, state_ret, cache_k, cache_v, ada_w, ada_b, norm1, norm2, final_norm, ev_w_in, ev_w_out, ret_decay_fwd, ret_decay_bwd, sc_conv_w, od_w_in, od_w_out, hy_conv_w, hy_w1, hy_b1, hy_fr1, hy_w2, hy_b2, hy_fr2, hy_w3, hy_bias, da_lambda, da_subln, moe_w_grp, moe_b_grp, moe_w_exp, moe_b_exp, moe_w1, moe_w3, moe_w2):
    x = (x_sample.reshape(N_SAMPLE, D_MODEL), x_prompt.reshape(N_PROMPT, D_MODEL))

    cvecs = jnp.zeros((16, D_MODEL), F32).at[:DEC_BATCH].set(c).at[DEC_BATCH].set(c_ctx)
    m = _mods(cvecs, ada_w, ada_b).reshape(DEPTH, 16, N_MOD, D_MODEL)
    seg_mods = jnp.concatenate(
        [m[:, :DEC_BATCH], jnp.broadcast_to(m[:, DEC_BATCH:DEC_BATCH + 1], (DEPTH, N_SEG - DEC_BATCH, N_MOD, D_MODEL))],
        axis=1)
    seg_mods = jnp.pad(seg_mods, ((0, 0), (0, 0), (0, MOD_ROWS - N_MOD), (0, 0)))

    tril = jnp.asarray(np.tril(np.ones((TM, TM), np.float32)), BF16)
    triu = jnp.asarray(np.triu(np.ones((LANES, LANES), np.float32), 1), BF16)

    def w_in(layer):
        w = ev_w_in if layer % 2 == 0 else od_w_in
        return w[layer // 2].astype(BF16)

    proj = None
    for l in range(DEPTH):
        mods = seg_mods[l]
        i = l // 2
        if l % 2 == 0:
            if proj is None:
                proj = _norm_mod_matmul(x, norm1[l], mods, w_in(l), 0, 1)
            scw = _pad_rows(sc_conv_w[i])
            ret_l, sc_l = _even_mixer(proj, scw, ret_decay_fwd[i], ret_decay_bwd[i], state=state_ret[:, i:i + 1])
            ret_c, sc_c, new_state = _even_mixer(proj, scw, ret_decay_fwd[i], ret_decay_bwd[i])
            mixed, w_out = ((ret_l, ret_c), (sc_l, sc_c)), ev_w_out[i]
        else:
            lambda_init = 0.8 - 0.6 * math.exp(-0.3 * l)
            if proj is None:
                proj = _norm_mod_matmul(x, norm1[l], mods, w_in(l), 0, 1)
            conv_w = _pad_rows(hy_conv_w[i])
            bias = _pad_rows(hy_bias[i])
            w1 = _pad2(hy_w1[i], LANES, LANES)
            b1 = _pad2(hy_b1[i][None], 1, LANES)
            f1 = _pad2(hy_fr1[i][None], 1, LANES)
            w2 = _pad2(hy_w2[i], LANES, LANES)
            b2 = _pad2(hy_b2[i][None], 1, LANES)
            f2 = _pad2(hy_fr2[i][None], 1, LANES)
            w3 = _pad2(hy_w3[i], LANES, hy_w3.shape[2])
            hy = []
            for latent in (True, False):
                seq = DEC_SEQ if latent else SEQ
                fwd_np, inv_np = _dft_tables(seq)
                zfeat, decay = _hyena_consts(seq)
                fwd_bf, fwd_lo = _split_table(jnp.asarray(fwd_np), f"dft_fwd_split_{seq}")
                inv_bf, _ = _split_table(jnp.asarray(inv_np), f"dft_inv_split_{seq}")
                filters = _hy_filters(seq, jnp.asarray(zfeat), jnp.asarray(decay), fwd_bf, fwd_lo,
                                      w1, b1, f1, w2, b2, f2, w3)
                hy.append(_hy_apply(proj, conv_w, fwd_bf, inv_bf, filters, bias, latent))
            cos, sin = _rope_tables(DEC_SEQ)
            att_l = _attention(proj, da_lambda[i], da_subln[i], lambda_init, True,
                               rope=(jnp.asarray(cos), jnp.asarray(sin)),
                               ctx_k=cache_k[:, i:i + 1], ctx_v=cache_v[:, i:i + 1])[0]
            att_c, new_k, new_v = _attention(proj, da_lambda[i], da_subln[i], lambda_init, False)
            mixed, w_out = (tuple(hy), (att_l, att_c)), od_w_out[i]
        w_route, b_route = _route_weights(moe_w_grp[l], moe_b_grp[l], moe_w_exp[l], moe_b_exp[l])
        x, *route = _proj_residual_router(x, mixed[0], mixed[1], w_out.astype(BF16), mods, 2,
                                          norm2[l], w_route, b_route, tril, triu)
        if l == DEPTH - 1:
            x = _moe_layer(x, route, l, mods, moe_w1, moe_w3, moe_w2, final_g=final_norm)
        else:
            x, proj = _moe_layer(x, route, l, mods, moe_w1, moe_w3, moe_w2,
                                 next_proj=(norm1[l + 1], seg_mods[l + 1], w_in(l + 1)))

    y_sample, y_prompt = x
    return (y_prompt.reshape(BATCH, SEQ, D_MODEL), y_sample.reshape(DEC_BATCH, DEC_SEQ, D_MODEL),
            new_state, new_k, new_v)
```

```python
import functools
import math

import numpy as np
import jax
import jax.numpy as jnp
from jax import lax
from jax.experimental import pallas as pl
from jax.experimental.pallas import tpu as pltpu

F32 = jnp.float32
BF16 = jnp.bfloat16
I32 = jnp.int32
HIGHEST = lax.Precision.HIGHEST

D_MODEL = 1024
BATCH = 32
SEQ = 256
DEPTH = 2
DEC_BATCH = 8
DEC_SEQ = 1024
PAST_LEN = 256
GRID_W = 64
N_MOD = 6
EPS = 1e-6
GN_EPS = 1e-5

RET_DK = 128
RET_DV = 128
RET_HEADS = 4
RET_W = 512
RET_QK = 512
SC_W = 512
EVEN_IN = 2 * RET_QK + 2 * RET_W + 3 * SC_W

HY_W = 512
HY_ORDER = 2
HY_EMB = 33
HY_FFN = 64
HY_TARGET = 1e-2
HY_FAST_PCT = 0.3
HY_SLOW_PCT = 1.5
DA_DH = 64
DA_HEADS = 4
DA_W = 512
ODD_IN = 3 * HY_W + 3 * DA_W
ROPE_THETA = 10000.0

N_GROUPS = 4
EXP_PER_GROUP = 8
N_EXPERTS = 32
TOP_K = 2
D_EXPERT = 512

N_SAMPLE = DEC_BATCH * DEC_SEQ
N_PROMPT = BATCH * SEQ
N_TOK = N_SAMPLE + N_PROMPT
SEG = 1024
N_SEG = N_TOK // SEG
MOD_ROWS = 8

LANES = 128
SUBLANES = 8
VMEM_LIMIT = 56 * 1024 * 1024

TM = 256
MOE_BLK = 1024
MOE_PARTS = 4
ROW_ALIGN = SUBLANES
R_LOC = (TOP_K * TM + N_EXPERTS * (ROW_ALIGN - 1) + LANES - 1) // LANES * LANES
CHUNKS = (512, 256, 128, 64, 32, 16, 8)
MOE_ROWS = ((N_TOK * TOP_K + N_EXPERTS * (N_TOK // TM) * (ROW_ALIGN - 1) + N_EXPERTS * (MOE_BLK - 1))
            // MOE_BLK + 1) * MOE_BLK
MOE_NB = MOE_ROWS // MOE_BLK
HY_SLAB = 512
ATT_TQ = 256


def _cparams(sem, vmem=VMEM_LIMIT):
    return pltpu.CompilerParams(dimension_semantics=sem, vmem_limit_bytes=vmem)


def _silu(x):
    return x * (1.0 / (1.0 + jnp.exp(-x)))


def _split_bf16(a):
    hi = a.astype(BF16)
    return hi, (a - hi.astype(F32)).astype(BF16)


def _dot_3pass(a, b):
    ah, al = _split_bf16(a)
    bh, bl = _split_bf16(b)
    dot = functools.partial(jnp.dot, preferred_element_type=F32)
    return dot(ah, bh) + (dot(ah, bl) + dot(al, bh))


MODS_TN = 1536


def _mods_kernel(c_ref, w_ref, b_ref, o_ref):
    s = _silu(c_ref[...])
    o_ref[0] = jnp.dot(s, w_ref[0], preferred_element_type=F32, precision=HIGHEST) + b_ref[0]


def _mods(cvecs, ada_w, ada_b):
    n = N_MOD * D_MODEL
    return pl.pallas_call(
        _mods_kernel,
        grid=(DEPTH, n // MODS_TN),
        in_specs=[
            pl.BlockSpec((16, D_MODEL), lambda l, j: (0, 0)),
            pl.BlockSpec((1, D_MODEL, MODS_TN), lambda l, j: (l, 0, j)),
            pl.BlockSpec((1, 1, MODS_TN), lambda l, j: (l, 0, j)),
        ],
        out_specs=pl.BlockSpec((1, 16, MODS_TN), lambda l, j: (l, 0, j)),
        out_shape=jax.ShapeDtypeStruct((DEPTH, 16, n), F32),
        compiler_params=_cparams(("parallel", "parallel")),
        name="adaln_mods",
    )(cvecs, ada_w, ada_b.reshape(DEPTH, 1, n))


def _norm_mod(x, g, mod, shift_idx, scale_idx):
    ms = jnp.mean(x * x, axis=-1, keepdims=True)
    y = x * lax.rsqrt(ms + EPS) * g
    return y * (1.0 + mod[scale_idx:scale_idx + 1]) + mod[shift_idx:shift_idx + 1]


N_LAT_TILES = N_SAMPLE // TM
TM_WIDE = 512
TM_ROUTE = 1024


def _group_specs(width, tm=TM):
    n_lat = N_SAMPLE // tm
    lat = pl.BlockSpec((tm, width), lambda i: (jnp.minimum(i, n_lat - 1), 0))
    ctx = pl.BlockSpec((tm, width), lambda i: (jnp.maximum(i - n_lat, 0), 0))
    return [lat, ctx]


def _group_pick(lat_ref, ctx_ref):
    n_lat = N_SAMPLE // lat_ref.shape[0]
    return jnp.where(pl.program_id(0) < n_lat, lat_ref[...], ctx_ref[...])


def _rows_specs(x, tm=TM):
    if isinstance(x, tuple):
        return _group_specs(D_MODEL, tm), list(x)
    return [pl.BlockSpec((tm, D_MODEL), lambda i: (i, 0))], [x]


def _rows_value(refs):
    return _group_pick(*refs) if len(refs) == 2 else refs[0][...]


def _nmm_kernel(*refs, n_x, shift_idx, scale_idx):
    g_ref, mod_ref, w_ref, o_ref = refs[n_x:]
    u = _norm_mod(_rows_value(refs[:n_x]), g_ref[...], mod_ref[0], shift_idx, scale_idx)
    o_ref[...] = jnp.dot(u.astype(BF16), w_ref[...], preferred_element_type=F32)


def _norm_mod_matmul(x, g, mods, w_bf16, shift_idx, scale_idx):
    n = w_bf16.shape[1]
    tm = TM_WIDE
    x_specs, x_args = _rows_specs(x, tm)
    return pl.pallas_call(
        functools.partial(_nmm_kernel, n_x=len(x_args), shift_idx=shift_idx, scale_idx=scale_idx),
        grid=(N_TOK // tm,),
        in_specs=x_specs + [
            pl.BlockSpec((1, D_MODEL), lambda i: (0, 0)),
            pl.BlockSpec((1, MOD_ROWS, D_MODEL), lambda i: (i * tm // SEG, 0, 0)),
            pl.BlockSpec((D_MODEL, n), lambda i: (0, 0), pipeline_mode=pl.Buffered(1)),
        ],
        out_specs=pl.BlockSpec((tm, n), lambda i: (i, 0)),
        out_shape=jax.ShapeDtypeStruct((N_TOK, n), F32),
        compiler_params=_cparams(("parallel",)),
        name="norm_mod_inproj",
    )(*x_args, g.reshape(1, D_MODEL), mods, w_bf16)


def _conv3_rows(z, w):
    n = z.shape[0]
    row = lax.broadcasted_iota(I32, z.shape, 0)
    zm = jnp.where(row == 0, 0.0, pltpu.roll(z, 1, 0))
    zp = jnp.where(row == n - 1, 0.0, pltpu.roll(z, n - 1, 0))
    return zm * w[0:1] + z * w[1:2] + zp * w[2:3]


CTX_NSEQ = 4


def _log_gamma(dec):
    return jnp.log1p(-jnp.exp(dec))


def _even_kernel(*refs, seq, latent, tq, heads, nseq):
    (q_ref, k_ref, v_ref, g_ref, gb_ref, gc_ref, hx_ref, scw_ref, df_ref, db_ref) = refs[:10]
    if latent:
        s0_ref, ret_ref, sc_ref, mask_ref = refs[10:]
    else:
        ret_ref, sc_ref, st_ref, mask_ref = refs[10:]

    decays = []
    for hh in range(heads):
        lgf = _log_gamma(df_ref[hh])
        lgb = _log_gamma(db_ref[hh])
        decays.append((lgf, lgb))

        @pl.when(pl.program_id(1) == 0)
        def _():
            for r in range(seq // tq):
                t = lax.broadcasted_iota(I32, (tq, seq), 0) + r * tq
                s = lax.broadcasted_iota(I32, (tq, seq), 1)
                d = (t - s).astype(F32)
                mf = jnp.where(d >= 0, jnp.exp(jnp.maximum(d, 0.0) * lgf), 0.0)
                mb = jnp.where(d <= 0, jnp.exp(jnp.maximum(-d, 0.0) * lgb), 0.0)
                mask_ref[hh, r * tq:(r + 1) * tq, :] = mf + mb

    for sq in range(nseq):
        whole = slice(sq * seq, (sq + 1) * seq)
        for hh in range(heads):
            lanes = slice(hh * RET_DK, (hh + 1) * RET_DK)
            lgf, lgb = decays[hh]
            k = k_ref[whole, lanes] * (RET_DK ** -0.5)
            kb = k.astype(BF16)
            vb = v_ref[whole, lanes].astype(BF16)
            if latent:
                s0f = s0_ref[sq, 0, 0, hh].astype(BF16)
                s0b = s0_ref[sq, 0, 1, hh].astype(BF16)
            for r in range(seq // tq):
                rows = slice(sq * seq + r * tq, sq * seq + (r + 1) * tq)
                qb = q_ref[rows, lanes].astype(BF16)
                s = lax.dot_general(qb, kb, (((1,), (1,)), ((), ())), preferred_element_type=F32)
                p = (s * mask_ref[hh, r * tq:(r + 1) * tq, :]).astype(BF16)
                o = jnp.dot(p, vb, preferred_element_type=F32)
                if latent:
                    tpos = (lax.broadcasted_iota(I32, (tq, 1), 0) + r * tq).astype(F32)
                    o = o + jnp.dot(qb, s0f, preferred_element_type=F32) * jnp.exp((tpos + 1.0) * lgf)
                    o = o + jnp.dot(qb, s0b, preferred_element_type=F32) * jnp.exp((seq - tpos) * lgb)
                mu = jnp.mean(o, axis=-1, keepdims=True)
                var = jnp.mean(jnp.square(o - mu), axis=-1, keepdims=True)
                on = (o - mu) * lax.rsqrt(var + GN_EPS)
                ret_ref[rows, lanes] = _silu(g_ref[rows, lanes]) * on

            if not latent:
                spos = lax.broadcasted_iota(I32, (seq, 1), 0).astype(F32)
                kf = (k * jnp.exp((seq - 1.0 - spos) * lgf)).T.astype(BF16)
                kr = (k * jnp.exp(spos * lgb)).T.astype(BF16)
                st_ref[sq, 0, 0, hh] = jnp.dot(kf, vb, preferred_element_type=F32)
                st_ref[sq, 0, 1, hh] = jnp.dot(kr, vb, preferred_element_type=F32)

        sc_ref[whole, :] = gb_ref[whole, :] * _conv3_rows(gc_ref[whole, :] * hx_ref[whole, :], scw_ref[...])


def _even_mixer(proj, sc_w_pad, dec_f, dec_b, state=None):
    latent = state is not None
    seq = DEC_SEQ if latent else SEQ
    nb = DEC_BATCH if latent else BATCH
    tq = min(seq, 256)
    heads = 2 if latent else RET_HEADS
    nseq = 1 if latent else CTX_NSEQ
    rows = nseq * seq
    row0 = 0 if latent else N_SAMPLE // rows
    steps = RET_HEADS // heads
    hw = heads * RET_DK

    def col(c0):
        return pl.BlockSpec((rows, hw), lambda h, b: (row0 + b, c0 * steps + h))

    in_specs = [col(0), col(1), col(2), col(3), col(4), col(5), col(6),
                pl.BlockSpec((SUBLANES, hw), lambda h, b: (0, h)),
                pl.BlockSpec((heads, 1, 1), lambda h, b: (h, 0, 0)),
                pl.BlockSpec((heads, 1, 1), lambda h, b: (h, 0, 0))]
    args = [proj] * 7 + [sc_w_pad, dec_f.reshape(RET_HEADS, 1, 1), dec_b.reshape(RET_HEADS, 1, 1)]
    out_spec = pl.BlockSpec((rows, hw), lambda h, b: (b, h))
    out_shape = [jax.ShapeDtypeStruct((nb * seq, RET_W), F32), jax.ShapeDtypeStruct((nb * seq, SC_W), F32)]
    out_specs = [out_spec, out_spec]
    st_spec = pl.BlockSpec((nseq, 1, 2, heads, RET_DK, RET_DV), lambda h, b: (b, 0, 0, h, 0, 0))
    if latent:
        in_specs.append(st_spec)
        args.append(state)
    else:
        out_shape.append(jax.ShapeDtypeStruct((BATCH, 1, 2, RET_HEADS, RET_DK, RET_DV), F32))
        out_specs.append(st_spec)
    return pl.pallas_call(
        functools.partial(_even_kernel, seq=seq, latent=latent, tq=tq, heads=heads, nseq=nseq),
        grid=(steps, nb // nseq),
        in_specs=in_specs,
        out_specs=out_specs,
        out_shape=out_shape,
        scratch_shapes=[pltpu.VMEM((heads, seq, seq), F32)],
        compiler_params=_cparams(("arbitrary", "arbitrary")),
        name="even_mixer_latent" if latent else "even_mixer_context",
    )(*args)


def _dft_tables(length):
    f = np.arange(length, dtype=np.float64)[:, None]
    t = np.arange(length, dtype=np.float64)[None, :]
    ang = np.pi * ((f * t) % (2 * length)) / length
    cos, sin = np.cos(ang), np.sin(ang)
    sin[0, :] = (-1.0) ** np.arange(length)
    fwd = np.concatenate([cos, sin], axis=0)
    wgt = np.full((2 * length,), 2.0)
    wgt[0] = 1.0
    wgt[length] = 1.0
    inv = (fwd * wgt[:, None] / (2.0 * length)).T
    return fwd.astype(np.float32), np.ascontiguousarray(inv).astype(np.float32)


def _hyena_consts(length):
    t = np.linspace(0.0, 1.0, length)[:, None]
    bands = (HY_EMB - 1) // 2
    w = 2.0 * np.pi * np.arange(length)[:, None] / length
    f = np.linspace(1e-4, bands - 1, bands)[None, :]
    z = np.concatenate([t, np.cos(f * w), -np.sin(f * w)], axis=-1)
    zpad = np.zeros((length, LANES))
    zpad[:, :HY_EMB] = z
    deltas = np.abs(np.linspace(math.log(HY_TARGET) / HY_SLOW_PCT, math.log(HY_TARGET) / HY_FAST_PCT, HY_W))
    decay = np.exp(-t * deltas[None, :])
    return zpad.astype(np.float32), decay.astype(np.float32)


def _split_kernel(x_ref, hi_ref, lo_ref):
    hi, lo = _split_bf16(x_ref[...])
    hi_ref[...] = hi
    lo_ref[...] = lo


def _split_table(x, name):
    rows = min(x.shape[0], 1024)
    spec = pl.BlockSpec((rows, x.shape[1]), lambda i: (i, 0))
    out = jax.ShapeDtypeStruct(x.shape, BF16)
    return pl.pallas_call(
        _split_kernel,
        grid=(x.shape[0] // rows,),
        in_specs=[spec],
        out_specs=[spec, spec],
        out_shape=[out, out],
        compiler_params=_cparams(("parallel",)),
        name=name,
    )(x)


def _dot_3pass_split(ah, al, b):
    bh, bl = _split_bf16(b)
    dot = functools.partial(jnp.dot, preferred_element_type=F32)
    return dot(ah, bh) + (dot(ah, bl) + dot(al, bh))


def _hy_filter_kernel(z_ref, dec_ref, w1_ref, b1_ref, f1_ref, w2_ref, b2_ref, f2_ref, w3f_ref, w3b_ref,
                      fwd_ref, fwd_lo_ref, ka_ref, ka2_ref, kb_ref, *, seq):
    h = jnp.sin(f1_ref[...] * (jnp.dot(z_ref[...], w1_ref[...], preferred_element_type=F32,
                                       precision=HIGHEST) + b1_ref[...]))
    h = jnp.sin(f2_ref[...] * (jnp.dot(h, w2_ref[...], preferred_element_type=F32,
                                       precision=HIGHEST) + b2_ref[...]))
    dec = dec_ref[...]
    hf = jnp.dot(h, w3f_ref[...], preferred_element_type=F32, precision=HIGHEST) * dec
    hb = jnp.dot(h, w3b_ref[...], preferred_element_type=F32, precision=HIGHEST) * dec
    row = lax.broadcasted_iota(I32, hb.shape, 0)
    hb = jnp.where(row == 0, 0.0, hb)
    inv_norm = 1.0 / (jnp.sum(jnp.abs(hf), axis=0, keepdims=True)
                      + jnp.sum(jnp.abs(hb), axis=0, keepdims=True) + EPS)
    p = (hf + hb) * inv_norm
    q = (hf - hb) * inv_norm
    fp = _dot_3pass_split(fwd_ref[...], fwd_lo_ref[...], p)
    fq = _dot_3pass_split(fwd_ref[seq:, :], fwd_lo_ref[seq:, :], q)
    kr = fp[:seq]
    row = lax.broadcasted_iota(I32, kr.shape, 0)
    ka_ref[0] = kr
    ka2_ref[0] = jnp.where(row == 0, fp[seq:seq + 1], kr)
    kb_ref[0] = jnp.where(row == 0, 0.0, -fq)


def _hy_filters(seq, zfeat, decay, fwd_hi, fwd_lo, w1, b1, f1, w2, b2, f2, w3):
    ns = HY_W // HY_SLAB
    per_o = 2 * ns
    small = lambda shape: pl.BlockSpec(shape, lambda o, s: (0, 0))
    out_spec = pl.BlockSpec((1, seq, HY_SLAB), lambda o, s: (o, 0, s))
    out = jax.ShapeDtypeStruct((HY_ORDER, seq, HY_W), F32)
    return pl.pallas_call(
        functools.partial(_hy_filter_kernel, seq=seq),
        grid=(HY_ORDER, ns),
        in_specs=[
            small((seq, LANES)),
            pl.BlockSpec((seq, HY_SLAB), lambda o, s: (0, s)),
            small((LANES, LANES)), small((1, LANES)), small((1, LANES)),
            small((LANES, LANES)), small((1, LANES)), small((1, LANES)),
            pl.BlockSpec((LANES, HY_SLAB), lambda o, s: (0, o * per_o + s)),
            pl.BlockSpec((LANES, HY_SLAB), lambda o, s: (0, o * per_o + ns + s)),
            small((2 * seq, seq)), small((2 * seq, seq)),
        ],
        out_specs=[out_spec, out_spec, out_spec],
        out_shape=[out, out, out],
        compiler_params=_cparams(("parallel", "parallel")),
        name=f"hyena_filters_{seq}",
    )(zfeat, decay, w1, b1, f1, w2, b2, f2, w3, w3, fwd_hi, fwd_lo)


def _hy_apply_kernel(x1_ref, x2_ref, v_ref, c1_ref, c2_ref, cv_ref, fwd_ref, inv_ref,
                     ka_ref, ka2_ref, kb_ref, bias_ref, o_ref, *, seq, nseq):
    def long_conv(sig, o):
        spec = jnp.dot(fwd_ref[...], sig.astype(BF16), preferred_element_type=F32)
        xr, xs = spec[:seq], spec[seq:]
        kb = kb_ref[o]
        yr = (xr * ka_ref[o] + xs * kb).astype(BF16)
        ys = (xs * ka2_ref[o] - xr * kb).astype(BF16)
        y = jnp.dot(inv_ref[:, :seq], yr, preferred_element_type=F32)
        y = y + jnp.dot(inv_ref[:, seq:], ys, preferred_element_type=F32)
        return y + sig * bias_ref[o:o + 1]

    for i in range(nseq):
        rows = slice(i * seq, (i + 1) * seq)
        x1 = _conv3_rows(x1_ref[rows, :], c1_ref[...])
        x2 = _conv3_rows(x2_ref[rows, :], c2_ref[...])
        v = _conv3_rows(v_ref[rows, :], cv_ref[...])
        z = x1 * long_conv(v, 0)
        o_ref[rows, :] = x2 * long_conv(z, 1)


HY_NSEQ = {True: 1, False: 4}


def _hy_apply(proj, conv_w_pad, fwd_bf, inv_bf, filters, bias_pad, latent):
    seq = DEC_SEQ if latent else SEQ
    nb = DEC_BATCH if latent else BATCH
    nseq = HY_NSEQ[latent]
    rows = nseq * seq
    row0 = 0 if latent else N_SAMPLE // rows
    ns = HY_W // HY_SLAB
    once = pl.Buffered(1)
    const = lambda shape: pl.BlockSpec(shape, lambda s, b: (0,) * len(shape), pipeline_mode=once)
    col = lambda c0: pl.BlockSpec((rows, HY_SLAB), lambda s, b: (row0 + b, c0 + s))
    cw = lambda c0: pl.BlockSpec((SUBLANES, HY_SLAB), lambda s, b: (0, c0 + s))
    filt = pl.BlockSpec((HY_ORDER, seq, HY_SLAB), lambda s, b: (0, 0, s), pipeline_mode=once)
    in_specs = [col(0), col(ns), col(2 * ns), cw(0), cw(ns), cw(2 * ns),
                const((2 * seq, seq)), const((seq, 2 * seq)), filt, filt, filt,
                pl.BlockSpec((SUBLANES, HY_SLAB), lambda s, b: (0, s))]
    args = [proj] * 3 + [conv_w_pad] * 3 + [fwd_bf, inv_bf, *filters, bias_pad]
    return pl.pallas_call(
        functools.partial(_hy_apply_kernel, seq=seq, nseq=nseq),
        grid=(ns, nb // nseq),
        in_specs=in_specs,
        out_specs=pl.BlockSpec((rows, HY_SLAB), lambda s, b: (b, s)),
        out_shape=jax.ShapeDtypeStruct((nb * seq, HY_W), F32),
        compiler_params=_cparams(("arbitrary", "arbitrary")),
        name="hyena_latent" if latent else "hyena_context",
    )(*args)


def _rope_tables(length):
    rows = length // GRID_W
    row = np.repeat(np.arange(rows, dtype=np.float64), GRID_W)
    colp = np.tile(np.arange(GRID_W, dtype=np.float64), rows)
    half = DA_DH // 2
    freqs = ROPE_THETA ** (-np.arange(0, half, 2, dtype=np.float64) / half)
    ang = np.concatenate([row[:, None] * freqs[None, :]] * 2 + [colp[:, None] * freqs[None, :]] * 2, axis=1)
    ang = np.concatenate([ang, ang], axis=1)
    lane = np.arange(LANES)
    sign = np.where((lane % half) < half // 2, -1.0, 1.0)[None, :]
    return np.cos(ang).astype(np.float32), (np.sin(ang) * sign).astype(np.float32)


def _rope(x, cos, sin_signed):
    q = DA_DH // 4
    lane = lax.broadcasted_iota(I32, x.shape, 1)
    partner = jnp.where((lane % (2 * q)) < q, pltpu.roll(x, LANES - q, 1), pltpu.roll(x, q, 1))
    return x * cos + partner * sin_signed


def _attn_kernel(*refs, seq, latent, lambda_init, heads, nseq):
    q_ref, k_ref, v_ref, lam_ref, sub_ref = refs[:5]
    if latent:
        cos_ref, sin_ref, ck_ref, cv_ref, o_ref = refs[5:]
    else:
        o_ref, nk_ref, nv_ref = refs[5:]
    hw = 2 * DA_DH
    lv = lam_ref[...]
    lam = (jnp.exp(jnp.sum(lv[0:1] * lv[1:2], axis=1, keepdims=True))
           - jnp.exp(jnp.sum(lv[2:3] * lv[3:4], axis=1, keepdims=True)) + lambda_init)
    scale = DA_DH ** -0.5
    dn = (((1,), (1,)), ((), ()))
    tq = min(seq, ATT_TQ)
    for sq, hh in [(sq, hh) for sq in range(nseq) for hh in range(heads)]:
        whole = slice(sq * seq, (sq + 1) * seq)
        lanes = slice(hh * hw, (hh + 1) * hw)
        k = k_ref[whole, lanes]
        v = v_ref[whole, lanes]
        if latent:
            k = _rope(k, cos_ref[...], sin_ref[...])
            k_all = jnp.concatenate([k, jnp.concatenate([ck_ref[sq, 0, 0, hh], ck_ref[sq, 0, 1, hh]], axis=1)], axis=0)
            v_all = jnp.concatenate([v, cv_ref[sq, 0, hh]], axis=0)
        else:
            k_all, v_all = k, v
            nk_ref[sq, 0, 0, hh] = k[:, :DA_DH]
            nk_ref[sq, 0, 1, hh] = k[:, DA_DH:]
            nv_ref[sq, 0, hh] = v
        kb = k_all.astype(BF16)
        vb = v_all.astype(BF16)
        for r in range(seq // tq):
            rows = slice(sq * seq + r * tq, sq * seq + (r + 1) * tq)
            q = q_ref[rows, lanes]
            if latent:
                q = _rope(q, cos_ref[r * tq:(r + 1) * tq, :], sin_ref[r * tq:(r + 1) * tq, :])
            q = q * scale
            first = lax.broadcasted_iota(I32, q.shape, 1) < DA_DH
            q1 = jnp.where(first, q, 0.0).astype(BF16)
            q2 = jnp.where(first, 0.0, q).astype(BF16)
            s1 = lax.dot_general(q1, kb, dn, preferred_element_type=F32)
            s2 = lax.dot_general(q2, kb, dn, preferred_element_type=F32)
            e1 = jnp.exp(s1 - jnp.max(s1, axis=-1, keepdims=True))
            e2 = jnp.exp(s2 - jnp.max(s2, axis=-1, keepdims=True))
            o1 = jnp.dot(e1.astype(BF16), vb, preferred_element_type=F32)
            o2 = jnp.dot(e2.astype(BF16), vb, preferred_element_type=F32)
            o = (o1 * (1.0 / jnp.sum(e1, axis=-1, keepdims=True))
                 - o2 * (lam / jnp.sum(e2, axis=-1, keepdims=True)))
            o = o * lax.rsqrt(jnp.mean(o * o, axis=-1, keepdims=True) + EPS) * sub_ref[...]
            o_ref[rows, lanes] = o * (1.0 - lambda_init)


def _attention(proj, lam_vec, subln, lambda_init, latent, rope=None, ctx_k=None, ctx_v=None):
    seq = DEC_SEQ if latent else SEQ
    nb = DEC_BATCH if latent else BATCH
    hw = 2 * DA_DH
    heads = DA_HEADS
    nseq = 1 if latent else CTX_NSEQ
    rows = nseq * seq
    row0 = 0 if latent else N_SAMPLE // rows
    steps = DA_HEADS // heads
    col = lambda c: pl.BlockSpec((rows, heads * hw), lambda b, h: (row0 + b, c * steps + h))
    in_specs = [col(3), col(4), col(5),
                pl.BlockSpec((4, DA_DH), lambda b, h: (0, 0)),
                pl.BlockSpec((1, hw), lambda b, h: (0, 0))]
    args = [proj] * 3 + [lam_vec, subln.reshape(1, hw)]
    out_spec = pl.BlockSpec((rows, heads * hw), lambda b, h: (b, h))
    out_shape = [jax.ShapeDtypeStruct((nb * seq, DA_W), F32)]
    out_specs = [out_spec]
    k_spec = pl.BlockSpec((nseq, 1, 2, heads, SEQ, DA_DH), lambda b, h: (b, 0, 0, h, 0, 0))
    v_spec = pl.BlockSpec((nseq, 1, heads, SEQ, hw), lambda b, h: (b, 0, h, 0, 0))
    if latent:
        in_specs += [pl.BlockSpec((seq, hw), lambda b, h: (0, 0))] * 2 + [k_spec, v_spec]
        args += [rope[0], rope[1], ctx_k, ctx_v]
    else:
        out_shape += [jax.ShapeDtypeStruct((BATCH, 1, 2, DA_HEADS, SEQ, DA_DH), F32),
                      jax.ShapeDtypeStruct((BATCH, 1, DA_HEADS, SEQ, hw), F32)]
        out_specs += [k_spec, v_spec]
    return pl.pallas_call(
        functools.partial(_attn_kernel, seq=seq, latent=latent, lambda_init=lambda_init, heads=heads, nseq=nseq),
        grid=(nb // nseq, steps),
        in_specs=in_specs,
        out_specs=out_specs,
        out_shape=out_shape,
        compiler_params=_cparams(("arbitrary", "arbitrary")),
        name="diff_attn_latent" if latent else "diff_attn_context",
    )(*args)


HALF_D = D_MODEL // 2
HI16 = -65536


def _pack_bf16_pair(lo, hi):
    lo_bits = lax.bitcast_convert_type(lo.astype(BF16).astype(F32), I32)
    hi_bits = lax.bitcast_convert_type(hi.astype(BF16).astype(F32), I32)
    return (hi_bits & HI16) | lax.shift_right_logical(lo_bits, 16)


def _pack_exact_bf16_pair(lo, hi):
    return ((lax.bitcast_convert_type(hi, I32) & HI16)
            | lax.shift_right_logical(lax.bitcast_convert_type(lo, I32), 16))


def _unpack_bf16_pair(p):
    lo = lax.bitcast_convert_type(lax.shift_left(p, 16), F32).astype(BF16)
    hi = lax.bitcast_convert_type(p & HI16, F32).astype(BF16)
    return lo, hi


GRP_EXPERT, GRP_REL, GRP_COUNT = range(3)


def _route_rows(x, g, mod, wr, br, tril_ref, triu_ref, carry_ref):
    u = _norm_mod(x, g, mod, 3, 4)
    u_packed = _pack_bf16_pair(u[:, :HALF_D], u[:, HALF_D:])
    logits = _dot_3pass(u, wr) + br
    lane = lax.broadcasted_iota(I32, logits.shape, 1)
    lane_f = lane.astype(F32)
    neg = -jnp.inf
    big = float(LANES)

    def first_max(vals):
        m = jnp.max(vals, axis=-1, keepdims=True)
        return m, jnp.min(jnp.where(vals == m, lane_f, big), axis=-1, keepdims=True)

    is_grp = lane < N_GROUPS
    gmax, gidx = first_max(jnp.where(is_grp, logits, neg))
    gsum = jnp.sum(jnp.where(is_grp, jnp.exp(logits - gmax), 0.0), axis=-1, keepdims=True)
    g_w = 1.0 / gsum
    lo = N_GROUPS + EXP_PER_GROUP * gidx
    in_grp = jnp.logical_and(lane_f >= lo, lane_f < lo + EXP_PER_GROUP)
    el = jnp.where(in_grp, logits, neg)
    v1, i1 = first_max(el)
    v2, i2 = first_max(jnp.where(lane_f == i1, neg, el))
    e = jnp.exp(v2 - v1)
    p1 = 1.0 / (1.0 + e)
    hit1 = lane_f == i1 - N_GROUPS
    hit2 = lane_f == i2 - N_GROUPS
    onehot = jnp.where(jnp.logical_or(hit1, hit2), 1.0, 0.0)
    cum = jnp.dot(tril_ref[...], onehot.astype(BF16), preferred_element_type=F32)
    n = cum[TM - 1:TM, :]
    n_pad = jnp.floor((n + (ROW_ALIGN - 1.0)) * (1.0 / ROW_ALIGN)) * ROW_ALIGN
    loc_off = jnp.dot(jnp.broadcast_to(n_pad, (SUBLANES, LANES)).astype(BF16), triu_ref[...],
                      preferred_element_type=F32)[0:1]
    base = cum - onehot + loc_off
    loc1 = jnp.sum(jnp.where(hit1, base, 0.0), axis=-1, keepdims=True)
    loc2 = jnp.sum(jnp.where(hit2, base, 0.0), axis=-1, keepdims=True)
    info = jnp.where(lane == 0, g_w * p1, jnp.where(lane == 1, g_w * (e * p1), jnp.where(lane == 2, loc1, loc2)))
    start8 = lax.broadcasted_iota(I32, (LANES, LANES), 0).astype(F32) * ROW_ALIGN
    owner = jnp.where(jnp.logical_and(start8 >= loc_off, start8 < loc_off + n_pad), 1.0, 0.0)
    lane128 = lax.broadcasted_iota(I32, (LANES, LANES), 1)
    c_exp = jnp.sum(owner * lane128.astype(F32), axis=-1, keepdims=True)
    c_rel = jnp.sum(owner * (carry_ref[...] + start8 - loc_off), axis=-1, keepdims=True)
    chunks = jnp.where(lane128 == 0, c_exp, jnp.where(lane128 == 1, c_rel, 0.0)).T
    n_chunks = jnp.sum(n_pad, axis=-1, keepdims=True) * (1.0 / ROW_ALIGN)
    row = lax.broadcasted_iota(I32, (SUBLANES, LANES), 0)
    grp = jnp.where(row == GRP_EXPERT, chunks[0:1], jnp.where(row == GRP_REL, chunks[1:2], n_chunks))
    carry_ref[...] = carry_ref[...] + n_pad
    return u_packed, info, info.T[:SUBLANES], grp.astype(I32)


def _proj_res_router_kernel(*refs, n_x, gate_idx):
    (al_ref, ac_ref, bl_ref, bc_ref, wa_ref, wb_ref, mod_ref, g2_ref, wr_ref, br_ref, tril_ref, triu_ref,
     o_ref, u_ref, loct_ref, info_ref, grp_ref, tot_ref, carry_ref) = refs[n_x:]

    @pl.when(pl.program_id(0) == 0)
    def _():
        carry_ref[...] = jnp.zeros_like(carry_ref)

    a = _group_pick(al_ref, ac_ref)
    b = _group_pick(bl_ref, bc_ref)
    y = jnp.dot(a.astype(BF16), wa_ref[...], preferred_element_type=F32)
    y = y + jnp.dot(b.astype(BF16), wb_ref[...], preferred_element_type=F32)
    mod = mod_ref[0]
    x_new = _rows_value(refs[:n_x]) + mod[gate_idx:gate_idx + 1] * y
    o_ref[...] = x_new
    for t in range(TM_ROUTE // TM):
        rows = slice(t * TM, (t + 1) * TM)
        u_packed, info, info_t, grp = _route_rows(x_new[rows], g2_ref[...], mod, wr_ref[...], br_ref[...],
                                                 tril_ref, triu_ref, carry_ref)
        u_ref[rows, :] = u_packed
        info_ref[rows, :] = info
        loct_ref[t] = info_t
        grp_ref[t] = grp
    tot_ref[...] = jnp.broadcast_to(carry_ref[...], tot_ref.shape).astype(I32)


def _proj_residual_router(x, a, b, w_bf16, mods, gate_idx, norm2, w_route, b_route, tril, triu):
    half = a[0].shape[1]
    tm = TM_ROUTE
    sub = tm // TM
    nt = N_TOK // TM
    x_specs, x_args = _rows_specs(x, tm)
    const = lambda shape: pl.BlockSpec(shape, lambda i: (0,) * len(shape))
    return pl.pallas_call(
        functools.partial(_proj_res_router_kernel, n_x=len(x_args), gate_idx=gate_idx),
        grid=(N_TOK // tm,),
        in_specs=x_specs + _group_specs(half, tm) + _group_specs(half, tm) + [
            pl.BlockSpec((half, D_MODEL), lambda i: (0, 0)),
            pl.BlockSpec((half, D_MODEL), lambda i: (1, 0)),
            pl.BlockSpec((1, MOD_ROWS, D_MODEL), lambda i: (i * tm // SEG, 0, 0)),
            const((1, D_MODEL)), const((D_MODEL, LANES)), const((1, LANES)), const((TM, TM)), const((LANES, LANES)),
        ],
        out_specs=[
            pl.BlockSpec((tm, D_MODEL), lambda i: (i, 0)),
            pl.BlockSpec((tm, HALF_D), lambda i: (i, 0)),
            pl.BlockSpec((sub, SUBLANES, TM), lambda i: (i, 0, 0)),
            pl.BlockSpec((tm, LANES), lambda i: (i, 0)),
            pl.BlockSpec((sub, SUBLANES, LANES), lambda i: (i, 0, 0)),
            pl.BlockSpec((SUBLANES, LANES), lambda i: (0, 0)),
        ],
        out_shape=[
            jax.ShapeDtypeStruct((N_TOK, D_MODEL), F32),
            jax.ShapeDtypeStruct((N_TOK, HALF_D), I32),
            jax.ShapeDtypeStruct((nt, SUBLANES, TM), F32),
            jax.ShapeDtypeStruct((N_TOK, LANES), F32),
            jax.ShapeDtypeStruct((nt, SUBLANES, LANES), I32),
            jax.ShapeDtypeStruct((SUBLANES, LANES), I32),
        ],
        scratch_shapes=[pltpu.VMEM((1, LANES), F32)],
        compiler_params=_cparams(("arbitrary",)),
        name="outproj_residual_router",
    )(*x_args, a[0], a[1], b[0], b[1], w_bf16, w_bf16, mods, norm2.reshape(1, D_MODEL), w_route, b_route, tril, triu)


def _for_chunks(n, fn):
    for size in CHUNKS:
        @pl.when((n & size) != 0)
        def _():
            fn(n & ~(2 * size - 1), size)


N_COPIES = R_LOC // ROW_ALIGN
DUMP_ROWS = (2 * R_LOC + MOE_BLK - 1) // MOE_BLK * MOE_BLK


def _start_tile_copies(grp_ref, seg_ref, make, dummy, n=None):
    if n is None:
        n = grp_ref[0, GRP_COUNT, 0]
    for c in range(N_COPIES):
        real = c < n
        slot = seg_ref[0, grp_ref[0, GRP_EXPERT, c]] + grp_ref[0, GRP_REL, c]
        d_loc, d_slot = dummy(c)
        loc = jnp.where(real, c * ROW_ALIGN, d_loc)
        slot = jnp.where(real, slot, d_slot)
        make(pl.ds(pl.multiple_of(loc, ROW_ALIGN), ROW_ALIGN), pl.ds(pl.multiple_of(slot, ROW_ALIGN), ROW_ALIGN)).start()


def _wait_tile_copies(make):
    make(pl.ds(0, R_LOC), pl.ds(0, R_LOC)).wait()


def _all_experts(body):
    lax.fori_loop(0, N_EXPERTS, lambda e, c: (body(e), c)[1], 0)


def _dispatch_kernel(grp_ref, seg_ref, loct_ref, u_ref, xb_ref, buf, zbuf, sem):
    i = pl.program_id(0)
    last = pl.num_programs(0) - 1
    cur = i % 2

    def maker(b):
        return lambda loc, slot: pltpu.make_async_copy(buf.at[b, loc], xb_ref.at[slot], sem.at[b])

    @pl.when(i >= 2)
    def _():
        _wait_tile_copies(maker(cur))

    lo, hi = _unpack_bf16_pair(u_ref[...])
    row = lax.broadcasted_iota(I32, (R_LOC, TM), 0).astype(F32)
    sel = jnp.logical_or(row == loct_ref[0, 2:3, :], row == loct_ref[0, 3:4, :])
    pt = jnp.where(sel, 1.0, 0.0).astype(BF16)
    buf[cur] = _pack_exact_bf16_pair(jnp.dot(pt, lo, preferred_element_type=F32),
                                     jnp.dot(pt, hi, preferred_element_type=F32))
    _start_tile_copies(grp_ref, seg_ref, maker(cur),
                       lambda c: (c * ROW_ALIGN, MOE_ROWS + cur * R_LOC + c * ROW_ALIGN))

    @pl.when(i == last)
    def _():
        @pl.when(i >= 1)
        def _():
            _wait_tile_copies(maker(1 - cur))

        _wait_tile_copies(maker(cur))

        zbuf[...] = jnp.zeros_like(zbuf)

        def tail(e, act):
            total = seg_ref[1, e]
            dst = seg_ref[0, e] + total

            def one(off, size):
                act(pltpu.make_async_copy(zbuf.at[pl.ds(0, size)],
                                          xb_ref.at[pl.ds(pl.multiple_of(dst + off, ROW_ALIGN), size)], sem.at[2]))

            _for_chunks((-total) & (MOE_BLK - 1), one)

        _all_experts(lambda e: tail(e, lambda cp: cp.start()))
        _all_experts(lambda e: tail(e, lambda cp: cp.wait()))

        e_last = N_EXPERTS - 1
        used = (seg_ref[0, e_last] + seg_ref[1, e_last] + MOE_BLK - 1) // MOE_BLK

        def spare(j):
            return pltpu.make_async_copy(zbuf, xb_ref.at[pl.ds(pl.multiple_of(j * MOE_BLK, MOE_BLK), MOE_BLK)],
                                         sem.at[2])

        lax.fori_loop(used, (MOE_ROWS + DUMP_ROWS) // MOE_BLK, lambda j, c: (spare(j).start(), c)[1], 0)
        lax.fori_loop(used, (MOE_ROWS + DUMP_ROWS) // MOE_BLK, lambda j, c: (spare(j).wait(), c)[1], 0)


def _dispatch(grp, seg, loct, u):
    return pl.pallas_call(
        _dispatch_kernel,
        grid=(N_TOK // TM,),
        in_specs=[
            pl.BlockSpec((1, SUBLANES, LANES), lambda i: (i, 0, 0), memory_space=pltpu.SMEM),
            pl.BlockSpec(memory_space=pltpu.SMEM),
            pl.BlockSpec((1, SUBLANES, TM), lambda i: (i, 0, 0)),
            pl.BlockSpec((TM, HALF_D), lambda i: (i, 0)),
        ],
        out_specs=pl.BlockSpec(memory_space=pl.ANY),
        out_shape=jax.ShapeDtypeStruct((MOE_ROWS + DUMP_ROWS, HALF_D), I32),
        scratch_shapes=[pltpu.VMEM((2, R_LOC, HALF_D), I32), pltpu.VMEM((MOE_BLK, HALF_D), I32),
                        pltpu.SemaphoreType.DMA((3,))],
        compiler_params=_cparams(("arbitrary",)),
        name="moe_dispatch",
    )(grp, seg, loct, u)


def _expert_kernel(seg_ref, w1_ref, w3_ref, w2_ref, xb_ref, yb_ref, w1b, w3b, w2b, xbuf, ybuf, xsem, ysem):
    e = pl.program_id(0)
    last_e = pl.num_programs(0) - 1
    blk0 = seg_ref[0, e] // MOE_BLK
    nblk = (seg_ref[1, e] + MOE_BLK - 1) // MOE_BLK
    used = (seg_ref[0, N_EXPERTS - 1] + seg_ref[1, N_EXPERTS - 1] + MOE_BLK - 1) // MOE_BLK

    def rows(g):
        return pl.ds(pl.multiple_of(g * MOE_BLK, MOE_BLK), MOE_BLK)

    def fetch(g, b):
        return pltpu.make_async_copy(xb_ref.at[rows(g)], xbuf.at[b], xsem.at[b])

    def flush(g, b):
        return pltpu.make_async_copy(ybuf.at[b], yb_ref.at[rows(g)], ysem.at[b])

    row_queue = 1

    @pl.when(e == 0)
    def _():
        fetch(0, 0).start(priority=row_queue)

    @pl.when(nblk > 0)
    def _():
        w1b[...] = w1_ref[0, 0].astype(BF16)
        w3b[...] = w3_ref[0, 0].astype(BF16)
        w2b[...] = w2_ref[0, 0].astype(BF16)

        def block(i, carry):
            g = blk0 + i
            b = g % 2
            fetch(g, b).wait()

            @pl.when(g + 1 < used)
            def _():
                fetch(g + 1, 1 - b).start(priority=row_queue)

            @pl.when(g >= 2)
            def _():
                flush(g - 2, b).wait()

            def ffn(x_packed):
                lo, hi = _unpack_bf16_pair(x_packed)

                def up(wb):
                    return (jnp.dot(lo, wb[:HALF_D, :], preferred_element_type=F32)
                            + jnp.dot(hi, wb[HALF_D:, :], preferred_element_type=F32))

                hdn = (_silu(up(w1b)) * up(w3b)).astype(BF16)
                y = jnp.dot(hdn, w2b[...], preferred_element_type=F32)
                return _pack_bf16_pair(y[:, :HALF_D], y[:, HALF_D:])

            slots_here = seg_ref[1, e] - i * MOE_BLK
            quarter = MOE_BLK // MOE_PARTS
            for parts in range(1, MOE_PARTS + 1):
                rows_used = parts * quarter
                fits = slots_here > rows_used - quarter
                if parts < MOE_PARTS:
                    fits = jnp.logical_and(fits, slots_here <= rows_used)

                @pl.when(fits)
                def _():
                    ybuf[b, :rows_used, :] = ffn(xbuf[b, :rows_used, :])
                    if rows_used < MOE_BLK:
                        ybuf[b, rows_used:, :] = jnp.zeros((MOE_BLK - rows_used, HALF_D), I32)

            flush(g, b).start(priority=row_queue)
            return carry

        lax.fori_loop(0, nblk, block, 0)

    @pl.when(e == last_e)
    def _():
        @pl.when(used >= 2)
        def _():
            flush(used - 2, used % 2).wait()

        flush(used - 1, (used - 1) % 2).wait()
        xbuf[0] = jnp.zeros((MOE_BLK, HALF_D), I32)

        def spare(g):
            return pltpu.make_async_copy(xbuf.at[0], yb_ref.at[rows(g)], xsem.at[0])

        lax.fori_loop(used, MOE_NB, lambda g, c: (spare(g).start(), c)[1], 0)
        lax.fori_loop(used, MOE_NB, lambda g, c: (spare(g).wait(), c)[1], 0)


def _experts(seg, xb, w1, w3, w2, layer):
    hbm = pl.BlockSpec(memory_space=pl.ANY)
    w_in = pl.BlockSpec((1, 1, D_MODEL, D_EXPERT), lambda e, s: (layer, e, 0, 0))
    w_out = pl.BlockSpec((1, 1, D_EXPERT, D_MODEL), lambda e, s: (layer, e, 0, 0))
    return pl.pallas_call(
        _expert_kernel,
        grid_spec=pltpu.PrefetchScalarGridSpec(
            num_scalar_prefetch=1,
            grid=(N_EXPERTS,),
            in_specs=[w_in, w_in, w_out, hbm],
            out_specs=hbm,
            scratch_shapes=[pltpu.VMEM((D_MODEL, D_EXPERT), BF16), pltpu.VMEM((D_MODEL, D_EXPERT), BF16),
                            pltpu.VMEM((D_EXPERT, D_MODEL), BF16),
                            pltpu.VMEM((2, MOE_BLK, HALF_D), I32), pltpu.VMEM((2, MOE_BLK, HALF_D), I32),
                            pltpu.SemaphoreType.DMA((2,)), pltpu.SemaphoreType.DMA((2,))],
        ),
        out_shape=jax.ShapeDtypeStruct((MOE_ROWS, HALF_D), I32),
        compiler_params=_cparams(("arbitrary",)),
        name="moe_experts",
    )(seg, w1, w3, w2, xb)


def _combine_kernel(*refs, final):
    grp_ref, nxt_ref, seg_ref, x_ref, info_ref, mod_ref, yb_ref = refs[:7]
    if final:
        fin_ref, olat_ref, octx_ref, ybuf, sem = refs[7:]
    else:
        g1n_ref, modn_ref, wn_ref, o_ref, proj_ref, ybuf, sem = refs[7:]
    i = pl.program_id(0)
    cur = i % 2

    def maker(b):
        return lambda loc, slot: pltpu.make_async_copy(yb_ref.at[slot], ybuf.at[b, loc], sem.at[b])

    def dummy(c):
        return R_LOC + c * ROW_ALIGN, c * ROW_ALIGN

    @pl.when(i == 0)
    def _():
        ybuf[...] = jnp.zeros_like(ybuf)
        _start_tile_copies(grp_ref, seg_ref, maker(0), dummy)

    has_next = i + 1 < pl.num_programs(0)
    _start_tile_copies(nxt_ref, seg_ref, maker(1 - cur), dummy, jnp.where(has_next, nxt_ref[0, GRP_COUNT, 0], 0))
    _wait_tile_copies(maker(cur))

    @pl.when(jnp.logical_not(has_next))
    def _():
        _wait_tile_copies(maker(1 - cur))

    info = info_ref[...]
    col = lax.broadcasted_iota(I32, (TM, R_LOC), 1).astype(F32)
    p = jnp.where(col == info[:, 2:3], info[:, 0:1], jnp.where(col == info[:, 3:4], info[:, 1:2], 0.0))
    p = p.astype(BF16)
    lo, hi = _unpack_bf16_pair(ybuf[cur, :R_LOC, :])
    y = jnp.concatenate([jnp.dot(p, lo, preferred_element_type=F32),
                         jnp.dot(p, hi, preferred_element_type=F32)], axis=1)
    out = x_ref[...] + mod_ref[0][5:6] * y
    if not final:
        o_ref[...] = out
        u_next = _norm_mod(out, g1n_ref[...], modn_ref[0], 0, 1)
        proj_ref[...] = jnp.dot(u_next.astype(BF16), wn_ref[...], preferred_element_type=F32)
        return
    out = out * lax.rsqrt(jnp.mean(out * out, axis=-1, keepdims=True) + EPS) * fin_ref[...]

    @pl.when(i < N_LAT_TILES)
    def _():
        olat_ref[...] = out

    @pl.when(i >= N_LAT_TILES)
    def _():
        octx_ref[...] = out


def _combine(x, grp, seg, info, mods, yb, final_g=None, next_proj=None):
    final = final_g is not None
    nt = N_TOK // TM
    in_specs = [
        pl.BlockSpec((1, SUBLANES, LANES), lambda i: (i, 0, 0), memory_space=pltpu.SMEM),
        pl.BlockSpec((1, SUBLANES, LANES), lambda i: (jnp.minimum(i + 1, nt - 1), 0, 0), memory_space=pltpu.SMEM),
        pl.BlockSpec(memory_space=pltpu.SMEM),
        pl.BlockSpec((TM, D_MODEL), lambda i: (i, 0)),
        pl.BlockSpec((TM, LANES), lambda i: (i, 0)),
        pl.BlockSpec((1, MOD_ROWS, D_MODEL), lambda i: (i * TM // SEG, 0, 0)),
        pl.BlockSpec(memory_space=pl.ANY),
    ]
    args = [grp, grp, seg, x, info, mods, yb]
    if final:
        in_specs.append(pl.BlockSpec((1, D_MODEL), lambda i: (0, 0)))
        args.append(final_g.reshape(1, D_MODEL))
        out_specs = _group_specs(D_MODEL)
        out_shape = [jax.ShapeDtypeStruct((N_SAMPLE, D_MODEL), F32), jax.ShapeDtypeStruct((N_PROMPT, D_MODEL), F32)]
    else:
        g1n, modn, wn = next_proj
        n = wn.shape[1]
        in_specs += [pl.BlockSpec((1, D_MODEL), lambda i: (0, 0)),
                     pl.BlockSpec((1, MOD_ROWS, D_MODEL), lambda i: (i * TM // SEG, 0, 0)),
                     pl.BlockSpec((D_MODEL, n), lambda i: (0, 0), pipeline_mode=pl.Buffered(1))]
        args += [g1n.reshape(1, D_MODEL), modn, wn]
        out_specs = [pl.BlockSpec((TM, D_MODEL), lambda i: (i, 0)), pl.BlockSpec((TM, n), lambda i: (i, 0))]
        out_shape = [jax.ShapeDtypeStruct((N_TOK, D_MODEL), F32), jax.ShapeDtypeStruct((N_TOK, n), F32)]
    return pl.pallas_call(
        functools.partial(_combine_kernel, final=final),
        grid=(nt,),
        in_specs=in_specs,
        out_specs=out_specs,
        out_shape=out_shape,
        scratch_shapes=[pltpu.VMEM((2, 2 * R_LOC, HALF_D), I32), pltpu.SemaphoreType.DMA((2,))],
        compiler_params=_cparams(("arbitrary",)),
        name="moe_combine_final" if final else "moe_combine",
    )(*args)


def _route_weights(w_grp, b_grp, w_exp, b_exp):
    spare = LANES - N_GROUPS - N_EXPERTS
    w_route = jnp.concatenate([w_grp, w_exp, jnp.zeros((D_MODEL, spare), F32)], axis=1)
    b_route = jnp.concatenate([b_grp, b_exp, jnp.zeros((spare,), F32)])[None, :]
    return w_route, b_route


def _moe_layer(x, route, layer, mods, w1, w3, w2, final_g=None, next_proj=None):
    u, loct, info, grp, tot = route
    total = tot[0, :N_EXPERTS]
    padded = (total + MOE_BLK - 1) // MOE_BLK * MOE_BLK
    pad_end = jnp.cumsum(padded)
    pad_start = pad_end - padded
    seg = jnp.zeros((2, LANES), I32).at[0, :N_EXPERTS].set(pad_start).at[1, :N_EXPERTS].set(total)
    xb = _dispatch(grp, seg, loct, u)
    yb = _experts(seg, xb, w1, w3, w2, layer)
    return _combine(x, grp, seg, info, mods, yb, final_g, next_proj)


def _pad_rows(w, rows=SUBLANES):
    return jnp.zeros((rows, w.shape[1]), w.dtype).at[:w.shape[0]].set(w)


def _pad2(w, rows, cols):
    return jnp.zeros((rows, cols), w.dtype).at[:w.shape[0], :w.shape[1]].set(w)


def kernel(x_prompt, x_sample, c, c_ctx, state_ret, cache_k, cache_v, ada_w, ada_b, norm1, norm2, final_norm, ev_w_in, ev_w_out, ret_decay_fwd, ret_decay_bwd, sc_conv_w, od_w_in, od_w_out, hy_conv_w, hy_w1, hy_b1, hy_fr1, hy_w2, hy_b2, hy_fr2, hy_w3, hy_bias, da_lambda, da_subln, moe_w_grp, moe_b_grp, moe_w_exp, moe_b_exp, moe_w1, moe_w3, moe_w2):
    x = (x_sample.reshape(N_SAMPLE, D_MODEL), x_prompt.reshape(N_PROMPT, D_MODEL))

    cvecs = jnp.zeros((16, D_MODEL), F32).at[:DEC_BATCH].set(c).at[DEC_BATCH].set(c_ctx)
    m = _mods(cvecs, ada_w, ada_b).reshape(DEPTH, 16, N_MOD, D_MODEL)
    seg_mods = jnp.concatenate(
        [m[:, :DEC_BATCH], jnp.broadcast_to(m[:, DEC_BATCH:DEC_BATCH + 1], (DEPTH, N_SEG - DEC_BATCH, N_MOD, D_MODEL))],
        axis=1)
    seg_mods = jnp.pad(seg_mods, ((0, 0), (0, 0), (0, MOD_ROWS - N_MOD), (0, 0)))

    tril = jnp.asarray(np.tril(np.ones((TM, TM), np.float32)), BF16)
    triu = jnp.asarray(np.triu(np.ones((LANES, LANES), np.float32), 1), BF16)

    def w_in(layer):
        w = ev_w_in if layer % 2 == 0 else od_w_in
        return w[layer // 2].astype(BF16)

    proj = None
    for l in range(DEPTH):
        mods = seg_mods[l]
        i = l // 2
        if l % 2 == 0:
            if proj is None:
                proj = _norm_mod_matmul(x, norm1[l], mods, w_in(l), 0, 1)
            scw = _pad_rows(sc_conv_w[i])
            ret_l, sc_l = _even_mixer(proj, scw, ret_decay_fwd[i], ret_decay_bwd[i], state=state_ret[:, i:i + 1])
            ret_c, sc_c, new_state = _even_mixer(proj, scw, ret_decay_fwd[i], ret_decay_bwd[i])
            mixed, w_out = ((ret_l, ret_c), (sc_l, sc_c)), ev_w_out[i]
        else:
            lambda_init = 0.8 - 0.6 * math.exp(-0.3 * l)
            if proj is None:
                proj = _norm_mod_matmul(x, norm1[l], mods, w_in(l), 0, 1)
            conv_w = _pad_rows(hy_conv_w[i])
            bias = _pad_rows(hy_bias[i])
            w1 = _pad2(hy_w1[i], LANES, LANES)
            b1 = _pad2(hy_b1[i][None], 1, LANES)
            f1 = _pad2(hy_fr1[i][None], 1, LANES)
            w2 = _pad2(hy_w2[i], LANES, LANES)
            b2 = _pad2(hy_b2[i][None], 1, LANES)
            f2 = _pad2(hy_fr2[i][None], 1, LANES)
            w3 = _pad2(hy_w3[i], LANES, hy_w3.shape[2])
            hy = []
            for latent in (True, False):
                seq = DEC_SEQ if latent else SEQ
                fwd_np, inv_np = _dft_tables(seq)
                zfeat, decay = _hyena_consts(seq)
                fwd_bf, fwd_lo = _split_table(jnp.asarray(fwd_np), f"dft_fwd_split_{seq}")
                inv_bf, _ = _split_table(jnp.asarray(inv_np), f"dft_inv_split_{seq}")
                filters = _hy_filters(seq, jnp.asarray(zfeat), jnp.asarray(decay), fwd_bf, fwd_lo,
                                      w1, b1, f1, w2, b2, f2, w3)
                hy.append(_hy_apply(proj, conv_w, fwd_bf, inv_bf, filters, bias, latent))
            cos, sin = _rope_tables(DEC_SEQ)
            att_l = _attention(proj, da_lambda[i], da_subln[i], lambda_init, True,
                               rope=(jnp.asarray(cos), jnp.asarray(sin)),
                               ctx_k=cache_k[:, i:i + 1], ctx_v=cache_v[:, i:i + 1])[0]
            att_c, new_k, new_v = _attention(proj, da_lambda[i], da_subln[i], lambda_init, False)
            mixed, w_out = (tuple(hy), (att_l, att_c)), od_w_out[i]
        w_route, b_route = _route_weights(moe_w_grp[l], moe_b_grp[l], moe_w_exp[l], moe_b_exp[l])
        x, *route = _proj_residual_router(x, mixed[0], mixed[1], w_out.astype(BF16), mods, 2,
                                          norm2[l], w_route, b_route, tril, triu)
        if l == DEPTH - 1:
            x = _moe_layer(x, route, l, mods, moe_w1, moe_w3, moe_w2, final_g=final_norm)
        else:
            x, proj = _moe_layer(x, route, l, mods, moe_w1, moe_w3, moe_w2,
                                 next_proj=(norm1[l + 1], seg_mods[l + 1], w_in(l + 1)))

    y_sample, y_prompt = x
    return (y_prompt.reshape(BATCH, SEQ, D_MODEL), y_sample.reshape(DEC_BATCH, DEC_SEQ, D_MODEL),
            new_state, new_k, new_v)
```

```python
import functools
import math

import numpy as np
import jax
import jax.numpy as jnp
from jax import lax
from jax.experimental import pallas as pl
from jax.experimental.pallas import tpu as pltpu

F32 = jnp.float32
BF16 = jnp.bfloat16
I32 = jnp.int32
HIGHEST = lax.Precision.HIGHEST

D_MODEL = 1024
BATCH = 32
SEQ = 256
DEPTH = 2
DEC_BATCH = 8
DEC_SEQ = 1024
PAST_LEN = 256
GRID_W = 64
N_MOD = 6
EPS = 1e-6
GN_EPS = 1e-5

RET_DK = 128
RET_DV = 128
RET_HEADS = 4
RET_W = 512
RET_QK = 512
SC_W = 512
EVEN_IN = 2 * RET_QK + 2 * RET_W + 3 * SC_W

HY_W = 512
HY_ORDER = 2
HY_EMB = 33
HY_FFN = 64
HY_TARGET = 1e-2
HY_FAST_PCT = 0.3
HY_SLOW_PCT = 1.5
DA_DH = 64
DA_HEADS = 4
DA_W = 512
ODD_IN = 3 * HY_W + 3 * DA_W
ROPE_THETA = 10000.0

N_GROUPS = 4
EXP_PER_GROUP = 8
N_EXPERTS = 32
TOP_K = 2
D_EXPERT = 512

N_SAMPLE = DEC_BATCH * DEC_SEQ
N_PROMPT = BATCH * SEQ
N_TOK = N_SAMPLE + N_PROMPT
SEG = 1024
N_SEG = N_TOK // SEG
MOD_ROWS = 8

LANES = 128
SUBLANES = 8
VMEM_LIMIT = 56 * 1024 * 1024

TM = 256
MOE_BLK = 1024
MOE_PARTS = 4
ROW_ALIGN = SUBLANES
R_LOC = (TOP_K * TM + N_EXPERTS * (ROW_ALIGN - 1) + LANES - 1) // LANES * LANES
CHUNKS = (512, 256, 128, 64, 32, 16, 8)
MOE_ROWS = ((N_TOK * TOP_K + N_EXPERTS * (N_TOK // TM) * (ROW_ALIGN - 1) + N_EXPERTS * (MOE_BLK - 1))
            // MOE_BLK + 1) * MOE_BLK
MOE_NB = MOE_ROWS // MOE_BLK
HY_SLAB = 512
ATT_TQ = 256


def _cparams(sem, vmem=VMEM_LIMIT):
    return pltpu.CompilerParams(dimension_semantics=sem, vmem_limit_bytes=vmem)


def _silu(x):
    return x * (1.0 / (1.0 + jnp.exp(-x)))


def _split_bf16(a):
    hi = a.astype(BF16)
    return hi, (a - hi.astype(F32)).astype(BF16)


def _dot_3pass(a, b):
    ah, al = _split_bf16(a)
    bh, bl = _split_bf16(b)
    dot = functools.partial(jnp.dot, preferred_element_type=F32)
    return dot(ah, bh) + (dot(ah, bl) + dot(al, bh))


MODS_TN = 1536


def _mods_kernel(c_ref, w_ref, b_ref, o_ref):
    s = _silu(c_ref[...])
    o_ref[0] = jnp.dot(s, w_ref[0], preferred_element_type=F32, precision=HIGHEST) + b_ref[0]


def _mods(cvecs, ada_w, ada_b):
    n = N_MOD * D_MODEL
    return pl.pallas_call(
        _mods_kernel,
        grid=(DEPTH, n // MODS_TN),
        in_specs=[
            pl.BlockSpec((16, D_MODEL), lambda l, j: (0, 0)),
            pl.BlockSpec((1, D_MODEL, MODS_TN), lambda l, j: (l, 0, j)),
            pl.BlockSpec((1, 1, MODS_TN), lambda l, j: (l, 0, j)),
        ],
        out_specs=pl.BlockSpec((1, 16, MODS_TN), lambda l, j: (l, 0, j)),
        out_shape=jax.ShapeDtypeStruct((DEPTH, 16, n), F32),
        compiler_params=_cparams(("parallel", "parallel")),
        name="adaln_mods",
    )(cvecs, ada_w, ada_b.reshape(DEPTH, 1, n))


def _norm_mod(x, g, mod, shift_idx, scale_idx):
    ms = jnp.mean(x * x, axis=-1, keepdims=True)
    y = x * lax.rsqrt(ms + EPS) * g
    return y * (1.0 + mod[scale_idx:scale_idx + 1]) + mod[shift_idx:shift_idx + 1]


N_LAT_TILES = N_SAMPLE // TM
TM_WIDE = 512
TM_ROUTE = 1024


def _group_specs(width, tm=TM):
    n_lat = N_SAMPLE // tm
    lat = pl.BlockSpec((tm, width), lambda i: (jnp.minimum(i, n_lat - 1), 0))
    ctx = pl.BlockSpec((tm, width), lambda i: (jnp.maximum(i - n_lat, 0), 0))
    return [lat, ctx]


def _group_pick(lat_ref, ctx_ref):
    n_lat = N_SAMPLE // lat_ref.shape[0]
    return jnp.where(pl.program_id(0) < n_lat, lat_ref[...], ctx_ref[...])


def _rows_specs(x, tm=TM):
    if isinstance(x, tuple):
        return _group_specs(D_MODEL, tm), list(x)
    return [pl.BlockSpec((tm, D_MODEL), lambda i: (i, 0))], [x]


def _rows_value(refs):
    return _group_pick(*refs) if len(refs) == 2 else refs[0][...]


def _nmm_kernel(*refs, n_x, shift_idx, scale_idx):
    g_ref, mod_ref, w_ref, o_ref, wb_ref = refs[n_x:]

    @pl.when(pl.program_id(0) == 0)
    def _():
        wb_ref[...] = w_ref[...].astype(BF16)

    u = _norm_mod(_rows_value(refs[:n_x]), g_ref[...], mod_ref[0], shift_idx, scale_idx)
    o_ref[...] = jnp.dot(u.astype(BF16), wb_ref[...], preferred_element_type=F32)


def _norm_mod_matmul(x, g, mods, w, shift_idx, scale_idx):
    n = w.shape[1]
    tm = TM_WIDE
    x_specs, x_args = _rows_specs(x, tm)
    return pl.pallas_call(
        functools.partial(_nmm_kernel, n_x=len(x_args), shift_idx=shift_idx, scale_idx=scale_idx),
        grid=(N_TOK // tm,),
        in_specs=x_specs + [
            pl.BlockSpec((1, D_MODEL), lambda i: (0, 0)),
            pl.BlockSpec((1, MOD_ROWS, D_MODEL), lambda i: (i * tm // SEG, 0, 0)),
            pl.BlockSpec((D_MODEL, n), lambda i: (0, 0), pipeline_mode=pl.Buffered(1)),
        ],
        out_specs=pl.BlockSpec((tm, n), lambda i: (i, 0)),
        out_shape=jax.ShapeDtypeStruct((N_TOK, n), F32),
        scratch_shapes=[pltpu.VMEM((D_MODEL, n), BF16)],
        compiler_params=_cparams(("arbitrary",)),
        name="norm_mod_inproj",
    )(*x_args, g.reshape(1, D_MODEL), mods, w)


def _conv3_rows(z, w):
    n = z.shape[0]
    row = lax.broadcasted_iota(I32, z.shape, 0)
    zm = jnp.where(row == 0, 0.0, pltpu.roll(z, 1, 0))
    zp = jnp.where(row == n - 1, 0.0, pltpu.roll(z, n - 1, 0))
    return zm * w[0:1] + z * w[1:2] + zp * w[2:3]


CTX_NSEQ = 4


def _log_gamma(dec):
    return jnp.log1p(-jnp.exp(dec))


def _even_kernel(*refs, seq, latent, tq, heads, nseq):
    (q_ref, k_ref, v_ref, g_ref, gb_ref, gc_ref, hx_ref, scw_ref, df_ref, db_ref) = refs[:10]
    if latent:
        s0_ref, ret_ref, sc_ref, mask_ref = refs[10:]
    else:
        ret_ref, sc_ref, st_ref, mask_ref = refs[10:]

    decays = []
    for hh in range(heads):
        lgf = _log_gamma(df_ref[hh])
        lgb = _log_gamma(db_ref[hh])
        decays.append((lgf, lgb))

        @pl.when(pl.program_id(1) == 0)
        def _():
            for r in range(seq // tq):
                t = lax.broadcasted_iota(I32, (tq, seq), 0) + r * tq
                s = lax.broadcasted_iota(I32, (tq, seq), 1)
                d = (t - s).astype(F32)
                mf = jnp.where(d >= 0, jnp.exp(jnp.maximum(d, 0.0) * lgf), 0.0)
                mb = jnp.where(d <= 0, jnp.exp(jnp.maximum(-d, 0.0) * lgb), 0.0)
                mask_ref[hh, r * tq:(r + 1) * tq, :] = mf + mb

    for sq in range(nseq):
        whole = slice(sq * seq, (sq + 1) * seq)
        for hh in range(heads):
            lanes = slice(hh * RET_DK, (hh + 1) * RET_DK)
            lgf, lgb = decays[hh]
            k = k_ref[whole, lanes] * (RET_DK ** -0.5)
            kb = k.astype(BF16)
            vb = v_ref[whole, lanes].astype(BF16)
            if latent:
                s0f = s0_ref[sq, 0, 0, hh].astype(BF16)
                s0b = s0_ref[sq, 0, 1, hh].astype(BF16)
            for r in range(seq // tq):
                rows = slice(sq * seq + r * tq, sq * seq + (r + 1) * tq)
                qb = q_ref[rows, lanes].astype(BF16)
                s = lax.dot_general(qb, kb, (((1,), (1,)), ((), ())), preferred_element_type=F32)
                p = (s * mask_ref[hh, r * tq:(r + 1) * tq, :]).astype(BF16)
                o = jnp.dot(p, vb, preferred_element_type=F32)
                if latent:
                    tpos = (lax.broadcasted_iota(I32, (tq, 1), 0) + r * tq).astype(F32)
                    o = o + jnp.dot(qb, s0f, preferred_element_type=F32) * jnp.exp((tpos + 1.0) * lgf)
                    o = o + jnp.dot(qb, s0b, preferred_element_type=F32) * jnp.exp((seq - tpos) * lgb)
                mu = jnp.mean(o, axis=-1, keepdims=True)
                var = jnp.mean(jnp.square(o - mu), axis=-1, keepdims=True)
                on = (o - mu) * lax.rsqrt(var + GN_EPS)
                ret_ref[rows, lanes] = _silu(g_ref[rows, lanes]) * on

            if not latent:
                spos = lax.broadcasted_iota(I32, (seq, 1), 0).astype(F32)
                kf = (k * jnp.exp((seq - 1.0 - spos) * lgf)).T.astype(BF16)
                kr = (k * jnp.exp(spos * lgb)).T.astype(BF16)
                st_ref[sq, 0, 0, hh] = jnp.dot(kf, vb, preferred_element_type=F32)
                st_ref[sq, 0, 1, hh] = jnp.dot(kr, vb, preferred_element_type=F32)

        sc_ref[whole, :] = gb_ref[whole, :] * _conv3_rows(gc_ref[whole, :] * hx_ref[whole, :], scw_ref[...])


def _even_mixer(proj, sc_w_pad, dec_f, dec_b, state=None):
    latent = state is not None
    seq = DEC_SEQ if latent else SEQ
    nb = DEC_BATCH if latent else BATCH
    tq = min(seq, 256)
    heads = 2 if latent else RET_HEADS
    nseq = 1 if latent else CTX_NSEQ
    rows = nseq * seq
    row0 = 0 if latent else N_SAMPLE // rows
    steps = RET_HEADS // heads
    hw = heads * RET_DK

    def col(c0):
        return pl.BlockSpec((rows, hw), lambda h, b: (row0 + b, c0 * steps + h))

    in_specs = [col(0), col(1), col(2), col(3), col(4), col(5), col(6),
                pl.BlockSpec((SUBLANES, hw), lambda h, b: (0, h)),
                pl.BlockSpec((heads, 1, 1), lambda h, b: (h, 0, 0)),
                pl.BlockSpec((heads, 1, 1), lambda h, b: (h, 0, 0))]
    args = [proj] * 7 + [sc_w_pad, dec_f.reshape(RET_HEADS, 1, 1), dec_b.reshape(RET_HEADS, 1, 1)]
    out_spec = pl.BlockSpec((rows, hw), lambda h, b: (b, h))
    out_shape = [jax.ShapeDtypeStruct((nb * seq, RET_W), F32), jax.ShapeDtypeStruct((nb * seq, SC_W), F32)]
    out_specs = [out_spec, out_spec]
    st_spec = pl.BlockSpec((nseq, 1, 2, heads, RET_DK, RET_DV), lambda h, b: (b, 0, 0, h, 0, 0))
    if latent:
        in_specs.append(st_spec)
        args.append(state)
    else:
        out_shape.append(jax.ShapeDtypeStruct((BATCH, 1, 2, RET_HEADS, RET_DK, RET_DV), F32))
        out_specs.append(st_spec)
    return pl.pallas_call(
        functools.partial(_even_kernel, seq=seq, latent=latent, tq=tq, heads=heads, nseq=nseq),
        grid=(steps, nb // nseq),
        in_specs=in_specs,
        out_specs=out_specs,
        out_shape=out_shape,
        scratch_shapes=[pltpu.VMEM((heads, seq, seq), F32)],
        compiler_params=_cparams(("arbitrary", "arbitrary")),
        name="even_mixer_latent" if latent else "even_mixer_context",
    )(*args)


def _dft_tables(length):
    f = np.arange(length, dtype=np.float64)[:, None]
    t = np.arange(length, dtype=np.float64)[None, :]
    ang = np.pi * ((f * t) % (2 * length)) / length
    cos, sin = np.cos(ang), np.sin(ang)
    sin[0, :] = (-1.0) ** np.arange(length)
    fwd = np.concatenate([cos, sin], axis=0)
    wgt = np.full((2 * length,), 2.0)
    wgt[0] = 1.0
    wgt[length] = 1.0
    inv = (fwd * wgt[:, None] / (2.0 * length)).T
    return fwd.astype(np.float32), np.ascontiguousarray(inv).astype(np.float32)


def _hyena_consts(length):
    t = np.linspace(0.0, 1.0, length)[:, None]
    bands = (HY_EMB - 1) // 2
    w = 2.0 * np.pi * np.arange(length)[:, None] / length
    f = np.linspace(1e-4, bands - 1, bands)[None, :]
    z = np.concatenate([t, np.cos(f * w), -np.sin(f * w)], axis=-1)
    zpad = np.zeros((length, LANES))
    zpad[:, :HY_EMB] = z
    deltas = np.abs(np.linspace(math.log(HY_TARGET) / HY_SLOW_PCT, math.log(HY_TARGET) / HY_FAST_PCT, HY_W))
    decay = np.exp(-t * deltas[None, :])
    return zpad.astype(np.float32), decay.astype(np.float32)


def _split_kernel(x_ref, hi_ref, lo_ref):
    hi, lo = _split_bf16(x_ref[...])
    hi_ref[...] = hi
    lo_ref[...] = lo


def _split_table(x, name):
    rows = min(x.shape[0], 1024)
    spec = pl.BlockSpec((rows, x.shape[1]), lambda i: (i, 0))
    out = jax.ShapeDtypeStruct(x.shape, BF16)
    return pl.pallas_call(
        _split_kernel,
        grid=(x.shape[0] // rows,),
        in_specs=[spec],
        out_specs=[spec, spec],
        out_shape=[out, out],
        compiler_params=_cparams(("parallel",)),
        name=name,
    )(x)


def _dot_3pass_split(ah, al, b):
    bh, bl = _split_bf16(b)
    dot = functools.partial(jnp.dot, preferred_element_type=F32)
    return dot(ah, bh) + (dot(ah, bl) + dot(al, bh))


def _hy_filter_kernel(z_ref, dec_ref, w1_ref, b1_ref, f1_ref, w2_ref, b2_ref, f2_ref, w3f_ref, w3b_ref,
                      fwd_ref, fwd_lo_ref, ka_ref, ka2_ref, kb_ref, *, seq):
    h = jnp.sin(f1_ref[...] * (jnp.dot(z_ref[...], w1_ref[...], preferred_element_type=F32,
                                       precision=HIGHEST) + b1_ref[...]))
    h = jnp.sin(f2_ref[...] * (jnp.dot(h, w2_ref[...], preferred_element_type=F32,
                                       precision=HIGHEST) + b2_ref[...]))
    dec = dec_ref[...]
    hf = jnp.dot(h, w3f_ref[...], preferred_element_type=F32, precision=HIGHEST) * dec
    hb = jnp.dot(h, w3b_ref[...], preferred_element_type=F32, precision=HIGHEST) * dec
    row = lax.broadcasted_iota(I32, hb.shape, 0)
    hb = jnp.where(row == 0, 0.0, hb)
    inv_norm = 1.0 / (jnp.sum(jnp.abs(hf), axis=0, keepdims=True)
                      + jnp.sum(jnp.abs(hb), axis=0, keepdims=True) + EPS)
    p = (hf + hb) * inv_norm
    q = (hf - hb) * inv_norm
    fp = _dot_3pass_split(fwd_ref[...], fwd_lo_ref[...], p)
    fq = _dot_3pass_split(fwd_ref[seq:, :], fwd_lo_ref[seq:, :], q)
    kr = fp[:seq]
    row = lax.broadcasted_iota(I32, kr.shape, 0)
    ka_ref[0] = kr
    ka2_ref[0] = jnp.where(row == 0, fp[seq:seq + 1], kr)
    kb_ref[0] = jnp.where(row == 0, 0.0, -fq)


def _hy_filters(seq, zfeat, decay, fwd_hi, fwd_lo, w1, b1, f1, w2, b2, f2, w3):
    ns = HY_W // HY_SLAB
    per_o = 2 * ns
    small = lambda shape: pl.BlockSpec(shape, lambda o, s: (0, 0))
    out_spec = pl.BlockSpec((1, seq, HY_SLAB), lambda o, s: (o, 0, s))
    out = jax.ShapeDtypeStruct((HY_ORDER, seq, HY_W), F32)
    return pl.pallas_call(
        functools.partial(_hy_filter_kernel, seq=seq),
        grid=(HY_ORDER, ns),
        in_specs=[
            small((seq, LANES)),
            pl.BlockSpec((seq, HY_SLAB), lambda o, s: (0, s)),
            small((LANES, LANES)), small((1, LANES)), small((1, LANES)),
            small((LANES, LANES)), small((1, LANES)), small((1, LANES)),
            pl.BlockSpec((LANES, HY_SLAB), lambda o, s: (0, o * per_o + s)),
            pl.BlockSpec((LANES, HY_SLAB), lambda o, s: (0, o * per_o + ns + s)),
            small((2 * seq, seq)), small((2 * seq, seq)),
        ],
        out_specs=[out_spec, out_spec, out_spec],
        out_shape=[out, out, out],
        compiler_params=_cparams(("parallel", "parallel")),
        name=f"hyena_filters_{seq}",
    )(zfeat, decay, w1, b1, f1, w2, b2, f2, w3, w3, fwd_hi, fwd_lo)


def _hy_apply_kernel(x1_ref, x2_ref, v_ref, c1_ref, c2_ref, cv_ref, fwd_ref, inv_ref,
                     ka_ref, ka2_ref, kb_ref, bias_ref, o_ref, *, seq, nseq):
    def long_conv(sig, o):
        spec = jnp.dot(fwd_ref[...], sig.astype(BF16), preferred_element_type=F32)
        xr, xs = spec[:seq], spec[seq:]
        kb = kb_ref[o]
        yr = (xr * ka_ref[o] + xs * kb).astype(BF16)
        ys = (xs * ka2_ref[o] - xr * kb).astype(BF16)
        y = jnp.dot(inv_ref[:, :seq], yr, preferred_element_type=F32)
        y = y + jnp.dot(inv_ref[:, seq:], ys, preferred_element_type=F32)
        return y + sig * bias_ref[o:o + 1]

    for i in range(nseq):
        rows = slice(i * seq, (i + 1) * seq)
        x1 = _conv3_rows(x1_ref[rows, :], c1_ref[...])
        x2 = _conv3_rows(x2_ref[rows, :], c2_ref[...])
        v = _conv3_rows(v_ref[rows, :], cv_ref[...])
        z = x1 * long_conv(v, 0)
        o_ref[rows, :] = x2 * long_conv(z, 1)


HY_NSEQ = {True: 1, False: 4}


def _hy_apply(proj, conv_w_pad, fwd_bf, inv_bf, filters, bias_pad, latent):
    seq = DEC_SEQ if latent else SEQ
    nb = DEC_BATCH if latent else BATCH
    nseq = HY_NSEQ[latent]
    rows = nseq * seq
    row0 = 0 if latent else N_SAMPLE // rows
    ns = HY_W // HY_SLAB
    once = pl.Buffered(1)
    const = lambda shape: pl.BlockSpec(shape, lambda s, b: (0,) * len(shape), pipeline_mode=once)
    col = lambda c0: pl.BlockSpec((rows, HY_SLAB), lambda s, b: (row0 + b, c0 + s))
    cw = lambda c0: pl.BlockSpec((SUBLANES, HY_SLAB), lambda s, b: (0, c0 + s))
    filt = pl.BlockSpec((HY_ORDER, seq, HY_SLAB), lambda s, b: (0, 0, s), pipeline_mode=once)
    in_specs = [col(0), col(ns), col(2 * ns), cw(0), cw(ns), cw(2 * ns),
                const((2 * seq, seq)), const((seq, 2 * seq)), filt, filt, filt,
                pl.BlockSpec((SUBLANES, HY_SLAB), lambda s, b: (0, s))]
    args = [proj] * 3 + [conv_w_pad] * 3 + [fwd_bf, inv_bf, *filters, bias_pad]
    return pl.pallas_call(
        functools.partial(_hy_apply_kernel, seq=seq, nseq=nseq),
        grid=(ns, nb // nseq),
        in_specs=in_specs,
        out_specs=pl.BlockSpec((rows, HY_SLAB), lambda s, b: (b, s)),
        out_shape=jax.ShapeDtypeStruct((nb * seq, HY_W), F32),
        compiler_params=_cparams(("arbitrary", "arbitrary")),
        name="hyena_latent" if latent else "hyena_context",
    )(*args)


def _rope_tables(length):
    rows = length // GRID_W
    row = np.repeat(np.arange(rows, dtype=np.float64), GRID_W)
    colp = np.tile(np.arange(GRID_W, dtype=np.float64), rows)
    half = DA_DH // 2
    freqs = ROPE_THETA ** (-np.arange(0, half, 2, dtype=np.float64) / half)
    ang = np.concatenate([row[:, None] * freqs[None, :]] * 2 + [colp[:, None] * freqs[None, :]] * 2, axis=1)
    ang = np.concatenate([ang, ang], axis=1)
    lane = np.arange(LANES)
    sign = np.where((lane % half) < half // 2, -1.0, 1.0)[None, :]
    return np.cos(ang).astype(np.float32), (np.sin(ang) * sign).astype(np.float32)


def _rope(x, cos, sin_signed):
    q = DA_DH // 4
    lane = lax.broadcasted_iota(I32, x.shape, 1)
    partner = jnp.where((lane % (2 * q)) < q, pltpu.roll(x, LANES - q, 1), pltpu.roll(x, q, 1))
    return x * cos + partner * sin_signed


def _attn_kernel(*refs, seq, latent, lambda_init, heads, nseq):
    q_ref, k_ref, v_ref, lam_ref, sub_ref = refs[:5]
    if latent:
        cos_ref, sin_ref, ck_ref, cv_ref, o_ref = refs[5:]
    else:
        o_ref, nk_ref, nv_ref = refs[5:]
    hw = 2 * DA_DH
    lv = lam_ref[...]
    lam = (jnp.exp(jnp.sum(lv[0:1] * lv[1:2], axis=1, keepdims=True))
           - jnp.exp(jnp.sum(lv[2:3] * lv[3:4], axis=1, keepdims=True)) + lambda_init)
    scale = DA_DH ** -0.5
    dn = (((1,), (1,)), ((), ()))
    tq = min(seq, ATT_TQ)
    for sq, hh in [(sq, hh) for sq in range(nseq) for hh in range(heads)]:
        whole = slice(sq * seq, (sq + 1) * seq)
        lanes = slice(hh * hw, (hh + 1) * hw)
        k = k_ref[whole, lanes]
        v = v_ref[whole, lanes]
        if latent:
            k = _rope(k, cos_ref[...], sin_ref[...])
            k_all = jnp.concatenate([k, jnp.concatenate([ck_ref[sq, 0, 0, hh], ck_ref[sq, 0, 1, hh]], axis=1)], axis=0)
            v_all = jnp.concatenate([v, cv_ref[sq, 0, hh]], axis=0)
        else:
            k_all, v_all = k, v
            nk_ref[sq, 0, 0, hh] = k[:, :DA_DH]
            nk_ref[sq, 0, 1, hh] = k[:, DA_DH:]
            nv_ref[sq, 0, hh] = v
        kb = k_all.astype(BF16)
        vb = v_all.astype(BF16)
        for r in range(seq // tq):
            rows = slice(sq * seq + r * tq, sq * seq + (r + 1) * tq)
            q = q_ref[rows, lanes]
            if latent:
                q = _rope(q, cos_ref[r * tq:(r + 1) * tq, :], sin_ref[r * tq:(r + 1) * tq, :])
            q = q * scale
            first = lax.broadcasted_iota(I32, q.shape, 1) < DA_DH
            q1 = jnp.where(first, q, 0.0).astype(BF16)
            q2 = jnp.where(first, 0.0, q).astype(BF16)
            s1 = lax.dot_general(q1, kb, dn, preferred_element_type=F32)
            s2 = lax.dot_general(q2, kb, dn, preferred_element_type=F32)
            e1 = jnp.exp(s1 - jnp.max(s1, axis=-1, keepdims=True))
            e2 = jnp.exp(s2 - jnp.max(s2, axis=-1, keepdims=True))
            o1 = jnp.dot(e1.astype(BF16), vb, preferred_element_type=F32)
            o2 = jnp.dot(e2.astype(BF16), vb, preferred_element_type=F32)
            o = (o1 * (1.0 / jnp.sum(e1, axis=-1, keepdims=True))
                 - o2 * (lam / jnp.sum(e2, axis=-1, keepdims=True)))
            o = o * lax.rsqrt(jnp.mean(o * o, axis=-1, keepdims=True) + EPS) * sub_ref[...]
            o_ref[rows, lanes] = o * (1.0 - lambda_init)


def _attention(proj, lam_vec, subln, lambda_init, latent, rope=None, ctx_k=None, ctx_v=None):
    seq = DEC_SEQ if latent else SEQ
    nb = DEC_BATCH if latent else BATCH
    hw = 2 * DA_DH
    heads = DA_HEADS
    nseq = 1 if latent else CTX_NSEQ
    rows = nseq * seq
    row0 = 0 if latent else N_SAMPLE // rows
    steps = DA_HEADS // heads
    col = lambda c: pl.BlockSpec((rows, heads * hw), lambda b, h: (row0 + b, c * steps + h))
    in_specs = [col(3), col(4), col(5),
                pl.BlockSpec((4, DA_DH), lambda b, h: (0, 0)),
                pl.BlockSpec((1, hw), lambda b, h: (0, 0))]
    args = [proj] * 3 + [lam_vec, subln.reshape(1, hw)]
    out_spec = pl.BlockSpec((rows, heads * hw), lambda b, h: (b, h))
    out_shape = [jax.ShapeDtypeStruct((nb * seq, DA_W), F32)]
    out_specs = [out_spec]
    k_spec = pl.BlockSpec((nseq, 1, 2, heads, SEQ, DA_DH), lambda b, h: (b, 0, 0, h, 0, 0))
    v_spec = pl.BlockSpec((nseq, 1, heads, SEQ, hw), lambda b, h: (b, 0, h, 0, 0))
    if latent:
        in_specs += [pl.BlockSpec((seq, hw), lambda b, h: (0, 0))] * 2 + [k_spec, v_spec]
        args += [rope[0], rope[1], ctx_k, ctx_v]
    else:
        out_shape += [jax.ShapeDtypeStruct((BATCH, 1, 2, DA_HEADS, SEQ, DA_DH), F32),
                      jax.ShapeDtypeStruct((BATCH, 1, DA_HEADS, SEQ, hw), F32)]
        out_specs += [k_spec, v_spec]
    return pl.pallas_call(
        functools.partial(_attn_kernel, seq=seq, latent=latent, lambda_init=lambda_init, heads=heads, nseq=nseq),
        grid=(nb // nseq, steps),
        in_specs=in_specs,
        out_specs=out_specs,
        out_shape=out_shape,
        compiler_params=_cparams(("arbitrary", "arbitrary")),
        name="diff_attn_latent" if latent else "diff_attn_context",
    )(*args)


HALF_D = D_MODEL // 2
HI16 = -65536


def _pack_bf16_pair(lo, hi):
    lo_bits = lax.bitcast_convert_type(lo.astype(BF16).astype(F32), I32)
    hi_bits = lax.bitcast_convert_type(hi.astype(BF16).astype(F32), I32)
    return (hi_bits & HI16) | lax.shift_right_logical(lo_bits, 16)


def _pack_exact_bf16_pair(lo, hi):
    return ((lax.bitcast_convert_type(hi, I32) & HI16)
            | lax.shift_right_logical(lax.bitcast_convert_type(lo, I32), 16))


def _unpack_bf16_pair(p):
    lo = lax.bitcast_convert_type(lax.shift_left(p, 16), F32).astype(BF16)
    hi = lax.bitcast_convert_type(p & HI16, F32).astype(BF16)
    return lo, hi


GRP_EXPERT, GRP_REL, GRP_COUNT = range(3)


def _route_rows(x, g, mod, wr, br, tril_ref, triu_ref, carry_ref):
    u = _norm_mod(x, g, mod, 3, 4)
    u_packed = _pack_bf16_pair(u[:, :HALF_D], u[:, HALF_D:])
    logits = _dot_3pass(u, wr) + br
    lane = lax.broadcasted_iota(I32, logits.shape, 1)
    lane_f = lane.astype(F32)
    neg = -jnp.inf
    big = float(LANES)

    def first_max(vals):
        m = jnp.max(vals, axis=-1, keepdims=True)
        return m, jnp.min(jnp.where(vals == m, lane_f, big), axis=-1, keepdims=True)

    is_grp = lane < N_GROUPS
    gmax, gidx = first_max(jnp.where(is_grp, logits, neg))
    gsum = jnp.sum(jnp.where(is_grp, jnp.exp(logits - gmax), 0.0), axis=-1, keepdims=True)
    g_w = 1.0 / gsum
    lo = N_GROUPS + EXP_PER_GROUP * gidx
    in_grp = jnp.logical_and(lane_f >= lo, lane_f < lo + EXP_PER_GROUP)
    el = jnp.where(in_grp, logits, neg)
    v1, i1 = first_max(el)
    v2, i2 = first_max(jnp.where(lane_f == i1, neg, el))
    e = jnp.exp(v2 - v1)
    p1 = 1.0 / (1.0 + e)
    hit1 = lane_f == i1 - N_GROUPS
    hit2 = lane_f == i2 - N_GROUPS
    onehot = jnp.where(jnp.logical_or(hit1, hit2), 1.0, 0.0)
    cum = jnp.dot(tril_ref[...], onehot.astype(BF16), preferred_element_type=F32)
    n = cum[TM - 1:TM, :]
    n_pad = jnp.floor((n + (ROW_ALIGN - 1.0)) * (1.0 / ROW_ALIGN)) * ROW_ALIGN
    loc_off = jnp.dot(jnp.broadcast_to(n_pad, (SUBLANES, LANES)).astype(BF16), triu_ref[...],
                      preferred_element_type=F32)[0:1]
    base = cum - onehot + loc_off
    loc1 = jnp.sum(jnp.where(hit1, base, 0.0), axis=-1, keepdims=True)
    loc2 = jnp.sum(jnp.where(hit2, base, 0.0), axis=-1, keepdims=True)
    info = jnp.where(lane == 0, g_w * p1, jnp.where(lane == 1, g_w * (e * p1), jnp.where(lane == 2, loc1, loc2)))
    start8 = lax.broadcasted_iota(I32, (LANES, LANES), 0).astype(F32) * ROW_ALIGN
    owner = jnp.where(jnp.logical_and(start8 >= loc_off, start8 < loc_off + n_pad), 1.0, 0.0)
    lane128 = lax.broadcasted_iota(I32, (LANES, LANES), 1)
    c_exp = jnp.sum(owner * lane128.astype(F32), axis=-1, keepdims=True)
    c_rel = jnp.sum(owner * (carry_ref[...] + start8 - loc_off), axis=-1, keepdims=True)
    chunks = jnp.where(lane128 == 0, c_exp, jnp.where(lane128 == 1, c_rel, 0.0)).T
    n_chunks = jnp.sum(n_pad, axis=-1, keepdims=True) * (1.0 / ROW_ALIGN)
    row = lax.broadcasted_iota(I32, (SUBLANES, LANES), 0)
    grp = jnp.where(row == GRP_EXPERT, chunks[0:1], jnp.where(row == GRP_REL, chunks[1:2], n_chunks))
    carry_ref[...] = carry_ref[...] + n_pad
    return u_packed, info, info.T[:SUBLANES], grp.astype(I32)


def _proj_res_router_kernel(*refs, n_x, gate_idx):
    (al_ref, ac_ref, bl_ref, bc_ref, wa_ref, wb_ref, mod_ref, g2_ref, wr_ref, br_ref, tril_ref, triu_ref,
     o_ref, u_ref, loct_ref, info_ref, grp_ref, tot_ref, carry_ref) = refs[n_x:]

    @pl.when(pl.program_id(0) == 0)
    def _():
        carry_ref[...] = jnp.zeros_like(carry_ref)

    a = _group_pick(al_ref, ac_ref)
    b = _group_pick(bl_ref, bc_ref)
    y = jnp.dot(a.astype(BF16), wa_ref[...], preferred_element_type=F32)
    y = y + jnp.dot(b.astype(BF16), wb_ref[...], preferred_element_type=F32)
    mod = mod_ref[0]
    x_new = _rows_value(refs[:n_x]) + mod[gate_idx:gate_idx + 1] * y
    o_ref[...] = x_new
    for t in range(TM_ROUTE // TM):
        rows = slice(t * TM, (t + 1) * TM)
        u_packed, info, info_t, grp = _route_rows(x_new[rows], g2_ref[...], mod, wr_ref[...], br_ref[...],
                                                 tril_ref, triu_ref, carry_ref)
        u_ref[rows, :] = u_packed
        info_ref[rows, :] = info
        loct_ref[t] = info_t
        grp_ref[t] = grp
    tot_ref[...] = jnp.broadcast_to(carry_ref[...], tot_ref.shape).astype(I32)


def _proj_residual_router(x, a, b, w_bf16, mods, gate_idx, norm2, w_route, b_route, tril, triu):
    half = a[0].shape[1]
    tm = TM_ROUTE
    sub = tm // TM
    nt = N_TOK // TM
    x_specs, x_args = _rows_specs(x, tm)
    const = lambda shape: pl.BlockSpec(shape, lambda i: (0,) * len(shape))
    return pl.pallas_call(
        functools.partial(_proj_res_router_kernel, n_x=len(x_args), gate_idx=gate_idx),
        grid=(N_TOK // tm,),
        in_specs=x_specs + _group_specs(half, tm) + _group_specs(half, tm) + [
            pl.BlockSpec((half, D_MODEL), lambda i: (0, 0)),
            pl.BlockSpec((half, D_MODEL), lambda i: (1, 0)),
            pl.BlockSpec((1, MOD_ROWS, D_MODEL), lambda i: (i * tm // SEG, 0, 0)),
            const((1, D_MODEL)), const((D_MODEL, LANES)), const((1, LANES)), const((TM, TM)), const((LANES, LANES)),
        ],
        out_specs=[
            pl.BlockSpec((tm, D_MODEL), lambda i: (i, 0)),
            pl.BlockSpec((tm, HALF_D), lambda i: (i, 0)),
            pl.BlockSpec((sub, SUBLANES, TM), lambda i: (i, 0, 0)),
            pl.BlockSpec((tm, LANES), lambda i: (i, 0)),
            pl.BlockSpec((sub, SUBLANES, LANES), lambda i: (i, 0, 0)),
            pl.BlockSpec((SUBLANES, LANES), lambda i: (0, 0)),
        ],
        out_shape=[
            jax.ShapeDtypeStruct((N_TOK, D_MODEL), F32),
            jax.ShapeDtypeStruct((N_TOK, HALF_D), I32),
            jax.ShapeDtypeStruct((nt, SUBLANES, TM), F32),
            jax.ShapeDtypeStruct((N_TOK, LANES), F32),
            jax.ShapeDtypeStruct((nt, SUBLANES, LANES), I32),
            jax.ShapeDtypeStruct((SUBLANES, LANES), I32),
        ],
        scratch_shapes=[pltpu.VMEM((1, LANES), F32)],
        compiler_params=_cparams(("arbitrary",)),
        name="outproj_residual_router",
    )(*x_args, a[0], a[1], b[0], b[1], w_bf16, w_bf16, mods, norm2.reshape(1, D_MODEL), w_route, b_route, tril, triu)


def _for_chunks(n, fn):
    for size in CHUNKS:
        @pl.when((n & size) != 0)
        def _():
            fn(n & ~(2 * size - 1), size)


N_COPIES = R_LOC // ROW_ALIGN
DUMP_ROWS = (2 * R_LOC + MOE_BLK - 1) // MOE_BLK * MOE_BLK


def _start_tile_copies(grp_ref, seg_ref, make, dummy, n=None):
    if n is None:
        n = grp_ref[0, GRP_COUNT, 0]
    for c in range(N_COPIES):
        real = c < n
        slot = seg_ref[0, grp_ref[0, GRP_EXPERT, c]] + grp_ref[0, GRP_REL, c]
        d_loc, d_slot = dummy(c)
        loc = jnp.where(real, c * ROW_ALIGN, d_loc)
        slot = jnp.where(real, slot, d_slot)
        make(pl.ds(pl.multiple_of(loc, ROW_ALIGN), ROW_ALIGN), pl.ds(pl.multiple_of(slot, ROW_ALIGN), ROW_ALIGN)).start()


def _wait_tile_copies(make):
    make(pl.ds(0, R_LOC), pl.ds(0, R_LOC)).wait()


def _all_experts(body):
    lax.fori_loop(0, N_EXPERTS, lambda e, c: (body(e), c)[1], 0)


def _dispatch_kernel(grp_ref, seg_ref, loct_ref, u_ref, xb_ref, buf, zbuf, sem):
    i = pl.program_id(0)
    last = pl.num_programs(0) - 1
    cur = i % 2

    def maker(b):
        return lambda loc, slot: pltpu.make_async_copy(buf.at[b, loc], xb_ref.at[slot], sem.at[b])

    @pl.when(i >= 2)
    def _():
        _wait_tile_copies(maker(cur))

    lo, hi = _unpack_bf16_pair(u_ref[...])
    row = lax.broadcasted_iota(I32, (R_LOC, TM), 0).astype(F32)
    sel = jnp.logical_or(row == loct_ref[0, 2:3, :], row == loct_ref[0, 3:4, :])
    pt = jnp.where(sel, 1.0, 0.0).astype(BF16)
    buf[cur] = _pack_exact_bf16_pair(jnp.dot(pt, lo, preferred_element_type=F32),
                                     jnp.dot(pt, hi, preferred_element_type=F32))
    _start_tile_copies(grp_ref, seg_ref, maker(cur),
                       lambda c: (c * ROW_ALIGN, MOE_ROWS + cur * R_LOC + c * ROW_ALIGN))

    @pl.when(i == last)
    def _():
        @pl.when(i >= 1)
        def _():
            _wait_tile_copies(maker(1 - cur))

        _wait_tile_copies(maker(cur))

        zbuf[...] = jnp.zeros_like(zbuf)

        def tail(e, act):
            total = seg_ref[1, e]
            dst = seg_ref[0, e] + total

            def one(off, size):
                act(pltpu.make_async_copy(zbuf.at[pl.ds(0, size)],
                                          xb_ref.at[pl.ds(pl.multiple_of(dst + off, ROW_ALIGN), size)], sem.at[2]))

            _for_chunks((-total) & (MOE_BLK - 1), one)

        _all_experts(lambda e: tail(e, lambda cp: cp.start()))
        _all_experts(lambda e: tail(e, lambda cp: cp.wait()))

        e_last = N_EXPERTS - 1
        used = (seg_ref[0, e_last] + seg_ref[1, e_last] + MOE_BLK - 1) // MOE_BLK

        def spare(j):
            return pltpu.make_async_copy(zbuf, xb_ref.at[pl.ds(pl.multiple_of(j * MOE_BLK, MOE_BLK), MOE_BLK)],
                                         sem.at[2])

        lax.fori_loop(used, (MOE_ROWS + DUMP_ROWS) // MOE_BLK, lambda j, c: (spare(j).start(), c)[1], 0)
        lax.fori_loop(used, (MOE_ROWS + DUMP_ROWS) // MOE_BLK, lambda j, c: (spare(j).wait(), c)[1], 0)


def _dispatch(grp, seg, loct, u):
    return pl.pallas_call(
        _dispatch_kernel,
        grid=(N_TOK // TM,),
        in_specs=[
            pl.BlockSpec((1, SUBLANES, LANES), lambda i: (i, 0, 0), memory_space=pltpu.SMEM),
            pl.BlockSpec(memory_space=pltpu.SMEM),
            pl.BlockSpec((1, SUBLANES, TM), lambda i: (i, 0, 0)),
            pl.BlockSpec((TM, HALF_D), lambda i: (i, 0)),
        ],
        out_specs=pl.BlockSpec(memory_space=pl.ANY),
        out_shape=jax.ShapeDtypeStruct((MOE_ROWS + DUMP_ROWS, HALF_D), I32),
        scratch_shapes=[pltpu.VMEM((2, R_LOC, HALF_D), I32), pltpu.VMEM((MOE_BLK, HALF_D), I32),
                        pltpu.SemaphoreType.DMA((3,))],
        compiler_params=_cparams(("arbitrary",)),
        name="moe_dispatch",
    )(grp, seg, loct, u)


def _expert_kernel(seg_ref, w1_ref, w3_ref, w2_ref, xb_ref, yb_ref, w1b, w3b, w2b, xbuf, ybuf, xsem, ysem):
    e = pl.program_id(0)
    last_e = pl.num_programs(0) - 1
    blk0 = seg_ref[0, e] // MOE_BLK
    nblk = (seg_ref[1, e] + MOE_BLK - 1) // MOE_BLK
    used = (seg_ref[0, N_EXPERTS - 1] + seg_ref[1, N_EXPERTS - 1] + MOE_BLK - 1) // MOE_BLK

    def rows(g):
        return pl.ds(pl.multiple_of(g * MOE_BLK, MOE_BLK), MOE_BLK)

    def fetch(g, b):
        return pltpu.make_async_copy(xb_ref.at[rows(g)], xbuf.at[b], xsem.at[b])

    def flush(g, b):
        return pltpu.make_async_copy(ybuf.at[b], yb_ref.at[rows(g)], ysem.at[b])

    row_queue = 1

    @pl.when(e == 0)
    def _():
        fetch(0, 0).start(priority=row_queue)

    @pl.when(nblk > 0)
    def _():
        w1b[...] = w1_ref[0, 0].astype(BF16)
        w3b[...] = w3_ref[0, 0].astype(BF16)
        w2b[...] = w2_ref[0, 0].astype(BF16)

        def block(i, carry):
            g = blk0 + i
            b = g % 2
            fetch(g, b).wait()

            @pl.when(g + 1 < used)
            def _():
                fetch(g + 1, 1 - b).start(priority=row_queue)

            @pl.when(g >= 2)
            def _():
                flush(g - 2, b).wait()

            def ffn(x_packed):
                lo, hi = _unpack_bf16_pair(x_packed)

                def up(wb):
                    return (jnp.dot(lo, wb[:HALF_D, :], preferred_element_type=F32)
                            + jnp.dot(hi, wb[HALF_D:, :], preferred_element_type=F32))

                hdn = (_silu(up(w1b)) * up(w3b)).astype(BF16)
                y = jnp.dot(hdn, w2b[...], preferred_element_type=F32)
                return _pack_bf16_pair(y[:, :HALF_D], y[:, HALF_D:])

            slots_here = seg_ref[1, e] - i * MOE_BLK
            quarter = MOE_BLK // MOE_PARTS
            for parts in range(1, MOE_PARTS + 1):
                rows_used = parts * quarter
                fits = slots_here > rows_used - quarter
                if parts < MOE_PARTS:
                    fits = jnp.logical_and(fits, slots_here <= rows_used)

                @pl.when(fits)
                def _():
                    ybuf[b, :rows_used, :] = ffn(xbuf[b, :rows_used, :])
                    if rows_used < MOE_BLK:
                        ybuf[b, rows_used:, :] = jnp.zeros((MOE_BLK - rows_used, HALF_D), I32)

            flush(g, b).start(priority=row_queue)
            return carry

        lax.fori_loop(0, nblk, block, 0)

    @pl.when(e == last_e)
    def _():
        @pl.when(used >= 2)
        def _():
            flush(used - 2, used % 2).wait()

        flush(used - 1, (used - 1) % 2).wait()
        xbuf[0] = jnp.zeros((MOE_BLK, HALF_D), I32)

        def spare(g):
            return pltpu.make_async_copy(xbuf.at[0], yb_ref.at[rows(g)], xsem.at[0])

        lax.fori_loop(used, MOE_NB, lambda g, c: (spare(g).start(), c)[1], 0)
        lax.fori_loop(used, MOE_NB, lambda g, c: (spare(g).wait(), c)[1], 0)


def _experts(seg, xb, w1, w3, w2, layer):
    hbm = pl.BlockSpec(memory_space=pl.ANY)
    w_in = pl.BlockSpec((1, 1, D_MODEL, D_EXPERT), lambda e, s: (layer, e, 0, 0))
    w_out = pl.BlockSpec((1, 1, D_EXPERT, D_MODEL), lambda e, s: (layer, e, 0, 0))
    return pl.pallas_call(
        _expert_kernel,
        grid_spec=pltpu.PrefetchScalarGridSpec(
            num_scalar_prefetch=1,
            grid=(N_EXPERTS,),
            in_specs=[w_in, w_in, w_out, hbm],
            out_specs=hbm,
            scratch_shapes=[pltpu.VMEM((D_MODEL, D_EXPERT), BF16), pltpu.VMEM((D_MODEL, D_EXPERT), BF16),
                            pltpu.VMEM((D_EXPERT, D_MODEL), BF16),
                            pltpu.VMEM((2, MOE_BLK, HALF_D), I32), pltpu.VMEM((2, MOE_BLK, HALF_D), I32),
                            pltpu.SemaphoreType.DMA((2,)), pltpu.SemaphoreType.DMA((2,))],
        ),
        out_shape=jax.ShapeDtypeStruct((MOE_ROWS, HALF_D), I32),
        compiler_params=_cparams(("arbitrary",)),
        name="moe_experts",
    )(seg, w1, w3, w2, xb)


def _combine_kernel(*refs, final):
    grp_ref, nxt_ref, seg_ref, x_ref, info_ref, mod_ref, yb_ref = refs[:7]
    if final:
        fin_ref, olat_ref, octx_ref, ybuf, sem = refs[7:]
    else:
        g1n_ref, modn_ref, wn_ref, o_ref, proj_ref, ybuf, sem, wnb_ref = refs[7:]
    i = pl.program_id(0)
    cur = i % 2

    def maker(b):
        return lambda loc, slot: pltpu.make_async_copy(yb_ref.at[slot], ybuf.at[b, loc], sem.at[b])

    def dummy(c):
        return R_LOC + c * ROW_ALIGN, c * ROW_ALIGN

    @pl.when(i == 0)
    def _():
        ybuf[...] = jnp.zeros_like(ybuf)
        _start_tile_copies(grp_ref, seg_ref, maker(0), dummy)
        if not final:
            wnb_ref[...] = wn_ref[...].astype(BF16)

    has_next = i + 1 < pl.num_programs(0)
    _start_tile_copies(nxt_ref, seg_ref, maker(1 - cur), dummy, jnp.where(has_next, nxt_ref[0, GRP_COUNT, 0], 0))
    _wait_tile_copies(maker(cur))

    @pl.when(jnp.logical_not(has_next))
    def _():
        _wait_tile_copies(maker(1 - cur))

    info = info_ref[...]
    col = lax.broadcasted_iota(I32, (TM, R_LOC), 1).astype(F32)
    p = jnp.where(col == info[:, 2:3], info[:, 0:1], jnp.where(col == info[:, 3:4], info[:, 1:2], 0.0))
    p = p.astype(BF16)
    lo, hi = _unpack_bf16_pair(ybuf[cur, :R_LOC, :])
    y = jnp.concatenate([jnp.dot(p, lo, preferred_element_type=F32),
                         jnp.dot(p, hi, preferred_element_type=F32)], axis=1)
    out = x_ref[...] + mod_ref[0][5:6] * y
    if not final:
        o_ref[...] = out
        u_next = _norm_mod(out, g1n_ref[...], modn_ref[0], 0, 1)
        proj_ref[...] = jnp.dot(u_next.astype(BF16), wnb_ref[...], preferred_element_type=F32)
        return
    out = out * lax.rsqrt(jnp.mean(out * out, axis=-1, keepdims=True) + EPS) * fin_ref[...]

    @pl.when(i < N_LAT_TILES)
    def _():
        olat_ref[...] = out

    @pl.when(i >= N_LAT_TILES)
    def _():
        octx_ref[...] = out


def _combine(x, grp, seg, info, mods, yb, final_g=None, next_proj=None):
    final = final_g is not None
    nt = N_TOK // TM
    in_specs = [
        pl.BlockSpec((1, SUBLANES, LANES), lambda i: (i, 0, 0), memory_space=pltpu.SMEM),
        pl.BlockSpec((1, SUBLANES, LANES), lambda i: (jnp.minimum(i + 1, nt - 1), 0, 0), memory_space=pltpu.SMEM),
        pl.BlockSpec(memory_space=pltpu.SMEM),
        pl.BlockSpec((TM, D_MODEL), lambda i: (i, 0)),
        pl.BlockSpec((TM, LANES), lambda i: (i, 0)),
        pl.BlockSpec((1, MOD_ROWS, D_MODEL), lambda i: (i * TM // SEG, 0, 0)),
        pl.BlockSpec(memory_space=pl.ANY),
    ]
    args = [grp, grp, seg, x, info, mods, yb]
    if final:
        in_specs.append(pl.BlockSpec((1, D_MODEL), lambda i: (0, 0)))
        args.append(final_g.reshape(1, D_MODEL))
        out_specs = _group_specs(D_MODEL)
        out_shape = [jax.ShapeDtypeStruct((N_SAMPLE, D_MODEL), F32), jax.ShapeDtypeStruct((N_PROMPT, D_MODEL), F32)]
    else:
        g1n, modn, wn = next_proj
        n = wn.shape[1]
        in_specs += [pl.BlockSpec((1, D_MODEL), lambda i: (0, 0)),
                     pl.BlockSpec((1, MOD_ROWS, D_MODEL), lambda i: (i * TM // SEG, 0, 0)),
                     pl.BlockSpec((D_MODEL, n), lambda i: (0, 0), pipeline_mode=pl.Buffered(1))]
        args += [g1n.reshape(1, D_MODEL), modn, wn]
        out_specs = [pl.BlockSpec((TM, D_MODEL), lambda i: (i, 0)), pl.BlockSpec((TM, n), lambda i: (i, 0))]
        out_shape = [jax.ShapeDtypeStruct((N_TOK, D_MODEL), F32), jax.ShapeDtypeStruct((N_TOK, n), F32)]
    return pl.pallas_call(
        functools.partial(_combine_kernel, final=final),
        grid=(nt,),
        in_specs=in_specs,
        out_specs=out_specs,
        out_shape=out_shape,
        scratch_shapes=([pltpu.VMEM((2, 2 * R_LOC, HALF_D), I32), pltpu.SemaphoreType.DMA((2,))]
                        + ([] if final else [pltpu.VMEM(next_proj[2].shape, BF16)])),
        compiler_params=_cparams(("arbitrary",)),
        name="moe_combine_final" if final else "moe_combine",
    )(*args)


def _route_weights(w_grp, b_grp, w_exp, b_exp):
    spare = LANES - N_GROUPS - N_EXPERTS
    w_route = jnp.concatenate([w_grp, w_exp, jnp.zeros((D_MODEL, spare), F32)], axis=1)
    b_route = jnp.concatenate([b_grp, b_exp, jnp.zeros((spare,), F32)])[None, :]
    return w_route, b_route


def _moe_layer(x, route, layer, mods, w1, w3, w2, final_g=None, next_proj=None):
    u, loct, info, grp, tot = route
    total = tot[0, :N_EXPERTS]
    padded = (total + MOE_BLK - 1) // MOE_BLK * MOE_BLK
    pad_end = jnp.cumsum(padded)
    pad_start = pad_end - padded
    seg = jnp.zeros((2, LANES), I32).at[0, :N_EXPERTS].set(pad_start).at[1, :N_EXPERTS].set(total)
    xb = _dispatch(grp, seg, loct, u)
    yb = _experts(seg, xb, w1, w3, w2, layer)
    return _combine(x, grp, seg, info, mods, yb, final_g, next_proj)


def _pad_rows(w, rows=SUBLANES):
    return jnp.zeros((rows, w.shape[1]), w.dtype).at[:w.shape[0]].set(w)


def _pad2(w, rows, cols):
    return jnp.zeros((rows, cols), w.dtype).at[:w.shape[0], :w.shape[1]].set(w)


def kernel(x_prompt, x_sample, c, c_ctx, state_ret, cache_k, cache_v, ada_w, ada_b, norm1, norm2, final_norm, ev_w_in, ev_w_out, ret_decay_fwd, ret_decay_bwd, sc_conv_w, od_w_in, od_w_out, hy_conv_w, hy_w1, hy_b1, hy_fr1, hy_w2, hy_b2, hy_fr2, hy_w3, hy_bias, da_lambda, da_subln, moe_w_grp, moe_b_grp, moe_w_exp, moe_b_exp, moe_w1, moe_w3, moe_w2):
    x = (x_sample.reshape(N_SAMPLE, D_MODEL), x_prompt.reshape(N_PROMPT, D_MODEL))

    cvecs = jnp.zeros((16, D_MODEL), F32).at[:DEC_BATCH].set(c).at[DEC_BATCH].set(c_ctx)
    m = _mods(cvecs, ada_w, ada_b).reshape(DEPTH, 16, N_MOD, D_MODEL)
    seg_mods = jnp.concatenate(
        [m[:, :DEC_BATCH], jnp.broadcast_to(m[:, DEC_BATCH:DEC_BATCH + 1], (DEPTH, N_SEG - DEC_BATCH, N_MOD, D_MODEL))],
        axis=1)
    seg_mods = jnp.pad(seg_mods, ((0, 0), (0, 0), (0, MOD_ROWS - N_MOD), (0, 0)))

    tril = jnp.asarray(np.tril(np.ones((TM, TM), np.float32)), BF16)
    triu = jnp.asarray(np.triu(np.ones((LANES, LANES), np.float32), 1), BF16)

    def w_in(layer):
        w = ev_w_in if layer % 2 == 0 else od_w_in
        return w[layer // 2]

    proj = None
    for l in range(DEPTH):
        mods = seg_mods[l]
        i = l // 2
        if l % 2 == 0:
            if proj is None:
                proj = _norm_mod_matmul(x, norm1[l], mods, w_in(l), 0, 1)
            scw = _pad_rows(sc_conv_w[i])
            ret_l, sc_l = _even_mixer(proj, scw, ret_decay_fwd[i], ret_decay_bwd[i], state=state_ret[:, i:i + 1])
            ret_c, sc_c, new_state = _even_mixer(proj, scw, ret_decay_fwd[i], ret_decay_bwd[i])
            mixed, w_out = ((ret_l, ret_c), (sc_l, sc_c)), ev_w_out[i]
        else:
            lambda_init = 0.8 - 0.6 * math.exp(-0.3 * l)
            if proj is None:
                proj = _norm_mod_matmul(x, norm1[l], mods, w_in(l), 0, 1)
            conv_w = _pad_rows(hy_conv_w[i])
            bias = _pad_rows(hy_bias[i])
            w1 = _pad2(hy_w1[i], LANES, LANES)
            b1 = _pad2(hy_b1[i][None], 1, LANES)
            f1 = _pad2(hy_fr1[i][None], 1, LANES)
            w2 = _pad2(hy_w2[i], LANES, LANES)
            b2 = _pad2(hy_b2[i][None], 1, LANES)
            f2 = _pad2(hy_fr2[i][None], 1, LANES)
            w3 = _pad2(hy_w3[i], LANES, hy_w3.shape[2])
            hy = []
            for latent in (True, False):
                seq = DEC_SEQ if latent else SEQ
                fwd_np, inv_np = _dft_tables(seq)
                zfeat, decay = _hyena_consts(seq)
                fwd_bf, fwd_lo = _split_table(jnp.asarray(fwd_np), f"dft_fwd_split_{seq}")
                inv_bf, _ = _split_table(jnp.asarray(inv_np), f"dft_inv_split_{seq}")
                filters = _hy_filters(seq, jnp.asarray(zfeat), jnp.asarray(decay), fwd_bf, fwd_lo,
                                      w1, b1, f1, w2, b2, f2, w3)
                hy.append(_hy_apply(proj, conv_w, fwd_bf, inv_bf, filters, bias, latent))
            cos, sin = _rope_tables(DEC_SEQ)
            att_l = _attention(proj, da_lambda[i], da_subln[i], lambda_init, True,
                               rope=(jnp.asarray(cos), jnp.asarray(sin)),
                               ctx_k=cache_k[:, i:i + 1], ctx_v=cache_v[:, i:i + 1])[0]
            att_c, new_k, new_v = _attention(proj, da_lambda[i], da_subln[i], lambda_init, False)
            mixed, w_out = (tuple(hy), (att_l, att_c)), od_w_out[i]
        w_route, b_route = _route_weights(moe_w_grp[l], moe_b_grp[l], moe_w_exp[l], moe_b_exp[l])
        x, *route = _proj_residual_router(x, mixed[0], mixed[1], w_out.astype(BF16), mods, 2,
                                          norm2[l], w_route, b_route, tril, triu)
        if l == DEPTH - 1:
            x = _moe_layer(x, route, l, mods, moe_w1, moe_w3, moe_w2, final_g=final_norm)
        else:
            x, proj = _moe_layer(x, route, l, mods, moe_w1, moe_w3, moe_w2,
                                 next_proj=(norm1[l + 1], seg_mods[l + 1], w_in(l + 1)))

    y_sample, y_prompt = x
    return (y_prompt.reshape(BATCH, SEQ, D_MODEL), y_sample.reshape(DEC_BATCH, DEC_SEQ, D_MODEL),
            new_state, new_k, new_v)
```

```python
import functools
import math

import numpy as np
import jax
import jax.numpy as jnp
from jax import lax
from jax.experimental import pallas as pl
from jax.experimental.pallas import tpu as pltpu

F32 = jnp.float32
BF16 = jnp.bfloat16
I32 = jnp.int32
HIGHEST = lax.Precision.HIGHEST

D_MODEL = 1024
BATCH = 32
SEQ = 256
DEPTH = 2
DEC_BATCH = 8
DEC_SEQ = 1024
PAST_LEN = 256
GRID_W = 64
N_MOD = 6
EPS = 1e-6
GN_EPS = 1e-5

RET_DK = 128
RET_DV = 128
RET_HEADS = 4
RET_W = 512
RET_QK = 512
SC_W = 512
EVEN_IN = 2 * RET_QK + 2 * RET_W + 3 * SC_W

HY_W = 512
HY_ORDER = 2
HY_EMB = 33
HY_FFN = 64
HY_TARGET = 1e-2
HY_FAST_PCT = 0.3
HY_SLOW_PCT = 1.5
DA_DH = 64
DA_HEADS = 4
DA_W = 512
ODD_IN = 3 * HY_W + 3 * DA_W
ROPE_THETA = 10000.0

N_GROUPS = 4
EXP_PER_GROUP = 8
N_EXPERTS = 32
TOP_K = 2
D_EXPERT = 512

N_SAMPLE = DEC_BATCH * DEC_SEQ
N_PROMPT = BATCH * SEQ
N_TOK = N_SAMPLE + N_PROMPT
SEG = 1024
N_SEG = N_TOK // SEG
MOD_ROWS = 8

LANES = 128
SUBLANES = 8
VMEM_LIMIT = 56 * 1024 * 1024

TM = 256
MOE_BLK = 1024
MOE_PARTS = 4
ROW_ALIGN = SUBLANES
R_LOC = (TOP_K * TM + N_EXPERTS * (ROW_ALIGN - 1) + LANES - 1) // LANES * LANES
CHUNKS = (512, 256, 128, 64, 32, 16, 8)
MOE_ROWS = ((N_TOK * TOP_K + N_EXPERTS * (N_TOK // TM) * (ROW_ALIGN - 1) + N_EXPERTS * (MOE_BLK - 1))
            // MOE_BLK + 1) * MOE_BLK
MOE_NB = MOE_ROWS // MOE_BLK
HY_SLAB = 512
ATT_TQ = 256


def _cparams(sem, vmem=VMEM_LIMIT):
    return pltpu.CompilerParams(dimension_semantics=sem, vmem_limit_bytes=vmem)


def _silu(x):
    return x * (1.0 / (1.0 + jnp.exp(-x)))


def _split_bf16(a):
    hi = a.astype(BF16)
    return hi, (a - hi.astype(F32)).astype(BF16)


def _dot_3pass(a, b):
    ah, al = _split_bf16(a)
    bh, bl = _split_bf16(b)
    dot = functools.partial(jnp.dot, preferred_element_type=F32)
    return dot(ah, bh) + (dot(ah, bl) + dot(al, bh))


MODS_TN = 3072


def _mods_kernel(c_ref, w_ref, b_ref, o_ref):
    s = _silu(c_ref[...])
    o_ref[0] = jnp.dot(s, w_ref[0], preferred_element_type=F32, precision=HIGHEST) + b_ref[0]


def _mods(cvecs, ada_w, ada_b):
    n = N_MOD * D_MODEL
    return pl.pallas_call(
        _mods_kernel,
        grid=(DEPTH, n // MODS_TN),
        in_specs=[
            pl.BlockSpec((16, D_MODEL), lambda l, j: (0, 0)),
            pl.BlockSpec((1, D_MODEL, MODS_TN), lambda l, j: (l, 0, j)),
            pl.BlockSpec((1, 1, MODS_TN), lambda l, j: (l, 0, j)),
        ],
        out_specs=pl.BlockSpec((1, 16, MODS_TN), lambda l, j: (l, 0, j)),
        out_shape=jax.ShapeDtypeStruct((DEPTH, 16, n), F32),
        compiler_params=_cparams(("parallel", "parallel")),
        name="adaln_mods",
    )(cvecs, ada_w, ada_b.reshape(DEPTH, 1, n))


def _norm_mod(x, g, mod, shift_idx, scale_idx):
    ms = jnp.mean(x * x, axis=-1, keepdims=True)
    y = x * lax.rsqrt(ms + EPS) * g
    return y * (1.0 + mod[scale_idx:scale_idx + 1]) + mod[shift_idx:shift_idx + 1]


N_LAT_TILES = N_SAMPLE // TM
TM_WIDE = 512
TM_ROUTE = 1024


def _group_specs(width, tm=TM):
    n_lat = N_SAMPLE // tm
    lat = pl.BlockSpec((tm, width), lambda i: (jnp.minimum(i, n_lat - 1), 0))
    ctx = pl.BlockSpec((tm, width), lambda i: (jnp.maximum(i - n_lat, 0), 0))
    return [lat, ctx]


def _group_pick(lat_ref, ctx_ref):
    n_lat = N_SAMPLE // lat_ref.shape[0]
    return jnp.where(pl.program_id(0) < n_lat, lat_ref[...], ctx_ref[...])


def _rows_specs(x, tm=TM):
    if isinstance(x, tuple):
        return _group_specs(D_MODEL, tm), list(x)
    return [pl.BlockSpec((tm, D_MODEL), lambda i: (i, 0))], [x]


def _rows_value(refs):
    return _group_pick(*refs) if len(refs) == 2 else refs[0][...]


def _nmm_kernel(*refs, n_x, shift_idx, scale_idx):
    g_ref, mod_ref, w_ref, o_ref, wb_ref = refs[n_x:]

    @pl.when(pl.program_id(0) == 0)
    def _():
        wb_ref[...] = w_ref[...].astype(BF16)

    u = _norm_mod(_rows_value(refs[:n_x]), g_ref[...], mod_ref[0], shift_idx, scale_idx)
    o_ref[...] = jnp.dot(u.astype(BF16), wb_ref[...], preferred_element_type=F32)


def _norm_mod_matmul(x, g, mods, w, shift_idx, scale_idx):
    n = w.shape[1]
    tm = TM_WIDE
    x_specs, x_args = _rows_specs(x, tm)
    return pl.pallas_call(
        functools.partial(_nmm_kernel, n_x=len(x_args), shift_idx=shift_idx, scale_idx=scale_idx),
        grid=(N_TOK // tm,),
        in_specs=x_specs + [
            pl.BlockSpec((1, D_MODEL), lambda i: (0, 0)),
            pl.BlockSpec((1, MOD_ROWS, D_MODEL), lambda i: (i * tm // SEG, 0, 0)),
            pl.BlockSpec((D_MODEL, n), lambda i: (0, 0), pipeline_mode=pl.Buffered(1)),
        ],
        out_specs=pl.BlockSpec((tm, n), lambda i: (i, 0)),
        out_shape=jax.ShapeDtypeStruct((N_TOK, n), F32),
        scratch_shapes=[pltpu.VMEM((D_MODEL, n), BF16)],
        compiler_params=_cparams(("arbitrary",)),
        name="norm_mod_inproj",
    )(*x_args, g.reshape(1, D_MODEL), mods, w)


def _conv3_rows(z, w):
    n = z.shape[0]
    row = lax.broadcasted_iota(I32, z.shape, 0)
    zm = jnp.where(row == 0, 0.0, pltpu.roll(z, 1, 0))
    zp = jnp.where(row == n - 1, 0.0, pltpu.roll(z, n - 1, 0))
    return zm * w[0:1] + z * w[1:2] + zp * w[2:3]


CTX_NSEQ = 4


def _log_gamma(dec):
    return jnp.log1p(-jnp.exp(dec))


def _even_kernel(*refs, seq, latent, tq, heads, nseq):
    (q_ref, k_ref, v_ref, g_ref, gb_ref, gc_ref, hx_ref, scw_ref, df_ref, db_ref) = refs[:10]
    if latent:
        s0_ref, ret_ref, sc_ref, mask_ref = refs[10:]
    else:
        ret_ref, sc_ref, st_ref, mask_ref = refs[10:]

    decays = []
    for hh in range(heads):
        lgf = _log_gamma(df_ref[hh])
        lgb = _log_gamma(db_ref[hh])
        decays.append((lgf, lgb))

        @pl.when(pl.program_id(1) == 0)
        def _():
            for r in range(seq // tq):
                t = lax.broadcasted_iota(I32, (tq, seq), 0) + r * tq
                s = lax.broadcasted_iota(I32, (tq, seq), 1)
                d = (t - s).astype(F32)
                mf = jnp.where(d >= 0, jnp.exp(jnp.maximum(d, 0.0) * lgf), 0.0)
                mb = jnp.where(d <= 0, jnp.exp(jnp.maximum(-d, 0.0) * lgb), 0.0)
                mask_ref[hh, r * tq:(r + 1) * tq, :] = mf + mb

    for sq in range(nseq):
        whole = slice(sq * seq, (sq + 1) * seq)
        for hh in range(heads):
            lanes = slice(hh * RET_DK, (hh + 1) * RET_DK)
            lgf, lgb = decays[hh]
            k = k_ref[whole, lanes] * (RET_DK ** -0.5)
            kb = k.astype(BF16)
            vb = v_ref[whole, lanes].astype(BF16)
            if latent:
                s0f = s0_ref[sq, 0, 0, hh].astype(BF16)
                s0b = s0_ref[sq, 0, 1, hh].astype(BF16)
            for r in range(seq // tq):
                rows = slice(sq * seq + r * tq, sq * seq + (r + 1) * tq)
                qb = q_ref[rows, lanes].astype(BF16)
                s = lax.dot_general(qb, kb, (((1,), (1,)), ((), ())), preferred_element_type=F32)
                p = (s * mask_ref[hh, r * tq:(r + 1) * tq, :]).astype(BF16)
                o = jnp.dot(p, vb, preferred_element_type=F32)
                if latent:
                    tpos = (lax.broadcasted_iota(I32, (tq, 1), 0) + r * tq).astype(F32)
                    o = o + jnp.dot(qb, s0f, preferred_element_type=F32) * jnp.exp((tpos + 1.0) * lgf)
                    o = o + jnp.dot(qb, s0b, preferred_element_type=F32) * jnp.exp((seq - tpos) * lgb)
                mu = jnp.mean(o, axis=-1, keepdims=True)
                var = jnp.mean(jnp.square(o - mu), axis=-1, keepdims=True)
                on = (o - mu) * lax.rsqrt(var + GN_EPS)
                ret_ref[rows, lanes] = _silu(g_ref[rows, lanes]) * on

            if not latent:
                spos = lax.broadcasted_iota(I32, (seq, 1), 0).astype(F32)
                kf = (k * jnp.exp((seq - 1.0 - spos) * lgf)).T.astype(BF16)
                kr = (k * jnp.exp(spos * lgb)).T.astype(BF16)
                st_ref[sq, 0, 0, hh] = jnp.dot(kf, vb, preferred_element_type=F32)
                st_ref[sq, 0, 1, hh] = jnp.dot(kr, vb, preferred_element_type=F32)

        sc_ref[whole, :] = gb_ref[whole, :] * _conv3_rows(gc_ref[whole, :] * hx_ref[whole, :], scw_ref[...])


def _even_mixer(proj, sc_w_pad, dec_f, dec_b, state=None):
    latent = state is not None
    seq = DEC_SEQ if latent else SEQ
    nb = DEC_BATCH if latent else BATCH
    tq = min(seq, 256)
    heads = 2 if latent else RET_HEADS
    nseq = 1 if latent else CTX_NSEQ
    rows = nseq * seq
    row0 = 0 if latent else N_SAMPLE // rows
    steps = RET_HEADS // heads
    hw = heads * RET_DK

    def col(c0):
        return pl.BlockSpec((rows, hw), lambda h, b: (row0 + b, c0 * steps + h))

    in_specs = [col(0), col(1), col(2), col(3), col(4), col(5), col(6),
                pl.BlockSpec((SUBLANES, hw), lambda h, b: (0, h)),
                pl.BlockSpec((heads, 1, 1), lambda h, b: (h, 0, 0)),
                pl.BlockSpec((heads, 1, 1), lambda h, b: (h, 0, 0))]
    args = [proj] * 7 + [sc_w_pad, dec_f.reshape(RET_HEADS, 1, 1), dec_b.reshape(RET_HEADS, 1, 1)]
    out_spec = pl.BlockSpec((rows, hw), lambda h, b: (b, h))
    out_shape = [jax.ShapeDtypeStruct((nb * seq, RET_W), F32), jax.ShapeDtypeStruct((nb * seq, SC_W), F32)]
    out_specs = [out_spec, out_spec]
    st_spec = pl.BlockSpec((nseq, 1, 2, heads, RET_DK, RET_DV), lambda h, b: (b, 0, 0, h, 0, 0))
    if latent:
        in_specs.append(st_spec)
        args.append(state)
    else:
        out_shape.append(jax.ShapeDtypeStruct((BATCH, 1, 2, RET_HEADS, RET_DK, RET_DV), F32))
        out_specs.append(st_spec)
    return pl.pallas_call(
        functools.partial(_even_kernel, seq=seq, latent=latent, tq=tq, heads=heads, nseq=nseq),
        grid=(steps, nb // nseq),
        in_specs=in_specs,
        out_specs=out_specs,
        out_shape=out_shape,
        scratch_shapes=[pltpu.VMEM((heads, seq, seq), F32)],
        compiler_params=_cparams(("arbitrary", "arbitrary")),
        name="even_mixer_latent" if latent else "even_mixer_context",
    )(*args)


def _dft_tables(length):
    f = np.arange(length, dtype=np.float64)[:, None]
    t = np.arange(length, dtype=np.float64)[None, :]
    ang = np.pi * ((f * t) % (2 * length)) / length
    cos, sin = np.cos(ang), np.sin(ang)
    sin[0, :] = (-1.0) ** np.arange(length)
    fwd = np.concatenate([cos, sin], axis=0)
    wgt = np.full((2 * length,), 2.0)
    wgt[0] = 1.0
    wgt[length] = 1.0
    inv = (fwd * wgt[:, None] / (2.0 * length)).T
    return fwd.astype(np.float32), np.ascontiguousarray(inv).astype(np.float32)


def _hyena_consts(length):
    t = np.linspace(0.0, 1.0, length)[:, None]
    bands = (HY_EMB - 1) // 2
    w = 2.0 * np.pi * np.arange(length)[:, None] / length
    f = np.linspace(1e-4, bands - 1, bands)[None, :]
    z = np.concatenate([t, np.cos(f * w), -np.sin(f * w)], axis=-1)
    zpad = np.zeros((length, LANES))
    zpad[:, :HY_EMB] = z
    deltas = np.abs(np.linspace(math.log(HY_TARGET) / HY_SLOW_PCT, math.log(HY_TARGET) / HY_FAST_PCT, HY_W))
    decay = np.exp(-t * deltas[None, :])
    return zpad.astype(np.float32), decay.astype(np.float32)


def _split_kernel(x_ref, hi_ref, lo_ref):
    hi, lo = _split_bf16(x_ref[...])
    hi_ref[...] = hi
    lo_ref[...] = lo


def _split_table(x, name):
    rows = min(x.shape[0], 1024)
    spec = pl.BlockSpec((rows, x.shape[1]), lambda i: (i, 0))
    out = jax.ShapeDtypeStruct(x.shape, BF16)
    return pl.pallas_call(
        _split_kernel,
        grid=(x.shape[0] // rows,),
        in_specs=[spec],
        out_specs=[spec, spec],
        out_shape=[out, out],
        compiler_params=_cparams(("parallel",)),
        name=name,
    )(x)


def _dot_3pass_split(ah, al, b):
    bh, bl = _split_bf16(b)
    dot = functools.partial(jnp.dot, preferred_element_type=F32)
    return dot(ah, bh) + (dot(ah, bl) + dot(al, bh))


def _hy_filter_kernel(z_ref, dec_ref, w1_ref, b1_ref, f1_ref, w2_ref, b2_ref, f2_ref, w3f_ref, w3b_ref,
                      fwd_ref, fwd_lo_ref, ka_ref, ka2_ref, kb_ref, *, seq):
    h = jnp.sin(f1_ref[...] * (jnp.dot(z_ref[...], w1_ref[...], preferred_element_type=F32,
                                       precision=HIGHEST) + b1_ref[...]))
    h = jnp.sin(f2_ref[...] * (jnp.dot(h, w2_ref[...], preferred_element_type=F32,
                                       precision=HIGHEST) + b2_ref[...]))
    dec = dec_ref[...]
    hf = jnp.dot(h, w3f_ref[...], preferred_element_type=F32, precision=HIGHEST) * dec
    hb = jnp.dot(h, w3b_ref[...], preferred_element_type=F32, precision=HIGHEST) * dec
    row = lax.broadcasted_iota(I32, hb.shape, 0)
    hb = jnp.where(row == 0, 0.0, hb)
    inv_norm = 1.0 / (jnp.sum(jnp.abs(hf), axis=0, keepdims=True)
                      + jnp.sum(jnp.abs(hb), axis=0, keepdims=True) + EPS)
    p = (hf + hb) * inv_norm
    q = (hf - hb) * inv_norm
    fp = _dot_3pass_split(fwd_ref[...], fwd_lo_ref[...], p)
    fq = _dot_3pass_split(fwd_ref[seq:, :], fwd_lo_ref[seq:, :], q)
    kr = fp[:seq]
    row = lax.broadcasted_iota(I32, kr.shape, 0)
    ka_ref[0] = kr
    ka2_ref[0] = jnp.where(row == 0, fp[seq:seq + 1], kr)
    kb_ref[0] = jnp.where(row == 0, 0.0, -fq)


def _hy_filters(seq, zfeat, decay, fwd_hi, fwd_lo, w1, b1, f1, w2, b2, f2, w3):
    ns = HY_W // HY_SLAB
    per_o = 2 * ns
    small = lambda shape: pl.BlockSpec(shape, lambda o, s: (0, 0))
    out_spec = pl.BlockSpec((1, seq, HY_SLAB), lambda o, s: (o, 0, s))
    out = jax.ShapeDtypeStruct((HY_ORDER, seq, HY_W), F32)
    return pl.pallas_call(
        functools.partial(_hy_filter_kernel, seq=seq),
        grid=(HY_ORDER, ns),
        in_specs=[
            small((seq, LANES)),
            pl.BlockSpec((seq, HY_SLAB), lambda o, s: (0, s)),
            small((LANES, LANES)), small((1, LANES)), small((1, LANES)),
            small((LANES, LANES)), small((1, LANES)), small((1, LANES)),
            pl.BlockSpec((LANES, HY_SLAB), lambda o, s: (0, o * per_o + s)),
            pl.BlockSpec((LANES, HY_SLAB), lambda o, s: (0, o * per_o + ns + s)),
            small((2 * seq, seq)), small((2 * seq, seq)),
        ],
        out_specs=[out_spec, out_spec, out_spec],
        out_shape=[out, out, out],
        compiler_params=_cparams(("parallel", "parallel")),
        name=f"hyena_filters_{seq}",
    )(zfeat, decay, w1, b1, f1, w2, b2, f2, w3, w3, fwd_hi, fwd_lo)


def _hy_apply_kernel(x1_ref, x2_ref, v_ref, c1_ref, c2_ref, cv_ref, fwd_ref, inv_ref,
                     ka_ref, ka2_ref, kb_ref, bias_ref, o_ref, *, seq, nseq):
    def long_conv(sig, o):
        spec = jnp.dot(fwd_ref[...], sig.astype(BF16), preferred_element_type=F32)
        xr, xs = spec[:seq], spec[seq:]
        kb = kb_ref[o]
        yr = (xr * ka_ref[o] + xs * kb).astype(BF16)
        ys = (xs * ka2_ref[o] - xr * kb).astype(BF16)
        y = jnp.dot(inv_ref[:, :seq], yr, preferred_element_type=F32)
        y = y + jnp.dot(inv_ref[:, seq:], ys, preferred_element_type=F32)
        return y + sig * bias_ref[o:o + 1]

    for i in range(nseq):
        rows = slice(i * seq, (i + 1) * seq)
        x1 = _conv3_rows(x1_ref[rows, :], c1_ref[...])
        x2 = _conv3_rows(x2_ref[rows, :], c2_ref[...])
        v = _conv3_rows(v_ref[rows, :], cv_ref[...])
        z = x1 * long_conv(v, 0)
        o_ref[rows, :] = x2 * long_conv(z, 1)


HY_NSEQ = {True: 1, False: 8}


def _hy_apply(proj, conv_w_pad, fwd_bf, inv_bf, filters, bias_pad, latent):
    seq = DEC_SEQ if latent else SEQ
    nb = DEC_BATCH if latent else BATCH
    nseq = HY_NSEQ[latent]
    rows = nseq * seq
    row0 = 0 if latent else N_SAMPLE // rows
    ns = HY_W // HY_SLAB
    once = pl.Buffered(1)
    const = lambda shape: pl.BlockSpec(shape, lambda s, b: (0,) * len(shape), pipeline_mode=once)
    col = lambda c0: pl.BlockSpec((rows, HY_SLAB), lambda s, b: (row0 + b, c0 + s))
    cw = lambda c0: pl.BlockSpec((SUBLANES, HY_SLAB), lambda s, b: (0, c0 + s))
    filt = pl.BlockSpec((HY_ORDER, seq, HY_SLAB), lambda s, b: (0, 0, s), pipeline_mode=once)
    in_specs = [col(0), col(ns), col(2 * ns), cw(0), cw(ns), cw(2 * ns),
                const((2 * seq, seq)), const((seq, 2 * seq)), filt, filt, filt,
                pl.BlockSpec((SUBLANES, HY_SLAB), lambda s, b: (0, s))]
    args = [proj] * 3 + [conv_w_pad] * 3 + [fwd_bf, inv_bf, *filters, bias_pad]
    return pl.pallas_call(
        functools.partial(_hy_apply_kernel, seq=seq, nseq=nseq),
        grid=(ns, nb // nseq),
        in_specs=in_specs,
        out_specs=pl.BlockSpec((rows, HY_SLAB), lambda s, b: (b, s)),
        out_shape=jax.ShapeDtypeStruct((nb * seq, HY_W), F32),
        compiler_params=_cparams(("arbitrary", "arbitrary")),
        name="hyena_latent" if latent else "hyena_context",
    )(*args)


def _rope_tables(length):
    rows = length // GRID_W
    row = np.repeat(np.arange(rows, dtype=np.float64), GRID_W)
    colp = np.tile(np.arange(GRID_W, dtype=np.float64), rows)
    half = DA_DH // 2
    freqs = ROPE_THETA ** (-np.arange(0, half, 2, dtype=np.float64) / half)
    ang = np.concatenate([row[:, None] * freqs[None, :]] * 2 + [colp[:, None] * freqs[None, :]] * 2, axis=1)
    ang = np.concatenate([ang, ang], axis=1)
    lane = np.arange(LANES)
    sign = np.where((lane % half) < half // 2, -1.0, 1.0)[None, :]
    return np.cos(ang).astype(np.float32), (np.sin(ang) * sign).astype(np.float32)


def _rope(x, cos, sin_signed):
    q = DA_DH // 4
    lane = lax.broadcasted_iota(I32, x.shape, 1)
    partner = jnp.where((lane % (2 * q)) < q, pltpu.roll(x, LANES - q, 1), pltpu.roll(x, q, 1))
    return x * cos + partner * sin_signed


def _attn_kernel(*refs, seq, latent, lambda_init, heads, nseq):
    q_ref, k_ref, v_ref, lam_ref, sub_ref = refs[:5]
    if latent:
        cos_ref, sin_ref, ck_ref, cv_ref, o_ref = refs[5:]
    else:
        o_ref, nk_ref, nv_ref = refs[5:]
    hw = 2 * DA_DH
    lv = lam_ref[...]
    lam = (jnp.exp(jnp.sum(lv[0:1] * lv[1:2], axis=1, keepdims=True))
           - jnp.exp(jnp.sum(lv[2:3] * lv[3:4], axis=1, keepdims=True)) + lambda_init)
    scale = DA_DH ** -0.5
    dn = (((1,), (1,)), ((), ()))
    tq = min(seq, ATT_TQ)
    for sq, hh in [(sq, hh) for sq in range(nseq) for hh in range(heads)]:
        whole = slice(sq * seq, (sq + 1) * seq)
        lanes = slice(hh * hw, (hh + 1) * hw)
        k = k_ref[whole, lanes]
        v = v_ref[whole, lanes]
        if latent:
            k = _rope(k, cos_ref[...], sin_ref[...])
            k_all = jnp.concatenate([k, jnp.concatenate([ck_ref[sq, 0, 0, hh], ck_ref[sq, 0, 1, hh]], axis=1)], axis=0)
            v_all = jnp.concatenate([v, cv_ref[sq, 0, hh]], axis=0)
        else:
            k_all, v_all = k, v
            nk_ref[sq, 0, 0, hh] = k[:, :DA_DH]
            nk_ref[sq, 0, 1, hh] = k[:, DA_DH:]
            nv_ref[sq, 0, hh] = v
        kb = k_all.astype(BF16)
        vb = v_all.astype(BF16)
        for r in range(seq // tq):
            rows = slice(sq * seq + r * tq, sq * seq + (r + 1) * tq)
            q = q_ref[rows, lanes]
            if latent:
                q = _rope(q, cos_ref[r * tq:(r + 1) * tq, :], sin_ref[r * tq:(r + 1) * tq, :])
            q = q * scale
            first = lax.broadcasted_iota(I32, q.shape, 1) < DA_DH
            q1 = jnp.where(first, q, 0.0).astype(BF16)
            q2 = jnp.where(first, 0.0, q).astype(BF16)
            s1 = lax.dot_general(q1, kb, dn, preferred_element_type=F32)
            s2 = lax.dot_general(q2, kb, dn, preferred_element_type=F32)
            e1 = jnp.exp(s1 - jnp.max(s1, axis=-1, keepdims=True))
            e2 = jnp.exp(s2 - jnp.max(s2, axis=-1, keepdims=True))
            o1 = jnp.dot(e1.astype(BF16), vb, preferred_element_type=F32)
            o2 = jnp.dot(e2.astype(BF16), vb, preferred_element_type=F32)
            o = (o1 * (1.0 / jnp.sum(e1, axis=-1, keepdims=True))
                 - o2 * (lam / jnp.sum(e2, axis=-1, keepdims=True)))
            o = o * lax.rsqrt(jnp.mean(o * o, axis=-1, keepdims=True) + EPS) * sub_ref[...]
            o_ref[rows, lanes] = o * (1.0 - lambda_init)


def _attention(proj, lam_vec, subln, lambda_init, latent, rope=None, ctx_k=None, ctx_v=None):
    seq = DEC_SEQ if latent else SEQ
    nb = DEC_BATCH if latent else BATCH
    hw = 2 * DA_DH
    heads = DA_HEADS
    nseq = 1 if latent else CTX_NSEQ
    rows = nseq * seq
    row0 = 0 if latent else N_SAMPLE // rows
    steps = DA_HEADS // heads
    col = lambda c: pl.BlockSpec((rows, heads * hw), lambda b, h: (row0 + b, c * steps + h))
    in_specs = [col(3), col(4), col(5),
                pl.BlockSpec((4, DA_DH), lambda b, h: (0, 0)),
                pl.BlockSpec((1, hw), lambda b, h: (0, 0))]
    args = [proj] * 3 + [lam_vec, subln.reshape(1, hw)]
    out_spec = pl.BlockSpec((rows, heads * hw), lambda b, h: (b, h))
    out_shape = [jax.ShapeDtypeStruct((nb * seq, DA_W), F32)]
    out_specs = [out_spec]
    k_spec = pl.BlockSpec((nseq, 1, 2, heads, SEQ, DA_DH), lambda b, h: (b, 0, 0, h, 0, 0))
    v_spec = pl.BlockSpec((nseq, 1, heads, SEQ, hw), lambda b, h: (b, 0, h, 0, 0))
    if latent:
        in_specs += [pl.BlockSpec((seq, hw), lambda b, h: (0, 0))] * 2 + [k_spec, v_spec]
        args += [rope[0], rope[1], ctx_k, ctx_v]
    else:
        out_shape += [jax.ShapeDtypeStruct((BATCH, 1, 2, DA_HEADS, SEQ, DA_DH), F32),
                      jax.ShapeDtypeStruct((BATCH, 1, DA_HEADS, SEQ, hw), F32)]
        out_specs += [k_spec, v_spec]
    return pl.pallas_call(
        functools.partial(_attn_kernel, seq=seq, latent=latent, lambda_init=lambda_init, heads=heads, nseq=nseq),
        grid=(nb // nseq, steps),
        in_specs=in_specs,
        out_specs=out_specs,
        out_shape=out_shape,
        compiler_params=_cparams(("arbitrary", "arbitrary")),
        name="diff_attn_latent" if latent else "diff_attn_context",
    )(*args)


HALF_D = D_MODEL // 2
HI16 = -65536


def _pack_bf16_pair(lo, hi):
    lo_bits = lax.bitcast_convert_type(lo.astype(BF16).astype(F32), I32)
    hi_bits = lax.bitcast_convert_type(hi.astype(BF16).astype(F32), I32)
    return (hi_bits & HI16) | lax.shift_right_logical(lo_bits, 16)


def _pack_exact_bf16_pair(lo, hi):
    return ((lax.bitcast_convert_type(hi, I32) & HI16)
            | lax.shift_right_logical(lax.bitcast_convert_type(lo, I32), 16))


def _unpack_bf16_pair(p):
    lo = lax.bitcast_convert_type(lax.shift_left(p, 16), F32).astype(BF16)
    hi = lax.bitcast_convert_type(p & HI16, F32).astype(BF16)
    return lo, hi


GRP_EXPERT, GRP_REL, GRP_COUNT = range(3)


def _route_rows(x, g, mod, wr, br, tril_ref, triu_ref, carry_ref):
    u = _norm_mod(x, g, mod, 3, 4)
    u_packed = _pack_bf16_pair(u[:, :HALF_D], u[:, HALF_D:])
    logits = _dot_3pass(u, wr) + br
    lane = lax.broadcasted_iota(I32, logits.shape, 1)
    lane_f = lane.astype(F32)
    neg = -jnp.inf
    big = float(LANES)

    def first_max(vals):
        m = jnp.max(vals, axis=-1, keepdims=True)
        return m, jnp.min(jnp.where(vals == m, lane_f, big), axis=-1, keepdims=True)

    is_grp = lane < N_GROUPS
    gmax, gidx = first_max(jnp.where(is_grp, logits, neg))
    gsum = jnp.sum(jnp.where(is_grp, jnp.exp(logits - gmax), 0.0), axis=-1, keepdims=True)
    g_w = 1.0 / gsum
    lo = N_GROUPS + EXP_PER_GROUP * gidx
    in_grp = jnp.logical_and(lane_f >= lo, lane_f < lo + EXP_PER_GROUP)
    el = jnp.where(in_grp, logits, neg)
    v1, i1 = first_max(el)
    v2, i2 = first_max(jnp.where(lane_f == i1, neg, el))
    e = jnp.exp(v2 - v1)
    p1 = 1.0 / (1.0 + e)
    hit1 = lane_f == i1 - N_GROUPS
    hit2 = lane_f == i2 - N_GROUPS
    onehot = jnp.where(jnp.logical_or(hit1, hit2), 1.0, 0.0)
    cum = jnp.dot(tril_ref[...], onehot.astype(BF16), preferred_element_type=F32)
    n = cum[TM - 1:TM, :]
    n_pad = jnp.floor((n + (ROW_ALIGN - 1.0)) * (1.0 / ROW_ALIGN)) * ROW_ALIGN
    loc_off = jnp.dot(jnp.broadcast_to(n_pad, (SUBLANES, LANES)).astype(BF16), triu_ref[...],
                      preferred_element_type=F32)[0:1]
    base = cum - onehot + loc_off
    loc1 = jnp.sum(jnp.where(hit1, base, 0.0), axis=-1, keepdims=True)
    loc2 = jnp.sum(jnp.where(hit2, base, 0.0), axis=-1, keepdims=True)
    info = jnp.where(lane == 0, g_w * p1, jnp.where(lane == 1, g_w * (e * p1), jnp.where(lane == 2, loc1, loc2)))
    start8 = lax.broadcasted_iota(I32, (LANES, LANES), 0).astype(F32) * ROW_ALIGN
    owner = jnp.where(jnp.logical_and(start8 >= loc_off, start8 < loc_off + n_pad), 1.0, 0.0)
    lane128 = lax.broadcasted_iota(I32, (LANES, LANES), 1)
    c_exp = jnp.sum(owner * lane128.astype(F32), axis=-1, keepdims=True)
    c_rel = jnp.sum(owner * (carry_ref[...] + start8 - loc_off), axis=-1, keepdims=True)
    chunks = jnp.where(lane128 == 0, c_exp, jnp.where(lane128 == 1, c_rel, 0.0)).T
    n_chunks = jnp.sum(n_pad, axis=-1, keepdims=True) * (1.0 / ROW_ALIGN)
    row = lax.broadcasted_iota(I32, (SUBLANES, LANES), 0)
    grp = jnp.where(row == GRP_EXPERT, chunks[0:1], jnp.where(row == GRP_REL, chunks[1:2], n_chunks))
    carry_ref[...] = carry_ref[...] + n_pad
    return u_packed, info, info.T[:SUBLANES], grp.astype(I32)


def _proj_res_router_kernel(*refs, n_x, gate_idx):
    (al_ref, ac_ref, bl_ref, bc_ref, wa_ref, wb_ref, mod_ref, g2_ref, wr_ref, br_ref, tril_ref, triu_ref,
     o_ref, u_ref, loct_ref, info_ref, grp_ref, tot_ref, carry_ref) = refs[n_x:]

    @pl.when(pl.program_id(0) == 0)
    def _():
        carry_ref[...] = jnp.zeros_like(carry_ref)

    a = _group_pick(al_ref, ac_ref)
    b = _group_pick(bl_ref, bc_ref)
    y = jnp.dot(a.astype(BF16), wa_ref[...], preferred_element_type=F32)
    y = y + jnp.dot(b.astype(BF16), wb_ref[...], preferred_element_type=F32)
    mod = mod_ref[0]
    x_new = _rows_value(refs[:n_x]) + mod[gate_idx:gate_idx + 1] * y
    o_ref[...] = x_new
    for t in range(TM_ROUTE // TM):
        rows = slice(t * TM, (t + 1) * TM)
        u_packed, info, info_t, grp = _route_rows(x_new[rows], g2_ref[...], mod, wr_ref[...], br_ref[...],
                                                 tril_ref, triu_ref, carry_ref)
        u_ref[rows, :] = u_packed
        info_ref[rows, :] = info
        loct_ref[t] = info_t
        grp_ref[t] = grp
    tot_ref[...] = jnp.broadcast_to(carry_ref[...], tot_ref.shape).astype(I32)


def _proj_residual_router(x, a, b, w_bf16, mods, gate_idx, norm2, w_route, b_route, tril, triu):
    half = a[0].shape[1]
    tm = TM_ROUTE
    sub = tm // TM
    nt = N_TOK // TM
    x_specs, x_args = _rows_specs(x, tm)
    const = lambda shape: pl.BlockSpec(shape, lambda i: (0,) * len(shape))
    return pl.pallas_call(
        functools.partial(_proj_res_router_kernel, n_x=len(x_args), gate_idx=gate_idx),
        grid=(N_TOK // tm,),
        in_specs=x_specs + _group_specs(half, tm) + _group_specs(half, tm) + [
            pl.BlockSpec((half, D_MODEL), lambda i: (0, 0)),
            pl.BlockSpec((half, D_MODEL), lambda i: (1, 0)),
            pl.BlockSpec((1, MOD_ROWS, D_MODEL), lambda i: (i * tm // SEG, 0, 0)),
            const((1, D_MODEL)), const((D_MODEL, LANES)), const((1, LANES)), const((TM, TM)), const((LANES, LANES)),
        ],
        out_specs=[
            pl.BlockSpec((tm, D_MODEL), lambda i: (i, 0)),
            pl.BlockSpec((tm, HALF_D), lambda i: (i, 0)),
            pl.BlockSpec((sub, SUBLANES, TM), lambda i: (i, 0, 0)),
            pl.BlockSpec((tm, LANES), lambda i: (i, 0)),
            pl.BlockSpec((sub, SUBLANES, LANES), lambda i: (i, 0, 0)),
            pl.BlockSpec((SUBLANES, LANES), lambda i: (0, 0)),
        ],
        out_shape=[
            jax.ShapeDtypeStruct((N_TOK, D_MODEL), F32),
            jax.ShapeDtypeStruct((N_TOK, HALF_D), I32),
            jax.ShapeDtypeStruct((nt, SUBLANES, TM), F32),
            jax.ShapeDtypeStruct((N_TOK, LANES), F32),
            jax.ShapeDtypeStruct((nt, SUBLANES, LANES), I32),
            jax.ShapeDtypeStruct((SUBLANES, LANES), I32),
        ],
        scratch_shapes=[pltpu.VMEM((1, LANES), F32)],
        compiler_params=_cparams(("arbitrary",)),
        name="outproj_residual_router",
    )(*x_args, a[0], a[1], b[0], b[1], w_bf16, w_bf16, mods, norm2.reshape(1, D_MODEL), w_route, b_route, tril, triu)


def _for_chunks(n, fn):
    for size in CHUNKS:
        @pl.when((n & size) != 0)
        def _():
            fn(n & ~(2 * size - 1), size)


N_COPIES = R_LOC // ROW_ALIGN
DUMP_ROWS = (2 * R_LOC + MOE_BLK - 1) // MOE_BLK * MOE_BLK


def _start_tile_copies(grp_ref, seg_ref, make, dummy, n=None):
    if n is None:
        n = grp_ref[0, GRP_COUNT, 0]
    for c in range(N_COPIES):
        real = c < n
        slot = seg_ref[0, grp_ref[0, GRP_EXPERT, c]] + grp_ref[0, GRP_REL, c]
        d_loc, d_slot = dummy(c)
        loc = jnp.where(real, c * ROW_ALIGN, d_loc)
        slot = jnp.where(real, slot, d_slot)
        make(pl.ds(pl.multiple_of(loc, ROW_ALIGN), ROW_ALIGN), pl.ds(pl.multiple_of(slot, ROW_ALIGN), ROW_ALIGN)).start()


def _wait_tile_copies(make):
    make(pl.ds(0, R_LOC), pl.ds(0, R_LOC)).wait()


def _all_experts(body):
    lax.fori_loop(0, N_EXPERTS, lambda e, c: (body(e), c)[1], 0)


def _dispatch_kernel(grp_ref, seg_ref, loct_ref, u_ref, xb_ref, buf, zbuf, sem):
    i = pl.program_id(0)
    last = pl.num_programs(0) - 1
    cur = i % 2

    def maker(b):
        return lambda loc, slot: pltpu.make_async_copy(buf.at[b, loc], xb_ref.at[slot], sem.at[b])

    @pl.when(i >= 2)
    def _():
        _wait_tile_copies(maker(cur))

    lo, hi = _unpack_bf16_pair(u_ref[...])
    row = lax.broadcasted_iota(I32, (R_LOC, TM), 0).astype(F32)
    sel = jnp.logical_or(row == loct_ref[0, 2:3, :], row == loct_ref[0, 3:4, :])
    pt = jnp.where(sel, 1.0, 0.0).astype(BF16)
    buf[cur] = _pack_exact_bf16_pair(jnp.dot(pt, lo, preferred_element_type=F32),
                                     jnp.dot(pt, hi, preferred_element_type=F32))
    _start_tile_copies(grp_ref, seg_ref, maker(cur),
                       lambda c: (c * ROW_ALIGN, MOE_ROWS + cur * R_LOC + c * ROW_ALIGN))

    @pl.when(i == last)
    def _():
        @pl.when(i >= 1)
        def _():
            _wait_tile_copies(maker(1 - cur))

        _wait_tile_copies(maker(cur))

        zbuf[...] = jnp.zeros_like(zbuf)

        def tail(e, act):
            total = seg_ref[1, e]
            dst = seg_ref[0, e] + total

            def one(off, size):
                act(pltpu.make_async_copy(zbuf.at[pl.ds(0, size)],
                                          xb_ref.at[pl.ds(pl.multiple_of(dst + off, ROW_ALIGN), size)], sem.at[2]))

            _for_chunks((-total) & (MOE_BLK - 1), one)

        _all_experts(lambda e: tail(e, lambda cp: cp.start()))
        _all_experts(lambda e: tail(e, lambda cp: cp.wait()))

        e_last = N_EXPERTS - 1
        used = (seg_ref[0, e_last] + seg_ref[1, e_last] + MOE_BLK - 1) // MOE_BLK

        def spare(j):
            return pltpu.make_async_copy(zbuf, xb_ref.at[pl.ds(pl.multiple_of(j * MOE_BLK, MOE_BLK), MOE_BLK)],
                                         sem.at[2])

        lax.fori_loop(used, (MOE_ROWS + DUMP_ROWS) // MOE_BLK, lambda j, c: (spare(j).start(), c)[1], 0)
        lax.fori_loop(used, (MOE_ROWS + DUMP_ROWS) // MOE_BLK, lambda j, c: (spare(j).wait(), c)[1], 0)


def _dispatch(grp, seg, loct, u):
    return pl.pallas_call(
        _dispatch_kernel,
        grid=(N_TOK // TM,),
        in_specs=[
            pl.BlockSpec((1, SUBLANES, LANES), lambda i: (i, 0, 0), memory_space=pltpu.SMEM),
            pl.BlockSpec(memory_space=pltpu.SMEM),
            pl.BlockSpec((1, SUBLANES, TM), lambda i: (i, 0, 0)),
            pl.BlockSpec((TM, HALF_D), lambda i: (i, 0)),
        ],
        out_specs=pl.BlockSpec(memory_space=pl.ANY),
        out_shape=jax.ShapeDtypeStruct((MOE_ROWS + DUMP_ROWS, HALF_D), I32),
        scratch_shapes=[pltpu.VMEM((2, R_LOC, HALF_D), I32), pltpu.VMEM((MOE_BLK, HALF_D), I32),
                        pltpu.SemaphoreType.DMA((3,))],
        compiler_params=_cparams(("arbitrary",)),
        name="moe_dispatch",
    )(grp, seg, loct, u)


def _expert_kernel(seg_ref, w1_ref, w3_ref, w2_ref, xb_ref, yb_ref, w1b, w3b, w2b, xbuf, ybuf, xsem, ysem):
    e = pl.program_id(0)
    last_e = pl.num_programs(0) - 1
    blk0 = seg_ref[0, e] // MOE_BLK
    nblk = (seg_ref[1, e] + MOE_BLK - 1) // MOE_BLK
    used = (seg_ref[0, N_EXPERTS - 1] + seg_ref[1, N_EXPERTS - 1] + MOE_BLK - 1) // MOE_BLK

    def rows(g):
        return pl.ds(pl.multiple_of(g * MOE_BLK, MOE_BLK), MOE_BLK)

    def fetch(g, b):
        return pltpu.make_async_copy(xb_ref.at[rows(g)], xbuf.at[b], xsem.at[b])

    def flush(g, b):
        return pltpu.make_async_copy(ybuf.at[b], yb_ref.at[rows(g)], ysem.at[b])

    row_queue = 1

    @pl.when(e == 0)
    def _():
        fetch(0, 0).start(priority=row_queue)

    @pl.when(nblk > 0)
    def _():
        w1b[...] = w1_ref[0, 0].astype(BF16)
        w3b[...] = w3_ref[0, 0].astype(BF16)
        w2b[...] = w2_ref[0, 0].astype(BF16)

        def block(i, carry):
            g = blk0 + i
            b = g % 2
            fetch(g, b).wait()

            @pl.when(g + 1 < used)
            def _():
                fetch(g + 1, 1 - b).start(priority=row_queue)

            @pl.when(g >= 2)
            def _():
                flush(g - 2, b).wait()

            def ffn(x_packed):
                lo, hi = _unpack_bf16_pair(x_packed)

                def up(wb):
                    return (jnp.dot(lo, wb[:HALF_D, :], preferred_element_type=F32)
                            + jnp.dot(hi, wb[HALF_D:, :], preferred_element_type=F32))

                hdn = (_silu(up(w1b)) * up(w3b)).astype(BF16)
                y = jnp.dot(hdn, w2b[...], preferred_element_type=F32)
                return _pack_bf16_pair(y[:, :HALF_D], y[:, HALF_D:])

            slots_here = seg_ref[1, e] - i * MOE_BLK
            quarter = MOE_BLK // MOE_PARTS
            for parts in range(1, MOE_PARTS + 1):
                rows_used = parts * quarter
                fits = slots_here > rows_used - quarter
                if parts < MOE_PARTS:
                    fits = jnp.logical_and(fits, slots_here <= rows_used)

                @pl.when(fits)
                def _():
                    ybuf[b, :rows_used, :] = ffn(xbuf[b, :rows_used, :])
                    if rows_used < MOE_BLK:
                        ybuf[b, rows_used:, :] = jnp.zeros((MOE_BLK - rows_used, HALF_D), I32)

            flush(g, b).start(priority=row_queue)
            return carry

        lax.fori_loop(0, nblk, block, 0)

    @pl.when(e == last_e)
    def _():
        @pl.when(used >= 2)
        def _():
            flush(used - 2, used % 2).wait()

        flush(used - 1, (used - 1) % 2).wait()
        xbuf[0] = jnp.zeros((MOE_BLK, HALF_D), I32)

        def spare(g):
            return pltpu.make_async_copy(xbuf.at[0], yb_ref.at[rows(g)], xsem.at[0])

        lax.fori_loop(used, MOE_NB, lambda g, c: (spare(g).start(), c)[1], 0)
        lax.fori_loop(used, MOE_NB, lambda g, c: (spare(g).wait(), c)[1], 0)


def _experts(seg, xb, w1, w3, w2, layer):
    hbm = pl.BlockSpec(memory_space=pl.ANY)
    w_in = pl.BlockSpec((1, 1, D_MODEL, D_EXPERT), lambda e, s: (layer, e, 0, 0))
    w_out = pl.BlockSpec((1, 1, D_EXPERT, D_MODEL), lambda e, s: (layer, e, 0, 0))
    return pl.pallas_call(
        _expert_kernel,
        grid_spec=pltpu.PrefetchScalarGridSpec(
            num_scalar_prefetch=1,
            grid=(N_EXPERTS,),
            in_specs=[w_in, w_in, w_out, hbm],
            out_specs=hbm,
            scratch_shapes=[pltpu.VMEM((D_MODEL, D_EXPERT), BF16), pltpu.VMEM((D_MODEL, D_EXPERT), BF16),
                            pltpu.VMEM((D_EXPERT, D_MODEL), BF16),
                            pltpu.VMEM((2, MOE_BLK, HALF_D), I32), pltpu.VMEM((2, MOE_BLK, HALF_D), I32),
                            pltpu.SemaphoreType.DMA((2,)), pltpu.SemaphoreType.DMA((2,))],
        ),
        out_shape=jax.ShapeDtypeStruct((MOE_ROWS, HALF_D), I32),
        compiler_params=_cparams(("arbitrary",)),
        name="moe_experts",
    )(seg, w1, w3, w2, xb)


def _combine_kernel(*refs, final):
    grp_ref, nxt_ref, seg_ref, x_ref, info_ref, mod_ref, yb_ref = refs[:7]
    if final:
        fin_ref, olat_ref, octx_ref, ybuf, sem = refs[7:]
    else:
        g1n_ref, modn_ref, wn_ref, o_ref, proj_ref, ybuf, sem, wnb_ref = refs[7:]
    i = pl.program_id(0)
    cur = i % 2

    def maker(b):
        return lambda loc, slot: pltpu.make_async_copy(yb_ref.at[slot], ybuf.at[b, loc], sem.at[b])

    def dummy(c):
        return R_LOC + c * ROW_ALIGN, c * ROW_ALIGN

    @pl.when(i == 0)
    def _():
        ybuf[...] = jnp.zeros_like(ybuf)
        _start_tile_copies(grp_ref, seg_ref, maker(0), dummy)
        if not final:
            wnb_ref[...] = wn_ref[...].astype(BF16)

    has_next = i + 1 < pl.num_programs(0)
    _start_tile_copies(nxt_ref, seg_ref, maker(1 - cur), dummy, jnp.where(has_next, nxt_ref[0, GRP_COUNT, 0], 0))
    _wait_tile_copies(maker(cur))

    @pl.when(jnp.logical_not(has_next))
    def _():
        _wait_tile_copies(maker(1 - cur))

    info = info_ref[...]
    col = lax.broadcasted_iota(I32, (TM, R_LOC), 1).astype(F32)
    p = jnp.where(col == info[:, 2:3], info[:, 0:1], jnp.where(col == info[:, 3:4], info[:, 1:2], 0.0))
    p = p.astype(BF16)
    lo, hi = _unpack_bf16_pair(ybuf[cur, :R_LOC, :])
    y = jnp.concatenate([jnp.dot(p, lo, preferred_element_type=F32),
                         jnp.dot(p, hi, preferred_element_type=F32)], axis=1)
    out = x_ref[...] + mod_ref[0][5:6] * y
    if not final:
        o_ref[...] = out
        u_next = _norm_mod(out, g1n_ref[...], modn_ref[0], 0, 1)
        proj_ref[...] = jnp.dot(u_next.astype(BF16), wnb_ref[...], preferred_element_type=F32)
        return
    out = out * lax.rsqrt(jnp.mean(out * out, axis=-1, keepdims=True) + EPS) * fin_ref[...]

    @pl.when(i < N_LAT_TILES)
    def _():
        olat_ref[...] = out

    @pl.when(i >= N_LAT_TILES)
    def _():
        octx_ref[...] = out


def _combine(x, grp, seg, info, mods, yb, final_g=None, next_proj=None):
    final = final_g is not None
    nt = N_TOK // TM
    in_specs = [
        pl.BlockSpec((1, SUBLANES, LANES), lambda i: (i, 0, 0), memory_space=pltpu.SMEM),
        pl.BlockSpec((1, SUBLANES, LANES), lambda i: (jnp.minimum(i + 1, nt - 1), 0, 0), memory_space=pltpu.SMEM),
        pl.BlockSpec(memory_space=pltpu.SMEM),
        pl.BlockSpec((TM, D_MODEL), lambda i: (i, 0)),
        pl.BlockSpec((TM, LANES), lambda i: (i, 0)),
        pl.BlockSpec((1, MOD_ROWS, D_MODEL), lambda i: (i * TM // SEG, 0, 0)),
        pl.BlockSpec(memory_space=pl.ANY),
    ]
    args = [grp, grp, seg, x, info, mods, yb]
    if final:
        in_specs.append(pl.BlockSpec((1, D_MODEL), lambda i: (0, 0)))
        args.append(final_g.reshape(1, D_MODEL))
        out_specs = _group_specs(D_MODEL)
        out_shape = [jax.ShapeDtypeStruct((N_SAMPLE, D_MODEL), F32), jax.ShapeDtypeStruct((N_PROMPT, D_MODEL), F32)]
    else:
        g1n, modn, wn = next_proj
        n = wn.shape[1]
        in_specs += [pl.BlockSpec((1, D_MODEL), lambda i: (0, 0)),
                     pl.BlockSpec((1, MOD_ROWS, D_MODEL), lambda i: (i * TM // SEG, 0, 0)),
                     pl.BlockSpec((D_MODEL, n), lambda i: (0, 0), pipeline_mode=pl.Buffered(1))]
        args += [g1n.reshape(1, D_MODEL), modn, wn]
        out_specs = [pl.BlockSpec((TM, D_MODEL), lambda i: (i, 0)), pl.BlockSpec((TM, n), lambda i: (i, 0))]
        out_shape = [jax.ShapeDtypeStruct((N_TOK, D_MODEL), F32), jax.ShapeDtypeStruct((N_TOK, n), F32)]
    return pl.pallas_call(
        functools.partial(_combine_kernel, final=final),
        grid=(nt,),
        in_specs=in_specs,
        out_specs=out_specs,
        out_shape=out_shape,
        scratch_shapes=([pltpu.VMEM((2, 2 * R_LOC, HALF_D), I32), pltpu.SemaphoreType.DMA((2,))]
                        + ([] if final else [pltpu.VMEM(next_proj[2].shape, BF16)])),
        compiler_params=_cparams(("arbitrary",)),
        name="moe_combine_final" if final else "moe_combine",
    )(*args)


def _route_weights(w_grp, b_grp, w_exp, b_exp):
    spare = LANES - N_GROUPS - N_EXPERTS
    w_route = jnp.concatenate([w_grp, w_exp, jnp.zeros((D_MODEL, spare), F32)], axis=1)
    b_route = jnp.concatenate([b_grp, b_exp, jnp.zeros((spare,), F32)])[None, :]
    return w_route, b_route


def _moe_layer(x, route, layer, mods, w1, w3, w2, final_g=None, next_proj=None):
    u, loct, info, grp, tot = route
    total = tot[0, :N_EXPERTS]
    padded = (total + MOE_BLK - 1) // MOE_BLK * MOE_BLK
    pad_end = jnp.cumsum(padded)
    pad_start = pad_end - padded
    seg = jnp.zeros((2, LANES), I32).at[0, :N_EXPERTS].set(pad_start).at[1, :N_EXPERTS].set(total)
    xb = _dispatch(grp, seg, loct, u)
    yb = _experts(seg, xb, w1, w3, w2, layer)
    return _combine(x, grp, seg, info, mods, yb, final_g, next_proj)


def _pad_rows(w, rows=SUBLANES):
    return jnp.zeros((rows, w.shape[1]), w.dtype).at[:w.shape[0]].set(w)


def _pad2(w, rows, cols):
    return jnp.zeros((rows, cols), w.dtype).at[:w.shape[0], :w.shape[1]].set(w)


def kernel(x_prompt, x_sample, c, c_ctx, state_ret, cache_k, cache_v, ada_w, ada_b, norm1, norm2, final_norm, ev_w_in, ev_w_out, ret_decay_fwd, ret_decay_bwd, sc_conv_w, od_w_in, od_w_out, hy_conv_w, hy_w1, hy_b1, hy_fr1, hy_w2, hy_b2, hy_fr2, hy_w3, hy_bias, da_lambda, da_subln, moe_w_grp, moe_b_grp, moe_w_exp, moe_b_exp, moe_w1, moe_w3, moe_w2):
    x = (x_sample.reshape(N_SAMPLE, D_MODEL), x_prompt.reshape(N_PROMPT, D_MODEL))

    cvecs = jnp.zeros((16, D_MODEL), F32).at[:DEC_BATCH].set(c).at[DEC_BATCH].set(c_ctx)
    m = _mods(cvecs, ada_w, ada_b).reshape(DEPTH, 16, N_MOD, D_MODEL)
    seg_mods = jnp.concatenate(
        [m[:, :DEC_BATCH], jnp.broadcast_to(m[:, DEC_BATCH:DEC_BATCH + 1], (DEPTH, N_SEG - DEC_BATCH, N_MOD, D_MODEL))],
        axis=1)
    seg_mods = jnp.pad(seg_mods, ((0, 0), (0, 0), (0, MOD_ROWS - N_MOD), (0, 0)))

    tril = jnp.asarray(np.tril(np.ones((TM, TM), np.float32)), BF16)
    triu = jnp.asarray(np.triu(np.ones((LANES, LANES), np.float32), 1), BF16)

    def w_in(layer):
        w = ev_w_in if layer % 2 == 0 else od_w_in
        return w[layer // 2]

    proj = None
    for l in range(DEPTH):
        mods = seg_mods[l]
        i = l // 2
        if l % 2 == 0:
            if proj is None:
                proj = _norm_mod_matmul(x, norm1[l], mods, w_in(l), 0, 1)
            scw = _pad_rows(sc_conv_w[i])
            ret_l, sc_l = _even_mixer(proj, scw, ret_decay_fwd[i], ret_decay_bwd[i], state=state_ret[:, i:i + 1])
            ret_c, sc_c, new_state = _even_mixer(proj, scw, ret_decay_fwd[i], ret_decay_bwd[i])
            mixed, w_out = ((ret_l, ret_c), (sc_l, sc_c)), ev_w_out[i]
        else:
            lambda_init = 0.8 - 0.6 * math.exp(-0.3 * l)
            if proj is None:
                proj = _norm_mod_matmul(x, norm1[l], mods, w_in(l), 0, 1)
            conv_w = _pad_rows(hy_conv_w[i])
            bias = _pad_rows(hy_bias[i])
            w1 = _pad2(hy_w1[i], LANES, LANES)
            b1 = _pad2(hy_b1[i][None], 1, LANES)
            f1 = _pad2(hy_fr1[i][None], 1, LANES)
            w2 = _pad2(hy_w2[i], LANES, LANES)
            b2 = _pad2(hy_b2[i][None], 1, LANES)
            f2 = _pad2(hy_fr2[i][None], 1, LANES)
            w3 = _pad2(hy_w3[i], LANES, hy_w3.shape[2])
            hy = []
            for latent in (True, False):
                seq = DEC_SEQ if latent else SEQ
                fwd_np, inv_np = _dft_tables(seq)
                zfeat, decay = _hyena_consts(seq)
                fwd_bf, fwd_lo = _split_table(jnp.asarray(fwd_np), f"dft_fwd_split_{seq}")
                inv_bf, _ = _split_table(jnp.asarray(inv_np), f"dft_inv_split_{seq}")
                filters = _hy_filters(seq, jnp.asarray(zfeat), jnp.asarray(decay), fwd_bf, fwd_lo,
                                      w1, b1, f1, w2, b2, f2, w3)
                hy.append(_hy_apply(proj, conv_w, fwd_bf, inv_bf, filters, bias, latent))
            cos, sin = _rope_tables(DEC_SEQ)
            att_l = _attention(proj, da_lambda[i], da_subln[i], lambda_init, True,
                               rope=(jnp.asarray(cos), jnp.asarray(sin)),
                               ctx_k=cache_k[:, i:i + 1], ctx_v=cache_v[:, i:i + 1])[0]
            att_c, new_k, new_v = _attention(proj, da_lambda[i], da_subln[i], lambda_init, False)
            mixed, w_out = (tuple(hy), (att_l, att_c)), od_w_out[i]
        w_route, b_route = _route_weights(moe_w_grp[l], moe_b_grp[l], moe_w_exp[l], moe_b_exp[l])
        x, *route = _proj_residual_router(x, mixed[0], mixed[1], w_out.astype(BF16), mods, 2,
                                          norm2[l], w_route, b_route, tril, triu)
        if l == DEPTH - 1:
            x = _moe_layer(x, route, l, mods, moe_w1, moe_w3, moe_w2, final_g=final_norm)
        else:
            x, proj = _moe_layer(x, route, l, mods, moe_w1, moe_w3, moe_w2,
                                 next_proj=(norm1[l + 1], seg_mods[l + 1], w_in(l + 1)))

    y_sample, y_prompt = x
    return (y_prompt.reshape(BATCH, SEQ, D_MODEL), y_sample.reshape(DEC_BATCH, DEC_SEQ, D_MODEL),
            new_state, new_k, new_v)
```
